```python
import math
import jax, jax.numpy as jnp
from jax import lax
import numpy as np

D_MODEL = 2048
BATCH = 4
SEQ = 2048
DEPTH = 2

CTX_LEN = 256
GRID_W = 64
N_MIXERS = 4
GROUP_W = D_MODEL // N_MIXERS

CHUNK = 128
SGU_HEADS = 4
SGU_HEAD_DIM = GROUP_W // SGU_HEADS

MLA_HEADS = 4
QK_NOPE = 128
QK_ROPE = 64
V_DIM = GROUP_W // MLA_HEADS
Q_LORA = 384
KV_LORA = 256
ROPE_THETA = 10000.0
Q_BLOCK = 128

S5_CH = 16
S5_GROUPS = GROUP_W // S5_CH
S5_STATE = 64
S5_MAX_RE = -1e-4
S5_DT_MIN = 1e-3
S5_DT_MAX = 1e-1

CONV_WIDTH = 3

N_EXPERTS = 16
EXPERT_FF = D_MODEL // 2
CAPACITY_FACTOR = 2

EPS = 1e-6

IN_SGU = 2 * GROUP_W
IN_MLA = Q_LORA + KV_LORA + QK_ROPE
IN_S5 = GROUP_W
IN_CONV = 3 * GROUP_W
IN_COLS = IN_SGU + IN_MLA + IN_S5 + IN_CONV
IN_SPLITS = (IN_SGU, IN_SGU + IN_MLA, IN_SGU + IN_MLA + IN_S5)

kernel_name = "hybrid_parallel_dit_block"


def rmsnorm(x, g):
    xf = x.astype(jnp.float32)
    y = xf * lax.rsqrt(jnp.mean(xf * xf, axis=-1, keepdims=True) + EPS)
    return (y * g.astype(jnp.float32)).astype(x.dtype)


def sgu_mix(p, norm_g, w_s, b_s):
    bsz, n, _ = p.shape
    u, v = jnp.split(jax.nn.gelu(p), 2, axis=-1)
    v = rmsnorm(v, norm_g).reshape(bsz, n // CHUNK, CHUNK, SGU_HEADS, SGU_HEAD_DIM)
    mixed = jnp.einsum("hpq,bkqhc->bkphc", w_s, v) + jnp.swapaxes(b_s, 0, 1)[None, None, :, :, None]
    return u * mixed.reshape(bsz, n, GROUP_W)


def rotate_pairs(xh, ang):
    x1, x2 = jnp.split(xh, 2, axis=-1)
    cos = jnp.cos(ang).astype(xh.dtype)
    sin = jnp.sin(ang).astype(xh.dtype)
    return jnp.concatenate([x1 * cos - x2 * sin, x2 * cos + x1 * sin], axis=-1)


def rope_2d(x, ang_row, ang_col):
    xr, xc = jnp.split(x, 2, axis=-1)
    return jnp.concatenate([rotate_pairs(xr, ang_row), rotate_pairs(xc, ang_col)], axis=-1)


def mla_queries(p_q, q_norm_g, w_uq):
    bsz, n, _ = p_q.shape
    q = (rmsnorm(p_q, q_norm_g) @ w_uq).reshape(bsz, n, MLA_HEADS, QK_NOPE + QK_ROPE)
    return q[..., :QK_NOPE], q[..., QK_NOPE:]


def mla_keys_values(p_kv, kv_norm_g, w_ukv):
    bsz, n, _ = p_kv.shape
    kv = (rmsnorm(p_kv, kv_norm_g) @ w_ukv).reshape(bsz, n, MLA_HEADS, QK_NOPE + V_DIM)
    return kv[..., :QK_NOPE], kv[..., QK_NOPE:]


def attend(qn, qr, kn, kr, v):
    s = jnp.einsum("bqhd,bkhd->bhqk", qn, kn) + jnp.einsum("bqhr,bkr->bhqk", qr, kr)
    pr = jax.nn.softmax(s.astype(jnp.float32) * (QK_NOPE + QK_ROPE) ** -0.5, axis=-1)
    return jnp.einsum("bhqk,bkhd->bqhd", pr.astype(v.dtype), v)


def blocked_attention(qn, qr, kn, kr, v):
    bsz, n = qn.shape[0], qn.shape[1]
    nblk = n // Q_BLOCK

    def to_blocks(t):
        return jnp.moveaxis(t.reshape(bsz, nblk, Q_BLOCK, *t.shape[2:]), 1, 0)

    out = lax.map(lambda qb: attend(qb[0], qb[1], kn, kr, v), (to_blocks(qn), to_blocks(qr)))
    return jnp.moveaxis(out, 0, 1).reshape(bsz, n, MLA_HEADS * V_DIM)


def s5_discretise(a_re, a_im, log_dt, b_re, b_im, c_re, c_im):
    a = lax.complex(jnp.minimum(a_re.astype(jnp.float32), S5_MAX_RE), a_im.astype(jnp.float32))
    dt = jnp.exp(log_dt.astype(jnp.float32))[..., None]
    abar = jnp.exp(a * dt)
    bbar = ((abar - 1.0) / a)[..., None] * lax.complex(b_re.astype(jnp.float32), b_im.astype(jnp.float32))
    cmat = lax.complex(c_re.astype(jnp.float32), c_im.astype(jnp.float32))
    return abar, bbar, cmat


def ssm_combine(e1, e2):
    a1, b1 = e1
    a2, b2 = e2
    return a1 * a2, a2 * b1 + b2


def linear_scan(abar, bu, s0, reverse):
    if s0 is not None:
        edge = -1 if reverse else 0
        bu = bu.at[:, edge].add(abar * s0)
    a = jnp.broadcast_to(abar, bu.shape)
    _, s = lax.associative_scan(ssm_combine, (a, bu), reverse=reverse, axis=1)
    return s


def s5_states(u, abar, bbar, s0_f, s0_b):
    bsz, n, _ = u.shape
    uf = u.astype(jnp.float32).reshape(bsz, n, S5_GROUPS, S5_CH)
    bu = jnp.einsum("blgc,zgnc->zblgn", uf, bbar)
    s_f = linear_scan(abar[0], bu[0], s0_f, reverse=False)
    s_b = linear_scan(abar[1], bu[1], s0_b, reverse=True)
    return s_f, s_b


def s5_readout(u, s_f, s_b, cmat, d_skip, w_glu, b_glu):
    bsz, n, _ = u.shape
    uf = u.astype(jnp.float32).reshape(bsz, n, S5_GROUPS, S5_CH)
    y = jnp.real(jnp.einsum("blgn,gcn->blgc", s_f, cmat[0]) + jnp.einsum("blgn,gcn->blgc", s_b, cmat[1]))
    y = y + d_skip.astype(jnp.float32).reshape(S5_GROUPS, S5_CH) * uf
    g = jax.nn.gelu(y.reshape(bsz, n, GROUP_W).astype(u.dtype))
    return g * jax.nn.sigmoid(g @ w_glu + b_glu)


def conv_mix(p, conv_w):
    b_gate, c_gate, h = jnp.split(p, 3, axis=-1)
    z = c_gate * h
    y = lax.conv_general_dilated(
        z, conv_w[:, None, :], window_strides=(1,),
        padding=((CONV_WIDTH // 2, CONV_WIDTH // 2),),
        dimension_numbers=("NWC", "WIO", "NWC"), feature_group_count=GROUP_W)
    return b_gate * y


def token_mixing(h_l, h_c, ang_row, ang_col, with_ctx_out, w_in, w_out,
                 sgu_norm_g, sgu_w, sgu_b, q_norm_g, w_uq, kv_norm_g, w_ukv,
                 a_re, a_im, log_dt, b_re, b_im, c_re, c_im, d_skip, w_glu, b_glu, conv_w):
    p_l = jnp.split(h_l @ w_in, IN_SPLITS, axis=-1)
    p_c = jnp.split(h_c @ w_in, IN_SPLITS, axis=-1)

    kn_c, v_c = mla_keys_values(p_c[1][..., Q_LORA:Q_LORA + KV_LORA], kv_norm_g, w_ukv)
    kr_c = p_c[1][..., Q_LORA + KV_LORA:]
    qn_l, qr_l = mla_queries(p_l[1][..., :Q_LORA], q_norm_g, w_uq)
    kn_l, v_l = mla_keys_values(p_l[1][..., Q_LORA:Q_LORA + KV_LORA], kv_norm_g, w_ukv)
    qr_l = rope_2d(qr_l, ang_row[:, None, :], ang_col[:, None, :])
    kr_l = rope_2d(p_l[1][..., Q_LORA + KV_LORA:], ang_row, ang_col)
    attn_l = blocked_attention(qn_l, qr_l,
                               jnp.concatenate([kn_c, kn_l], axis=1),
                               jnp.concatenate([kr_c, kr_l], axis=1),
                               jnp.concatenate([v_c, v_l], axis=1))

    abar, bbar, cmat = s5_discretise(a_re, a_im, log_dt, b_re, b_im, c_re, c_im)
    sf_c, sb_c = s5_states(p_c[2], abar, bbar, None, None)
    sf_l, sb_l = s5_states(p_l[2], abar, bbar, sf_c[:, -1], sb_c[:, 0])
    ssm_l = s5_readout(p_l[2], sf_l, sb_l, cmat, d_skip, w_glu, b_glu)

    out_l = jnp.concatenate([sgu_mix(p_l[0], sgu_norm_g, sgu_w, sgu_b), attn_l, ssm_l,
                             conv_mix(p_l[3], conv_w)], axis=-1) @ w_out
    if not with_ctx_out:
        return out_l, None

    bsz, n_c = h_c.shape[0], h_c.shape[1]
    qn_c, qr_c = mla_queries(p_c[1][..., :Q_LORA], q_norm_g, w_uq)
    attn_c = attend(qn_c, qr_c, kn_c, kr_c, v_c).reshape(bsz, n_c, GROUP_W)
    ssm_c = s5_readout(p_c[2], sf_c, sb_c, cmat, d_skip, w_glu, b_glu)
    out_c = jnp.concatenate([sgu_mix(p_c[0], sgu_norm_g, sgu_w, sgu_b), attn_c, ssm_c,
                             conv_mix(p_c[3], conv_w)], axis=-1) @ w_out
    return out_l, out_c


def ec_moe(h, w_router, w_gate, w_up, w_down):
    bsz, n, _ = h.shape
    cap = max(1, CAPACITY_FACTOR * n // N_EXPERTS)
    aff = jax.nn.softmax((h @ w_router).astype(jnp.float32), axis=-1)
    top_aff, top_idx = lax.top_k(jnp.swapaxes(aff, 1, 2), cap)
    bidx = jnp.arange(bsz)[:, None, None]
    xs = h[bidx, top_idx]
    hid = jax.nn.silu(jnp.einsum("becd,edf->becf", xs, w_gate)) * jnp.einsum("becd,edf->becf", xs, w_up)
    ys = jnp.einsum("becf,efd->becd", hid, w_down) * top_aff[..., None].astype(h.dtype)
    return jnp.zeros_like(h).at[bidx, top_idx].add(ys)


def setup_inputs(seed: int = 0) -> dict:
    key = jax.random.key(seed)
    ks = iter(jax.random.split(key, 48))
    f32 = jnp.float32

    def nrm(shape, scale):
        return scale * jax.random.normal(next(ks), shape, f32)

    L, G, N = DEPTH, S5_GROUPS, S5_STATE
    return {
        "x": nrm((BATCH, SEQ, D_MODEL), 1.0),
        "c": nrm((BATCH, D_MODEL), 1.0),
        "ctx": nrm((BATCH, CTX_LEN, D_MODEL), 1.0),
        "c_ctx": nrm((D_MODEL,), 1.0),
        "norm1_g": 1.0 + nrm((L, D_MODEL), 0.1),
        "norm2_g": 1.0 + nrm((L, D_MODEL), 0.1),
        "w_ada": nrm((L, D_MODEL, 6 * D_MODEL), 0.5 * D_MODEL ** -0.5),
        "b_ada": nrm((L, 6 * D_MODEL), 0.02),
        "w_in": nrm((L, D_MODEL, IN_COLS), D_MODEL ** -0.5),
        "w_out": nrm((L, N_MIXERS * GROUP_W, D_MODEL), (N_MIXERS * GROUP_W) ** -0.5),
        "sgu_norm_g": 1.0 + nrm((L, GROUP_W), 0.1),
        "sgu_w": nrm((L, SGU_HEADS, CHUNK, CHUNK), CHUNK ** -0.5),
        "sgu_b": 1.0 + nrm((L, SGU_HEADS, CHUNK), 0.1),
        "mla_q_norm_g": 1.0 + nrm((L, Q_LORA), 0.1),
        "mla_w_uq": nrm((L, Q_LORA, MLA_HEADS * (QK_NOPE + QK_ROPE)), Q_LORA ** -0.5),
        "mla_kv_norm_g": 1.0 + nrm((L, KV_LORA), 0.1),
        "mla_w_ukv": nrm((L, KV_LORA, MLA_HEADS * (QK_NOPE + V_DIM)), KV_LORA ** -0.5),
        "s5_a_re": -0.5 + nrm((L, 2, G, N), 0.01),
        "s5_a_im": math.pi * jnp.arange(N, dtype=f32) + nrm((L, 2, G, N), 0.01),
        "s5_log_dt": jax.random.uniform(next(ks), (L, 2, G), f32, math.log(S5_DT_MIN), math.log(S5_DT_MAX)),
        "s5_b_re": nrm((L, 2, G, N, S5_CH), (2 * S5_CH) ** -0.5),
        "s5_b_im": nrm((L, 2, G, N, S5_CH), (2 * S5_CH) ** -0.5),
        "s5_c_re": nrm((L, 2, G, S5_CH, N), (2 * N) ** -0.5),
        "s5_c_im": nrm((L, 2, G, S5_CH, N), (2 * N) ** -0.5),
        "s5_d": nrm((L, GROUP_W), 1.0),
        "s5_w_glu": nrm((L, GROUP_W, GROUP_W), GROUP_W ** -0.5),
        "s5_b_glu": nrm((L, GROUP_W), 0.02),
        "conv_w": nrm((L, CONV_WIDTH, GROUP_W), CONV_WIDTH ** -0.5),
        "moe_w_router": nrm((L, D_MODEL, N_EXPERTS), D_MODEL ** -0.5),
        "moe_w_gate": nrm((L, N_EXPERTS, D_MODEL, EXPERT_FF), D_MODEL ** -0.5),
        "moe_w_up": nrm((L, N_EXPERTS, D_MODEL, EXPERT_FF), D_MODEL ** -0.5),
        "moe_w_down": nrm((L, N_EXPERTS, EXPERT_FF, D_MODEL), EXPERT_FF ** -0.5),
        "final_norm_g": 1.0 + nrm((D_MODEL,), 0.1),
    }


def reference(x, c, ctx, c_ctx, norm1_g, norm2_g, w_ada, b_ada, w_in, w_out,
              sgu_norm_g, sgu_w, sgu_b, mla_q_norm_g, mla_w_uq, mla_kv_norm_g, mla_w_ukv,
              s5_a_re, s5_a_im, s5_log_dt, s5_b_re, s5_b_im, s5_c_re, s5_c_im, s5_d,
              s5_w_glu, s5_b_glu, conv_w, moe_w_router, moe_w_gate, moe_w_up, moe_w_down,
              final_norm_g):
    n = x.shape[1]
    rows = n // GRID_W
    row_id = jnp.repeat(jnp.arange(rows, dtype=jnp.float32), GRID_W)
    col_id = jnp.tile(jnp.arange(GRID_W, dtype=jnp.float32), rows)
    n_freq = QK_ROPE // 4
    inv_freq = ROPE_THETA ** (-jnp.arange(n_freq, dtype=jnp.float32) / n_freq)
    ang_row = row_id[:, None] * inv_freq
    ang_col = col_id[:, None] * inv_freq

    silu_c = jax.nn.silu(c)
    silu_cc = jax.nn.silu(c_ctx)
    x_l, x_c = x, ctx
    for i in range(DEPTH):
        last = i == DEPTH - 1
        mod_l = [m[:, None, :] for m in jnp.split(silu_c @ w_ada[i] + b_ada[i], 6, axis=-1)]
        mod_c = jnp.split(silu_cc @ w_ada[i] + b_ada[i], 6, axis=-1)

        h_l = rmsnorm(x_l, norm1_g[i]) * (1.0 + mod_l[1]) + mod_l[0]
        h_c = rmsnorm(x_c, norm1_g[i]) * (1.0 + mod_c[1]) + mod_c[0]
        o_l, o_c = token_mixing(h_l, h_c, ang_row, ang_col, not last, w_in[i], w_out[i],
                                sgu_norm_g[i], sgu_w[i], sgu_b[i],
                                mla_q_norm_g[i], mla_w_uq[i], mla_kv_norm_g[i], mla_w_ukv[i],
                                s5_a_re[i], s5_a_im[i], s5_log_dt[i], s5_b_re[i], s5_b_im[i],
                                s5_c_re[i], s5_c_im[i], s5_d[i], s5_w_glu[i], s5_b_glu[i], conv_w[i])
        x_l = x_l + mod_l[2] * o_l
        h_l = rmsnorm(x_l, norm2_g[i]) * (1.0 + mod_l[4]) + mod_l[3]
        x_l = x_l + mod_l[5] * ec_moe(h_l, moe_w_router[i], moe_w_gate[i], moe_w_up[i], moe_w_down[i])

        if not last:
            x_c = x_c + mod_c[2] * o_c
            h_c = rmsnorm(x_c, norm2_g[i]) * (1.0 + mod_c[4]) + mod_c[3]
            x_c = x_c + mod_c[5] * ec_moe(h_c, moe_w_router[i], moe_w_gate[i], moe_w_up[i], moe_w_down[i])

    return rmsnorm(x_l, final_norm_g)
```

```python
import functools
import math

import jax
import jax.numpy as jnp
from jax import lax
from jax.experimental import pallas as pl
from jax.experimental.pallas import tpu as pltpu

F32 = jnp.float32
BF16 = jnp.bfloat16

D = 2048
B = 4
SEQ = 2048
CTX = 256
S2 = SEQ + CTX
DEPTH = 2
GW = 512
EPS = 1e-6

TM = 256
TPB = S2 // TM
LPB = SEQ // TM
NT = B * TPB
NLT = B * LPB

SGU_HEADS = 4
CHUNK = 128
MLA_HEADS = 4
QK_NOPE = 128
QK_ROPE = 64
Q_LORA = 384
KV_LORA = 256
QW = MLA_HEADS * (QK_NOPE + QK_ROPE)
ATT_SCALE = (QK_NOPE + QK_ROPE) ** -0.5

S5_G = 32
S5_N = 64
S5_CH = 16
S5_T = 128
S5_NCH = S2 // S5_T
S5_HALF = (S5_G // 2) * S5_N

N_EXP = 16
FF = D // 2
CAP = 2 * SEQ // N_EXP
CAP_C = 2 * CTX // N_EXP
FF_T = 256

IN_W = 3840
MIB = 1024 * 1024


def _cp(sem, vmem_mb):
    return pltpu.CompilerParams(dimension_semantics=sem, vmem_limit_bytes=vmem_mb * MIB)


def _all_tile(i):
    return i


def _lat_tile(i):
    return (i // LPB) * TPB + i % LPB


def _seg_of_tile(t):
    return jnp.where(t % TPB == TPB - 1, B, t // TPB)


def _modnorm(x, g, shift, scale):
    y = x * lax.rsqrt(jnp.mean(x * x, axis=-1, keepdims=True) + EPS) * g
    return y * (1.0 + scale) + shift


def _mod_kernel(c_ref, w_ref, b_ref, o_ref):
    a = c_ref[...]
    a = a * jax.nn.sigmoid(a)
    o_ref[...] = jnp.dot(a.astype(BF16), w_ref[...].astype(BF16),
                         preferred_element_type=F32) + b_ref[...]


def _modulation(c8, w_ada, b_ada):
    tn = 1024
    return pl.pallas_call(
        _mod_kernel,
        grid=(DEPTH, 6 * D // tn),
        in_specs=[pl.BlockSpec((8, D), lambda l, j: (0, 0)),
                  pl.BlockSpec((None, D, tn), lambda l, j: (l, 0, j)),
                  pl.BlockSpec((None, 1, tn), lambda l, j: (l, 0, j))],
        out_specs=pl.BlockSpec((None, 8, tn), lambda l, j: (l, 0, j)),
        out_shape=jax.ShapeDtypeStruct((DEPTH, 8, 6 * D), F32),
        compiler_params=_cp(("arbitrary", "arbitrary"), 40),
        name="modulation",
    )(c8, w_ada, b_ada.reshape(DEPTH, 1, 6 * D))


def _inproj_kernel(x_ref, mod_ref, g_ref, w_ref, sgu_ref, mla_ref, s5_ref, conv_ref):
    h = _modnorm(x_ref[...], g_ref[...], mod_ref[0:1, :], mod_ref[1:2, :]).astype(BF16)

    def mm(a, b):
        return jnp.dot(h, w_ref[:, a:b], preferred_element_type=F32)

    sgu_ref[...] = mm(0, 1024)
    mla_ref[...] = mm(1024, 1792)
    s5_ref[...] = mm(1792, 2304)
    conv_ref[:, :GW] = mm(2304, 2816)
    conv_ref[:, GW:] = mm(2816, 3328) * mm(3328, 3840)


def _inproj(x2d, mod, g1, w_in_r):
    return pl.pallas_call(
        _inproj_kernel,
        grid=(NT,),
        in_specs=[pl.BlockSpec((TM, D), lambda i: (i, 0)),
                  pl.BlockSpec((None, 6, D), lambda i: (_seg_of_tile(i), 0, 0)),
                  pl.BlockSpec((1, D), lambda i: (0, 0)),
                  pl.BlockSpec((D, IN_W), lambda i: (0, 0))],
        out_specs=[pl.BlockSpec((TM, 1024), lambda i: (i, 0)),
                   pl.BlockSpec((TM, 768), lambda i: (i, 0)),
                   pl.BlockSpec((TM, GW), lambda i: (i % TPB, i // TPB)),
                   pl.BlockSpec((TM, 1024), lambda i: (i, 0))],
        out_shape=[jax.ShapeDtypeStruct((B * S2, 1024), F32),
                   jax.ShapeDtypeStruct((B * S2, 768), F32),
                   jax.ShapeDtypeStruct((S2, B * GW), F32),
                   jax.ShapeDtypeStruct((B * S2, 1024), F32)],
        compiler_params=_cp(("arbitrary",), 56),
        name="inproj",
    )(x2d, mod, g1, w_in_r)


def _sgu_kernel(p_ref, g_ref, w_ref, bias_ref, o_ref):
    p = jax.nn.gelu(p_ref[...])
    u = p[:, :GW]
    v = p[:, GW:]
    v = v * lax.rsqrt(jnp.mean(v * v, axis=-1, keepdims=True) + EPS) * g_ref[...]
    vb = v.astype(BF16)
    for ck in range(TM // CHUNK):
        r = slice(ck * CHUNK, (ck + 1) * CHUNK)
        for hd in range(SGU_HEADS):
            c = slice(hd * 128, (hd + 1) * 128)
            m = jnp.dot(w_ref[hd], vb[r, c], preferred_element_type=F32)
            o_ref[r, c] = (u[r, c] * (m + bias_ref[:, c])).astype(BF16)


def _sgu(p_sgu, g, w_bf, bias_full, tile_fn, ntiles):
    return pl.pallas_call(
        _sgu_kernel,
        grid=(ntiles,),
        in_specs=[pl.BlockSpec((TM, 1024), lambda i: (tile_fn(i), 0)),
                  pl.BlockSpec((1, GW), lambda i: (0, 0)),
                  pl.BlockSpec((SGU_HEADS, CHUNK, CHUNK), lambda i: (0, 0, 0)),
                  pl.BlockSpec((CHUNK, GW), lambda i: (0, 0))],
        out_specs=pl.BlockSpec((TM, GW), lambda i: (tile_fn(i), 0)),
        out_shape=jax.ShapeDtypeStruct((B * S2, GW), BF16),
        compiler_params=_cp(("arbitrary",), 32),
        name="sgu",
    )(p_sgu, g, w_bf, bias_full)


def _conv_kernel(tile_fn, p_ref, zp_ref, zn_ref, w_ref, o_ref):
    r = tile_fn(pl.program_id(0)) % TPB
    bg = p_ref[:, :GW]
    z = p_ref[:, GW:]
    row = lax.broadcasted_iota(jnp.int32, (TM, GW), 0)
    has_prev = jnp.logical_and(r != 0, r != TPB - 1)
    has_next = r < LPB - 1
    prev_row = zp_ref[7:8, :] * has_prev.astype(F32)
    next_row = zn_ref[0:1, :] * has_next.astype(F32)
    zm = jnp.where(row == 0, prev_row, pltpu.roll(z, 1, 0))
    zp = jnp.where(row == TM - 1, next_row, pltpu.roll(z, TM - 1, 0))
    y = w_ref[0:1, :] * zm + w_ref[1:2, :] * z + w_ref[2:3, :] * zp
    o_ref[...] = (bg * y).astype(BF16)


def _conv(p_conv, conv_w, tile_fn, ntiles):
    rb = TM // 8
    nrb = B * S2 // 8
    return pl.pallas_call(
        functools.partial(_conv_kernel, tile_fn),
        grid=(ntiles,),
        in_specs=[pl.BlockSpec((TM, 1024), lambda i: (tile_fn(i), 0)),
                  pl.BlockSpec((8, GW), lambda i: (jnp.maximum(tile_fn(i) * rb - 1, 0), 1)),
                  pl.BlockSpec((8, GW), lambda i: (jnp.minimum((tile_fn(i) + 1) * rb, nrb - 1), 1)),
                  pl.BlockSpec((3, GW), lambda i: (0, 0))],
        out_specs=pl.BlockSpec((TM, GW), lambda i: (tile_fn(i), 0)),
        out_shape=jax.ShapeDtypeStruct((B * S2, GW), BF16),
        compiler_params=_cp(("arbitrary",), 32),
        name="conv",
    )(p_conv, p_conv, p_conv, conv_w)


def _kvq_kernel(p_ref, gq_ref, gkv_ref, wq_ref, wkv_ref, cos_ref, sin_ref,
                q_ref, kn_ref, kr_ref, v_ref):
    p = p_ref[...]
    cos = cos_ref[...]
    sin = sin_ref[...]
    cq = p[:, :Q_LORA]
    cq = cq * lax.rsqrt(jnp.mean(cq * cq, axis=-1, keepdims=True) + EPS) * gq_ref[...]
    q = jnp.dot(cq.astype(BF16), wq_ref[...], preferred_element_type=F32)
    nw = MLA_HEADS * QK_NOPE
    rw = MLA_HEADS * QK_ROPE
    qr = q[:, nw:nw + rw] * cos + q[:, nw + rw:] * sin
    q_ref[:, :nw] = (q[:, :nw] * ATT_SCALE).astype(BF16)
    q_ref[:, nw:] = (qr * ATT_SCALE).astype(BF16)
    ckv = p[:, Q_LORA:Q_LORA + KV_LORA]
    ckv = ckv * lax.rsqrt(jnp.mean(ckv * ckv, axis=-1, keepdims=True) + EPS) * gkv_ref[...]
    kv = jnp.dot(ckv.astype(BF16), wkv_ref[...], preferred_element_type=F32)
    kn_ref[...] = kv[:, :nw].astype(BF16)
    v_ref[...] = kv[:, nw:].astype(BF16)
    o = Q_LORA + KV_LORA
    kr = p[:, o:o + QK_ROPE] * cos[:, :QK_ROPE] + p[:, o + QK_ROPE:] * sin[:, :QK_ROPE]
    kr_ref[...] = kr.astype(BF16)


def _kvq(p_mla, gq, gkv, wq_r, wkv_r, cos_t, sin_t):
    return pl.pallas_call(
        _kvq_kernel,
        grid=(NT,),
        in_specs=[pl.BlockSpec((TM, 768), lambda i: (i, 0)),
                  pl.BlockSpec((1, Q_LORA), lambda i: (0, 0)),
                  pl.BlockSpec((1, KV_LORA), lambda i: (0, 0)),
                  pl.BlockSpec((Q_LORA, 1024), lambda i: (0, 0)),
                  pl.BlockSpec((KV_LORA, 1024), lambda i: (0, 0)),
                  pl.BlockSpec((TM, 256), lambda i: (i % TPB, 0)),
                  pl.BlockSpec((TM, 256), lambda i: (i % TPB, 0))],
        out_specs=[pl.BlockSpec((TM, QW), lambda i: (i, 0)),
                   pl.BlockSpec((TM, GW), lambda i: (i, 0)),
                   pl.BlockSpec((TM, QK_ROPE), lambda i: (i, 0)),
                   pl.BlockSpec((TM, GW), lambda i: (i, 0))],
        out_shape=[jax.ShapeDtypeStruct((B * S2, QW), BF16),
                   jax.ShapeDtypeStruct((B * S2, GW), BF16),
                   jax.ShapeDtypeStruct((B * S2, QK_ROPE), BF16),
                   jax.ShapeDtypeStruct((B * S2, GW), BF16)],
        compiler_params=_cp(("arbitrary",), 32),
        name="kvq",
    )(p_mla, gq, gkv, wq_r, wkv_r, cos_t, sin_t)


_NT_DIMS = (((1,), (1,)), ((), ()))


def _attn_kernel(q_ref, kn_ref, kr_ref, v_ref, o_ref):
    kr = kr_ref[...]
    nw = MLA_HEADS * QK_NOPE
    for hd in range(MLA_HEADS):
        c = slice(hd * 128, (hd + 1) * 128)
        qn = q_ref[:, c]
        qr = q_ref[:, nw + hd * QK_ROPE:nw + (hd + 1) * QK_ROPE]
        s = lax.dot_general(qn, kn_ref[:, c], _NT_DIMS, preferred_element_type=F32)
        s = s + lax.dot_general(qr, kr, _NT_DIMS, preferred_element_type=F32)
        m = jnp.max(s, axis=-1, keepdims=True)
        e = jnp.exp(s - m)
        l = jnp.sum(e, axis=-1, keepdims=True)
        o = jnp.dot(e.astype(BF16), v_ref[:, c], preferred_element_type=F32)
        o_ref[:, c] = (o / l).astype(BF16)


def _attention(q, kn3, kr3, v3, out_prev, ctx_only):
    if ctx_only:
        grid = (B,)
        qmap = lambda b: (b * TPB + TPB - 1, 0)
        nk = CTX
        kmap = lambda b: (b, TPB - 1, 0)
    else:
        grid = (B, LPB)
        qmap = lambda b, t: (b * TPB + t, 0)
        nk = S2
        kmap = lambda b, t: (b, 0, 0)
    in_specs = [pl.BlockSpec((TM, QW), qmap),
                pl.BlockSpec((None, nk, GW), kmap),
                pl.BlockSpec((None, nk, QK_ROPE), kmap),
                pl.BlockSpec((None, nk, GW), kmap)]
    args = [q, kn3, kr3, v3]
    kern = _attn_kernel
    aliases = {}
    if out_prev is not None:
        in_specs.append(pl.BlockSpec(memory_space=pl.ANY))
        args.append(out_prev)
        aliases = {4: 0}
        kern = lambda q_ref, kn_ref, kr_ref, v_ref, prev_ref, o_ref: _attn_kernel(
            q_ref, kn_ref, kr_ref, v_ref, o_ref)
    return pl.pallas_call(
        kern,
        grid=grid,
        in_specs=in_specs,
        out_specs=pl.BlockSpec((TM, GW), qmap),
        out_shape=jax.ShapeDtypeStruct((B * S2, GW), BF16),
        input_output_aliases=aliases,
        compiler_params=_cp(("arbitrary",) * len(grid), 48),
        name="attn_ctx" if ctx_only else "attn",
    )(*args)


def _s5_kernel(u_ref, w_ref, are_ref, aim_ref, c_ref, y_ref, st_re, st_im, buf):
    d = pl.program_id(0)
    k = pl.program_id(1)

    @pl.when(k == 0)
    def _():
        st_re[...] = jnp.zeros_like(st_re)
        st_im[...] = jnp.zeros_like(st_im)

    buf[...] = jnp.dot(u_ref[...], w_ref[...], preferred_element_type=F32)
    a_re = are_ref[...]
    a_im = aim_ref[...]

    def step(j, carry):
        sr, si = carry
        jj = jnp.where(d == 0, j, S5_T - 1 - j)
        r0 = pl.multiple_of(jj * 8, 8)
        br = buf[pl.ds(r0, 8), :S5_HALF]
        bi = buf[pl.ds(r0, 8), S5_HALF:]
        nr = a_re * sr - a_im * si + br
        ni = a_re * si + a_im * sr + bi
        buf[pl.ds(r0, 8), :S5_HALF] = nr
        buf[pl.ds(r0, 8), S5_HALF:] = ni
        return nr, ni

    sr, si = lax.fori_loop(0, S5_T, step, (st_re[...], st_im[...]))
    st_re[...] = sr
    st_im[...] = si
    y_ref[...] = jnp.dot(buf[...].astype(BF16), c_ref[...], preferred_element_type=F32)


def _s5_scan(u3, w_bd, a_re8, a_im8, c_bd):
    rows = 8 * S5_T

    def blk(d, k):
        return jnp.where(d == 0, (k + SEQ // S5_T) % S5_NCH, S5_NCH - 1 - k)

    return pl.pallas_call(
        _s5_kernel,
        grid=(2, S5_NCH),
        in_specs=[pl.BlockSpec((rows, GW), lambda d, k: (blk(d, k), 0)),
                  pl.BlockSpec((None, GW, 2 * S5_HALF), lambda d, k: (d, 0, 0)),
                  pl.BlockSpec((None, 8, S5_HALF), lambda d, k: (d, 0, 0)),
                  pl.BlockSpec((None, 8, S5_HALF), lambda d, k: (d, 0, 0)),
                  pl.BlockSpec((None, 2 * S5_HALF, GW), lambda d, k: (d, 0, 0))],
        out_specs=pl.BlockSpec((None, rows, GW), lambda d, k: (d, blk(d, k), 0)),
        out_shape=jax.ShapeDtypeStruct((2, S2 * 8, GW), F32),
        scratch_shapes=[pltpu.VMEM((8, S5_HALF), F32),
                        pltpu.VMEM((8, S5_HALF), F32),
                        pltpu.VMEM((rows, 2 * S5_HALF), F32)],
        compiler_params=_cp(("arbitrary", "arbitrary"), 48),
        name="s5_scan",
    )(u3, w_bd, a_re8, a_im8, c_bd)


def _glu_kernel(yf_ref, yb_ref, u_ref, d_ref, w_ref, b_ref, o_ref):
    h = GW // 2
    lo = yf_ref[:, :h] + yb_ref[:, :h]
    hi = yf_ref[:, GW + h:] + yb_ref[:, GW + h:]
    y = jnp.concatenate([lo, hi], axis=1) + d_ref[...] * u_ref[...]
    g = jax.nn.gelu(y)
    z = jnp.dot(g.astype(BF16), w_ref[...], preferred_element_type=F32) + b_ref[...]
    o_ref[...] = (g * jax.nn.sigmoid(z)).astype(BF16)


def _s5_glu(y8, u_tb, d_skip, w_glu_bf, b_glu):
    tr = 512
    y4 = y8.reshape(2, S2 * B, 2 * GW)
    return pl.pallas_call(
        _glu_kernel,
        grid=(S2 * B // tr,),
        in_specs=[pl.BlockSpec((None, tr, 2 * GW), lambda i: (0, i, 0)),
                  pl.BlockSpec((None, tr, 2 * GW), lambda i: (1, i, 0)),
                  pl.BlockSpec((tr, GW), lambda i: (i, 0)),
                  pl.BlockSpec((1, GW), lambda i: (0, 0)),
                  pl.BlockSpec((GW, GW), lambda i: (0, 0)),
                  pl.BlockSpec((1, GW), lambda i: (0, 0))],
        out_specs=pl.BlockSpec((tr, GW), lambda i: (i, 0)),
        out_shape=jax.ShapeDtypeStruct((S2 * B, GW), BF16),
        compiler_params=_cp(("arbitrary",), 32),
        name="s5_glu",
    )(y4, y4, u_tb, d_skip, w_glu_bf, b_glu)


def _outproj_kernel(a0_ref, a1_ref, a2_ref, a3_ref, w_ref, x_ref, mod_ref, g2_ref, wr_ref,
                    x1_ref, h2_ref, aff_ref):
    o = jnp.dot(a0_ref[...], w_ref[0:GW, :], preferred_element_type=F32)
    o = o + jnp.dot(a1_ref[...], w_ref[GW:2 * GW, :], preferred_element_type=F32)
    o = o + jnp.dot(a2_ref[...], w_ref[2 * GW:3 * GW, :], preferred_element_type=F32)
    o = o + jnp.dot(a3_ref[...], w_ref[3 * GW:, :], preferred_element_type=F32)
    x1 = x_ref[...] + mod_ref[2:3, :] * o
    x1_ref[...] = x1
    h2 = _modnorm(x1, g2_ref[...], mod_ref[3:4, :], mod_ref[4:5, :])
    h2_ref[...] = h2.astype(BF16)
    lg = lax.dot_general(wr_ref[...], h2, _NT_DIMS, precision=lax.Precision.HIGHEST,
                         preferred_element_type=F32)
    e = jnp.exp(lg - jnp.max(lg, axis=0, keepdims=True))
    aff_ref[...] = e / jnp.sum(e, axis=0, keepdims=True)


def _outproj(sgu_o, attn_o, ssm_tb, conv_o, w_out_bf, x2d, mod, g2, wr_t, tile_fn, ntiles):
    row = lambda i: (tile_fn(i), 0)
    return pl.pallas_call(
        _outproj_kernel,
        grid=(ntiles,),
        in_specs=[pl.BlockSpec((TM, GW), row),
                  pl.BlockSpec((TM, GW), row),
                  pl.BlockSpec((TM, GW), lambda i: (tile_fn(i) % TPB, tile_fn(i) // TPB)),
                  pl.BlockSpec((TM, GW), row),
                  pl.BlockSpec((D, D), lambda i: (0, 0)),
                  pl.BlockSpec((TM, D), row),
                  pl.BlockSpec((None, 6, D), lambda i: (_seg_of_tile(tile_fn(i)), 0, 0)),
                  pl.BlockSpec((1, D), lambda i: (0, 0)),
                  pl.BlockSpec((N_EXP, D), lambda i: (0, 0))],
        out_specs=[pl.BlockSpec((TM, D), row),
                   pl.BlockSpec((TM, D), row),
                   pl.BlockSpec((N_EXP, TM), lambda i: (0, tile_fn(i)))],
        out_shape=[jax.ShapeDtypeStruct((B * S2, D), F32),
                   jax.ShapeDtypeStruct((B * S2, D), BF16),
                   jax.ShapeDtypeStruct((N_EXP, B * S2), F32)],
        compiler_params=_cp(("arbitrary",), 48),
        name="outproj",
    )(sgu_o, attn_o, ssm_tb, conv_o, w_out_bf, x2d, mod, g2, wr_t)


def _select_one(a, tri, cap, p_ref, w_ref):
    ne, n = a.shape
    bits = pltpu.bitcast(a, jnp.int32)
    thr = jnp.zeros((ne, 1), jnp.int32)
    for bit in range(30, -1, -1):
        cand = thr | (1 << bit)
        cnt = jnp.sum(jnp.where(bits >= cand, 1.0, 0.0), axis=1, keepdims=True)
        thr = jnp.where(cnt >= cap, cand, thr)
    gt = jnp.where(bits > thr, 1.0, 0.0)
    eq = jnp.where(bits == thr, 1.0, 0.0)
    need = cap - jnp.sum(gt, axis=1, keepdims=True)
    eq_before = jnp.dot(eq.astype(BF16), tri, preferred_element_type=F32) - eq
    sel = gt + eq * jnp.where(eq_before < need, 1.0, 0.0)
    rank = jnp.dot(sel.astype(BF16), tri, preferred_element_type=F32) - 1.0
    rank = jnp.where(sel > 0.5, rank, -1.0)
    slot = lax.broadcasted_iota(jnp.int32, (cap, n), 0).astype(F32)
    for e in range(ne):
        hit = rank[e:e + 1, :] == slot
        p_ref[e * cap:(e + 1) * cap, :] = jnp.where(hit, 1.0, 0.0).astype(BF16)
        w_ref[e * cap:(e + 1) * cap, :] = jnp.sum(jnp.where(hit, a[e:e + 1, :], 0.0),
                                                  axis=1, keepdims=True)


def _select_kernel(with_ctx, a_ref, tri_ref, *out_refs):
    _select_one(a_ref[:, :SEQ], tri_ref[...], CAP, out_refs[0], out_refs[1])
    if with_ctx:
        _select_one(a_ref[:, SEQ:], tri_ref[:CTX, :CTX], CAP_C, out_refs[2], out_refs[3])


def _select(aff_t, tri, with_ctx):
    eg = 8
    out_specs = [pl.BlockSpec((None, eg * CAP, SEQ), lambda b, j: (b, j, 0)),
                 pl.BlockSpec((None, eg * CAP, 1), lambda b, j: (b, j, 0))]
    out_shape = [jax.ShapeDtypeStruct((B, N_EXP * CAP, SEQ), BF16),
                 jax.ShapeDtypeStruct((B, N_EXP * CAP, 1), F32)]
    if with_ctx:
        out_specs += [pl.BlockSpec((None, eg * CAP_C, CTX), lambda b, j: (b, j, 0)),
                      pl.BlockSpec((None, eg * CAP_C, 1), lambda b, j: (b, j, 0))]
        out_shape += [jax.ShapeDtypeStruct((B, N_EXP * CAP_C, CTX), BF16),
                      jax.ShapeDtypeStruct((B, N_EXP * CAP_C, 1), F32)]
    return pl.pallas_call(
        functools.partial(_select_kernel, with_ctx),
        grid=(B, N_EXP // eg),
        in_specs=[pl.BlockSpec((eg, S2), lambda b, j: (j, b)),
                  pl.BlockSpec((SEQ, SEQ), lambda b, j: (0, 0))],
        out_specs=out_specs,
        out_shape=out_shape,
        compiler_params=_cp(("arbitrary", "arbitrary"), 56),
        name="select",
    )(aff_t, tri)


def _gather_kernel(p_ref, h_ref, o_ref):
    o_ref[...] = jnp.dot(p_ref[...], h_ref[...], preferred_element_type=F32).astype(BF16)


def _gather(p_hot, h3, ctx_only):
    if ctx_only:
        rows, n, tr = N_EXP * CAP_C, CTX, N_EXP * CAP_C
        hmap = lambda b, r: (b, TPB - 1, 0)
    else:
        rows, n, tr = N_EXP * CAP, SEQ, 512
        hmap = lambda b, r: (b, 0, 0)
    return pl.pallas_call(
        _gather_kernel,
        grid=(B, rows // tr),
        in_specs=[pl.BlockSpec((None, tr, n), lambda b, r: (b, r, 0)),
                  pl.BlockSpec((None, n, D), hmap)],
        out_specs=pl.BlockSpec((None, tr, D), lambda b, r: (b, r, 0)),
        out_shape=jax.ShapeDtypeStruct((B, rows, D), BF16),
        compiler_params=_cp(("arbitrary", "arbitrary"), 48),
        name="gather_ctx" if ctx_only else "gather",
    )(p_hot, h3)


def _ffn_kernel(with_ctx, *refs):
    if with_ctx:
        (x_ref, xc_ref, wg_ref, wu_ref, wd_ref, ws_ref, wsc_ref,
         y_ref, yc_ref, acc, accc) = refs
    else:
        x_ref, wg_ref, wu_ref, wd_ref, ws_ref, y_ref, acc = refs
    f = pl.program_id(1)
    last = pl.num_programs(1) - 1
    wg = wg_ref[...].astype(BF16)
    wu = wu_ref[...].astype(BF16)
    wd = wd_ref[...].astype(BF16)

    def run(xr, wsr, yr, ac, rows):
        x = xr[...].reshape(rows, D)
        gate = jnp.dot(x, wg, preferred_element_type=F32)
        up = jnp.dot(x, wu, preferred_element_type=F32)
        hid = (gate * jax.nn.sigmoid(gate) * up).astype(BF16)
        part = jnp.dot(hid, wd, preferred_element_type=F32)

        @pl.when(f == 0)
        def _():
            ac[...] = part

        @pl.when(f > 0)
        def _():
            ac[...] += part

        @pl.when(f == last)
        def _():
            y = ac[...] * wsr[...].reshape(rows, 1)
            yr[...] = y.astype(BF16).reshape(yr.shape)

    run(x_ref, ws_ref, y_ref, acc, B * CAP)
    if with_ctx:
        run(xc_ref, wsc_ref, yc_ref, accc, B * CAP_C)


def _ffn(xs, ws, xc, wsc, w_gate, w_up, w_down):
    with_ctx = xc is not None
    in_specs = [pl.BlockSpec((B, CAP, D), lambda e, f: (0, e, 0))]
    args = [xs]
    if with_ctx:
        in_specs.append(pl.BlockSpec((B, CAP_C, D), lambda e, f: (0, e, 0)))
        args.append(xc)
    in_specs += [pl.BlockSpec((None, D, FF_T), lambda e, f: (e, 0, f)),
                 pl.BlockSpec((None, D, FF_T), lambda e, f: (e, 0, f)),
                 pl.BlockSpec((None, FF_T, D), lambda e, f: (e, f, 0)),
                 pl.BlockSpec((B, CAP, 1), lambda e, f: (0, e, 0))]
    args += [w_gate, w_up, w_down, ws]
    out_specs = [pl.BlockSpec((B, CAP, D), lambda e, f: (0, e, 0))]
    out_shape = [jax.ShapeDtypeStruct((B, N_EXP * CAP, D), BF16)]
    scratch = [pltpu.VMEM((B * CAP, D), F32)]
    if with_ctx:
        in_specs.append(pl.BlockSpec((B, CAP_C, 1), lambda e, f: (0, e, 0)))
        args.append(wsc)
        out_specs.append(pl.BlockSpec((B, CAP_C, D), lambda e, f: (0, e, 0)))
        out_shape.append(jax.ShapeDtypeStruct((B, N_EXP * CAP_C, D), BF16))
        scratch.append(pltpu.VMEM((B * CAP_C, D), F32))
    return pl.pallas_call(
        functools.partial(_ffn_kernel, with_ctx),
        grid=(N_EXP, FF // FF_T),
        in_specs=in_specs,
        out_specs=out_specs,
        out_shape=out_shape,
        scratch_shapes=scratch,
        compiler_params=_cp(("arbitrary", "arbitrary"), 56),
        name="ffn",
    )(*args)


_TN_DIMS = (((0,), (0,)), ((), ()))


def _scatter_kernel(p_ref, y_ref, x_ref, mod_ref, o_ref):
    upd = lax.dot_general(p_ref[...], y_ref[...], _TN_DIMS, preferred_element_type=F32)
    o_ref[...] = x_ref[...] + mod_ref[5:6, :] * upd


def _scatter(p_hot, y, x3, mod, ctx_only):
    dn = D // 2
    if ctx_only:
        rows, tok = N_EXP * CAP_C, CTX
        grid = (B, 2, 1)
        xmap = lambda b, h, t: (b, TPB - 1, h)
        mmap = lambda b, h, t: (B, 0, h)
    else:
        rows, tok = N_EXP * CAP, TM
        grid = (B, 2, LPB)
        xmap = lambda b, h, t: (b, t, h)
        mmap = lambda b, h, t: (b, 0, h)
    return pl.pallas_call(
        _scatter_kernel,
        grid=grid,
        in_specs=[pl.BlockSpec((None, rows, tok), lambda b, h, t: (b, 0, t)),
                  pl.BlockSpec((None, rows, dn), lambda b, h, t: (b, 0, h)),
                  pl.BlockSpec((None, TM, dn), xmap),
                  pl.BlockSpec((None, 6, dn), mmap)],
        out_specs=pl.BlockSpec((None, TM, dn), xmap),
        out_shape=jax.ShapeDtypeStruct((B, S2, D), F32),
        input_output_aliases={2: 0},
        compiler_params=_cp(("arbitrary",) * 3, 48),
        name="scatter_ctx" if ctx_only else "scatter",
    )(p_hot, y, x3, mod)


def _final_kernel(x_ref, g_ref, o_ref):
    x = x_ref[...]
    o_ref[...] = x * lax.rsqrt(jnp.mean(x * x, axis=-1, keepdims=True) + EPS) * g_ref[...]


def _final_norm(x2d, g):
    return pl.pallas_call(
        _final_kernel,
        grid=(NLT,),
        in_specs=[pl.BlockSpec((TM, D), lambda i: (_lat_tile(i), 0)),
                  pl.BlockSpec((1, D), lambda i: (0, 0))],
        out_specs=pl.BlockSpec((TM, D), lambda i: (i, 0)),
        out_shape=jax.ShapeDtypeStruct((B * SEQ, D), F32),
        compiler_params=_cp(("arbitrary",), 32),
        name="final_norm",
    )(x2d, g)


def _rope_tables():
    n_freq = QK_ROPE // 4
    grid_w = 64
    pos = jnp.arange(SEQ, dtype=F32)
    inv_freq = 10000.0 ** (-jnp.arange(n_freq, dtype=F32) / n_freq)
    ang_r = jnp.floor(pos / grid_w)[:, None] * inv_freq
    ang_c = (pos - grid_w * jnp.floor(pos / grid_w))[:, None] * inv_freq
    cr, sr, cc, sc = jnp.cos(ang_r), jnp.sin(ang_r), jnp.cos(ang_c), jnp.sin(ang_c)
    cos = jnp.concatenate([cr, cr, cc, cc], axis=1)
    sin = jnp.concatenate([-sr, sr, -sc, sc], axis=1)
    cos = jnp.concatenate([cos, jnp.ones((CTX, QK_ROPE), F32)], axis=0)
    sin = jnp.concatenate([sin, jnp.zeros((CTX, QK_ROPE), F32)], axis=0)
    return jnp.tile(cos, (1, MLA_HEADS)), jnp.tile(sin, (1, MLA_HEADS))


def _pair_swap(w):
    return jnp.concatenate([w[..., 16:32], w[..., 0:16], w[..., 48:64], w[..., 32:48]], axis=-1)


def _s5_operators(a_re, a_im, log_dt, b_re, b_im, c_re, c_im):
    a = lax.complex(jnp.minimum(a_re.astype(F32), -1e-4), a_im.astype(F32))
    dt = jnp.exp(log_dt.astype(F32))[..., None]
    abar = jnp.exp(a * dt)
    bbar = ((abar - 1.0) / a)[..., None] * lax.complex(b_re.astype(F32), b_im.astype(F32))
    hg = S5_G // 2
    eye = jnp.eye(hg, dtype=F32)

    def in_op(m):
        m = jnp.transpose(m, (0, 1, 3, 2)).reshape(2, 2, hg, S5_CH, S5_N)
        return jnp.einsum("zhgcn,gk->zhgckn", m, eye).reshape(2, GW, S5_HALF)

    def out_op(m):
        m = m.reshape(2, 2, hg, S5_CH, S5_N)
        return jnp.einsum("zhgcn,gk->zgnhkc", m, eye).reshape(2, S5_HALF, GW)

    w_bd = jnp.concatenate([in_op(jnp.real(bbar)), in_op(jnp.imag(bbar))], axis=-1).astype(BF16)
    c_bd = jnp.concatenate([out_op(c_re.astype(F32)), out_op(-c_im.astype(F32))], axis=1).astype(BF16)

    def rows8(m):
        m = m.reshape(2, 1, 2, S5_HALF)
        return jnp.broadcast_to(m, (2, B, 2, S5_HALF)).reshape(2, 2 * B, S5_HALF)

    return w_bd, rows8(jnp.real(abar)), rows8(jnp.imag(abar)), c_bd


def _s5_inputs(u_tb):
    u = u_tb.reshape(S2, B, 2, GW // 2).astype(BF16)
    z = jnp.zeros_like(u[:, :, 0])
    lo = jnp.concatenate([u[:, :, 0], z], axis=-1)
    hi = jnp.concatenate([z, u[:, :, 1]], axis=-1)
    return jnp.stack([lo, hi], axis=2).reshape(S2 * B * 2, GW)


def kernel(x, c, ctx, c_ctx, norm1_g, norm2_g, w_ada, b_ada, w_in, w_out, sgu_norm_g, sgu_w,
           sgu_b, mla_q_norm_g, mla_w_uq, mla_kv_norm_g, mla_w_ukv, s5_a_re, s5_a_im, s5_log_dt,
           s5_b_re, s5_b_im, s5_c_re, s5_c_im, s5_d, s5_w_glu, s5_b_glu, conv_w, moe_w_router,
           moe_w_gate, moe_w_up, moe_w_down, final_norm_g):
    c8 = jnp.concatenate([c, c_ctx[None, :], jnp.zeros((3, D), F32)], axis=0)
    mod_all = _modulation(c8, w_ada, b_ada).reshape(DEPTH, 8, 6, D)
    cos_t, sin_t = _rope_tables()
    tri = jnp.triu(jnp.ones((SEQ, SEQ), BF16))
    x3 = jnp.concatenate([x, ctx], axis=1)

    for i in range(DEPTH):
        last = i == DEPTH - 1
        mod = mod_all[i]
        tile_fn, ntiles = (_lat_tile, NLT) if last else (_all_tile, NT)

        wi = w_in[i]
        kr_cols = wi[:, 1024 + Q_LORA + KV_LORA:1728]
        w_in_r = jnp.concatenate([wi[:, :1728], _pair_swap(kr_cols), wi[:, 1728:]], axis=1).astype(BF16)
        x2d = x3.reshape(B * S2, D)
        p_sgu, p_mla, u_tb, p_conv = _inproj(x2d, mod, norm1_g[i][None, :], w_in_r)

        bias_full = jnp.repeat(jnp.swapaxes(sgu_b[i], 0, 1), 128, axis=1)
        sgu_o = _sgu(p_sgu, sgu_norm_g[i][None, :], sgu_w[i].astype(BF16), bias_full, tile_fn, ntiles)

        conv_o = _conv(p_conv, conv_w[i], tile_fn, ntiles)

        wq = mla_w_uq[i].reshape(Q_LORA, MLA_HEADS, QK_NOPE + QK_ROPE)
        wq_n = wq[:, :, :QK_NOPE].reshape(Q_LORA, -1)
        wq_r = wq[:, :, QK_NOPE:]
        wq_ext = jnp.concatenate([wq_n, wq_r.reshape(Q_LORA, -1),
                                  _pair_swap(wq_r).reshape(Q_LORA, -1)], axis=1).astype(BF16)
        wkv = mla_w_ukv[i].reshape(KV_LORA, MLA_HEADS, 2 * QK_NOPE)
        wkv_ext = jnp.concatenate([wkv[:, :, :QK_NOPE].reshape(KV_LORA, -1),
                                   wkv[:, :, QK_NOPE:].reshape(KV_LORA, -1)], axis=1).astype(BF16)
        q, kn, kr, v = _kvq(p_mla, mla_q_norm_g[i][None, :], mla_kv_norm_g[i][None, :],
                            wq_ext, wkv_ext, cos_t, sin_t)
        kn3 = kn.reshape(B, S2, GW)
        kr3 = kr.reshape(B, S2, QK_ROPE)
        v3 = v.reshape(B, S2, GW)
        attn_o = _attention(q, kn3, kr3, v3, None, ctx_only=False)
        if not last:
            attn_o = _attention(q, kn3, kr3, v3, attn_o, ctx_only=True)

        w_bd, a_re8, a_im8, c_bd = _s5_operators(s5_a_re[i], s5_a_im[i], s5_log_dt[i], s5_b_re[i],
                                                 s5_b_im[i], s5_c_re[i], s5_c_im[i])
        y8 = _s5_scan(_s5_inputs(u_tb), w_bd, a_re8, a_im8, c_bd)
        ssm_tb = _s5_glu(y8, u_tb.reshape(S2 * B, GW), s5_d[i][None, :],
                         s5_w_glu[i].astype(BF16), s5_b_glu[i][None, :]).reshape(S2, B * GW)

        x1, h2, aff_t = _outproj(sgu_o, attn_o, ssm_tb, conv_o, w_out[i].astype(BF16), x2d, mod,
                                 norm2_g[i][None, :], jnp.transpose(moe_w_router[i]), tile_fn, ntiles)

        sel = _select(aff_t, tri, not last)
        h3 = h2.reshape(B, S2, D)
        xs = _gather(sel[0], h3, ctx_only=False)
        xc = _gather(sel[2], h3, ctx_only=True) if not last else None
        ys = _ffn(xs, sel[1], xc, sel[3] if not last else None,
                  moe_w_gate[i], moe_w_up[i], moe_w_down[i])
        x3 = _scatter(sel[0], ys[0], x1.reshape(B, S2, D), mod, ctx_only=False)
        if not last:
            x3 = _scatter(sel[2], ys[1], x3, mod, ctx_only=True)

    return _final_norm(x3.reshape(B * S2, D), final_norm_g[None, :]).reshape(B, SEQ, D)
```

```python
import functools
import math

import jax
import jax.numpy as jnp
from jax import lax
from jax.experimental import pallas as pl
from jax.experimental.pallas import tpu as pltpu

F32 = jnp.float32
BF16 = jnp.bfloat16

D = 2048
B = 4
SEQ = 2048
CTX = 256
S2 = SEQ + CTX
DEPTH = 2
GW = 512
EPS = 1e-6

TM = 256
TPB = S2 // TM
LPB = SEQ // TM
NT = B * TPB
NLT = B * LPB

SGU_HEADS = 4
CHUNK = 128
MLA_HEADS = 4
QK_NOPE = 128
QK_ROPE = 64
Q_LORA = 384
KV_LORA = 256
QW = MLA_HEADS * (QK_NOPE + QK_ROPE)
ATT_SCALE = (QK_NOPE + QK_ROPE) ** -0.5

S5_G = 32
S5_N = 64
S5_CH = 16
S5_T = 128
S5_NCH = S2 // S5_T
S5_HALF = (S5_G // 2) * S5_N
S5_LT = S5_HALF // 128

N_EXP = 16
FF = D // 2
CAP = 2 * SEQ // N_EXP
CAP_C = 2 * CTX // N_EXP
FF_T = 256

IN_W = 3840
MIB = 1024 * 1024


def _cp(sem, vmem_mb):
    return pltpu.CompilerParams(dimension_semantics=sem, vmem_limit_bytes=vmem_mb * MIB)


def _all_tile(i):
    return i


def _lat_tile(i):
    return (i // LPB) * TPB + i % LPB


def _seg_of_tile(t):
    return jnp.where(t % TPB == TPB - 1, B, t // TPB)


def _modnorm(x, g, shift, scale):
    y = x * lax.rsqrt(jnp.mean(x * x, axis=-1, keepdims=True) + EPS) * g
    return y * (1.0 + scale) + shift


def _mod_kernel(c_ref, w_ref, b_ref, o_ref):
    a = c_ref[...]
    a = a * jax.nn.sigmoid(a)
    o_ref[...] = jnp.dot(a.astype(BF16), w_ref[...].astype(BF16),
                         preferred_element_type=F32) + b_ref[...]


def _modulation(c8, w_ada, b_ada):
    tn = 1024
    return pl.pallas_call(
        _mod_kernel,
        grid=(DEPTH, 6 * D // tn),
        in_specs=[pl.BlockSpec((8, D), lambda l, j: (0, 0)),
                  pl.BlockSpec((None, D, tn), lambda l, j: (l, 0, j)),
                  pl.BlockSpec((None, 1, tn), lambda l, j: (l, 0, j))],
        out_specs=pl.BlockSpec((None, 8, tn), lambda l, j: (l, 0, j)),
        out_shape=jax.ShapeDtypeStruct((DEPTH, 8, 6 * D), F32),
        compiler_params=_cp(("arbitrary", "arbitrary"), 40),
        name="modulation",
    )(c8, w_ada, b_ada.reshape(DEPTH, 1, 6 * D))


def _inproj_kernel(x_ref, mod_ref, g_ref, w_ref, sgu_ref, mla_ref, s5_ref, conv_ref):
    h = _modnorm(x_ref[...], g_ref[...], mod_ref[0:1, :], mod_ref[1:2, :]).astype(BF16)

    def mm(a, b):
        return jnp.dot(h, w_ref[:, a:b], preferred_element_type=F32)

    sgu_ref[...] = mm(0, 1024)
    mla_ref[...] = mm(1024, 1792)
    s5_ref[...] = mm(1792, 2304)
    conv_ref[:, :GW] = mm(2304, 2816)
    conv_ref[:, GW:] = mm(2816, 3328) * mm(3328, 3840)


def _inproj(x2d, mod, g1, w_in_r):
    return pl.pallas_call(
        _inproj_kernel,
        grid=(NT,),
        in_specs=[pl.BlockSpec((TM, D), lambda i: (i, 0)),
                  pl.BlockSpec((None, 6, D), lambda i: (_seg_of_tile(i), 0, 0)),
                  pl.BlockSpec((1, D), lambda i: (0, 0)),
                  pl.BlockSpec((D, IN_W), lambda i: (0, 0))],
        out_specs=[pl.BlockSpec((TM, 1024), lambda i: (i, 0)),
                   pl.BlockSpec((TM, 768), lambda i: (i, 0)),
                   pl.BlockSpec((TM, GW), lambda i: (i, 0)),
                   pl.BlockSpec((TM, 1024), lambda i: (i, 0))],
        out_shape=[jax.ShapeDtypeStruct((B * S2, 1024), F32),
                   jax.ShapeDtypeStruct((B * S2, 768), F32),
                   jax.ShapeDtypeStruct((B * S2, GW), F32),
                   jax.ShapeDtypeStruct((B * S2, 1024), F32)],
        compiler_params=_cp(("arbitrary",), 56),
        name="inproj",
    )(x2d, mod, g1, w_in_r)


def _sgu_kernel(p_ref, g_ref, w_ref, bias_ref, o_ref):
    p = jax.nn.gelu(p_ref[...])
    u = p[:, :GW]
    v = p[:, GW:]
    v = v * lax.rsqrt(jnp.mean(v * v, axis=-1, keepdims=True) + EPS) * g_ref[...]
    vb = v.astype(BF16)
    for ck in range(TM // CHUNK):
        r = slice(ck * CHUNK, (ck + 1) * CHUNK)
        for hd in range(SGU_HEADS):
            c = slice(hd * 128, (hd + 1) * 128)
            m = jnp.dot(w_ref[hd], vb[r, c], preferred_element_type=F32)
            o_ref[r, c] = (u[r, c] * (m + bias_ref[:, c])).astype(BF16)


def _sgu(p_sgu, g, w_bf, bias_full, tile_fn, ntiles):
    return pl.pallas_call(
        _sgu_kernel,
        grid=(ntiles,),
        in_specs=[pl.BlockSpec((TM, 1024), lambda i: (tile_fn(i), 0)),
                  pl.BlockSpec((1, GW), lambda i: (0, 0)),
                  pl.BlockSpec((SGU_HEADS, CHUNK, CHUNK), lambda i: (0, 0, 0)),
                  pl.BlockSpec((CHUNK, GW), lambda i: (0, 0))],
        out_specs=pl.BlockSpec((TM, GW), lambda i: (tile_fn(i), 0)),
        out_shape=jax.ShapeDtypeStruct((B * S2, GW), BF16),
        compiler_params=_cp(("arbitrary",), 32),
        name="sgu",
    )(p_sgu, g, w_bf, bias_full)


def _conv_kernel(tile_fn, p_ref, zp_ref, zn_ref, w_ref, o_ref):
    r = tile_fn(pl.program_id(0)) % TPB
    bg = p_ref[:, :GW]
    z = p_ref[:, GW:]
    row = lax.broadcasted_iota(jnp.int32, (TM, GW), 0)
    has_prev = jnp.logical_and(r != 0, r != TPB - 1)
    has_next = r < LPB - 1
    prev_row = zp_ref[7:8, :] * has_prev.astype(F32)
    next_row = zn_ref[0:1, :] * has_next.astype(F32)
    zm = jnp.where(row == 0, prev_row, pltpu.roll(z, 1, 0))
    zp = jnp.where(row == TM - 1, next_row, pltpu.roll(z, TM - 1, 0))
    y = w_ref[0:1, :] * zm + w_ref[1:2, :] * z + w_ref[2:3, :] * zp
    o_ref[...] = (bg * y).astype(BF16)


def _conv(p_conv, conv_w, tile_fn, ntiles):
    rb = TM // 8
    nrb = B * S2 // 8
    return pl.pallas_call(
        functools.partial(_conv_kernel, tile_fn),
        grid=(ntiles,),
        in_specs=[pl.BlockSpec((TM, 1024), lambda i: (tile_fn(i), 0)),
                  pl.BlockSpec((8, GW), lambda i: (jnp.maximum(tile_fn(i) * rb - 1, 0), 1)),
                  pl.BlockSpec((8, GW), lambda i: (jnp.minimum((tile_fn(i) + 1) * rb, nrb - 1), 1)),
                  pl.BlockSpec((3, GW), lambda i: (0, 0))],
        out_specs=pl.BlockSpec((TM, GW), lambda i: (tile_fn(i), 0)),
        out_shape=jax.ShapeDtypeStruct((B * S2, GW), BF16),
        compiler_params=_cp(("arbitrary",), 32),
        name="conv",
    )(p_conv, p_conv, p_conv, conv_w)


def _kvq_kernel(p_ref, gq_ref, gkv_ref, wq_ref, wkv_ref, cos_ref, sin_ref,
                q_ref, kn_ref, kr_ref, v_ref):
    p = p_ref[...]
    cos = cos_ref[...]
    sin = sin_ref[...]
    cq = p[:, :Q_LORA]
    cq = cq * lax.rsqrt(jnp.mean(cq * cq, axis=-1, keepdims=True) + EPS) * gq_ref[...]
    q = jnp.dot(cq.astype(BF16), wq_ref[...], preferred_element_type=F32)
    nw = MLA_HEADS * QK_NOPE
    rw = MLA_HEADS * QK_ROPE
    qr = q[:, nw:nw + rw] * cos + q[:, nw + rw:] * sin
    q_ref[:, :nw] = (q[:, :nw] * ATT_SCALE).astype(BF16)
    q_ref[:, nw:] = (qr * ATT_SCALE).astype(BF16)
    ckv = p[:, Q_LORA:Q_LORA + KV_LORA]
    ckv = ckv * lax.rsqrt(jnp.mean(ckv * ckv, axis=-1, keepdims=True) + EPS) * gkv_ref[...]
    kv = jnp.dot(ckv.astype(BF16), wkv_ref[...], preferred_element_type=F32)
    kn_ref[...] = kv[:, :nw].astype(BF16)
    v_ref[...] = kv[:, nw:].astype(BF16)
    o = Q_LORA + KV_LORA
    kr = p[:, o:o + QK_ROPE] * cos[:, :QK_ROPE] + p[:, o + QK_ROPE:] * sin[:, :QK_ROPE]
    kr_ref[...] = kr.astype(BF16)


def _kvq(p_mla, gq, gkv, wq_r, wkv_r, cos_t, sin_t):
    return pl.pallas_call(
        _kvq_kernel,
        grid=(NT,),
        in_specs=[pl.BlockSpec((TM, 768), lambda i: (i, 0)),
                  pl.BlockSpec((1, Q_LORA), lambda i: (0, 0)),
                  pl.BlockSpec((1, KV_LORA), lambda i: (0, 0)),
                  pl.BlockSpec((Q_LORA, 1024), lambda i: (0, 0)),
                  pl.BlockSpec((KV_LORA, 1024), lambda i: (0, 0)),
                  pl.BlockSpec((TM, 256), lambda i: (i % TPB, 0)),
                  pl.BlockSpec((TM, 256), lambda i: (i % TPB, 0))],
        out_specs=[pl.BlockSpec((TM, QW), lambda i: (i, 0)),
                   pl.BlockSpec((TM, GW), lambda i: (i, 0)),
                   pl.BlockSpec((TM, QK_ROPE), lambda i: (i, 0)),
                   pl.BlockSpec((TM, GW), lambda i: (i, 0))],
        out_shape=[jax.ShapeDtypeStruct((B * S2, QW), BF16),
                   jax.ShapeDtypeStruct((B * S2, GW), BF16),
                   jax.ShapeDtypeStruct((B * S2, QK_ROPE), BF16),
                   jax.ShapeDtypeStruct((B * S2, GW), BF16)],
        compiler_params=_cp(("arbitrary",), 32),
        name="kvq",
    )(p_mla, gq, gkv, wq_r, wkv_r, cos_t, sin_t)


_NT_DIMS = (((1,), (1,)), ((), ()))


def _attn_kernel(q_ref, kn_ref, kr_ref, v_ref, o_ref):
    kr = kr_ref[...]
    nw = MLA_HEADS * QK_NOPE
    for hd in range(MLA_HEADS):
        c = slice(hd * 128, (hd + 1) * 128)
        qn = q_ref[:, c]
        qr = q_ref[:, nw + hd * QK_ROPE:nw + (hd + 1) * QK_ROPE]
        s = lax.dot_general(qn, kn_ref[:, c], _NT_DIMS, preferred_element_type=F32)
        s = s + lax.dot_general(qr, kr, _NT_DIMS, preferred_element_type=F32)
        m = jnp.max(s, axis=-1, keepdims=True)
        e = jnp.exp(s - m)
        l = jnp.sum(e, axis=-1, keepdims=True)
        o = jnp.dot(e.astype(BF16), v_ref[:, c], preferred_element_type=F32)
        o_ref[:, c] = (o / l).astype(BF16)


def _attention(q, kn3, kr3, v3, out_prev, ctx_only):
    if ctx_only:
        grid = (B,)
        qmap = lambda b: (b * TPB + TPB - 1, 0)
        nk = CTX
        kmap = lambda b: (b, TPB - 1, 0)
    else:
        grid = (B, LPB)
        qmap = lambda b, t: (b * TPB + t, 0)
        nk = S2
        kmap = lambda b, t: (b, 0, 0)
    in_specs = [pl.BlockSpec((TM, QW), qmap),
                pl.BlockSpec((None, nk, GW), kmap),
                pl.BlockSpec((None, nk, QK_ROPE), kmap),
                pl.BlockSpec((None, nk, GW), kmap)]
    args = [q, kn3, kr3, v3]
    kern = _attn_kernel
    aliases = {}
    if out_prev is not None:
        in_specs.append(pl.BlockSpec(memory_space=pl.ANY))
        args.append(out_prev)
        aliases = {4: 0}
        kern = lambda q_ref, kn_ref, kr_ref, v_ref, prev_ref, o_ref: _attn_kernel(
            q_ref, kn_ref, kr_ref, v_ref, o_ref)
    return pl.pallas_call(
        kern,
        grid=grid,
        in_specs=in_specs,
        out_specs=pl.BlockSpec((TM, GW), qmap),
        out_shape=jax.ShapeDtypeStruct((B * S2, GW), BF16),
        input_output_aliases=aliases,
        compiler_params=_cp(("arbitrary",) * len(grid), 48),
        name="attn_ctx" if ctx_only else "attn",
    )(*args)


def _s5_kernel(backward, *refs):
    if backward:
        (u_ref, w_ref, are_ref, aim_ref, c_ref, yf_ref, d_ref, wg_ref, bg_ref,
         o_ref, st_re, st_im, buf) = refs
    else:
        u_ref, w_ref, are_ref, aim_ref, c_ref, o_ref, st_re, st_im, buf = refs
    k = pl.program_id(0)
    hw = GW // 2
    rows = B * S5_T

    @pl.when(k == 0)
    def _():
        st_re[...] = jnp.zeros_like(st_re)
        st_im[...] = jnp.zeros_like(st_im)

    u = u_ref[...].reshape(rows, GW)
    ub = u.astype(BF16)
    nlt = S5_LT
    for h in range(2):
        bu = jnp.dot(ub[:, h * hw:(h + 1) * hw], w_ref[h], preferred_element_type=F32)
        for b in range(B):
            for c in range(2 * nlt):
                buf[c, pl.ds(2 * b + h, S5_T, stride=8), :] = (
                    bu[b * S5_T:(b + 1) * S5_T, c * 128:(c + 1) * 128])
    a_re = are_ref[...]
    a_im = aim_ref[...]

    def step(j, carry):
        sr, si = carry
        r0 = pl.multiple_of((S5_T - 1 - j if backward else j) * 8, 8)
        br = buf[0:nlt, pl.ds(r0, 8), :]
        bi = buf[nlt:2 * nlt, pl.ds(r0, 8), :]
        nr = a_re * sr - a_im * si + br
        ni = a_re * si + a_im * sr + bi
        buf[0:nlt, pl.ds(r0, 8), :] = nr
        buf[nlt:2 * nlt, pl.ds(r0, 8), :] = ni
        return nr, ni

    sr, si = lax.fori_loop(0, S5_T, step, (st_re[...], st_im[...]))
    st_re[...] = sr
    st_im[...] = si

    ys = []
    for h in range(2):
        s = jnp.concatenate(
            [jnp.concatenate([buf[c, pl.ds(2 * b + h, S5_T, stride=8), :] for c in range(2 * nlt)],
                             axis=1) for b in range(B)], axis=0)
        ys.append(jnp.dot(s.astype(BF16), c_ref[h], preferred_element_type=F32))
    y = jnp.concatenate(ys, axis=1)
    if backward:
        y = y + yf_ref[...].reshape(rows, GW) + d_ref[...] * u
        g = jax.nn.gelu(y)
        z = jnp.dot(g.astype(BF16), wg_ref[...], preferred_element_type=F32) + bg_ref[...]
        o_ref[...] = (g * jax.nn.sigmoid(z)).astype(BF16).reshape(B, S5_T, GW)
    else:
        o_ref[...] = y.reshape(B, S5_T, GW)


def _s5_pass(u3, w_bd, a_re8, a_im8, c_bd, glu):
    backward = glu is not None
    dr = 1 if backward else 0
    if backward:
        blk = lambda k: (0, S5_NCH - 1 - k, 0)
    else:
        blk = lambda k: (0, (k + SEQ // S5_T) % S5_NCH, 0)
    fixed = lambda k: (dr, 0, 0, 0)
    in_specs = [pl.BlockSpec((B, S5_T, GW), blk),
                pl.BlockSpec((None, 2, GW // 2, 2 * S5_HALF), fixed),
                pl.BlockSpec((None, S5_LT, 8, 128), fixed),
                pl.BlockSpec((None, S5_LT, 8, 128), fixed),
                pl.BlockSpec((None, 2, 2 * S5_HALF, GW // 2), fixed)]
    args = [u3, w_bd, a_re8, a_im8, c_bd]
    if backward:
        in_specs += [pl.BlockSpec((B, S5_T, GW), blk),
                     pl.BlockSpec((1, GW), lambda k: (0, 0)),
                     pl.BlockSpec((GW, GW), lambda k: (0, 0)),
                     pl.BlockSpec((1, GW), lambda k: (0, 0))]
        args += list(glu)
    return pl.pallas_call(
        functools.partial(_s5_kernel, backward),
        grid=(S5_NCH,),
        in_specs=in_specs,
        out_specs=pl.BlockSpec((B, S5_T, GW), blk),
        out_shape=jax.ShapeDtypeStruct((B, S2, GW), BF16 if backward else F32),
        scratch_shapes=[pltpu.VMEM((S5_LT, 8, 128), F32),
                        pltpu.VMEM((S5_LT, 8, 128), F32),
                        pltpu.VMEM((2 * S5_LT, 8 * S5_T, 128), F32)],
        compiler_params=_cp(("arbitrary",), 48),
        name="s5_bwd_glu" if backward else "s5_fwd",
    )(*args)


def _outproj_kernel(a0_ref, a1_ref, a2_ref, a3_ref, w_ref, x_ref, mod_ref, g2_ref, wr_ref,
                    x1_ref, h2_ref, aff_ref):
    acts = (a0_ref[...], a1_ref[...], a2_ref[...], a3_ref[...])
    halves = []
    for c in (slice(0, D // 2), slice(D // 2, D)):
        o = jnp.dot(acts[0], w_ref[0:GW, c], preferred_element_type=F32)
        for m in range(1, 4):
            o = o + jnp.dot(acts[m], w_ref[m * GW:(m + 1) * GW, c], preferred_element_type=F32)
        halves.append(o)
    x1 = x_ref[...] + mod_ref[2:3, :] * jnp.concatenate(halves, axis=1)
    x1_ref[...] = x1
    h2 = _modnorm(x1, g2_ref[...], mod_ref[3:4, :], mod_ref[4:5, :]).astype(BF16)
    h2_ref[...] = h2
    lg2 = lax.dot_general(wr_ref[...], h2, _NT_DIMS, preferred_element_type=F32)
    lg = lg2[:N_EXP] + lg2[N_EXP:]
    e = jnp.exp(lg - jnp.max(lg, axis=0, keepdims=True))
    aff_ref[...] = e / jnp.sum(e, axis=0, keepdims=True)


def _outproj(sgu_o, attn_o, ssm_tb, conv_o, w_out_bf, x2d, mod, g2, wr_t, tile_fn, ntiles):
    row = lambda i: (tile_fn(i), 0)
    return pl.pallas_call(
        _outproj_kernel,
        grid=(ntiles,),
        in_specs=[pl.BlockSpec((TM, GW), row),
                  pl.BlockSpec((TM, GW), row),
                  pl.BlockSpec((TM, GW), row),
                  pl.BlockSpec((TM, GW), row),
                  pl.BlockSpec((D, D), lambda i: (0, 0)),
                  pl.BlockSpec((TM, D), row),
                  pl.BlockSpec((None, 6, D), lambda i: (_seg_of_tile(tile_fn(i)), 0, 0)),
                  pl.BlockSpec((1, D), lambda i: (0, 0)),
                  pl.BlockSpec((2 * N_EXP, D), lambda i: (0, 0))],
        out_specs=[pl.BlockSpec((TM, D), row),
                   pl.BlockSpec((TM, D), row),
                   pl.BlockSpec((N_EXP, TM), lambda i: (0, tile_fn(i)))],
        out_shape=[jax.ShapeDtypeStruct((B * S2, D), F32),
                   jax.ShapeDtypeStruct((B * S2, D), BF16),
                   jax.ShapeDtypeStruct((N_EXP, B * S2), F32)],
        compiler_params=_cp(("arbitrary",), 48),
        name="outproj",
    )(sgu_o, attn_o, ssm_tb, conv_o, w_out_bf, x2d, mod, g2, wr_t)


def _select_one(a, tri, cap, p_ref, w_ref):
    ne, n = a.shape
    bits = pltpu.bitcast(a, jnp.int32)
    thr = jnp.zeros((ne, 1), jnp.int32)
    for bit in range(30, -1, -1):
        cand = thr | (1 << bit)
        cnt = jnp.sum(jnp.where(bits >= cand, 1.0, 0.0), axis=1, keepdims=True)
        thr = jnp.where(cnt >= cap, cand, thr)
    gt = jnp.where(bits > thr, 1.0, 0.0)
    eq = jnp.where(bits == thr, 1.0, 0.0)
    need = cap - jnp.sum(gt, axis=1, keepdims=True)
    eq_before = jnp.dot(eq.astype(BF16), tri, preferred_element_type=F32) - eq
    sel = gt + eq * jnp.where(eq_before < need, 1.0, 0.0)
    rank = jnp.dot(sel.astype(BF16), tri, preferred_element_type=F32) - 1.0
    rank = jnp.where(sel > 0.5, rank, -1.0)
    slot = lax.broadcasted_iota(jnp.int32, (cap, n), 0).astype(F32)
    for e in range(ne):
        hit = rank[e:e + 1, :] == slot
        p_ref[e * cap:(e + 1) * cap, :] = jnp.where(hit, 1.0, 0.0).astype(BF16)
        w_ref[e * cap:(e + 1) * cap, :] = jnp.sum(jnp.where(hit, a[e:e + 1, :], 0.0),
                                                  axis=1, keepdims=True)


def _select_kernel(with_ctx, a_ref, tri_ref, *out_refs):
    _select_one(a_ref[:, :SEQ], tri_ref[...], CAP, out_refs[0], out_refs[1])
    if with_ctx:
        _select_one(a_ref[:, SEQ:], tri_ref[:CTX, :CTX], CAP_C, out_refs[2], out_refs[3])


def _select(aff_t, tri, with_ctx):
    eg = 8
    out_specs = [pl.BlockSpec((None, eg * CAP, SEQ), lambda b, j: (b, j, 0)),
                 pl.BlockSpec((None, eg * CAP, 1), lambda b, j: (b, j, 0))]
    out_shape = [jax.ShapeDtypeStruct((B, N_EXP * CAP, SEQ), BF16),
                 jax.ShapeDtypeStruct((B, N_EXP * CAP, 1), F32)]
    if with_ctx:
        out_specs += [pl.BlockSpec((None, eg * CAP_C, CTX), lambda b, j: (b, j, 0)),
                      pl.BlockSpec((None, eg * CAP_C, 1), lambda b, j: (b, j, 0))]
        out_shape += [jax.ShapeDtypeStruct((B, N_EXP * CAP_C, CTX), BF16),
                      jax.ShapeDtypeStruct((B, N_EXP * CAP_C, 1), F32)]
    return pl.pallas_call(
        functools.partial(_select_kernel, with_ctx),
        grid=(B, N_EXP // eg),
        in_specs=[pl.BlockSpec((eg, S2), lambda b, j: (j, b)),
                  pl.BlockSpec((SEQ, SEQ), lambda b, j: (0, 0))],
        out_specs=out_specs,
        out_shape=out_shape,
        compiler_params=_cp(("arbitrary", "arbitrary"), 56),
        name="select",
    )(aff_t, tri)


def _gather_kernel(p_ref, h_ref, o_ref):
    o_ref[...] = jnp.dot(p_ref[...], h_ref[...], preferred_element_type=F32).astype(BF16)


def _gather(p_hot, h3, ctx_only):
    if ctx_only:
        rows, n, tr = N_EXP * CAP_C, CTX, N_EXP * CAP_C
        hmap = lambda b, r: (b, TPB - 1, 0)
    else:
        rows, n, tr = N_EXP * CAP, SEQ, 512
        hmap = lambda b, r: (b, 0, 0)
    return pl.pallas_call(
        _gather_kernel,
        grid=(B, rows // tr),
        in_specs=[pl.BlockSpec((None, tr, n), lambda b, r: (b, r, 0)),
                  pl.BlockSpec((None, n, D), hmap)],
        out_specs=pl.BlockSpec((None, tr, D), lambda b, r: (b, r, 0)),
        out_shape=jax.ShapeDtypeStruct((B, rows, D), BF16),
        compiler_params=_cp(("arbitrary", "arbitrary"), 48),
        name="gather_ctx" if ctx_only else "gather",
    )(p_hot, h3)


def _ffn_kernel(with_ctx, *refs):
    if with_ctx:
        (x_ref, xc_ref, wg_ref, wu_ref, wd_ref, ws_ref, wsc_ref,
         y_ref, yc_ref, acc, accc) = refs
    else:
        x_ref, wg_ref, wu_ref, wd_ref, ws_ref, y_ref, acc = refs
    f = pl.program_id(1)
    last = pl.num_programs(1) - 1
    wg = wg_ref[...].astype(BF16)
    wu = wu_ref[...].astype(BF16)
    wd = wd_ref[...].astype(BF16)

    def run(xr, wsr, yr, ac, rows):
        x = xr[...].reshape(rows, D)
        gate = jnp.dot(x, wg, preferred_element_type=F32)
        up = jnp.dot(x, wu, preferred_element_type=F32)
        hid = (gate * jax.nn.sigmoid(gate) * up).astype(BF16)
        part = jnp.dot(hid, wd, preferred_element_type=F32)

        @pl.when(f == 0)
        def _():
            ac[...] = part

        @pl.when(f > 0)
        def _():
            ac[...] += part

        @pl.when(f == last)
        def _():
            y = ac[...] * wsr[...].reshape(rows, 1)
            yr[...] = y.astype(BF16).reshape(yr.shape)

    run(x_ref, ws_ref, y_ref, acc, B * CAP)
    if with_ctx:
        run(xc_ref, wsc_ref, yc_ref, accc, B * CAP_C)


def _ffn(xs, ws, xc, wsc, w_gate, w_up, w_down, li):
    with_ctx = xc is not None
    in_specs = [pl.BlockSpec((B, CAP, D), lambda e, f: (0, e, 0))]
    args = [xs]
    if with_ctx:
        in_specs.append(pl.BlockSpec((B, CAP_C, D), lambda e, f: (0, e, 0)))
        args.append(xc)
    in_specs += [pl.BlockSpec((None, None, D, FF_T), lambda e, f: (li, e, 0, f)),
                 pl.BlockSpec((None, None, D, FF_T), lambda e, f: (li, e, 0, f)),
                 pl.BlockSpec((None, None, FF_T, D), lambda e, f: (li, e, f, 0)),
                 pl.BlockSpec((B, CAP, 1), lambda e, f: (0, e, 0))]
    args += [w_gate, w_up, w_down, ws]
    out_specs = [pl.BlockSpec((B, CAP, D), lambda e, f: (0, e, 0))]
    out_shape = [jax.ShapeDtypeStruct((B, N_EXP * CAP, D), BF16)]
    scratch = [pltpu.VMEM((B * CAP, D), F32)]
    if with_ctx:
        in_specs.append(pl.BlockSpec((B, CAP_C, 1), lambda e, f: (0, e, 0)))
        args.append(wsc)
        out_specs.append(pl.BlockSpec((B, CAP_C, D), lambda e, f: (0, e, 0)))
        out_shape.append(jax.ShapeDtypeStruct((B, N_EXP * CAP_C, D), BF16))
        scratch.append(pltpu.VMEM((B * CAP_C, D), F32))
    return pl.pallas_call(
        functools.partial(_ffn_kernel, with_ctx),
        grid=(N_EXP, FF // FF_T),
        in_specs=in_specs,
        out_specs=out_specs,
        out_shape=out_shape,
        scratch_shapes=scratch,
        compiler_params=_cp(("arbitrary", "arbitrary"), 56),
        name="ffn",
    )(*args)


_TN_DIMS = (((0,), (0,)), ((), ()))


def _scatter_kernel(p_ref, y_ref, x_ref, mod_ref, o_ref):
    upd = lax.dot_general(p_ref[...], y_ref[...], _TN_DIMS, preferred_element_type=F32)
    o_ref[...] = x_ref[...] + mod_ref[5:6, :] * upd


def _scatter(p_hot, y, x3, mod, ctx_only):
    dn = D // 2
    if ctx_only:
        rows, tok = N_EXP * CAP_C, CTX
        grid = (B, 2, 1)
        xmap = lambda b, h, t: (b, TPB - 1, h)
        mmap = lambda b, h, t: (B, 0, h)
    else:
        rows, tok = N_EXP * CAP, TM
        grid = (B, 2, LPB)
        xmap = lambda b, h, t: (b, t, h)
        mmap = lambda b, h, t: (b, 0, h)
    return pl.pallas_call(
        _scatter_kernel,
        grid=grid,
        in_specs=[pl.BlockSpec((None, rows, tok), lambda b, h, t: (b, 0, t)),
                  pl.BlockSpec((None, rows, dn), lambda b, h, t: (b, 0, h)),
                  pl.BlockSpec((None, TM, dn), xmap),
                  pl.BlockSpec((None, 6, dn), mmap)],
        out_specs=pl.BlockSpec((None, TM, dn), xmap),
        out_shape=jax.ShapeDtypeStruct((B, S2, D), F32),
        input_output_aliases={2: 0},
        compiler_params=_cp(("arbitrary",) * 3, 48),
        name="scatter_ctx" if ctx_only else "scatter",
    )(p_hot, y, x3, mod)


def _final_kernel(x_ref, g_ref, o_ref):
    x = x_ref[...]
    o_ref[...] = x * lax.rsqrt(jnp.mean(x * x, axis=-1, keepdims=True) + EPS) * g_ref[...]


def _final_norm(x2d, g):
    return pl.pallas_call(
        _final_kernel,
        grid=(NLT,),
        in_specs=[pl.BlockSpec((TM, D), lambda i: (_lat_tile(i), 0)),
                  pl.BlockSpec((1, D), lambda i: (0, 0))],
        out_specs=pl.BlockSpec((TM, D), lambda i: (i, 0)),
        out_shape=jax.ShapeDtypeStruct((B * SEQ, D), F32),
        compiler_params=_cp(("arbitrary",), 32),
        name="final_norm",
    )(x2d, g)


def _rope_tables():
    n_freq = QK_ROPE // 4
    grid_w = 64
    pos = jnp.arange(SEQ, dtype=F32)
    inv_freq = 10000.0 ** (-jnp.arange(n_freq, dtype=F32) / n_freq)
    ang_r = jnp.floor(pos / grid_w)[:, None] * inv_freq
    ang_c = (pos - grid_w * jnp.floor(pos / grid_w))[:, None] * inv_freq
    cr, sr, cc, sc = jnp.cos(ang_r), jnp.sin(ang_r), jnp.cos(ang_c), jnp.sin(ang_c)
    cos = jnp.concatenate([cr, cr, cc, cc], axis=1)
    sin = jnp.concatenate([-sr, sr, -sc, sc], axis=1)
    cos = jnp.concatenate([cos, jnp.ones((CTX, QK_ROPE), F32)], axis=0)
    sin = jnp.concatenate([sin, jnp.zeros((CTX, QK_ROPE), F32)], axis=0)
    return jnp.tile(cos, (1, MLA_HEADS)), jnp.tile(sin, (1, MLA_HEADS))


def _pair_swap(w):
    return jnp.concatenate([w[..., 16:32], w[..., 0:16], w[..., 48:64], w[..., 32:48]], axis=-1)


def _s5_operators(a_re, a_im, log_dt, b_re, b_im, c_re, c_im):
    a = lax.complex(jnp.minimum(a_re.astype(F32), -1e-4), a_im.astype(F32))
    dt = jnp.exp(log_dt.astype(F32))[..., None]
    abar = jnp.exp(a * dt)
    bbar = ((abar - 1.0) / a)[..., None] * lax.complex(b_re.astype(F32), b_im.astype(F32))
    hg = S5_G // 2
    eye = jnp.eye(hg, dtype=F32)

    def in_op(m):
        m = jnp.transpose(m, (0, 1, 3, 2)).reshape(2, 2, hg, S5_CH, S5_N)
        return jnp.einsum("zhgcn,gk->zhgckn", m, eye).reshape(2, GW, S5_HALF)

    def out_op(m):
        m = m.reshape(2, 2, hg, S5_CH, S5_N)
        return jnp.einsum("zhgcn,gk->zgnhkc", m, eye).reshape(2, S5_HALF, GW)

    w_bd = jnp.concatenate([in_op(jnp.real(bbar)), in_op(jnp.imag(bbar))], axis=-1)
    w_bd = w_bd.reshape(2, 2, GW // 2, 2 * S5_HALF).astype(BF16)
    c_bd = jnp.concatenate([out_op(c_re.astype(F32)), out_op(-c_im.astype(F32))], axis=1)
    c_bd = jnp.transpose(c_bd.reshape(2, 2 * S5_HALF, 2, GW // 2), (0, 2, 1, 3)).astype(BF16)

    def rows8(m):
        m = m.reshape(2, 1, 2, S5_LT, 128)
        m = jnp.broadcast_to(m, (2, B, 2, S5_LT, 128)).reshape(2, 2 * B, S5_LT, 128)
        return jnp.transpose(m, (0, 2, 1, 3))

    return w_bd, rows8(jnp.real(abar)), rows8(jnp.imag(abar)), c_bd


def kernel(x, c, ctx, c_ctx, norm1_g, norm2_g, w_ada, b_ada, w_in, w_out, sgu_norm_g, sgu_w,
           sgu_b, mla_q_norm_g, mla_w_uq, mla_kv_norm_g, mla_w_ukv, s5_a_re, s5_a_im, s5_log_dt,
           s5_b_re, s5_b_im, s5_c_re, s5_c_im, s5_d, s5_w_glu, s5_b_glu, conv_w, moe_w_router,
           moe_w_gate, moe_w_up, moe_w_down, final_norm_g):
    c8 = jnp.concatenate([c, c_ctx[None, :], jnp.zeros((3, D), F32)], axis=0)
    mod_all = _modulation(c8, w_ada, b_ada).reshape(DEPTH, 8, 6, D)
    cos_t, sin_t = _rope_tables()
    tri = jnp.triu(jnp.ones((SEQ, SEQ), BF16))
    x3 = jnp.concatenate([x, ctx], axis=1)

    for i in range(DEPTH):
        last = i == DEPTH - 1
        mod = mod_all[i]
        tile_fn, ntiles = (_lat_tile, NLT) if last else (_all_tile, NT)

        wi = w_in[i]
        kr_cols = wi[:, 1024 + Q_LORA + KV_LORA:1728]
        w_in_r = jnp.concatenate([wi[:, :1728], _pair_swap(kr_cols), wi[:, 1728:]], axis=1).astype(BF16)
        x2d = x3.reshape(B * S2, D)
        p_sgu, p_mla, p_s5, p_conv = _inproj(x2d, mod, norm1_g[i][None, :], w_in_r)

        bias_full = jnp.repeat(jnp.swapaxes(sgu_b[i], 0, 1), 128, axis=1)
        sgu_o = _sgu(p_sgu, sgu_norm_g[i][None, :], sgu_w[i].astype(BF16), bias_full, tile_fn, ntiles)

        conv_o = _conv(p_conv, conv_w[i], tile_fn, ntiles)

        wq = mla_w_uq[i].reshape(Q_LORA, MLA_HEADS, QK_NOPE + QK_ROPE)
        wq_n = wq[:, :, :QK_NOPE].reshape(Q_LORA, -1)
        wq_r = wq[:, :, QK_NOPE:]
        wq_ext = jnp.concatenate([wq_n, wq_r.reshape(Q_LORA, -1),
                                  _pair_swap(wq_r).reshape(Q_LORA, -1)], axis=1).astype(BF16)
        wkv = mla_w_ukv[i].reshape(KV_LORA, MLA_HEADS, 2 * QK_NOPE)
        wkv_ext = jnp.concatenate([wkv[:, :, :QK_NOPE].reshape(KV_LORA, -1),
                                   wkv[:, :, QK_NOPE:].reshape(KV_LORA, -1)], axis=1).astype(BF16)
        q, kn, kr, v = _kvq(p_mla, mla_q_norm_g[i][None, :], mla_kv_norm_g[i][None, :],
                            wq_ext, wkv_ext, cos_t, sin_t)
        kn3 = kn.reshape(B, S2, GW)
        kr3 = kr.reshape(B, S2, QK_ROPE)
        v3 = v.reshape(B, S2, GW)
        attn_o = _attention(q, kn3, kr3, v3, None, ctx_only=False)
        if not last:
            attn_o = _attention(q, kn3, kr3, v3, attn_o, ctx_only=True)

        w_bd, a_re8, a_im8, c_bd = _s5_operators(s5_a_re[i], s5_a_im[i], s5_log_dt[i], s5_b_re[i],
                                                 s5_b_im[i], s5_c_re[i], s5_c_im[i])
        u3 = p_s5.reshape(B, S2, GW)
        y_fwd = _s5_pass(u3, w_bd, a_re8, a_im8, c_bd, None)
        ssm_o = _s5_pass(u3, w_bd, a_re8, a_im8, c_bd,
                         (y_fwd, s5_d[i][None, :], s5_w_glu[i].astype(BF16), s5_b_glu[i][None, :]))
        ssm_o = ssm_o.reshape(B * S2, GW)

        wr_t = jnp.transpose(moe_w_router[i])
        wr_hi = wr_t.astype(BF16)
        wr2 = jnp.concatenate([wr_hi, (wr_t - wr_hi.astype(F32)).astype(BF16)], axis=0)
        x1, h2, aff_t = _outproj(sgu_o, attn_o, ssm_o, conv_o, w_out[i].astype(BF16), x2d, mod,
                                 norm2_g[i][None, :], wr2, tile_fn, ntiles)

        sel = _select(aff_t, tri, not last)
        h3 = h2.reshape(B, S2, D)
        xs = _gather(sel[0], h3, ctx_only=False)
        xc = _gather(sel[2], h3, ctx_only=True) if not last else None
        ys = _ffn(xs, sel[1], xc, sel[3] if not last else None,
                  moe_w_gate, moe_w_up, moe_w_down, i)
        x3 = _scatter(sel[0], ys[0], x1.reshape(B, S2, D), mod, ctx_only=False)
        if not last:
            x3 = _scatter(sel[2], ys[1], x3, mod, ctx_only=True)

    return _final_norm(x3.reshape(B * S2, D), final_norm_g[None, :]).reshape(B, SEQ, D)
```

```python
import functools

import jax
import jax.numpy as jnp
from jax import lax
from jax.experimental import pallas as pl
from jax.experimental.pallas import tpu as pltpu

F32 = jnp.float32
BF16 = jnp.bfloat16

D = 2048
B = 4
SEQ = 2048
CTX = 256
S2 = SEQ + CTX
DEPTH = 2
GW = 512
EPS = 1e-6

TM = 256
TPB = S2 // TM
LPB = SEQ // TM
NT = B * TPB
NLT = B * LPB

SGU_HEADS = 4
CHUNK = 128
MLA_HEADS = 4
QK_NOPE = 128
QK_ROPE = 64
QK_PAD = 256
Q_LORA = 384
KV_LORA = 256
ATT_SCALE = (QK_NOPE + QK_ROPE) ** -0.5

S5_G = 32
S5_N = 64
S5_CH = 16
S5_T = 128
S5_NCH = S2 // S5_T
S5_HALF = (S5_G // 2) * S5_N
S5_LT = S5_HALF // 128

N_EXP = 16
FF = D // 2
CAP = 2 * SEQ // N_EXP
CAP_C = 2 * CTX // N_EXP
FF_T = 256

IN_W = 3840
MIB = 1024 * 1024


def _cp(sem, vmem_mb):
    return pltpu.CompilerParams(dimension_semantics=sem, vmem_limit_bytes=vmem_mb * MIB)


def _lat_tile(i):
    return (i // LPB) * TPB + i % LPB


def _seg_of_tile(t):
    return jnp.where(t % TPB == TPB - 1, B, t // TPB)


def _rms(x, g):
    return x * lax.rsqrt(jnp.mean(x * x, axis=-1, keepdims=True) + EPS) * g


def _modnorm(x, g, shift, scale):
    return _rms(x, g) * (1.0 + scale) + shift


def _mod_kernel(c_ref, w_ref, b_ref, o_ref):
    a = c_ref[...]
    a = a * jax.nn.sigmoid(a)
    o_ref[...] = jnp.dot(a.astype(BF16), w_ref[...].astype(BF16),
                         preferred_element_type=F32) + b_ref[...]


def _modulation(c8, w_ada, b_ada):
    tn = 1024
    return pl.pallas_call(
        _mod_kernel,
        grid=(DEPTH, 6 * D // tn),
        in_specs=[pl.BlockSpec((8, D), lambda l, j: (0, 0)),
                  pl.BlockSpec((None, D, tn), lambda l, j: (l, 0, j)),
                  pl.BlockSpec((None, 1, tn), lambda l, j: (l, 0, j))],
        out_specs=pl.BlockSpec((None, 8, tn), lambda l, j: (l, 0, j)),
        out_shape=jax.ShapeDtypeStruct((DEPTH, 8, 6 * D), F32),
        compiler_params=_cp(("arbitrary", "arbitrary"), 40),
        name="modulation",
    )(c8, w_ada, b_ada.reshape(DEPTH, 1, 6 * D))


def _inproj_kernel(x_ref, mod_ref, g_ref, w_ref,
                   sg_ref, sw_ref, sb_ref,
                   gq_ref, gkv_ref, wq_ref, wkv_ref, tq1_ref, tq2_ref, tk1_ref, tk2_ref,
                   sgu_ref, q_ref, kc_ref, v_ref, s5_ref, conv_ref):
    h = _modnorm(x_ref[...], g_ref[...], mod_ref[0:1, :], mod_ref[1:2, :]).astype(BF16)

    def mm(a, b):
        return jnp.dot(h, w_ref[:, a:b], preferred_element_type=F32)

    p = jax.nn.gelu(mm(0, 2 * GW))
    u = p[:, :GW]
    vb = _rms(p[:, GW:], sg_ref[...]).astype(BF16)
    for ck in range(TM // CHUNK):
        r = slice(ck * CHUNK, (ck + 1) * CHUNK)
        for hd in range(SGU_HEADS):
            c = slice(hd * 128, (hd + 1) * 128)
            m = jnp.dot(sw_ref[hd], vb[r, c], preferred_element_type=F32)
            sgu_ref[r, c] = (u[r, c] * (m + sb_ref[:, c])).astype(BF16)

    pm = mm(2 * GW, 2 * GW + 768)
    cq = _rms(pm[:, :Q_LORA], gq_ref[...]).astype(BF16)
    q = jnp.dot(cq, wq_ref[...], preferred_element_type=F32)
    tq1 = tq1_ref[...]
    tq2 = tq2_ref[...]
    for hd in range(MLA_HEADS):
        c = slice(hd * QK_PAD, (hd + 1) * QK_PAD)
        blk = q[:, c]
        q_ref[:, c] = (blk * tq1 + pltpu.roll(blk, QK_PAD - QK_ROPE, 1) * tq2).astype(BF16)
    ckv = _rms(pm[:, Q_LORA:Q_LORA + KV_LORA], gkv_ref[...]).astype(BF16)
    kv = jnp.dot(ckv, wkv_ref[...], preferred_element_type=F32)
    v_ref[...] = kv[:, GW:].astype(BF16)
    kt = pm[:, Q_LORA + KV_LORA:]
    kr = (kt * tk1_ref[...] + pltpu.roll(kt, QK_ROPE, 1) * tk2_ref[...]).astype(BF16)
    for hd in range(MLA_HEADS):
        kc_ref[:, hd * QK_PAD:hd * QK_PAD + QK_NOPE] = kv[:, hd * 128:(hd + 1) * 128].astype(BF16)
        kc_ref[:, hd * QK_PAD + QK_NOPE:(hd + 1) * QK_PAD] = kr

    s5_ref[...] = mm(1792, 2304)
    conv_ref[:, :GW] = mm(2304, 2816)
    conv_ref[:, GW:] = mm(2816, 3328) * mm(3328, 3840)


def _inproj(x2d, mod, g1, w_in_r, li, sgu_p, mla_p):
    fix2 = lambda i: (0, 0)
    pos = lambda i: (i % TPB, 0)
    row = lambda i: (i, 0)
    qkw = MLA_HEADS * QK_PAD
    return pl.pallas_call(
        _inproj_kernel,
        grid=(NT,),
        in_specs=[pl.BlockSpec((TM, D), row),
                  pl.BlockSpec((None, 6, D), lambda i: (_seg_of_tile(i), 0, 0)),
                  pl.BlockSpec((1, D), fix2),
                  pl.BlockSpec((None, D, IN_W), lambda i: (li, 0, 0)),
                  pl.BlockSpec((1, GW), fix2),
                  pl.BlockSpec((SGU_HEADS, CHUNK, CHUNK), lambda i: (0, 0, 0)),
                  pl.BlockSpec((CHUNK, GW), fix2),
                  pl.BlockSpec((1, Q_LORA), fix2),
                  pl.BlockSpec((1, KV_LORA), fix2),
                  pl.BlockSpec((Q_LORA, qkw), fix2),
                  pl.BlockSpec((KV_LORA, 2 * GW), fix2),
                  pl.BlockSpec((TM, QK_PAD), pos),
                  pl.BlockSpec((TM, QK_PAD), pos),
                  pl.BlockSpec((TM, 128), pos),
                  pl.BlockSpec((TM, 128), pos)],
        out_specs=[pl.BlockSpec((TM, GW), row),
                   pl.BlockSpec((TM, qkw), row),
                   pl.BlockSpec((TM, qkw), row),
                   pl.BlockSpec((TM, GW), row),
                   pl.BlockSpec((TM, GW), row),
                   pl.BlockSpec((TM, 2 * GW), row)],
        out_shape=[jax.ShapeDtypeStruct((B * S2, GW), BF16),
                   jax.ShapeDtypeStruct((B * S2, qkw), BF16),
                   jax.ShapeDtypeStruct((B * S2, qkw), BF16),
                   jax.ShapeDtypeStruct((B * S2, GW), BF16),
                   jax.ShapeDtypeStruct((B * S2, GW), F32),
                   jax.ShapeDtypeStruct((B * S2, 2 * GW), F32)],
        compiler_params=_cp(("arbitrary",), 56),
        name="inproj",
    )(x2d, mod, g1, w_in_r, *sgu_p, *mla_p)


def _conv_kernel(p_ref, zp_ref, zn_ref, w_ref, o_ref):
    r = pl.program_id(0) % TPB
    bg = p_ref[:, :GW]
    z = p_ref[:, GW:]
    row = lax.broadcasted_iota(jnp.int32, (TM, GW), 0)
    has_prev = jnp.logical_and(r != 0, r != TPB - 1)
    has_next = r < LPB - 1
    prev_row = zp_ref[7:8, :] * has_prev.astype(F32)
    next_row = zn_ref[0:1, :] * has_next.astype(F32)
    zm = jnp.where(row == 0, prev_row, pltpu.roll(z, 1, 0))
    zp = jnp.where(row == TM - 1, next_row, pltpu.roll(z, TM - 1, 0))
    y = w_ref[0:1, :] * zm + w_ref[1:2, :] * z + w_ref[2:3, :] * zp
    o_ref[...] = (bg * y).astype(BF16)


def _conv(p_conv, conv_w):
    rb = TM // 8
    nrb = B * S2 // 8
    return pl.pallas_call(
        _conv_kernel,
        grid=(NT,),
        in_specs=[pl.BlockSpec((TM, 2 * GW), lambda i: (i, 0)),
                  pl.BlockSpec((8, GW), lambda i: (jnp.maximum(i * rb - 1, 0), 1)),
                  pl.BlockSpec((8, GW), lambda i: (jnp.minimum((i + 1) * rb, nrb - 1), 1)),
                  pl.BlockSpec((3, GW), lambda i: (0, 0))],
        out_specs=pl.BlockSpec((TM, GW), lambda i: (i, 0)),
        out_shape=jax.ShapeDtypeStruct((B * S2, GW), BF16),
        compiler_params=_cp(("arbitrary",), 32),
        name="conv",
    )(p_conv, p_conv, p_conv, conv_w)


_NT_DIMS = (((1,), (1,)), ((), ()))


def _attn_kernel(with_ctx, q_ref, kc_ref, v_ref, o_ref):
    def run(k0):
        for hd in range(MLA_HEADS):
            c = slice(hd * 128, (hd + 1) * 128)
            cq = slice(hd * QK_PAD, (hd + 1) * QK_PAD)
            s = lax.dot_general(q_ref[:, cq], kc_ref[k0:, cq], _NT_DIMS, preferred_element_type=F32)
            m = jnp.max(s, axis=-1, keepdims=True)
            e = jnp.exp(s - m)
            l = jnp.sum(e, axis=-1, keepdims=True)
            o = jnp.dot(e.astype(BF16), v_ref[k0:, c], preferred_element_type=F32)
            o_ref[:, c] = (o / l).astype(BF16)

    if with_ctx:
        t = pl.program_id(1)
        pl.when(t < LPB)(lambda: run(0))
        pl.when(t == LPB)(lambda: run(SEQ))
    else:
        run(0)


def _attention(q, kc3, v3, with_ctx):
    tpb = TPB if with_ctx else LPB
    qkw = MLA_HEADS * QK_PAD
    return pl.pallas_call(
        functools.partial(_attn_kernel, with_ctx),
        grid=(B, tpb),
        in_specs=[pl.BlockSpec((TM, qkw), lambda b, t: (b * TPB + t, 0)),
                  pl.BlockSpec((None, S2, qkw), lambda b, t: (b, 0, 0)),
                  pl.BlockSpec((None, S2, GW), lambda b, t: (b, 0, 0))],
        out_specs=pl.BlockSpec((TM, GW), lambda b, t: (b * tpb + t, 0)),
        out_shape=jax.ShapeDtypeStruct((B * tpb * TM, GW), BF16),
        compiler_params=_cp(("arbitrary", "arbitrary"), 48),
        name="attn",
    )(q, kc3, v3)


def _s5_kernel(backward, *refs):
    if backward:
        (u_ref, w_ref, are_ref, aim_ref, c_ref, yf_ref, d_ref, wg_ref, bg_ref,
         o_ref, st_re, st_im, buf) = refs
    else:
        u_ref, w_ref, are_ref, aim_ref, c_ref, o_ref, st_re, st_im, buf = refs
    k = pl.program_id(0)
    hw = GW // 2
    rows = B * S5_T
    nlt = S5_LT

    @pl.when(k == 0)
    def _():
        st_re[...] = jnp.zeros_like(st_re)
        st_im[...] = jnp.zeros_like(st_im)

    u = u_ref[...].reshape(rows, GW)
    ub = u.astype(BF16)
    for h in range(2):
        bu = jnp.dot(ub[:, h * hw:(h + 1) * hw], w_ref[h], preferred_element_type=F32)
        for b in range(B):
            for c in range(2 * nlt):
                buf[c, pl.ds(2 * b + h, S5_T, stride=8), :] = (
                    bu[b * S5_T:(b + 1) * S5_T, c * 128:(c + 1) * 128])
    a_re = are_ref[...]
    a_im = aim_ref[...]

    def step(j, carry):
        sr, si = carry
        r0 = pl.multiple_of((S5_T - 1 - j if backward else j) * 8, 8)
        br = buf[0:nlt, pl.ds(r0, 8), :]
        bi = buf[nlt:2 * nlt, pl.ds(r0, 8), :]
        nr = a_re * sr - a_im * si + br
        ni = a_re * si + a_im * sr + bi
        buf[0:nlt, pl.ds(r0, 8), :] = nr
        buf[nlt:2 * nlt, pl.ds(r0, 8), :] = ni
        return nr, ni

    sr, si = lax.fori_loop(0, S5_T, step, (st_re[...], st_im[...]))
    st_re[...] = sr
    st_im[...] = si

    ys = []
    for h in range(2):
        s = jnp.concatenate(
            [jnp.concatenate([buf[c, pl.ds(2 * b + h, S5_T, stride=8), :] for c in range(2 * nlt)],
                             axis=1) for b in range(B)], axis=0)
        ys.append(jnp.dot(s.astype(BF16), c_ref[h], preferred_element_type=F32))
    y = jnp.concatenate(ys, axis=1)
    if backward:
        y = y + yf_ref[...].reshape(rows, GW) + d_ref[...] * u
        g = jax.nn.gelu(y)
        z = jnp.dot(g.astype(BF16), wg_ref[...], preferred_element_type=F32) + bg_ref[...]
        o_ref[...] = (g * jax.nn.sigmoid(z)).astype(BF16).reshape(B, S5_T, GW)
    else:
        o_ref[...] = y.reshape(B, S5_T, GW)


def _s5_pass(u3, w_bd, a_re8, a_im8, c_bd, glu):
    backward = glu is not None
    dr = 1 if backward else 0
    if backward:
        blk = lambda k: (0, S5_NCH - 1 - k, 0)
    else:
        blk = lambda k: (0, (k + SEQ // S5_T) % S5_NCH, 0)
    fixed = lambda k: (dr, 0, 0, 0)
    in_specs = [pl.BlockSpec((B, S5_T, GW), blk),
                pl.BlockSpec((None, 2, GW // 2, 2 * S5_HALF), fixed),
                pl.BlockSpec((None, S5_LT, 8, 128), fixed),
                pl.BlockSpec((None, S5_LT, 8, 128), fixed),
                pl.BlockSpec((None, 2, 2 * S5_HALF, GW // 2), fixed)]
    args = [u3, w_bd, a_re8, a_im8, c_bd]
    if backward:
        in_specs += [pl.BlockSpec((B, S5_T, GW), blk),
                     pl.BlockSpec((1, GW), lambda k: (0, 0)),
                     pl.BlockSpec((GW, GW), lambda k: (0, 0)),
                     pl.BlockSpec((1, GW), lambda k: (0, 0))]
        args += list(glu)
    return pl.pallas_call(
        functools.partial(_s5_kernel, backward),
        grid=(S5_NCH,),
        in_specs=in_specs,
        out_specs=pl.BlockSpec((B, S5_T, GW), blk),
        out_shape=jax.ShapeDtypeStruct((B, S2, GW), BF16 if backward else F32),
        scratch_shapes=[pltpu.VMEM((S5_LT, 8, 128), F32),
                        pltpu.VMEM((S5_LT, 8, 128), F32),
                        pltpu.VMEM((2 * S5_LT, 8 * S5_T, 128), F32)],
        compiler_params=_cp(("arbitrary",), 48),
        name="s5_bwd_glu" if backward else "s5_fwd",
    )(*args)


def _outproj_kernel(a0_ref, a1_ref, a2_ref, a3_ref, w_ref, x_ref, mod_ref, g2_ref, wr_ref,
                    x1_ref, h2_ref, aff_ref):
    acts = (a0_ref[...], a1_ref[...], a2_ref[...], a3_ref[...])
    halves = []
    for c in (slice(0, D // 2), slice(D // 2, D)):
        o = jnp.dot(acts[0], w_ref[0:GW, c], preferred_element_type=F32)
        for m in range(1, 4):
            o = o + jnp.dot(acts[m], w_ref[m * GW:(m + 1) * GW, c], preferred_element_type=F32)
        halves.append(o)
    x1 = x_ref[...] + mod_ref[2:3, :] * jnp.concatenate(halves, axis=1)
    x1_ref[...] = x1
    h2 = _modnorm(x1, g2_ref[...], mod_ref[3:4, :], mod_ref[4:5, :]).astype(BF16)
    h2_ref[...] = h2
    lg2 = lax.dot_general(wr_ref[...], h2, _NT_DIMS, preferred_element_type=F32)
    lg = lg2[:N_EXP] + lg2[N_EXP:]
    e = jnp.exp(lg - jnp.max(lg, axis=0, keepdims=True))
    aff_ref[...] = e / jnp.sum(e, axis=0, keepdims=True)


def _outproj(sgu_o, attn_o, ssm_o, conv_o, w_out_bf, x2d, mod, g2, wr2, li, with_ctx):
    ntiles = NT if with_ctx else NLT
    full = (lambda i: i) if with_ctx else _lat_tile
    frow = lambda i: (full(i), 0)
    orow = lambda i: (i, 0)
    return pl.pallas_call(
        _outproj_kernel,
        grid=(ntiles,),
        in_specs=[pl.BlockSpec((TM, GW), frow),
                  pl.BlockSpec((TM, GW), orow),
                  pl.BlockSpec((TM, GW), frow),
                  pl.BlockSpec((TM, GW), frow),
                  pl.BlockSpec((None, D, D), lambda i: (li, 0, 0)),
                  pl.BlockSpec((TM, D), frow),
                  pl.BlockSpec((None, 6, D), lambda i: (_seg_of_tile(full(i)), 0, 0)),
                  pl.BlockSpec((1, D), lambda i: (0, 0)),
                  pl.BlockSpec((2 * N_EXP, D), lambda i: (0, 0))],
        out_specs=[pl.BlockSpec((TM, D), orow),
                   pl.BlockSpec((TM, D), orow),
                   pl.BlockSpec((N_EXP, TM), lambda i: (0, i))],
        out_shape=[jax.ShapeDtypeStruct((ntiles * TM, D), F32),
                   jax.ShapeDtypeStruct((ntiles * TM, D), BF16),
                   jax.ShapeDtypeStruct((N_EXP, ntiles * TM), F32)],
        compiler_params=_cp(("arbitrary",), 48),
        name="outproj",
    )(sgu_o, attn_o, ssm_o, conv_o, w_out_bf, x2d, mod, g2, wr2)


def _select_one(a, tri, cap, p_ref, w_ref):
    ne, n = a.shape
    bits = pltpu.bitcast(a, jnp.int32)
    thr = jnp.zeros((ne, 1), jnp.int32)
    for bit in range(30, -1, -1):
        cand = thr | (1 << bit)
        cnt = jnp.sum(jnp.where(bits >= cand, 1.0, 0.0), axis=1, keepdims=True)
        thr = jnp.where(cnt >= cap, cand, thr)
    gt = jnp.where(bits > thr, 1.0, 0.0)
    eq = jnp.where(bits == thr, 1.0, 0.0)
    need = cap - jnp.sum(gt, axis=1, keepdims=True)
    eq_before = jnp.dot(eq.astype(BF16), tri, preferred_element_type=F32) - eq
    sel = gt + eq * jnp.where(eq_before < need, 1.0, 0.0)
    rank = jnp.dot(sel.astype(BF16), tri, preferred_element_type=F32) - 1.0
    rank = jnp.where(sel > 0.5, rank, -1.0)
    slot = lax.broadcasted_iota(jnp.int32, (cap, n), 0).astype(F32)
    for e in range(ne):
        hit = rank[e:e + 1, :] == slot
        p_ref[e * cap:(e + 1) * cap, :] = jnp.where(hit, 1.0, 0.0).astype(BF16)
        w_ref[e * cap:(e + 1) * cap, :] = jnp.sum(jnp.where(hit, a[e:e + 1, :], 0.0),
                                                  axis=1, keepdims=True)


def _select_kernel(with_ctx, a_ref, tri_ref, *out_refs):
    _select_one(a_ref[:, :SEQ], tri_ref[...], CAP, out_refs[0], out_refs[1])
    if with_ctx:
        _select_one(a_ref[:, SEQ:], tri_ref[:CTX, :CTX], CAP_C, out_refs[2], out_refs[3])


def _select(aff_t, tri, with_ctx):
    eg = 8
    out_specs = [pl.BlockSpec((None, eg * CAP, SEQ), lambda b, j: (b, j, 0)),
                 pl.BlockSpec((None, eg * CAP, 1), lambda b, j: (b, j, 0))]
    out_shape = [jax.ShapeDtypeStruct((B, N_EXP * CAP, SEQ), BF16),
                 jax.ShapeDtypeStruct((B, N_EXP * CAP, 1), F32)]
    if with_ctx:
        out_specs += [pl.BlockSpec((None, eg * CAP_C, CTX), lambda b, j: (b, j, 0)),
                      pl.BlockSpec((None, eg * CAP_C, 1), lambda b, j: (b, j, 0))]
        out_shape += [jax.ShapeDtypeStruct((B, N_EXP * CAP_C, CTX), BF16),
                      jax.ShapeDtypeStruct((B, N_EXP * CAP_C, 1), F32)]
    return pl.pallas_call(
        functools.partial(_select_kernel, with_ctx),
        grid=(B, N_EXP // eg),
        in_specs=[pl.BlockSpec((eg, S2 if with_ctx else SEQ), lambda b, j: (j, b)),
                  pl.BlockSpec((SEQ, SEQ), lambda b, j: (0, 0))],
        out_specs=out_specs,
        out_shape=out_shape,
        compiler_params=_cp(("arbitrary", "arbitrary"), 56),
        name="select",
    )(aff_t, tri)


def _gather_kernel(p_ref, h_ref, o_ref):
    o_ref[...] = jnp.dot(p_ref[...], h_ref[...], preferred_element_type=F32).astype(BF16)


def _gather(p_hot, h3, ctx_only):
    if ctx_only:
        rows, n, tr = N_EXP * CAP_C, CTX, N_EXP * CAP_C
        hmap = lambda b, r: (b, TPB - 1, 0)
    else:
        rows, n, tr = N_EXP * CAP, SEQ, 512
        hmap = lambda b, r: (b, 0, 0)
    return pl.pallas_call(
        _gather_kernel,
        grid=(B, rows // tr),
        in_specs=[pl.BlockSpec((None, tr, n), lambda b, r: (b, r, 0)),
                  pl.BlockSpec((None, n, D), hmap)],
        out_specs=pl.BlockSpec((None, tr, D), lambda b, r: (b, r, 0)),
        out_shape=jax.ShapeDtypeStruct((B, rows, D), BF16),
        compiler_params=_cp(("arbitrary", "arbitrary"), 48),
        name="gather_ctx" if ctx_only else "gather",
    )(p_hot, h3)


def _ffn_kernel(with_ctx, *refs):
    if with_ctx:
        (x_ref, xc_ref, wg_ref, wu_ref, wd_ref, ws_ref, wsc_ref,
         y_ref, yc_ref, acc, accc) = refs
    else:
        x_ref, wg_ref, wu_ref, wd_ref, ws_ref, y_ref, acc = refs
    f = pl.program_id(1)
    last = pl.num_programs(1) - 1
    wg = wg_ref[...].astype(BF16)
    wu = wu_ref[...].astype(BF16)
    wd = wd_ref[...].astype(BF16)

    def run(xr, wsr, yr, ac, rows):
        x = xr[...].reshape(rows, D)
        gate = jnp.dot(x, wg, preferred_element_type=F32)
        up = jnp.dot(x, wu, preferred_element_type=F32)
        hid = (gate * jax.nn.sigmoid(gate) * up).astype(BF16)
        part = jnp.dot(hid, wd, preferred_element_type=F32)

        @pl.when(f == 0)
        def _():
            ac[...] = part

        @pl.when(f > 0)
        def _():
            ac[...] += part

        @pl.when(f == last)
        def _():
            y = ac[...] * wsr[...].reshape(rows, 1)
            yr[...] = y.astype(BF16).reshape(yr.shape)

    run(x_ref, ws_ref, y_ref, acc, B * CAP)
    if with_ctx:
        run(xc_ref, wsc_ref, yc_ref, accc, B * CAP_C)


def _ffn(xs, ws, xc, wsc, w_gate, w_up, w_down, li):
    with_ctx = xc is not None
    in_specs = [pl.BlockSpec((B, CAP, D), lambda e, f: (0, e, 0))]
    args = [xs]
    if with_ctx:
        in_specs.append(pl.BlockSpec((B, CAP_C, D), lambda e, f: (0, e, 0)))
        args.append(xc)
    in_specs += [pl.BlockSpec((None, None, D, FF_T), lambda e, f: (li, e, 0, f)),
                 pl.BlockSpec((None, None, D, FF_T), lambda e, f: (li, e, 0, f)),
                 pl.BlockSpec((None, None, FF_T, D), lambda e, f: (li, e, f, 0)),
                 pl.BlockSpec((B, CAP, 1), lambda e, f: (0, e, 0))]
    args += [w_gate, w_up, w_down, ws]
    out_specs = [pl.BlockSpec((B, CAP, D), lambda e, f: (0, e, 0))]
    out_shape = [jax.ShapeDtypeStruct((B, N_EXP * CAP, D), BF16)]
    scratch = [pltpu.VMEM((B * CAP, D), F32)]
    if with_ctx:
        in_specs.append(pl.BlockSpec((B, CAP_C, 1), lambda e, f: (0, e, 0)))
        args.append(wsc)
        out_specs.append(pl.BlockSpec((B, CAP_C, D), lambda e, f: (0, e, 0)))
        out_shape.append(jax.ShapeDtypeStruct((B, N_EXP * CAP_C, D), BF16))
        scratch.append(pltpu.VMEM((B * CAP_C, D), F32))
    return pl.pallas_call(
        functools.partial(_ffn_kernel, with_ctx),
        grid=(N_EXP, FF // FF_T),
        in_specs=in_specs,
        out_specs=out_specs,
        out_shape=out_shape,
        scratch_shapes=scratch,
        compiler_params=_cp(("arbitrary", "arbitrary"), 56),
        name="ffn",
    )(*args)


_TN_DIMS = (((0,), (0,)), ((), ()))


def _scatter_kernel(with_ctx, final, *refs):
    refs = list(refs)
    p_ref, y_ref = refs[:2]
    pc_ref, yc_ref = refs[2:4] if with_ctx else (None, None)
    rest = refs[4:] if with_ctx else refs[2:]
    x_ref, mod_ref = rest[:2]
    gf_ref = rest[2] if final else None
    o_ref = rest[-1]

    def finish(pr, yr):
        upd = lax.dot_general(pr[...], yr[...], _TN_DIMS, preferred_element_type=F32)
        x = x_ref[...] + mod_ref[5:6, :] * upd
        o_ref[...] = _rms(x, gf_ref[...]) if final else x

    if with_ctx:
        t = pl.program_id(1)
        pl.when(t < LPB)(lambda: finish(p_ref, y_ref))
        pl.when(t == LPB)(lambda: finish(pc_ref, yc_ref))
    else:
        finish(p_ref, y_ref)


def _scatter(p_hot, y, pc_hot, yc, x3, mod, final_g):
    with_ctx = pc_hot is not None
    final = final_g is not None
    tpb = TPB if with_ctx else LPB
    in_specs = [pl.BlockSpec((None, N_EXP * CAP, TM), lambda b, t: (b, 0, jnp.minimum(t, LPB - 1))),
                pl.BlockSpec((None, N_EXP * CAP, D), lambda b, t: (b, 0, 0))]
    args = [p_hot, y]
    if with_ctx:
        in_specs += [pl.BlockSpec((None, N_EXP * CAP_C, CTX), lambda b, t: (b, 0, 0)),
                     pl.BlockSpec((None, N_EXP * CAP_C, D), lambda b, t: (b, 0, 0))]
        args += [pc_hot, yc]
    in_specs += [pl.BlockSpec((None, TM, D), lambda b, t: (b, t, 0)),
                 pl.BlockSpec((None, 6, D), lambda b, t: (jnp.where(t == LPB, B, b), 0, 0))]
    args += [x3, mod]
    if final:
        in_specs.append(pl.BlockSpec((1, D), lambda b, t: (0, 0)))
        args.append(final_g)
    return pl.pallas_call(
        functools.partial(_scatter_kernel, with_ctx, final),
        grid=(B, tpb),
        in_specs=in_specs,
        out_specs=pl.BlockSpec((None, TM, D), lambda b, t: (b, t, 0)),
        out_shape=jax.ShapeDtypeStruct((B, tpb * TM, D), F32),
        compiler_params=_cp(("arbitrary", "arbitrary"), 56),
        name="scatter",
    )(*args)


def _rope_tables():
    n_freq = QK_ROPE // 4
    grid_w = 64
    pos = jnp.arange(SEQ, dtype=F32)
    inv_freq = 10000.0 ** (-jnp.arange(n_freq, dtype=F32) / n_freq)
    ang_r = jnp.floor(pos / grid_w)[:, None] * inv_freq
    ang_c = (pos - grid_w * jnp.floor(pos / grid_w))[:, None] * inv_freq
    cr, sr, cc, sc = jnp.cos(ang_r), jnp.sin(ang_r), jnp.cos(ang_c), jnp.sin(ang_c)
    cos = jnp.concatenate([cr, cr, cc, cc], axis=1)
    sin = jnp.concatenate([-sr, sr, -sc, sc], axis=1)
    cos = jnp.concatenate([cos, jnp.ones((CTX, QK_ROPE), F32)], axis=0)
    sin = jnp.concatenate([sin, jnp.zeros((CTX, QK_ROPE), F32)], axis=0)
    z = jnp.zeros((S2, QK_ROPE), F32)
    tq1 = jnp.concatenate([jnp.full((S2, QK_NOPE), ATT_SCALE, F32), cos * ATT_SCALE, z], axis=1)
    tq2 = jnp.concatenate([jnp.zeros((S2, QK_NOPE), F32), sin * ATT_SCALE, z], axis=1)
    tk1 = jnp.concatenate([cos, z], axis=1)
    tk2 = jnp.concatenate([sin, z], axis=1)
    return tq1, tq2, tk1, tk2


def _pair_swap(w):
    return jnp.concatenate([w[..., 16:32], w[..., 0:16], w[..., 48:64], w[..., 32:48]], axis=-1)


def _s5_operators(a_re, a_im, log_dt, b_re, b_im, c_re, c_im):
    a = lax.complex(jnp.minimum(a_re.astype(F32), -1e-4), a_im.astype(F32))
    dt = jnp.exp(log_dt.astype(F32))[..., None]
    abar = jnp.exp(a * dt)
    bbar = ((abar - 1.0) / a)[..., None] * lax.complex(b_re.astype(F32), b_im.astype(F32))
    hg = S5_G // 2
    eye = jnp.eye(hg, dtype=F32)

    def in_op(m):
        m = jnp.transpose(m, (0, 1, 3, 2)).reshape(2, 2, hg, S5_CH, S5_N)
        return jnp.einsum("zhgcn,gk->zhgckn", m, eye).reshape(2, GW, S5_HALF)

    def out_op(m):
        m = m.reshape(2, 2, hg, S5_CH, S5_N)
        return jnp.einsum("zhgcn,gk->zgnhkc", m, eye).reshape(2, S5_HALF, GW)

    w_bd = jnp.concatenate([in_op(jnp.real(bbar)), in_op(jnp.imag(bbar))], axis=-1)
    w_bd = w_bd.reshape(2, 2, GW // 2, 2 * S5_HALF).astype(BF16)
    c_bd = jnp.concatenate([out_op(c_re.astype(F32)), out_op(-c_im.astype(F32))], axis=1)
    c_bd = jnp.transpose(c_bd.reshape(2, 2 * S5_HALF, 2, GW // 2), (0, 2, 1, 3)).astype(BF16)

    def rows8(m):
        m = m.reshape(2, 1, 2, S5_LT, 128)
        m = jnp.broadcast_to(m, (2, B, 2, S5_LT, 128)).reshape(2, 2 * B, S5_LT, 128)
        return jnp.transpose(m, (0, 2, 1, 3))

    return w_bd, rows8(jnp.real(abar)), rows8(jnp.imag(abar)), c_bd


def kernel(x, c, ctx, c_ctx, norm1_g, norm2_g, w_ada, b_ada, w_in, w_out, sgu_norm_g, sgu_w,
           sgu_b, mla_q_norm_g, mla_w_uq, mla_kv_norm_g, mla_w_ukv, s5_a_re, s5_a_im, s5_log_dt,
           s5_b_re, s5_b_im, s5_c_re, s5_c_im, s5_d, s5_w_glu, s5_b_glu, conv_w, moe_w_router,
           moe_w_gate, moe_w_up, moe_w_down, final_norm_g):
    c8 = jnp.concatenate([c, c_ctx[None, :], jnp.zeros((3, D), F32)], axis=0)
    mod_all = _modulation(c8, w_ada, b_ada).reshape(DEPTH, 8, 6, D)
    rope_t = _rope_tables()
    tri = jnp.triu(jnp.ones((SEQ, SEQ), BF16))
    kr0 = 2 * GW + Q_LORA + KV_LORA
    w_in_r = jnp.concatenate([w_in[:, :, :kr0 + QK_ROPE],
                              _pair_swap(w_in[:, :, kr0:kr0 + QK_ROPE]),
                              w_in[:, :, kr0 + QK_ROPE:]], axis=2).astype(BF16)
    w_out_bf = w_out.astype(BF16)
    x3 = jnp.concatenate([x, ctx], axis=1)

    for i in range(DEPTH):
        last = i == DEPTH - 1
        mod = mod_all[i]
        x2d = x3.reshape(B * S2, D)

        wq = mla_w_uq[i].reshape(Q_LORA, MLA_HEADS, QK_NOPE + QK_ROPE)
        wq_r = wq[:, :, QK_NOPE:]
        wq_ext = jnp.concatenate([wq[:, :, :QK_NOPE], wq_r, _pair_swap(wq_r)], axis=2)
        wq_ext = wq_ext.reshape(Q_LORA, MLA_HEADS * QK_PAD).astype(BF16)
        wkv = mla_w_ukv[i].reshape(KV_LORA, MLA_HEADS, 2 * QK_NOPE)
        wkv_ext = jnp.concatenate([wkv[:, :, :QK_NOPE].reshape(KV_LORA, -1),
                                   wkv[:, :, QK_NOPE:].reshape(KV_LORA, -1)], axis=1).astype(BF16)
        sgu_p = (sgu_norm_g[i][None, :], sgu_w[i].astype(BF16),
                 jnp.repeat(jnp.swapaxes(sgu_b[i], 0, 1), 128, axis=1))
        mla_p = (mla_q_norm_g[i][None, :], mla_kv_norm_g[i][None, :], wq_ext, wkv_ext) + rope_t
        sgu_o, q, kc, v, p_s5, p_conv = _inproj(x2d, mod, norm1_g[i][None, :], w_in_r, i,
                                                sgu_p, mla_p)

        conv_o = _conv(p_conv, conv_w[i])
        attn_o = _attention(q, kc.reshape(B, S2, MLA_HEADS * QK_PAD), v.reshape(B, S2, GW), not last)

        w_bd, a_re8, a_im8, c_bd = _s5_operators(s5_a_re[i], s5_a_im[i], s5_log_dt[i], s5_b_re[i],
                                                 s5_b_im[i], s5_c_re[i], s5_c_im[i])
        u3 = p_s5.reshape(B, S2, GW)
        y_fwd = _s5_pass(u3, w_bd, a_re8, a_im8, c_bd, None)
        ssm_o = _s5_pass(u3, w_bd, a_re8, a_im8, c_bd,
                         (y_fwd, s5_d[i][None, :], s5_w_glu[i].astype(BF16), s5_b_glu[i][None, :]))
        ssm_o = ssm_o.reshape(B * S2, GW)

        wr_t = jnp.transpose(moe_w_router[i])
        wr_hi = wr_t.astype(BF16)
        wr2 = jnp.concatenate([wr_hi, (wr_t - wr_hi.astype(F32)).astype(BF16)], axis=0)
        x1, h2, aff_t = _outproj(sgu_o, attn_o, ssm_o, conv_o, w_out_bf, x2d, mod,
                                 norm2_g[i][None, :], wr2, i, not last)

        rows_b = SEQ if last else S2
        sel = _select(aff_t, tri, not last)
        h3 = h2.reshape(B, rows_b, D)
        xs = _gather(sel[0], h3, ctx_only=False)
        xc = _gather(sel[2], h3, ctx_only=True) if not last else None
        ys = _ffn(xs, sel[1], xc, sel[3] if not last else None, moe_w_gate, moe_w_up, moe_w_down, i)
        x1_3 = x1.reshape(B, rows_b, D)
        if last:
            x3 = _scatter(sel[0], ys[0], None, None, x1_3, mod, final_norm_g[None, :])
        else:
            x3 = _scatter(sel[0], ys[0], sel[2], ys[1], x1_3, mod, None)

    return x3
```

```python
import functools

import jax
import jax.numpy as jnp
from jax import lax
from jax.experimental import pallas as pl
from jax.experimental.pallas import tpu as pltpu

F32 = jnp.float32
BF16 = jnp.bfloat16

D = 2048
B = 4
SEQ = 2048
CTX = 256
S2 = SEQ + CTX
DEPTH = 2
GW = 512
EPS = 1e-6

TM = 256
TPB = S2 // TM
LPB = SEQ // TM
NT = B * TPB
NLT = B * LPB

SGU_HEADS = 4
CHUNK = 128
MLA_HEADS = 4
QK_NOPE = 128
QK_ROPE = 64
QK_PAD = 256
Q_LORA = 384
KV_LORA = 256
ATT_SCALE = (QK_NOPE + QK_ROPE) ** -0.5

S5_G = 32
S5_N = 64
S5_CH = 16
S5_T = 128
S5_NCH = S2 // S5_T
S5_HALF = (S5_G // 2) * S5_N
S5_LT = S5_HALF // 128

N_EXP = 16
FF = D // 2
CAP = 2 * SEQ // N_EXP
CAP_C = 2 * CTX // N_EXP
FF_T = 256

IN_W = 3840
MIB = 1024 * 1024


def _cp(sem, vmem_mb):
    return pltpu.CompilerParams(dimension_semantics=sem, vmem_limit_bytes=vmem_mb * MIB)


def _lat_tile(i):
    return (i // LPB) * TPB + i % LPB


def _seg_of_tile(t):
    return jnp.where(t % TPB == TPB - 1, B, t // TPB)


def _rms(x, g):
    return x * lax.rsqrt(jnp.mean(x * x, axis=-1, keepdims=True) + EPS) * g


def _modnorm(x, g, shift, scale):
    return _rms(x, g) * (1.0 + scale) + shift


def _mod_kernel(c_ref, w_ref, b_ref, o_ref):
    a = c_ref[...]
    a = a * jax.nn.sigmoid(a)
    o_ref[...] = jnp.dot(a.astype(BF16), w_ref[...].astype(BF16),
                         preferred_element_type=F32) + b_ref[...]


def _modulation(c8, w_ada, b_ada):
    tn = 1024
    return pl.pallas_call(
        _mod_kernel,
        grid=(DEPTH, 6 * D // tn),
        in_specs=[pl.BlockSpec((8, D), lambda l, j: (0, 0)),
                  pl.BlockSpec((None, D, tn), lambda l, j: (l, 0, j)),
                  pl.BlockSpec((None, 1, tn), lambda l, j: (l, 0, j))],
        out_specs=pl.BlockSpec((None, 8, tn), lambda l, j: (l, 0, j)),
        out_shape=jax.ShapeDtypeStruct((DEPTH, 8, 6 * D), F32),
        compiler_params=_cp(("arbitrary", "arbitrary"), 40),
        name="modulation",
    )(c8, w_ada, b_ada.reshape(DEPTH, 1, 6 * D))


def _inproj_kernel(x_ref, mod_ref, g_ref, w_ref,
                   sg_ref, sw_ref, sb_ref,
                   gq_ref, gkv_ref, wq_ref, wkv_ref, tq1_ref, tq2_ref, tk1_ref, tk2_ref,
                   sgu_ref, q_ref, kc_ref, v_ref, s5_ref, conv_ref):
    h = _modnorm(x_ref[...], g_ref[...], mod_ref[0:1, :], mod_ref[1:2, :]).astype(BF16)

    def mm(a, b):
        return jnp.dot(h, w_ref[:, a:b], preferred_element_type=F32)

    p = jax.nn.gelu(mm(0, 2 * GW))
    u = p[:, :GW]
    vb = _rms(p[:, GW:], sg_ref[...]).astype(BF16)
    for ck in range(TM // CHUNK):
        r = slice(ck * CHUNK, (ck + 1) * CHUNK)
        for hd in range(SGU_HEADS):
            c = slice(hd * 128, (hd + 1) * 128)
            m = jnp.dot(sw_ref[hd], vb[r, c], preferred_element_type=F32)
            sgu_ref[r, c] = (u[r, c] * (m + sb_ref[:, c])).astype(BF16)

    pm = mm(2 * GW, 2 * GW + 768)
    cq = _rms(pm[:, :Q_LORA], gq_ref[...]).astype(BF16)
    q = jnp.dot(cq, wq_ref[...], preferred_element_type=F32)
    tq1 = tq1_ref[...]
    tq2 = tq2_ref[...]
    for hd in range(MLA_HEADS):
        c = slice(hd * QK_PAD, (hd + 1) * QK_PAD)
        blk = q[:, c]
        q_ref[:, c] = (blk * tq1 + pltpu.roll(blk, QK_PAD - QK_ROPE, 1) * tq2).astype(BF16)
    ckv = _rms(pm[:, Q_LORA:Q_LORA + KV_LORA], gkv_ref[...]).astype(BF16)
    kv = jnp.dot(ckv, wkv_ref[...], preferred_element_type=F32)
    v_ref[...] = kv[:, GW:].astype(BF16)
    kt = pm[:, Q_LORA + KV_LORA:]
    kr = (kt * tk1_ref[...] + pltpu.roll(kt, QK_ROPE, 1) * tk2_ref[...]).astype(BF16)
    for hd in range(MLA_HEADS):
        kc_ref[:, hd * QK_PAD:hd * QK_PAD + QK_NOPE] = kv[:, hd * 128:(hd + 1) * 128].astype(BF16)
        kc_ref[:, hd * QK_PAD + QK_NOPE:(hd + 1) * QK_PAD] = kr

    s5_ref[...] = mm(1792, 2304)
    conv_ref[:, :GW] = mm(2304, 2816)
    conv_ref[:, GW:] = mm(2816, 3328) * mm(3328, 3840)


def _inproj(x2d, mod, g1, w_in_r, li, sgu_p, mla_p):
    fix2 = lambda i: (0, 0)
    pos = lambda i: (i % TPB, 0)
    row = lambda i: (i, 0)
    qkw = MLA_HEADS * QK_PAD
    return pl.pallas_call(
        _inproj_kernel,
        grid=(NT,),
        in_specs=[pl.BlockSpec((TM, D), row),
                  pl.BlockSpec((None, 6, D), lambda i: (_seg_of_tile(i), 0, 0)),
                  pl.BlockSpec((1, D), fix2),
                  pl.BlockSpec((None, D, IN_W), lambda i: (li, 0, 0)),
                  pl.BlockSpec((1, GW), fix2),
                  pl.BlockSpec((SGU_HEADS, CHUNK, CHUNK), lambda i: (0, 0, 0)),
                  pl.BlockSpec((CHUNK, GW), fix2),
                  pl.BlockSpec((1, Q_LORA), fix2),
                  pl.BlockSpec((1, KV_LORA), fix2),
                  pl.BlockSpec((Q_LORA, qkw), fix2),
                  pl.BlockSpec((KV_LORA, 2 * GW), fix2),
                  pl.BlockSpec((TM, QK_PAD), pos),
                  pl.BlockSpec((TM, QK_PAD), pos),
                  pl.BlockSpec((TM, 128), pos),
                  pl.BlockSpec((TM, 128), pos)],
        out_specs=[pl.BlockSpec((TM, GW), row),
                   pl.BlockSpec((TM, qkw), row),
                   pl.BlockSpec((TM, qkw), row),
                   pl.BlockSpec((TM, GW), row),
                   pl.BlockSpec((TM, GW), row),
                   pl.BlockSpec((TM, 2 * GW), row)],
        out_shape=[jax.ShapeDtypeStruct((B * S2, GW), BF16),
                   jax.ShapeDtypeStruct((B * S2, qkw), BF16),
                   jax.ShapeDtypeStruct((B * S2, qkw), BF16),
                   jax.ShapeDtypeStruct((B * S2, GW), BF16),
                   jax.ShapeDtypeStruct((B * S2, GW), F32),
                   jax.ShapeDtypeStruct((B * S2, 2 * GW), F32)],
        compiler_params=_cp(("arbitrary",), 56),
        name="inproj",
    )(x2d, mod, g1, w_in_r, *sgu_p, *mla_p)


def _conv_kernel(p_ref, zp_ref, zn_ref, w_ref, o_ref):
    r = pl.program_id(0) % TPB
    bg = p_ref[:, :GW]
    z = p_ref[:, GW:]
    row = lax.broadcasted_iota(jnp.int32, (TM, GW), 0)
    has_prev = jnp.logical_and(r != 0, r != TPB - 1)
    has_next = r < LPB - 1
    prev_row = zp_ref[7:8, :] * has_prev.astype(F32)
    next_row = zn_ref[0:1, :] * has_next.astype(F32)
    zm = jnp.where(row == 0, prev_row, pltpu.roll(z, 1, 0))
    zp = jnp.where(row == TM - 1, next_row, pltpu.roll(z, TM - 1, 0))
    y = w_ref[0:1, :] * zm + w_ref[1:2, :] * z + w_ref[2:3, :] * zp
    o_ref[...] = (bg * y).astype(BF16)


def _conv(p_conv, conv_w):
    rb = TM // 8
    nrb = B * S2 // 8
    return pl.pallas_call(
        _conv_kernel,
        grid=(NT,),
        in_specs=[pl.BlockSpec((TM, 2 * GW), lambda i: (i, 0)),
                  pl.BlockSpec((8, GW), lambda i: (jnp.maximum(i * rb - 1, 0), 1)),
                  pl.BlockSpec((8, GW), lambda i: (jnp.minimum((i + 1) * rb, nrb - 1), 1)),
                  pl.BlockSpec((3, GW), lambda i: (0, 0))],
        out_specs=pl.BlockSpec((TM, GW), lambda i: (i, 0)),
        out_shape=jax.ShapeDtypeStruct((B * S2, GW), BF16),
        compiler_params=_cp(("arbitrary",), 32),
        name="conv",
    )(p_conv, p_conv, p_conv, conv_w)


_NT_DIMS = (((1,), (1,)), ((), ()))


def _attn_kernel(with_ctx, q_ref, kc_ref, v_ref, o_ref):
    def run(k0):
        for hd in range(MLA_HEADS):
            c = slice(hd * 128, (hd + 1) * 128)
            cq = slice(hd * QK_PAD, (hd + 1) * QK_PAD)
            s = lax.dot_general(q_ref[:, cq], kc_ref[k0:, cq], _NT_DIMS, preferred_element_type=F32)
            m = jnp.max(s, axis=-1, keepdims=True)
            e = jnp.exp(s - m)
            l = jnp.sum(e, axis=-1, keepdims=True)
            o = jnp.dot(e.astype(BF16), v_ref[k0:, c], preferred_element_type=F32)
            o_ref[:, c] = (o / l).astype(BF16)

    if with_ctx:
        t = pl.program_id(1)
        pl.when(t < LPB)(lambda: run(0))
        pl.when(t == LPB)(lambda: run(SEQ))
    else:
        run(0)


def _attention(q, kc3, v3, with_ctx):
    tpb = TPB if with_ctx else LPB
    qkw = MLA_HEADS * QK_PAD
    return pl.pallas_call(
        functools.partial(_attn_kernel, with_ctx),
        grid=(B, tpb),
        in_specs=[pl.BlockSpec((TM, qkw), lambda b, t: (b * TPB + t, 0)),
                  pl.BlockSpec((None, S2, qkw), lambda b, t: (b, 0, 0)),
                  pl.BlockSpec((None, S2, GW), lambda b, t: (b, 0, 0))],
        out_specs=pl.BlockSpec((TM, GW), lambda b, t: (b * tpb + t, 0)),
        out_shape=jax.ShapeDtypeStruct((B * tpb * TM, GW), BF16),
        compiler_params=_cp(("arbitrary", "arbitrary"), 48),
        name="attn",
    )(q, kc3, v3)


def _s5_kernel(backward, *refs):
    if backward:
        (u_ref, w_ref, are_ref, aim_ref, c_ref, yf_ref, d_ref, wg_ref, bg_ref,
         o_ref, st_re, st_im, buf) = refs
    else:
        u_ref, w_ref, are_ref, aim_ref, c_ref, o_ref, st_re, st_im, buf = refs
    k = pl.program_id(0)
    hw = GW // 2
    rows = B * S5_T
    nlt = S5_LT

    @pl.when(k == 0)
    def _():
        st_re[...] = jnp.zeros_like(st_re)
        st_im[...] = jnp.zeros_like(st_im)

    u = u_ref[...].reshape(rows, GW)
    ub = u.astype(BF16)
    for h in range(2):
        bu = jnp.dot(ub[:, h * hw:(h + 1) * hw], w_ref[h], preferred_element_type=F32)
        for b in range(B):
            for c in range(2 * nlt):
                buf[c, pl.ds(2 * b + h, S5_T, stride=8), :] = (
                    bu[b * S5_T:(b + 1) * S5_T, c * 128:(c + 1) * 128])
    a_re = are_ref[...]
    a_im = aim_ref[...]

    def step(j, carry):
        sr, si = carry
        r0 = pl.multiple_of((S5_T - 1 - j if backward else j) * 8, 8)
        br = buf[0:nlt, pl.ds(r0, 8), :]
        bi = buf[nlt:2 * nlt, pl.ds(r0, 8), :]
        nr = a_re * sr - a_im * si + br
        ni = a_re * si + a_im * sr + bi
        buf[0:nlt, pl.ds(r0, 8), :] = nr
        buf[nlt:2 * nlt, pl.ds(r0, 8), :] = ni
        return nr, ni

    sr, si = lax.fori_loop(0, S5_T, step, (st_re[...], st_im[...]))
    st_re[...] = sr
    st_im[...] = si

    ys = []
    for h in range(2):
        s = jnp.concatenate(
            [jnp.concatenate([buf[c, pl.ds(2 * b + h, S5_T, stride=8), :] for c in range(2 * nlt)],
                             axis=1) for b in range(B)], axis=0)
        ys.append(jnp.dot(s.astype(BF16), c_ref[h], preferred_element_type=F32))
    y = jnp.concatenate(ys, axis=1)
    if backward:
        y = y + yf_ref[...].reshape(rows, GW) + d_ref[...] * u
        g = jax.nn.gelu(y)
        z = jnp.dot(g.astype(BF16), wg_ref[...], preferred_element_type=F32) + bg_ref[...]
        o_ref[...] = (g * jax.nn.sigmoid(z)).astype(BF16).reshape(B, S5_T, GW)
    else:
        o_ref[...] = y.reshape(B, S5_T, GW)


def _s5_pass(u3, w_bd, a_re8, a_im8, c_bd, glu):
    backward = glu is not None
    dr = 1 if backward else 0
    if backward:
        blk = lambda k: (0, S5_NCH - 1 - k, 0)
    else:
        blk = lambda k: (0, (k + SEQ // S5_T) % S5_NCH, 0)
    fixed = lambda k: (dr, 0, 0, 0)
    in_specs = [pl.BlockSpec((B, S5_T, GW), blk),
                pl.BlockSpec((None, 2, GW // 2, 2 * S5_HALF), fixed),
                pl.BlockSpec((None, S5_LT, 8, 128), fixed),
                pl.BlockSpec((None, S5_LT, 8, 128), fixed),
                pl.BlockSpec((None, 2, 2 * S5_HALF, GW // 2), fixed)]
    args = [u3, w_bd, a_re8, a_im8, c_bd]
    if backward:
        in_specs += [pl.BlockSpec((B, S5_T, GW), blk),
                     pl.BlockSpec((1, GW), lambda k: (0, 0)),
                     pl.BlockSpec((GW, GW), lambda k: (0, 0)),
                     pl.BlockSpec((1, GW), lambda k: (0, 0))]
        args += list(glu)
    return pl.pallas_call(
        functools.partial(_s5_kernel, backward),
        grid=(S5_NCH,),
        in_specs=in_specs,
        out_specs=pl.BlockSpec((B, S5_T, GW), blk),
        out_shape=jax.ShapeDtypeStruct((B, S2, GW), BF16 if backward else F32),
        scratch_shapes=[pltpu.VMEM((S5_LT, 8, 128), F32),
                        pltpu.VMEM((S5_LT, 8, 128), F32),
                        pltpu.VMEM((2 * S5_LT, 8 * S5_T, 128), F32)],
        compiler_params=_cp(("arbitrary",), 48),
        name="s5_bwd_glu" if backward else "s5_fwd",
    )(*args)


def _outproj_kernel(a0_ref, a1_ref, a2_ref, a3_ref, w_ref, x_ref, mod_ref, g2_ref, wr_ref,
                    x1_ref, h2_ref, aff_ref):
    acts = (a0_ref[...], a1_ref[...], a2_ref[...], a3_ref[...])
    halves = []
    for c in (slice(0, D // 2), slice(D // 2, D)):
        o = jnp.dot(acts[0], w_ref[0:GW, c], preferred_element_type=F32)
        for m in range(1, 4):
            o = o + jnp.dot(acts[m], w_ref[m * GW:(m + 1) * GW, c], preferred_element_type=F32)
        halves.append(o)
    x1 = x_ref[...] + mod_ref[2:3, :] * jnp.concatenate(halves, axis=1)
    x1_ref[...] = x1
    h2 = _modnorm(x1, g2_ref[...], mod_ref[3:4, :], mod_ref[4:5, :]).astype(BF16)
    h2_ref[...] = h2
    lg2 = lax.dot_general(wr_ref[...], h2, _NT_DIMS, preferred_element_type=F32)
    lg = lg2[:N_EXP] + lg2[N_EXP:]
    e = jnp.exp(lg - jnp.max(lg, axis=0, keepdims=True))
    aff_ref[...] = e / jnp.sum(e, axis=0, keepdims=True)


def _outproj(sgu_o, attn_o, ssm_o, conv_o, w_out_bf, x2d, mod, g2, wr2, li, with_ctx):
    ntiles = NT if with_ctx else NLT
    full = (lambda i: i) if with_ctx else _lat_tile
    frow = lambda i: (full(i), 0)
    orow = lambda i: (i, 0)
    return pl.pallas_call(
        _outproj_kernel,
        grid=(ntiles,),
        in_specs=[pl.BlockSpec((TM, GW), frow),
                  pl.BlockSpec((TM, GW), orow),
                  pl.BlockSpec((TM, GW), frow),
                  pl.BlockSpec((TM, GW), frow),
                  pl.BlockSpec((None, D, D), lambda i: (li, 0, 0)),
                  pl.BlockSpec((TM, D), frow),
                  pl.BlockSpec((None, 6, D), lambda i: (_seg_of_tile(full(i)), 0, 0)),
                  pl.BlockSpec((1, D), lambda i: (0, 0)),
                  pl.BlockSpec((2 * N_EXP, D), lambda i: (0, 0))],
        out_specs=[pl.BlockSpec((TM, D), orow),
                   pl.BlockSpec((TM, D), orow),
                   pl.BlockSpec((N_EXP, TM), lambda i: (0, i))],
        out_shape=[jax.ShapeDtypeStruct((ntiles * TM, D), F32),
                   jax.ShapeDtypeStruct((ntiles * TM, D), BF16),
                   jax.ShapeDtypeStruct((N_EXP, ntiles * TM), F32)],
        compiler_params=_cp(("arbitrary",), 48),
        name="outproj",
    )(sgu_o, attn_o, ssm_o, conv_o, w_out_bf, x2d, mod, g2, wr2)


def _one_hot_rows(rank_row, cap):
    slot = lax.broadcasted_iota(jnp.int32, (cap, rank_row.shape[-1]), 0).astype(F32)
    return jnp.where(rank_row == slot, 1.0, 0.0).astype(BF16)


def _select_one(a, tri, cap, rank_ref, w_ref):
    ne, n = a.shape
    bits = pltpu.bitcast(a, jnp.int32)
    thr = jnp.zeros((ne, 1), jnp.int32)
    for bit in range(30, -1, -1):
        cand = thr | (1 << bit)
        cnt = jnp.sum(jnp.where(bits >= cand, 1.0, 0.0), axis=1, keepdims=True)
        thr = jnp.where(cnt >= cap, cand, thr)
    gt = jnp.where(bits > thr, 1.0, 0.0)
    eq = jnp.where(bits == thr, 1.0, 0.0)
    need = cap - jnp.sum(gt, axis=1, keepdims=True)
    eq_before = jnp.dot(eq.astype(BF16), tri, preferred_element_type=F32) - eq
    sel = gt + eq * jnp.where(eq_before < need, 1.0, 0.0)
    rank = jnp.dot(sel.astype(BF16), tri, preferred_element_type=F32) - 1.0
    rank = jnp.where(sel > 0.5, rank, -1.0)
    slot = lax.broadcasted_iota(jnp.int32, (cap, n), 0).astype(F32)
    for e in range(ne):
        rank_ref[e] = rank[e:e + 1, :]
        hit = rank[e:e + 1, :] == slot
        w_ref[e * cap:(e + 1) * cap, :] = jnp.sum(jnp.where(hit, a[e:e + 1, :], 0.0),
                                                  axis=1, keepdims=True)


def _select_kernel(with_ctx, a_ref, tri_ref, *out_refs):
    _select_one(a_ref[:, :SEQ], tri_ref[...], CAP, out_refs[0], out_refs[1])
    if with_ctx:
        _select_one(a_ref[:, SEQ:], tri_ref[:CTX, :CTX], CAP_C, out_refs[2], out_refs[3])


def _select(aff_t, tri, with_ctx):
    out_specs = [pl.BlockSpec((None, N_EXP, 1, SEQ), lambda b: (b, 0, 0, 0)),
                 pl.BlockSpec((None, N_EXP * CAP, 1), lambda b: (b, 0, 0))]
    out_shape = [jax.ShapeDtypeStruct((B, N_EXP, 1, SEQ), F32),
                 jax.ShapeDtypeStruct((B, N_EXP * CAP, 1), F32)]
    if with_ctx:
        out_specs += [pl.BlockSpec((None, N_EXP, 1, CTX), lambda b: (b, 0, 0, 0)),
                      pl.BlockSpec((None, N_EXP * CAP_C, 1), lambda b: (b, 0, 0))]
        out_shape += [jax.ShapeDtypeStruct((B, N_EXP, 1, CTX), F32),
                      jax.ShapeDtypeStruct((B, N_EXP * CAP_C, 1), F32)]
    return pl.pallas_call(
        functools.partial(_select_kernel, with_ctx),
        grid=(B,),
        in_specs=[pl.BlockSpec((N_EXP, S2 if with_ctx else SEQ), lambda b: (0, b)),
                  pl.BlockSpec((SEQ, SEQ), lambda b: (0, 0))],
        out_specs=out_specs,
        out_shape=out_shape,
        compiler_params=_cp(("arbitrary",), 48),
        name="select",
    )(aff_t, tri)


def _gather_kernel(cap, rank_ref, h_ref, o_ref):
    p = jnp.concatenate([_one_hot_rows(rank_ref[j], cap) for j in range(rank_ref.shape[0])], axis=0)
    o_ref[...] = jnp.dot(p, h_ref[...], preferred_element_type=F32).astype(BF16)


def _gather(rank, h3, ctx_only):
    if ctx_only:
        cap, n, ne = CAP_C, CTX, N_EXP
        hmap = lambda b, r: (b, TPB - 1, 0)
    else:
        cap, n, ne = CAP, SEQ, 2
        hmap = lambda b, r: (b, 0, 0)
    return pl.pallas_call(
        functools.partial(_gather_kernel, cap),
        grid=(B, N_EXP // ne),
        in_specs=[pl.BlockSpec((None, ne, 1, n), lambda b, r: (b, r, 0, 0)),
                  pl.BlockSpec((None, n, D), hmap)],
        out_specs=pl.BlockSpec((None, ne * cap, D), lambda b, r: (b, r, 0)),
        out_shape=jax.ShapeDtypeStruct((B, N_EXP * cap, D), BF16),
        compiler_params=_cp(("arbitrary", "arbitrary"), 48),
        name="gather_ctx" if ctx_only else "gather",
    )(rank, h3)


def _ffn_kernel(with_ctx, *refs):
    if with_ctx:
        (x_ref, xc_ref, wg_ref, wu_ref, wd_ref, ws_ref, wsc_ref,
         y_ref, yc_ref, acc, accc) = refs
    else:
        x_ref, wg_ref, wu_ref, wd_ref, ws_ref, y_ref, acc = refs
    f = pl.program_id(1)
    last = FF // FF_T - 1

    def run(kind, xr, wsr, yr, ac, rows, wg, wu, wd):
        x = xr[...].reshape(rows, D)
        gate = jnp.dot(x, wg, preferred_element_type=F32)
        up = jnp.dot(x, wu, preferred_element_type=F32)
        hid = (gate * jax.nn.sigmoid(gate) * up).astype(BF16)
        part = jnp.dot(hid, wd, preferred_element_type=F32)
        if kind == "first":
            ac[...] = part
        elif kind == "mid":
            ac[...] += part
        else:
            y = (ac[...] + part) * wsr[...].reshape(rows, 1)
            yr[...] = y.astype(BF16).reshape(yr.shape)

    def step(kind):
        wg = wg_ref[...].astype(BF16)
        wu = wu_ref[...].astype(BF16)
        wd = wd_ref[...].astype(BF16)
        run(kind, x_ref, ws_ref, y_ref, acc, B * CAP, wg, wu, wd)
        if with_ctx:
            run(kind, xc_ref, wsc_ref, yc_ref, accc, B * CAP_C, wg, wu, wd)

    pl.when(f == 0)(lambda: step("first"))
    pl.when(jnp.logical_and(f > 0, f < last))(lambda: step("mid"))
    pl.when(f == last)(lambda: step("last"))


def _ffn(xs, ws, xc, wsc, w_gate, w_up, w_down, li):
    with_ctx = xc is not None
    in_specs = [pl.BlockSpec((B, CAP, D), lambda e, f: (0, e, 0))]
    args = [xs]
    if with_ctx:
        in_specs.append(pl.BlockSpec((B, CAP_C, D), lambda e, f: (0, e, 0)))
        args.append(xc)
    in_specs += [pl.BlockSpec((None, None, D, FF_T), lambda e, f: (li, e, 0, f)),
                 pl.BlockSpec((None, None, D, FF_T), lambda e, f: (li, e, 0, f)),
                 pl.BlockSpec((None, None, FF_T, D), lambda e, f: (li, e, f, 0)),
                 pl.BlockSpec((B, CAP, 1), lambda e, f: (0, e, 0))]
    args += [w_gate, w_up, w_down, ws]
    out_specs = [pl.BlockSpec((B, CAP, D), lambda e, f: (0, e, 0))]
    out_shape = [jax.ShapeDtypeStruct((B, N_EXP * CAP, D), BF16)]
    scratch = [pltpu.VMEM((B * CAP, D), F32)]
    if with_ctx:
        in_specs.append(pl.BlockSpec((B, CAP_C, 1), lambda e, f: (0, e, 0)))
        args.append(wsc)
        out_specs.append(pl.BlockSpec((B, CAP_C, D), lambda e, f: (0, e, 0)))
        out_shape.append(jax.ShapeDtypeStruct((B, N_EXP * CAP_C, D), BF16))
        scratch.append(pltpu.VMEM((B * CAP_C, D), F32))
    return pl.pallas_call(
        functools.partial(_ffn_kernel, with_ctx),
        grid=(N_EXP, FF // FF_T),
        in_specs=in_specs,
        out_specs=out_specs,
        out_shape=out_shape,
        scratch_shapes=scratch,
        compiler_params=_cp(("arbitrary", "arbitrary"), 56),
        name="ffn",
    )(*args)


_TN_DIMS = (((0,), (0,)), ((), ()))


def _scatter_kernel(with_ctx, final, *refs):
    refs = list(refs)
    p_ref, y_ref = refs[:2]
    pc_ref, yc_ref = refs[2:4] if with_ctx else (None, None)
    rest = refs[4:] if with_ctx else refs[2:]
    x_ref, mod_ref = rest[:2]
    gf_ref = rest[2] if final else None
    o_ref = rest[-1]

    def finish(rank_ref, yr, cap):
        p = jnp.concatenate([_one_hot_rows(rank_ref[e], cap) for e in range(N_EXP)], axis=0)
        upd = lax.dot_general(p, yr[...], _TN_DIMS, preferred_element_type=F32)
        x = x_ref[...] + mod_ref[5:6, :] * upd
        o_ref[...] = _rms(x, gf_ref[...]) if final else x

    if with_ctx:
        t = pl.program_id(1)
        pl.when(t < LPB)(lambda: finish(p_ref, y_ref, CAP))
        pl.when(t == LPB)(lambda: finish(pc_ref, yc_ref, CAP_C))
    else:
        finish(p_ref, y_ref, CAP)


def _scatter(rank, y, rank_c, yc, x3, mod, final_g):
    with_ctx = rank_c is not None
    final = final_g is not None
    tpb = TPB if with_ctx else LPB
    in_specs = [pl.BlockSpec((None, N_EXP, 1, TM), lambda b, t: (b, 0, 0, jnp.minimum(t, LPB - 1))),
                pl.BlockSpec((None, N_EXP * CAP, D), lambda b, t: (b, 0, 0))]
    args = [rank, y]
    if with_ctx:
        in_specs += [pl.BlockSpec((None, N_EXP, 1, CTX), lambda b, t: (b, 0, 0, 0)),
                     pl.BlockSpec((None, N_EXP * CAP_C, D), lambda b, t: (b, 0, 0))]
        args += [rank_c, yc]
    in_specs += [pl.BlockSpec((None, TM, D), lambda b, t: (b, t, 0)),
                 pl.BlockSpec((None, 6, D), lambda b, t: (jnp.where(t == LPB, B, b), 0, 0))]
    args += [x3, mod]
    if final:
        in_specs.append(pl.BlockSpec((1, D), lambda b, t: (0, 0)))
        args.append(final_g)
    return pl.pallas_call(
        functools.partial(_scatter_kernel, with_ctx, final),
        grid=(B, tpb),
        in_specs=in_specs,
        out_specs=pl.BlockSpec((None, TM, D), lambda b, t: (b, t, 0)),
        out_shape=jax.ShapeDtypeStruct((B, tpb * TM, D), F32),
        compiler_params=_cp(("arbitrary", "arbitrary"), 56),
        name="scatter",
    )(*args)


def _rope_tables():
    n_freq = QK_ROPE // 4
    grid_w = 64
    pos = jnp.arange(SEQ, dtype=F32)
    inv_freq = 10000.0 ** (-jnp.arange(n_freq, dtype=F32) / n_freq)
    ang_r = jnp.floor(pos / grid_w)[:, None] * inv_freq
    ang_c = (pos - grid_w * jnp.floor(pos / grid_w))[:, None] * inv_freq
    cr, sr, cc, sc = jnp.cos(ang_r), jnp.sin(ang_r), jnp.cos(ang_c), jnp.sin(ang_c)
    cos = jnp.concatenate([cr, cr, cc, cc], axis=1)
    sin = jnp.concatenate([-sr, sr, -sc, sc], axis=1)
    cos = jnp.concatenate([cos, jnp.ones((CTX, QK_ROPE), F32)], axis=0)
    sin = jnp.concatenate([sin, jnp.zeros((CTX, QK_ROPE), F32)], axis=0)
    z = jnp.zeros((S2, QK_ROPE), F32)
    tq1 = jnp.concatenate([jnp.full((S2, QK_NOPE), ATT_SCALE, F32), cos * ATT_SCALE, z], axis=1)
    tq2 = jnp.concatenate([jnp.zeros((S2, QK_NOPE), F32), sin * ATT_SCALE, z], axis=1)
    tk1 = jnp.concatenate([cos, z], axis=1)
    tk2 = jnp.concatenate([sin, z], axis=1)
    return tq1, tq2, tk1, tk2


def _pair_swap(w):
    return jnp.concatenate([w[..., 16:32], w[..., 0:16], w[..., 48:64], w[..., 32:48]], axis=-1)


def _s5_operators(a_re, a_im, log_dt, b_re, b_im, c_re, c_im):
    a = lax.complex(jnp.minimum(a_re.astype(F32), -1e-4), a_im.astype(F32))
    dt = jnp.exp(log_dt.astype(F32))[..., None]
    abar = jnp.exp(a * dt)
    bbar = ((abar - 1.0) / a)[..., None] * lax.complex(b_re.astype(F32), b_im.astype(F32))
    hg = S5_G // 2
    eye = jnp.eye(hg, dtype=F32)

    def in_op(m):
        m = jnp.transpose(m, (0, 1, 3, 2)).reshape(2, 2, hg, S5_CH, S5_N)
        return jnp.einsum("zhgcn,gk->zhgckn", m, eye).reshape(2, GW, S5_HALF)

    def out_op(m):
        m = m.reshape(2, 2, hg, S5_CH, S5_N)
        return jnp.einsum("zhgcn,gk->zgnhkc", m, eye).reshape(2, S5_HALF, GW)

    w_bd = jnp.concatenate([in_op(jnp.real(bbar)), in_op(jnp.imag(bbar))], axis=-1)
    w_bd = w_bd.reshape(2, 2, GW // 2, 2 * S5_HALF).astype(BF16)
    c_bd = jnp.concatenate([out_op(c_re.astype(F32)), out_op(-c_im.astype(F32))], axis=1)
    c_bd = jnp.transpose(c_bd.reshape(2, 2 * S5_HALF, 2, GW // 2), (0, 2, 1, 3)).astype(BF16)

    def rows8(m):
        m = m.reshape(2, 1, 2, S5_LT, 128)
        m = jnp.broadcast_to(m, (2, B, 2, S5_LT, 128)).reshape(2, 2 * B, S5_LT, 128)
        return jnp.transpose(m, (0, 2, 1, 3))

    return w_bd, rows8(jnp.real(abar)), rows8(jnp.imag(abar)), c_bd


def kernel(x, c, ctx, c_ctx, norm1_g, norm2_g, w_ada, b_ada, w_in, w_out, sgu_norm_g, sgu_w,
           sgu_b, mla_q_norm_g, mla_w_uq, mla_kv_norm_g, mla_w_ukv, s5_a_re, s5_a_im, s5_log_dt,
           s5_b_re, s5_b_im, s5_c_re, s5_c_im, s5_d, s5_w_glu, s5_b_glu, conv_w, moe_w_router,
           moe_w_gate, moe_w_up, moe_w_down, final_norm_g):
    c8 = jnp.concatenate([c, c_ctx[None, :], jnp.zeros((3, D), F32)], axis=0)
    mod_all = _modulation(c8, w_ada, b_ada).reshape(DEPTH, 8, 6, D)
    rope_t = _rope_tables()
    tri = jnp.triu(jnp.ones((SEQ, SEQ), BF16))
    kr0 = 2 * GW + Q_LORA + KV_LORA
    w_in_r = jnp.concatenate([w_in[:, :, :kr0 + QK_ROPE],
                              _pair_swap(w_in[:, :, kr0:kr0 + QK_ROPE]),
                              w_in[:, :, kr0 + QK_ROPE:]], axis=2).astype(BF16)
    w_out_bf = w_out.astype(BF16)
    x3 = jnp.concatenate([x, ctx], axis=1)

    for i in range(DEPTH):
        last = i == DEPTH - 1
        mod = mod_all[i]
        x2d = x3.reshape(B * S2, D)

        wq = mla_w_uq[i].reshape(Q_LORA, MLA_HEADS, QK_NOPE + QK_ROPE)
        wq_r = wq[:, :, QK_NOPE:]
        wq_ext = jnp.concatenate([wq[:, :, :QK_NOPE], wq_r, _pair_swap(wq_r)], axis=2)
        wq_ext = wq_ext.reshape(Q_LORA, MLA_HEADS * QK_PAD).astype(BF16)
        wkv = mla_w_ukv[i].reshape(KV_LORA, MLA_HEADS, 2 * QK_NOPE)
        wkv_ext = jnp.concatenate([wkv[:, :, :QK_NOPE].reshape(KV_LORA, -1),
                                   wkv[:, :, QK_NOPE:].reshape(KV_LORA, -1)], axis=1).astype(BF16)
        sgu_p = (sgu_norm_g[i][None, :], sgu_w[i].astype(BF16),
                 jnp.repeat(jnp.swapaxes(sgu_b[i], 0, 1), 128, axis=1))
        mla_p = (mla_q_norm_g[i][None, :], mla_kv_norm_g[i][None, :], wq_ext, wkv_ext) + rope_t
        sgu_o, q, kc, v, p_s5, p_conv = _inproj(x2d, mod, norm1_g[i][None, :], w_in_r, i,
                                                sgu_p, mla_p)

        conv_o = _conv(p_conv, conv_w[i])
        attn_o = _attention(q, kc.reshape(B, S2, MLA_HEADS * QK_PAD), v.reshape(B, S2, GW), not last)

        w_bd, a_re8, a_im8, c_bd = _s5_operators(s5_a_re[i], s5_a_im[i], s5_log_dt[i], s5_b_re[i],
                                                 s5_b_im[i], s5_c_re[i], s5_c_im[i])
        u3 = p_s5.reshape(B, S2, GW)
        y_fwd = _s5_pass(u3, w_bd, a_re8, a_im8, c_bd, None)
        ssm_o = _s5_pass(u3, w_bd, a_re8, a_im8, c_bd,
                         (y_fwd, s5_d[i][None, :], s5_w_glu[i].astype(BF16), s5_b_glu[i][None, :]))
        ssm_o = ssm_o.reshape(B * S2, GW)

        wr_t = jnp.transpose(moe_w_router[i])
        wr_hi = wr_t.astype(BF16)
        wr2 = jnp.concatenate([wr_hi, (wr_t - wr_hi.astype(F32)).astype(BF16)], axis=0)
        x1, h2, aff_t = _outproj(sgu_o, attn_o, ssm_o, conv_o, w_out_bf, x2d, mod,
                                 norm2_g[i][None, :], wr2, i, not last)

        rows_b = SEQ if last else S2
        sel = _select(aff_t, tri, not last)
        h3 = h2.reshape(B, rows_b, D)
        xs = _gather(sel[0], h3, ctx_only=False)
        xc = _gather(sel[2], h3, ctx_only=True) if not last else None
        ys = _ffn(xs, sel[1], xc, sel[3] if not last else None, moe_w_gate, moe_w_up, moe_w_down, i)
        x1_3 = x1.reshape(B, rows_b, D)
        if last:
            x3 = _scatter(sel[0], ys[0], None, None, x1_3, mod, final_norm_g[None, :])
        else:
            x3 = _scatter(sel[0], ys[0], sel[2], ys[1], x1_3, mod, None)

    return x3
```

```python
import functools

import jax
import jax.numpy as jnp
from jax import lax
from jax.experimental import pallas as pl
from jax.experimental.pallas import tpu as pltpu

F32 = jnp.float32
BF16 = jnp.bfloat16

D = 2048
B = 4
SEQ = 2048
CTX = 256
S2 = SEQ + CTX
DEPTH = 2
GW = 512
EPS = 1e-6

TM = 256
TPB = S2 // TM
LPB = SEQ // TM
NT = B * TPB
NLT = B * LPB

SGU_HEADS = 4
CHUNK = 128
MLA_HEADS = 4
QK_NOPE = 128
QK_ROPE = 64
QK_PAD = 256
Q_LORA = 384
KV_LORA = 256
ATT_SCALE = (QK_NOPE + QK_ROPE) ** -0.5

S5_G = 32
S5_N = 64
S5_CH = 16
S5_T = 128
S5_NCH = S2 // S5_T
S5_HALF = (S5_G // 2) * S5_N
S5_LT = S5_HALF // 128

N_EXP = 16
FF = D // 2
CAP = 2 * SEQ // N_EXP
CAP_C = 2 * CTX // N_EXP
FF_T = 256

IN_W = 3840
MIB = 1024 * 1024


def _cp(sem, vmem_mb):
    return pltpu.CompilerParams(dimension_semantics=sem, vmem_limit_bytes=vmem_mb * MIB)


def _lat_tile(i):
    return (i // LPB) * TPB + i % LPB


def _seg_of_tile(t):
    return jnp.where(t % TPB == TPB - 1, B, t // TPB)


def _rms(x, g):
    return x * lax.rsqrt(jnp.mean(x * x, axis=-1, keepdims=True) + EPS) * g


def _modnorm(x, g, shift, scale):
    return _rms(x, g) * (1.0 + scale) + shift


def _mod_kernel(c_ref, w_ref, b_ref, o_ref):
    a = c_ref[...]
    a = a * jax.nn.sigmoid(a)
    o_ref[...] = jnp.dot(a.astype(BF16), w_ref[...].astype(BF16),
                         preferred_element_type=F32) + b_ref[...]


def _modulation(c8, w_ada, b_ada):
    tn = 1024
    return pl.pallas_call(
        _mod_kernel,
        grid=(DEPTH, 6 * D // tn),
        in_specs=[pl.BlockSpec((8, D), lambda l, j: (0, 0)),
                  pl.BlockSpec((None, D, tn), lambda l, j: (l, 0, j)),
                  pl.BlockSpec((None, 1, tn), lambda l, j: (l, 0, j))],
        out_specs=pl.BlockSpec((None, 8, tn), lambda l, j: (l, 0, j)),
        out_shape=jax.ShapeDtypeStruct((DEPTH, 8, 6 * D), F32),
        compiler_params=_cp(("arbitrary", "arbitrary"), 40),
        name="modulation",
    )(c8, w_ada, b_ada.reshape(DEPTH, 1, 6 * D))


IN_RAW = 3776
KR0 = 2 * GW + Q_LORA + KV_LORA


def _winprep_kernel(w_ref, o_ref):
    w = w_ref[...]
    k1 = KR0 + QK_ROPE
    o_ref[:, :k1] = w[:, :k1].astype(BF16)
    o_ref[:, k1:k1 + QK_ROPE] = _pair_swap(w[:, KR0:k1]).astype(BF16)
    o_ref[:, k1 + QK_ROPE:] = w[:, k1:].astype(BF16)


def _winprep(w_in):
    tr = 256
    return pl.pallas_call(
        _winprep_kernel,
        grid=(DEPTH, D // tr),
        in_specs=[pl.BlockSpec((None, tr, IN_RAW), lambda l, r: (l, r, 0))],
        out_specs=pl.BlockSpec((None, tr, IN_W), lambda l, r: (l, r, 0)),
        out_shape=jax.ShapeDtypeStruct((DEPTH, D, IN_W), BF16),
        compiler_params=_cp(("arbitrary", "arbitrary"), 32),
        name="winprep",
    )(w_in)


def _stream_rows(refs, tile):
    if len(refs) == 1:
        return refs[0][...]
    return jnp.where(tile % TPB == TPB - 1, refs[1][...], refs[0][...])


def _stream_specs(xs):
    if len(xs) == 1:
        return lambda full: [pl.BlockSpec((TM, D), lambda i: (full(i), 0))]
    lat = lambda t: (t // TPB) * LPB + jnp.minimum(t % TPB, LPB - 1)
    return lambda full: [pl.BlockSpec((TM, D), lambda i: (lat(full(i)), 0)),
                         pl.BlockSpec((CTX, D), lambda i: (full(i) // TPB, 0))]


def _inproj_kernel(nx, *refs):
    (mod_ref, g_ref, w_ref, sg_ref, sw_ref, sb_ref,
     gq_ref, gkv_ref, wq_ref, wkv_ref, tq1_ref, tq2_ref, tk1_ref, tk2_ref,
     sgu_ref, q_ref, kc_ref, v_ref, s5_ref, conv_ref) = refs[nx:]
    x = _stream_rows(refs[:nx], pl.program_id(0))
    h = _modnorm(x, g_ref[...], mod_ref[0:1, :], mod_ref[1:2, :]).astype(BF16)

    def mm(a, b):
        return jnp.dot(h, w_ref[:, a:b], preferred_element_type=F32)

    p = jax.nn.gelu(mm(0, 2 * GW))
    u = p[:, :GW]
    vb = _rms(p[:, GW:], sg_ref[...]).astype(BF16)
    for ck in range(TM // CHUNK):
        r = slice(ck * CHUNK, (ck + 1) * CHUNK)
        for hd in range(SGU_HEADS):
            c = slice(hd * 128, (hd + 1) * 128)
            m = jnp.dot(sw_ref[hd], vb[r, c], preferred_element_type=F32)
            sgu_ref[r, c] = (u[r, c] * (m + sb_ref[:, c])).astype(BF16)

    pm = mm(2 * GW, 2 * GW + 768)
    cq = _rms(pm[:, :Q_LORA], gq_ref[...]).astype(BF16)
    q = jnp.dot(cq, wq_ref[...], preferred_element_type=F32)
    tq1 = tq1_ref[...]
    tq2 = tq2_ref[...]
    for hd in range(MLA_HEADS):
        c = slice(hd * QK_PAD, (hd + 1) * QK_PAD)
        blk = q[:, c]
        q_ref[:, c] = (blk * tq1 + pltpu.roll(blk, QK_PAD - QK_ROPE, 1) * tq2).astype(BF16)
    ckv = _rms(pm[:, Q_LORA:Q_LORA + KV_LORA], gkv_ref[...]).astype(BF16)
    kv = jnp.dot(ckv, wkv_ref[...], preferred_element_type=F32)
    v_ref[...] = kv[:, GW:].astype(BF16)
    kt = pm[:, Q_LORA + KV_LORA:]
    kr = (kt * tk1_ref[...] + pltpu.roll(kt, QK_ROPE, 1) * tk2_ref[...]).astype(BF16)
    for hd in range(MLA_HEADS):
        kc_ref[:, hd * QK_PAD:hd * QK_PAD + QK_NOPE] = kv[:, hd * 128:(hd + 1) * 128].astype(BF16)
        kc_ref[:, hd * QK_PAD + QK_NOPE:(hd + 1) * QK_PAD] = kr

    s5_ref[...] = mm(1792, 2304)
    conv_ref[:, :GW] = mm(2304, 2816)
    conv_ref[:, GW:] = mm(2816, 3328) * mm(3328, 3840)


def _inproj(xs, mod, g1, w_in_r, li, sgu_p, mla_p):
    fix2 = lambda i: (0, 0)
    pos = lambda i: (i % TPB, 0)
    row = lambda i: (i, 0)
    qkw = MLA_HEADS * QK_PAD
    return pl.pallas_call(
        functools.partial(_inproj_kernel, len(xs)),
        grid=(NT,),
        in_specs=_stream_specs(xs)(lambda i: i) + [
                  pl.BlockSpec((None, 6, D), lambda i: (_seg_of_tile(i), 0, 0)),
                  pl.BlockSpec((1, D), fix2),
                  pl.BlockSpec((None, D, IN_W), lambda i: (li, 0, 0)),
                  pl.BlockSpec((1, GW), fix2),
                  pl.BlockSpec((SGU_HEADS, CHUNK, CHUNK), lambda i: (0, 0, 0)),
                  pl.BlockSpec((CHUNK, GW), fix2),
                  pl.BlockSpec((1, Q_LORA), fix2),
                  pl.BlockSpec((1, KV_LORA), fix2),
                  pl.BlockSpec((Q_LORA, qkw), fix2),
                  pl.BlockSpec((KV_LORA, 2 * GW), fix2),
                  pl.BlockSpec((TM, QK_PAD), pos),
                  pl.BlockSpec((TM, QK_PAD), pos),
                  pl.BlockSpec((TM, 128), pos),
                  pl.BlockSpec((TM, 128), pos)],
        out_specs=[pl.BlockSpec((TM, GW), row),
                   pl.BlockSpec((TM, qkw), row),
                   pl.BlockSpec((TM, qkw), row),
                   pl.BlockSpec((TM, GW), row),
                   pl.BlockSpec((TM, GW), row),
                   pl.BlockSpec((TM, 2 * GW), row)],
        out_shape=[jax.ShapeDtypeStruct((B * S2, GW), BF16),
                   jax.ShapeDtypeStruct((B * S2, qkw), BF16),
                   jax.ShapeDtypeStruct((B * S2, qkw), BF16),
                   jax.ShapeDtypeStruct((B * S2, GW), BF16),
                   jax.ShapeDtypeStruct((B * S2, GW), F32),
                   jax.ShapeDtypeStruct((B * S2, 2 * GW), F32)],
        compiler_params=_cp(("arbitrary",), 56),
        name="inproj",
    )(*xs, mod, g1, w_in_r, *sgu_p, *mla_p)


def _conv_kernel(p_ref, zp_ref, zn_ref, w_ref, o_ref):
    r = pl.program_id(0) % TPB
    bg = p_ref[:, :GW]
    z = p_ref[:, GW:]
    row = lax.broadcasted_iota(jnp.int32, (TM, GW), 0)
    has_prev = jnp.logical_and(r != 0, r != TPB - 1)
    has_next = r < LPB - 1
    prev_row = zp_ref[7:8, :] * has_prev.astype(F32)
    next_row = zn_ref[0:1, :] * has_next.astype(F32)
    zm = jnp.where(row == 0, prev_row, pltpu.roll(z, 1, 0))
    zp = jnp.where(row == TM - 1, next_row, pltpu.roll(z, TM - 1, 0))
    y = w_ref[0:1, :] * zm + w_ref[1:2, :] * z + w_ref[2:3, :] * zp
    o_ref[...] = (bg * y).astype(BF16)


def _conv(p_conv, conv_w):
    rb = TM // 8
    nrb = B * S2 // 8
    return pl.pallas_call(
        _conv_kernel,
        grid=(NT,),
        in_specs=[pl.BlockSpec((TM, 2 * GW), lambda i: (i, 0)),
                  pl.BlockSpec((8, GW), lambda i: (jnp.maximum(i * rb - 1, 0), 1)),
                  pl.BlockSpec((8, GW), lambda i: (jnp.minimum((i + 1) * rb, nrb - 1), 1)),
                  pl.BlockSpec((3, GW), lambda i: (0, 0))],
        out_specs=pl.BlockSpec((TM, GW), lambda i: (i, 0)),
        out_shape=jax.ShapeDtypeStruct((B * S2, GW), BF16),
        compiler_params=_cp(("arbitrary",), 32),
        name="conv",
    )(p_conv, p_conv, p_conv, conv_w)


_NT_DIMS = (((1,), (1,)), ((), ()))


def _attn_kernel(with_ctx, q_ref, kc_ref, v_ref, o_ref):
    def run(k0):
        for hd in range(MLA_HEADS):
            c = slice(hd * 128, (hd + 1) * 128)
            cq = slice(hd * QK_PAD, (hd + 1) * QK_PAD)
            s = lax.dot_general(q_ref[:, cq], kc_ref[k0:, cq], _NT_DIMS, preferred_element_type=F32)
            m = jnp.max(s, axis=-1, keepdims=True)
            e = jnp.exp(s - m)
            l = jnp.sum(e, axis=-1, keepdims=True)
            o = jnp.dot(e.astype(BF16), v_ref[k0:, c], preferred_element_type=F32)
            o_ref[:, c] = (o / l).astype(BF16)

    if with_ctx:
        t = pl.program_id(1)
        pl.when(t < LPB)(lambda: run(0))
        pl.when(t == LPB)(lambda: run(SEQ))
    else:
        run(0)


def _attention(q, kc3, v3, with_ctx):
    tpb = TPB if with_ctx else LPB
    qkw = MLA_HEADS * QK_PAD
    return pl.pallas_call(
        functools.partial(_attn_kernel, with_ctx),
        grid=(B, tpb),
        in_specs=[pl.BlockSpec((TM, qkw), lambda b, t: (b * TPB + t, 0)),
                  pl.BlockSpec((None, S2, qkw), lambda b, t: (b, 0, 0)),
                  pl.BlockSpec((None, S2, GW), lambda b, t: (b, 0, 0))],
        out_specs=pl.BlockSpec((TM, GW), lambda b, t: (b * tpb + t, 0)),
        out_shape=jax.ShapeDtypeStruct((B * tpb * TM, GW), BF16),
        compiler_params=_cp(("arbitrary", "arbitrary"), 48),
        name="attn",
    )(q, kc3, v3)


def _s5_kernel(backward, *refs):
    if backward:
        (u_ref, w_ref, are_ref, aim_ref, c_ref, yf_ref, d_ref, wg_ref, bg_ref,
         o_ref, st_re, st_im, buf) = refs
    else:
        u_ref, w_ref, are_ref, aim_ref, c_ref, o_ref, st_re, st_im, buf = refs
    k = pl.program_id(0)
    hw = GW // 2
    rows = B * S5_T
    nlt = S5_LT

    @pl.when(k == 0)
    def _():
        st_re[...] = jnp.zeros_like(st_re)
        st_im[...] = jnp.zeros_like(st_im)

    u = u_ref[...].reshape(rows, GW)
    ub = u.astype(BF16)
    for h in range(2):
        bu = jnp.dot(ub[:, h * hw:(h + 1) * hw], w_ref[h], preferred_element_type=F32)
        for b in range(B):
            for c in range(2 * nlt):
                buf[c, pl.ds(2 * b + h, S5_T, stride=8), :] = (
                    bu[b * S5_T:(b + 1) * S5_T, c * 128:(c + 1) * 128])
    a_re = are_ref[...]
    a_im = aim_ref[...]

    def step(j, carry):
        sr, si = carry
        r0 = pl.multiple_of((S5_T - 1 - j if backward else j) * 8, 8)
        br = buf[0:nlt, pl.ds(r0, 8), :]
        bi = buf[nlt:2 * nlt, pl.ds(r0, 8), :]
        nr = a_re * sr - a_im * si + br
        ni = a_re * si + a_im * sr + bi
        buf[0:nlt, pl.ds(r0, 8), :] = nr
        buf[nlt:2 * nlt, pl.ds(r0, 8), :] = ni
        return nr, ni

    sr, si = lax.fori_loop(0, S5_T, step, (st_re[...], st_im[...]), unroll=4)
    st_re[...] = sr
    st_im[...] = si

    ys = []
    for h in range(2):
        s = jnp.concatenate(
            [jnp.concatenate([buf[c, pl.ds(2 * b + h, S5_T, stride=8), :] for c in range(2 * nlt)],
                             axis=1) for b in range(B)], axis=0)
        ys.append(jnp.dot(s.astype(BF16), c_ref[h], preferred_element_type=F32))
    y = jnp.concatenate(ys, axis=1)
    if backward:
        y = y + yf_ref[...].reshape(rows, GW) + d_ref[...] * u
        g = jax.nn.gelu(y)
        z = jnp.dot(g.astype(BF16), wg_ref[...], preferred_element_type=F32) + bg_ref[...]
        o_ref[...] = (g * jax.nn.sigmoid(z)).astype(BF16).reshape(B, S5_T, GW)
    else:
        o_ref[...] = y.reshape(B, S5_T, GW)


def _s5_pass(u3, w_bd, a_re8, a_im8, c_bd, glu):
    backward = glu is not None
    dr = 1 if backward else 0
    if backward:
        blk = lambda k: (0, S5_NCH - 1 - k, 0)
    else:
        blk = lambda k: (0, (k + SEQ // S5_T) % S5_NCH, 0)
    fixed = lambda k: (dr, 0, 0, 0)
    in_specs = [pl.BlockSpec((B, S5_T, GW), blk),
                pl.BlockSpec((None, 2, GW // 2, 2 * S5_HALF), fixed),
                pl.BlockSpec((None, S5_LT, 8, 128), fixed),
                pl.BlockSpec((None, S5_LT, 8, 128), fixed),
                pl.BlockSpec((None, 2, 2 * S5_HALF, GW // 2), fixed)]
    args = [u3, w_bd, a_re8, a_im8, c_bd]
    if backward:
        in_specs += [pl.BlockSpec((B, S5_T, GW), blk),
                     pl.BlockSpec((1, GW), lambda k: (0, 0)),
                     pl.BlockSpec((GW, GW), lambda k: (0, 0)),
                     pl.BlockSpec((1, GW), lambda k: (0, 0))]
        args += list(glu)
    return pl.pallas_call(
        functools.partial(_s5_kernel, backward),
        grid=(S5_NCH,),
        in_specs=in_specs,
        out_specs=pl.BlockSpec((B, S5_T, GW), blk),
        out_shape=jax.ShapeDtypeStruct((B, S2, GW), BF16 if backward else F32),
        scratch_shapes=[pltpu.VMEM((S5_LT, 8, 128), F32),
                        pltpu.VMEM((S5_LT, 8, 128), F32),
                        pltpu.VMEM((2 * S5_LT, 8 * S5_T, 128), F32)],
        compiler_params=_cp(("arbitrary",), 48),
        name="s5_bwd_glu" if backward else "s5_fwd",
    )(*args)


def _outproj_kernel(nx, full, *refs):
    (a0_ref, a1_ref, a2_ref, a3_ref, w_ref, mod_ref, g2_ref, wr_ref,
     x1_ref, h2_ref, aff_ref) = refs[nx:]
    acts = (a0_ref[...], a1_ref[...], a2_ref[...], a3_ref[...])
    halves = []
    for c in (slice(0, D // 2), slice(D // 2, D)):
        o = jnp.dot(acts[0], w_ref[0:GW, c], preferred_element_type=F32)
        for m in range(1, 4):
            o = o + jnp.dot(acts[m], w_ref[m * GW:(m + 1) * GW, c], preferred_element_type=F32)
        halves.append(o)
    x = _stream_rows(refs[:nx], full(pl.program_id(0)))
    x1 = x + mod_ref[2:3, :] * jnp.concatenate(halves, axis=1)
    x1_ref[...] = x1
    h2 = _modnorm(x1, g2_ref[...], mod_ref[3:4, :], mod_ref[4:5, :]).astype(BF16)
    h2_ref[...] = h2
    lg2 = lax.dot_general(wr_ref[...], h2, _NT_DIMS, preferred_element_type=F32)
    lg = lg2[:N_EXP] + lg2[N_EXP:]
    e = jnp.exp(lg - jnp.max(lg, axis=0, keepdims=True))
    aff_ref[...] = e / jnp.sum(e, axis=0, keepdims=True)


def _outproj(sgu_o, attn_o, ssm_o, conv_o, w_out_bf, xs, mod, g2, wr2, li, with_ctx):
    ntiles = NT if with_ctx else NLT
    full = (lambda i: i) if with_ctx else _lat_tile
    frow = lambda i: (full(i), 0)
    orow = lambda i: (i, 0)
    return pl.pallas_call(
        functools.partial(_outproj_kernel, len(xs), full),
        grid=(ntiles,),
        in_specs=_stream_specs(xs)(full) + [
                  pl.BlockSpec((TM, GW), frow),
                  pl.BlockSpec((TM, GW), orow),
                  pl.BlockSpec((TM, GW), frow),
                  pl.BlockSpec((TM, GW), frow),
                  pl.BlockSpec((None, D, D), lambda i: (li, 0, 0)),
                  pl.BlockSpec((None, 6, D), lambda i: (_seg_of_tile(full(i)), 0, 0)),
                  pl.BlockSpec((1, D), lambda i: (0, 0)),
                  pl.BlockSpec((2 * N_EXP, D), lambda i: (0, 0))],
        out_specs=[pl.BlockSpec((TM, D), orow),
                   pl.BlockSpec((TM, D), orow),
                   pl.BlockSpec((N_EXP, TM), lambda i: (0, i))],
        out_shape=[jax.ShapeDtypeStruct((ntiles * TM, D), F32),
                   jax.ShapeDtypeStruct((ntiles * TM, D), BF16),
                   jax.ShapeDtypeStruct((N_EXP, ntiles * TM), F32)],
        compiler_params=_cp(("arbitrary",), 48),
        name="outproj",
    )(*xs, sgu_o, attn_o, ssm_o, conv_o, w_out_bf, mod, g2, wr2)


def _one_hot_rows(rank_row, cap):
    slot = lax.broadcasted_iota(jnp.int32, (cap, rank_row.shape[-1]), 0).astype(F32)
    return jnp.where(rank_row == slot, 1.0, 0.0).astype(BF16)


def _select_one(a, tri, cap, rank_ref, w_ref):
    ne, n = a.shape
    bits = pltpu.bitcast(a, jnp.int32)
    thr = jnp.zeros((ne, 1), jnp.int32)
    for bit in range(30, -1, -1):
        cand = thr | (1 << bit)
        cnt = jnp.sum(jnp.where(bits >= cand, 1.0, 0.0), axis=1, keepdims=True)
        thr = jnp.where(cnt >= cap, cand, thr)
    gt = jnp.where(bits > thr, 1.0, 0.0)
    eq = jnp.where(bits == thr, 1.0, 0.0)
    need = cap - jnp.sum(gt, axis=1, keepdims=True)
    eq_before = jnp.dot(eq.astype(BF16), tri, preferred_element_type=F32) - eq
    sel = gt + eq * jnp.where(eq_before < need, 1.0, 0.0)
    rank = jnp.dot(sel.astype(BF16), tri, preferred_element_type=F32) - 1.0
    rank = jnp.where(sel > 0.5, rank, -1.0)
    slot = lax.broadcasted_iota(jnp.int32, (cap, n), 0).astype(F32)
    for e in range(ne):
        rank_ref[e] = rank[e:e + 1, :]
        hit = rank[e:e + 1, :] == slot
        w_ref[e * cap:(e + 1) * cap, :] = jnp.sum(jnp.where(hit, a[e:e + 1, :], 0.0),
                                                  axis=1, keepdims=True)


def _select_kernel(with_ctx, a_ref, tri_ref, *out_refs):
    _select_one(a_ref[:, :SEQ], tri_ref[...], CAP, out_refs[0], out_refs[1])
    if with_ctx:
        _select_one(a_ref[:, SEQ:], tri_ref[:CTX, :CTX], CAP_C, out_refs[2], out_refs[3])


def _select(aff_t, tri, with_ctx):
    out_specs = [pl.BlockSpec((None, N_EXP, 1, SEQ), lambda b: (b, 0, 0, 0)),
                 pl.BlockSpec((None, N_EXP * CAP, 1), lambda b: (b, 0, 0))]
    out_shape = [jax.ShapeDtypeStruct((B, N_EXP, 1, SEQ), F32),
                 jax.ShapeDtypeStruct((B, N_EXP * CAP, 1), F32)]
    if with_ctx:
        out_specs += [pl.BlockSpec((None, N_EXP, 1, CTX), lambda b: (b, 0, 0, 0)),
                      pl.BlockSpec((None, N_EXP * CAP_C, 1), lambda b: (b, 0, 0))]
        out_shape += [jax.ShapeDtypeStruct((B, N_EXP, 1, CTX), F32),
                      jax.ShapeDtypeStruct((B, N_EXP * CAP_C, 1), F32)]
    return pl.pallas_call(
        functools.partial(_select_kernel, with_ctx),
        grid=(B,),
        in_specs=[pl.BlockSpec((N_EXP, S2 if with_ctx else SEQ), lambda b: (0, b)),
                  pl.BlockSpec((SEQ, SEQ), lambda b: (0, 0))],
        out_specs=out_specs,
        out_shape=out_shape,
        compiler_params=_cp(("arbitrary",), 48),
        name="select",
    )(aff_t, tri)


def _gather_kernel(cap, rank_ref, h_ref, o_ref):
    p = jnp.concatenate([_one_hot_rows(rank_ref[j], cap) for j in range(rank_ref.shape[0])], axis=0)
    o_ref[...] = jnp.dot(p, h_ref[...], preferred_element_type=F32).astype(BF16)


def _gather(rank, h3, ctx_only):
    if ctx_only:
        cap, n, ne = CAP_C, CTX, N_EXP
        hmap = lambda b, r: (b, TPB - 1, 0)
    else:
        cap, n, ne = CAP, SEQ, 2
        hmap = lambda b, r: (b, 0, 0)
    return pl.pallas_call(
        functools.partial(_gather_kernel, cap),
        grid=(B, N_EXP // ne),
        in_specs=[pl.BlockSpec((None, ne, 1, n), lambda b, r: (b, r, 0, 0)),
                  pl.BlockSpec((None, n, D), hmap)],
        out_specs=pl.BlockSpec((None, ne * cap, D), lambda b, r: (b, r, 0)),
        out_shape=jax.ShapeDtypeStruct((B, N_EXP * cap, D), BF16),
        compiler_params=_cp(("arbitrary", "arbitrary"), 48),
        name="gather_ctx" if ctx_only else "gather",
    )(rank, h3)


def _ffn_kernel(with_ctx, *refs):
    if with_ctx:
        (x_ref, xc_ref, wg_ref, wu_ref, wd_ref, ws_ref, wsc_ref,
         y_ref, yc_ref, acc, accc) = refs
    else:
        x_ref, wg_ref, wu_ref, wd_ref, ws_ref, y_ref, acc = refs
    f = pl.program_id(1)
    last = FF // FF_T - 1

    def run(kind, xr, wsr, yr, ac, rows, wg, wu, wd):
        x = xr[...].reshape(rows, D)
        gate = jnp.dot(x, wg, preferred_element_type=F32)
        up = jnp.dot(x, wu, preferred_element_type=F32)
        hid = (gate * jax.nn.sigmoid(gate) * up).astype(BF16)
        part = jnp.dot(hid, wd, preferred_element_type=F32)
        if kind == "first":
            ac[...] = part
        elif kind == "mid":
            ac[...] += part
        else:
            y = (ac[...] + part) * wsr[...].reshape(rows, 1)
            yr[...] = y.astype(BF16).reshape(yr.shape)

    def step(kind):
        wg = wg_ref[...].astype(BF16)
        wu = wu_ref[...].astype(BF16)
        wd = wd_ref[...].astype(BF16)
        run(kind, x_ref, ws_ref, y_ref, acc, B * CAP, wg, wu, wd)
        if with_ctx:
            run(kind, xc_ref, wsc_ref, yc_ref, accc, B * CAP_C, wg, wu, wd)

    pl.when(f == 0)(lambda: step("first"))
    pl.when(jnp.logical_and(f > 0, f < last))(lambda: step("mid"))
    pl.when(f == last)(lambda: step("last"))


def _ffn(xs, ws, xc, wsc, w_gate, w_up, w_down, li):
    with_ctx = xc is not None
    in_specs = [pl.BlockSpec((B, CAP, D), lambda e, f: (0, e, 0))]
    args = [xs]
    if with_ctx:
        in_specs.append(pl.BlockSpec((B, CAP_C, D), lambda e, f: (0, e, 0)))
        args.append(xc)
    in_specs += [pl.BlockSpec((None, None, D, FF_T), lambda e, f: (li, e, 0, f)),
                 pl.BlockSpec((None, None, D, FF_T), lambda e, f: (li, e, 0, f)),
                 pl.BlockSpec((None, None, FF_T, D), lambda e, f: (li, e, f, 0)),
                 pl.BlockSpec((B, CAP, 1), lambda e, f: (0, e, 0))]
    args += [w_gate, w_up, w_down, ws]
    out_specs = [pl.BlockSpec((B, CAP, D), lambda e, f: (0, e, 0))]
    out_shape = [jax.ShapeDtypeStruct((B, N_EXP * CAP, D), BF16)]
    scratch = [pltpu.VMEM((B * CAP, D), F32)]
    if with_ctx:
        in_specs.append(pl.BlockSpec((B, CAP_C, 1), lambda e, f: (0, e, 0)))
        args.append(wsc)
        out_specs.append(pl.BlockSpec((B, CAP_C, D), lambda e, f: (0, e, 0)))
        out_shape.append(jax.ShapeDtypeStruct((B, N_EXP * CAP_C, D), BF16))
        scratch.append(pltpu.VMEM((B * CAP_C, D), F32))
    return pl.pallas_call(
        functools.partial(_ffn_kernel, with_ctx),
        grid=(N_EXP, FF // FF_T),
        in_specs=in_specs,
        out_specs=out_specs,
        out_shape=out_shape,
        scratch_shapes=scratch,
        compiler_params=_cp(("arbitrary", "arbitrary"), 56),
        name="ffn",
    )(*args)


_TN_DIMS = (((0,), (0,)), ((), ()))


def _scatter_kernel(with_ctx, final, *refs):
    refs = list(refs)
    p_ref, y_ref = refs[:2]
    pc_ref, yc_ref = refs[2:4] if with_ctx else (None, None)
    rest = refs[4:] if with_ctx else refs[2:]
    x_ref, mod_ref = rest[:2]
    gf_ref = rest[2] if final else None
    o_ref = rest[-1]

    def finish(rank_ref, yr, cap):
        p = jnp.concatenate([_one_hot_rows(rank_ref[e], cap) for e in range(N_EXP)], axis=0)
        upd = lax.dot_general(p, yr[...], _TN_DIMS, preferred_element_type=F32)
        x = x_ref[...] + mod_ref[5:6, :] * upd
        o_ref[...] = _rms(x, gf_ref[...]) if final else x

    if with_ctx:
        t = pl.program_id(1)
        pl.when(t < LPB)(lambda: finish(p_ref, y_ref, CAP))
        pl.when(t == LPB)(lambda: finish(pc_ref, yc_ref, CAP_C))
    else:
        finish(p_ref, y_ref, CAP)


def _scatter(rank, y, rank_c, yc, x3, mod, final_g):
    with_ctx = rank_c is not None
    final = final_g is not None
    tpb = TPB if with_ctx else LPB
    in_specs = [pl.BlockSpec((None, N_EXP, 1, TM), lambda b, t: (b, 0, 0, jnp.minimum(t, LPB - 1))),
                pl.BlockSpec((None, N_EXP * CAP, D), lambda b, t: (b, 0, 0))]
    args = [rank, y]
    if with_ctx:
        in_specs += [pl.BlockSpec((None, N_EXP, 1, CTX), lambda b, t: (b, 0, 0, 0)),
                     pl.BlockSpec((None, N_EXP * CAP_C, D), lambda b, t: (b, 0, 0))]
        args += [rank_c, yc]
    in_specs += [pl.BlockSpec((None, TM, D), lambda b, t: (b, t, 0)),
                 pl.BlockSpec((None, 6, D), lambda b, t: (jnp.where(t == LPB, B, b), 0, 0))]
    args += [x3, mod]
    if final:
        in_specs.append(pl.BlockSpec((1, D), lambda b, t: (0, 0)))
        args.append(final_g)
    return pl.pallas_call(
        functools.partial(_scatter_kernel, with_ctx, final),
        grid=(B, tpb),
        in_specs=in_specs,
        out_specs=pl.BlockSpec((None, TM, D), lambda b, t: (b, t, 0)),
        out_shape=jax.ShapeDtypeStruct((B, tpb * TM, D), F32),
        compiler_params=_cp(("arbitrary", "arbitrary"), 56),
        name="scatter",
    )(*args)


def _rope_tables():
    n_freq = QK_ROPE // 4
    grid_w = 64
    pos = jnp.arange(SEQ, dtype=F32)
    inv_freq = 10000.0 ** (-jnp.arange(n_freq, dtype=F32) / n_freq)
    ang_r = jnp.floor(pos / grid_w)[:, None] * inv_freq
    ang_c = (pos - grid_w * jnp.floor(pos / grid_w))[:, None] * inv_freq
    cr, sr, cc, sc = jnp.cos(ang_r), jnp.sin(ang_r), jnp.cos(ang_c), jnp.sin(ang_c)
    cos = jnp.concatenate([cr, cr, cc, cc], axis=1)
    sin = jnp.concatenate([-sr, sr, -sc, sc], axis=1)
    cos = jnp.concatenate([cos, jnp.ones((CTX, QK_ROPE), F32)], axis=0)
    sin = jnp.concatenate([sin, jnp.zeros((CTX, QK_ROPE), F32)], axis=0)
    z = jnp.zeros((S2, QK_ROPE), F32)
    tq1 = jnp.concatenate([jnp.full((S2, QK_NOPE), ATT_SCALE, F32), cos * ATT_SCALE, z], axis=1)
    tq2 = jnp.concatenate([jnp.zeros((S2, QK_NOPE), F32), sin * ATT_SCALE, z], axis=1)
    tk1 = jnp.concatenate([cos, z], axis=1)
    tk2 = jnp.concatenate([sin, z], axis=1)
    return tq1, tq2, tk1, tk2


def _pair_swap(w):
    return jnp.concatenate([w[..., 16:32], w[..., 0:16], w[..., 48:64], w[..., 32:48]], axis=-1)


def _s5_operators(a_re, a_im, log_dt, b_re, b_im, c_re, c_im):
    a = lax.complex(jnp.minimum(a_re.astype(F32), -1e-4), a_im.astype(F32))
    dt = jnp.exp(log_dt.astype(F32))[..., None]
    abar = jnp.exp(a * dt)
    bbar = ((abar - 1.0) / a)[..., None] * lax.complex(b_re.astype(F32), b_im.astype(F32))
    hg = S5_G // 2
    eye = jnp.eye(hg, dtype=F32)

    def in_op(m):
        m = jnp.transpose(m, (0, 1, 3, 2)).reshape(2, 2, hg, S5_CH, S5_N)
        return jnp.einsum("zhgcn,gk->zhgckn", m, eye).reshape(2, GW, S5_HALF)

    def out_op(m):
        m = m.reshape(2, 2, hg, S5_CH, S5_N)
        return jnp.einsum("zhgcn,gk->zgnhkc", m, eye).reshape(2, S5_HALF, GW)

    w_bd = jnp.concatenate([in_op(jnp.real(bbar)), in_op(jnp.imag(bbar))], axis=-1)
    w_bd = w_bd.reshape(2, 2, GW // 2, 2 * S5_HALF).astype(BF16)
    c_bd = jnp.concatenate([out_op(c_re.astype(F32)), out_op(-c_im.astype(F32))], axis=1)
    c_bd = jnp.transpose(c_bd.reshape(2, 2 * S5_HALF, 2, GW // 2), (0, 2, 1, 3)).astype(BF16)

    def rows8(m):
        m = m.reshape(2, 1, 2, S5_LT, 128)
        m = jnp.broadcast_to(m, (2, B, 2, S5_LT, 128)).reshape(2, 2 * B, S5_LT, 128)
        return jnp.transpose(m, (0, 2, 1, 3))

    return w_bd, rows8(jnp.real(abar)), rows8(jnp.imag(abar)), c_bd


def kernel(x, c, ctx, c_ctx, norm1_g, norm2_g, w_ada, b_ada, w_in, w_out, sgu_norm_g, sgu_w,
           sgu_b, mla_q_norm_g, mla_w_uq, mla_kv_norm_g, mla_w_ukv, s5_a_re, s5_a_im, s5_log_dt,
           s5_b_re, s5_b_im, s5_c_re, s5_c_im, s5_d, s5_w_glu, s5_b_glu, conv_w, moe_w_router,
           moe_w_gate, moe_w_up, moe_w_down, final_norm_g):
    c8 = jnp.concatenate([c, c_ctx[None, :], jnp.zeros((3, D), F32)], axis=0)
    mod_all = _modulation(c8, w_ada, b_ada).reshape(DEPTH, 8, 6, D)
    rope_t = _rope_tables()
    tri = jnp.triu(jnp.ones((SEQ, SEQ), BF16))
    w_in_r = _winprep(w_in)
    w_out_bf = w_out.astype(BF16)
    xs = (x.reshape(B * SEQ, D), ctx.reshape(B * CTX, D))

    for i in range(DEPTH):
        last = i == DEPTH - 1
        mod = mod_all[i]
        if i > 0:
            xs = (x3.reshape(B * S2, D),)

        wq = mla_w_uq[i].reshape(Q_LORA, MLA_HEADS, QK_NOPE + QK_ROPE)
        wq_r = wq[:, :, QK_NOPE:]
        wq_ext = jnp.concatenate([wq[:, :, :QK_NOPE], wq_r, _pair_swap(wq_r)], axis=2)
        wq_ext = wq_ext.reshape(Q_LORA, MLA_HEADS * QK_PAD).astype(BF16)
        wkv = mla_w_ukv[i].reshape(KV_LORA, MLA_HEADS, 2 * QK_NOPE)
        wkv_ext = jnp.concatenate([wkv[:, :, :QK_NOPE].reshape(KV_LORA, -1),
                                   wkv[:, :, QK_NOPE:].reshape(KV_LORA, -1)], axis=1).astype(BF16)
        sgu_p = (sgu_norm_g[i][None, :], sgu_w[i].astype(BF16),
                 jnp.repeat(jnp.swapaxes(sgu_b[i], 0, 1), 128, axis=1))
        mla_p = (mla_q_norm_g[i][None, :], mla_kv_norm_g[i][None, :], wq_ext, wkv_ext) + rope_t
        sgu_o, q, kc, v, p_s5, p_conv = _inproj(xs, mod, norm1_g[i][None, :], w_in_r, i,
                                                sgu_p, mla_p)

        conv_o = _conv(p_conv, conv_w[i])
        attn_o = _attention(q, kc.reshape(B, S2, MLA_HEADS * QK_PAD), v.reshape(B, S2, GW), not last)

        w_bd, a_re8, a_im8, c_bd = _s5_operators(s5_a_re[i], s5_a_im[i], s5_log_dt[i], s5_b_re[i],
                                                 s5_b_im[i], s5_c_re[i], s5_c_im[i])
        u3 = p_s5.reshape(B, S2, GW)
        y_fwd = _s5_pass(u3, w_bd, a_re8, a_im8, c_bd, None)
        ssm_o = _s5_pass(u3, w_bd, a_re8, a_im8, c_bd,
                         (y_fwd, s5_d[i][None, :], s5_w_glu[i].astype(BF16), s5_b_glu[i][None, :]))
        ssm_o = ssm_o.reshape(B * S2, GW)

        wr_t = jnp.transpose(moe_w_router[i])
        wr_hi = wr_t.astype(BF16)
        wr2 = jnp.concatenate([wr_hi, (wr_t - wr_hi.astype(F32)).astype(BF16)], axis=0)
        x1, h2, aff_t = _outproj(sgu_o, attn_o, ssm_o, conv_o, w_out_bf, xs, mod,
                                 norm2_g[i][None, :], wr2, i, not last)

        rows_b = SEQ if last else S2
        sel = _select(aff_t, tri, not last)
        h3 = h2.reshape(B, rows_b, D)
        xs = _gather(sel[0], h3, ctx_only=False)
        xc = _gather(sel[2], h3, ctx_only=True) if not last else None
        ys = _ffn(xs, sel[1], xc, sel[3] if not last else None, moe_w_gate, moe_w_up, moe_w_down, i)
        x1_3 = x1.reshape(B, rows_b, D)
        if last:
            x3 = _scatter(sel[0], ys[0], None, None, x1_3, mod, final_norm_g[None, :])
        else:
            x3 = _scatter(sel[0], ys[0], sel[2], ys[1], x1_3, mod, None)

    return x3
```

```python
import functools

import jax
import jax.numpy as jnp
from jax import lax
from jax.experimental import pallas as pl
from jax.experimental.pallas import tpu as pltpu

F32 = jnp.float32
BF16 = jnp.bfloat16

D = 2048
B = 4
SEQ = 2048
CTX = 256
S2 = SEQ + CTX
DEPTH = 2
GW = 512
EPS = 1e-6

TM = 256
TPB = S2 // TM
LPB = SEQ // TM
NT = B * TPB
NLT = B * LPB

SGU_HEADS = 4
CHUNK = 128
MLA_HEADS = 4
QK_NOPE = 128
QK_ROPE = 64
QK_PAD = 256
Q_LORA = 384
KV_LORA = 256
ATT_SCALE = (QK_NOPE + QK_ROPE) ** -0.5

S5_G = 32
S5_N = 64
S5_CH = 16
S5_T = 128
S5_NCH = S2 // S5_T
S5_HALF = (S5_G // 2) * S5_N
S5_LT = S5_HALF // 128

N_EXP = 16
FF = D // 2
CAP = 2 * SEQ // N_EXP
CAP_C = 2 * CTX // N_EXP
FF_T = 256

IN_W = 3840
MIB = 1024 * 1024


def _cp(sem, vmem_mb):
    return pltpu.CompilerParams(dimension_semantics=sem, vmem_limit_bytes=vmem_mb * MIB)


def _lat_tile(i):
    return (i // LPB) * TPB + i % LPB


def _seg_of_tile(t):
    return jnp.where(t % TPB == TPB - 1, B, t // TPB)


def _rms(x, g):
    return x * lax.rsqrt(jnp.mean(x * x, axis=-1, keepdims=True) + EPS) * g


def _modnorm(x, g, shift, scale):
    return _rms(x, g) * (1.0 + scale) + shift


def _mod_kernel(c_ref, w_ref, b_ref, o_ref):
    a = c_ref[...]
    a = a * jax.nn.sigmoid(a)
    o_ref[...] = jnp.dot(a.astype(BF16), w_ref[...].astype(BF16),
                         preferred_element_type=F32) + b_ref[...]


def _modulation(c8, w_ada, b_ada):
    tn = 1024
    return pl.pallas_call(
        _mod_kernel,
        grid=(DEPTH, 6 * D // tn),
        in_specs=[pl.BlockSpec((8, D), lambda l, j: (0, 0)),
                  pl.BlockSpec((None, D, tn), lambda l, j: (l, 0, j)),
                  pl.BlockSpec((None, 1, tn), lambda l, j: (l, 0, j))],
        out_specs=pl.BlockSpec((None, 8, tn), lambda l, j: (l, 0, j)),
        out_shape=jax.ShapeDtypeStruct((DEPTH, 8, 6 * D), F32),
        compiler_params=_cp(("arbitrary", "arbitrary"), 40),
        name="modulation",
    )(c8, w_ada, b_ada.reshape(DEPTH, 1, 6 * D))


IN_RAW = 3776
KR0 = 2 * GW + Q_LORA + KV_LORA


WP_T = 256
WP_SWAP = KR0 // WP_T


def _winprep_kernel(prev_ref, cur_ref, o_ref):
    j = pl.program_id(1)
    keep = WP_T - QK_ROPE

    @pl.when(j < WP_SWAP)
    def _():
        o_ref[...] = cur_ref[...].astype(BF16)

    @pl.when(j == WP_SWAP)
    def _():
        cur = cur_ref[...]
        o_ref[:keep, :] = cur[:keep, :].astype(BF16)
        kr = cur[keep - QK_ROPE:keep, :]
        sw = jnp.concatenate([kr[16:32], kr[0:16], kr[48:64], kr[32:48]], axis=0)
        o_ref[keep:, :] = sw.astype(BF16)

    @pl.when(j > WP_SWAP)
    def _():
        o_ref[:QK_ROPE, :] = prev_ref[...].astype(BF16)
        o_ref[QK_ROPE:, :] = cur_ref[:keep, :].astype(BF16)


def _winprep(w_in_t):
    assert KR0 + QK_ROPE == (WP_SWAP + 1) * WP_T - QK_ROPE
    sub = WP_T // QK_ROPE
    return pl.pallas_call(
        _winprep_kernel,
        grid=(DEPTH, IN_W // WP_T),
        in_specs=[pl.BlockSpec((None, QK_ROPE, D), lambda l, j: (l, jnp.maximum(sub * j - 1, 0), 0)),
                  pl.BlockSpec((None, WP_T, D), lambda l, j: (l, j, 0))],
        out_specs=pl.BlockSpec((None, WP_T, D), lambda l, j: (l, j, 0)),
        out_shape=jax.ShapeDtypeStruct((DEPTH, IN_W, D), BF16),
        compiler_params=_cp(("arbitrary", "arbitrary"), 32),
        name="winprep",
    )(w_in_t, w_in_t)


def _stream_rows(refs, tile):
    if len(refs) == 1:
        return refs[0][...]
    return jnp.where(tile % TPB == TPB - 1, refs[1][...], refs[0][...])


def _stream_specs(xs):
    if len(xs) == 1:
        return lambda full: [pl.BlockSpec((TM, D), lambda i: (full(i), 0))]
    lat = lambda t: (t // TPB) * LPB + jnp.minimum(t % TPB, LPB - 1)
    return lambda full: [pl.BlockSpec((TM, D), lambda i: (lat(full(i)), 0)),
                         pl.BlockSpec((CTX, D), lambda i: (full(i) // TPB, 0))]


def _inproj_kernel(nx, *refs):
    (mod_ref, g_ref, w_ref, sg_ref, sw_ref, sb_ref,
     gq_ref, gkv_ref, wq_ref, wkv_ref, tq1_ref, tq2_ref, tk1_ref, tk2_ref,
     sgu_ref, q_ref, kc_ref, v_ref, s5_ref, conv_ref) = refs[nx:]
    x = _stream_rows(refs[:nx], pl.program_id(0))
    h = _modnorm(x, g_ref[...], mod_ref[0:1, :], mod_ref[1:2, :]).astype(BF16)

    def mm(a, b):
        return lax.dot_general(h, w_ref[a:b, :], _NT_DIMS, preferred_element_type=F32)

    p = jax.nn.gelu(mm(0, 2 * GW))
    u = p[:, :GW]
    vb = _rms(p[:, GW:], sg_ref[...]).astype(BF16)
    for ck in range(TM // CHUNK):
        r = slice(ck * CHUNK, (ck + 1) * CHUNK)
        for hd in range(SGU_HEADS):
            c = slice(hd * 128, (hd + 1) * 128)
            m = jnp.dot(sw_ref[hd], vb[r, c], preferred_element_type=F32)
            sgu_ref[r, c] = (u[r, c] * (m + sb_ref[:, c])).astype(BF16)

    pm = mm(2 * GW, 2 * GW + 768)
    cq = _rms(pm[:, :Q_LORA], gq_ref[...]).astype(BF16)
    q = jnp.dot(cq, wq_ref[...], preferred_element_type=F32)
    tq1 = tq1_ref[...]
    tq2 = tq2_ref[...]
    for hd in range(MLA_HEADS):
        c = slice(hd * QK_PAD, (hd + 1) * QK_PAD)
        blk = q[:, c]
        q_ref[:, c] = (blk * tq1 + pltpu.roll(blk, QK_PAD - QK_ROPE, 1) * tq2).astype(BF16)
    ckv = _rms(pm[:, Q_LORA:Q_LORA + KV_LORA], gkv_ref[...]).astype(BF16)
    kv = jnp.dot(ckv, wkv_ref[...], preferred_element_type=F32)
    v_ref[...] = kv[:, GW:].astype(BF16)
    kt = pm[:, Q_LORA + KV_LORA:]
    kr = (kt * tk1_ref[...] + pltpu.roll(kt, QK_ROPE, 1) * tk2_ref[...]).astype(BF16)
    for hd in range(MLA_HEADS):
        kc_ref[:, hd * QK_PAD:hd * QK_PAD + QK_NOPE] = kv[:, hd * 128:(hd + 1) * 128].astype(BF16)
        kc_ref[:, hd * QK_PAD + QK_NOPE:(hd + 1) * QK_PAD] = kr

    s5_ref[...] = mm(1792, 2304)
    conv_ref[:, :GW] = mm(2304, 2816)
    conv_ref[:, GW:] = mm(2816, 3328) * mm(3328, 3840)


def _inproj(xs, mod, g1, w_in_r, li, sgu_p, mla_p):
    fix2 = lambda i: (0, 0)
    pos = lambda i: (i % TPB, 0)
    row = lambda i: (i, 0)
    qkw = MLA_HEADS * QK_PAD
    return pl.pallas_call(
        functools.partial(_inproj_kernel, len(xs)),
        grid=(NT,),
        in_specs=_stream_specs(xs)(lambda i: i) + [
                  pl.BlockSpec((None, 6, D), lambda i: (_seg_of_tile(i), 0, 0)),
                  pl.BlockSpec((1, D), fix2),
                  pl.BlockSpec((None, IN_W, D), lambda i: (li, 0, 0)),
                  pl.BlockSpec((1, GW), fix2),
                  pl.BlockSpec((SGU_HEADS, CHUNK, CHUNK), lambda i: (0, 0, 0)),
                  pl.BlockSpec((CHUNK, GW), fix2),
                  pl.BlockSpec((1, Q_LORA), fix2),
                  pl.BlockSpec((1, KV_LORA), fix2),
                  pl.BlockSpec((Q_LORA, qkw), fix2),
                  pl.BlockSpec((KV_LORA, 2 * GW), fix2),
                  pl.BlockSpec((TM, QK_PAD), pos),
                  pl.BlockSpec((TM, QK_PAD), pos),
                  pl.BlockSpec((TM, 128), pos),
                  pl.BlockSpec((TM, 128), pos)],
        out_specs=[pl.BlockSpec((TM, GW), row),
                   pl.BlockSpec((TM, qkw), row),
                   pl.BlockSpec((TM, qkw), row),
                   pl.BlockSpec((TM, GW), row),
                   pl.BlockSpec((TM, GW), row),
                   pl.BlockSpec((TM, 2 * GW), row)],
        out_shape=[jax.ShapeDtypeStruct((B * S2, GW), BF16),
                   jax.ShapeDtypeStruct((B * S2, qkw), BF16),
                   jax.ShapeDtypeStruct((B * S2, qkw), BF16),
                   jax.ShapeDtypeStruct((B * S2, GW), BF16),
                   jax.ShapeDtypeStruct((B * S2, GW), F32),
                   jax.ShapeDtypeStruct((B * S2, 2 * GW), F32)],
        compiler_params=_cp(("arbitrary",), 56),
        name="inproj",
    )(*xs, mod, g1, w_in_r, *sgu_p, *mla_p)


def _conv_tile(tile, p_ref, zp_ref, zn_ref, w_ref):
    r = tile % TPB
    bg = p_ref[:, :GW]
    z = p_ref[:, GW:]
    row = lax.broadcasted_iota(jnp.int32, (TM, GW), 0)
    has_prev = jnp.logical_and(r != 0, r != TPB - 1)
    has_next = r < LPB - 1
    prev_row = zp_ref[7:8, :] * has_prev.astype(F32)
    next_row = zn_ref[0:1, :] * has_next.astype(F32)
    zm = jnp.where(row == 0, prev_row, pltpu.roll(z, 1, 0))
    zp = jnp.where(row == TM - 1, next_row, pltpu.roll(z, TM - 1, 0))
    y = w_ref[0:1, :] * zm + w_ref[1:2, :] * z + w_ref[2:3, :] * zp
    return (bg * y).astype(BF16)


def _conv_specs(full):
    rb = TM // 8
    nrb = B * S2 // 8
    return [pl.BlockSpec((TM, 2 * GW), lambda i: (full(i), 0)),
            pl.BlockSpec((8, GW), lambda i: (jnp.maximum(full(i) * rb - 1, 0), 1)),
            pl.BlockSpec((8, GW), lambda i: (jnp.minimum((full(i) + 1) * rb, nrb - 1), 1)),
            pl.BlockSpec((3, GW), lambda i: (0, 0))]


_NT_DIMS = (((1,), (1,)), ((), ()))


def _attn_kernel(with_ctx, q_ref, kc_ref, v_ref, o_ref):
    def run(k0):
        for hd in range(MLA_HEADS):
            c = slice(hd * 128, (hd + 1) * 128)
            cq = slice(hd * QK_PAD, (hd + 1) * QK_PAD)
            s = lax.dot_general(q_ref[:, cq], kc_ref[k0:, cq], _NT_DIMS, preferred_element_type=F32)
            m = jnp.max(s, axis=-1, keepdims=True)
            e = jnp.exp(s - m)
            l = jnp.sum(e, axis=-1, keepdims=True)
            o = jnp.dot(e.astype(BF16), v_ref[k0:, c], preferred_element_type=F32)
            o_ref[:, c] = (o / l).astype(BF16)

    if with_ctx:
        t = pl.program_id(1)
        pl.when(t < LPB)(lambda: run(0))
        pl.when(t == LPB)(lambda: run(SEQ))
    else:
        run(0)


def _attention(q, kc3, v3, with_ctx):
    tpb = TPB if with_ctx else LPB
    qkw = MLA_HEADS * QK_PAD
    return pl.pallas_call(
        functools.partial(_attn_kernel, with_ctx),
        grid=(B, tpb),
        in_specs=[pl.BlockSpec((TM, qkw), lambda b, t: (b * TPB + t, 0)),
                  pl.BlockSpec((None, S2, qkw), lambda b, t: (b, 0, 0)),
                  pl.BlockSpec((None, S2, GW), lambda b, t: (b, 0, 0))],
        out_specs=pl.BlockSpec((TM, GW), lambda b, t: (b * tpb + t, 0)),
        out_shape=jax.ShapeDtypeStruct((B * tpb * TM, GW), BF16),
        compiler_params=_cp(("arbitrary", "arbitrary"), 48),
        name="attn",
    )(q, kc3, v3)


def _s5_kernel(backward, *refs):
    if backward:
        (u_ref, w_ref, are_ref, aim_ref, c_ref, yf_ref, d_ref, wg_ref, bg_ref,
         o_ref, st_re, st_im, buf) = refs
    else:
        u_ref, w_ref, are_ref, aim_ref, c_ref, o_ref, st_re, st_im, buf = refs
    k = pl.program_id(0)
    hw = GW // 2
    rows = B * S5_T
    nlt = S5_LT

    @pl.when(k == 0)
    def _():
        st_re[...] = jnp.zeros_like(st_re)
        st_im[...] = jnp.zeros_like(st_im)

    u = u_ref[...].reshape(rows, GW)
    ub = u.astype(BF16)
    for h in range(2):
        bu = jnp.dot(ub[:, h * hw:(h + 1) * hw], w_ref[h], preferred_element_type=F32)
        for b in range(B):
            for c in range(2 * nlt):
                buf[c, pl.ds(2 * b + h, S5_T, stride=8), :] = (
                    bu[b * S5_T:(b + 1) * S5_T, c * 128:(c + 1) * 128])
    a_re = are_ref[...]
    a_im = aim_ref[...]

    def step(j, carry):
        sr, si = carry
        r0 = pl.multiple_of((S5_T - 1 - j if backward else j) * 8, 8)
        br = buf[0:nlt, pl.ds(r0, 8), :]
        bi = buf[nlt:2 * nlt, pl.ds(r0, 8), :]
        nr = a_re * sr - a_im * si + br
        ni = a_re * si + a_im * sr + bi
        buf[0:nlt, pl.ds(r0, 8), :] = nr
        buf[nlt:2 * nlt, pl.ds(r0, 8), :] = ni
        return nr, ni

    sr, si = lax.fori_loop(0, S5_T, step, (st_re[...], st_im[...]), unroll=4)
    st_re[...] = sr
    st_im[...] = si

    ys = []
    for h in range(2):
        s = jnp.concatenate(
            [jnp.concatenate([buf[c, pl.ds(2 * b + h, S5_T, stride=8), :] for c in range(2 * nlt)],
                             axis=1) for b in range(B)], axis=0)
        ys.append(jnp.dot(s.astype(BF16), c_ref[h], preferred_element_type=F32))
    y = jnp.concatenate(ys, axis=1)
    if backward:
        y = y + yf_ref[...].reshape(rows, GW) + d_ref[...] * u
        g = jax.nn.gelu(y)
        z = jnp.dot(g.astype(BF16), wg_ref[...], preferred_element_type=F32) + bg_ref[...]
        o_ref[...] = (g * jax.nn.sigmoid(z)).astype(BF16).reshape(B, S5_T, GW)
    else:
        o_ref[...] = y.reshape(B, S5_T, GW)


def _s5_pass(u3, w_bd, a_re8, a_im8, c_bd, glu):
    backward = glu is not None
    dr = 1 if backward else 0
    if backward:
        blk = lambda k: (0, S5_NCH - 1 - k, 0)
    else:
        blk = lambda k: (0, (k + SEQ // S5_T) % S5_NCH, 0)
    fixed = lambda k: (dr, 0, 0, 0)
    in_specs = [pl.BlockSpec((B, S5_T, GW), blk),
                pl.BlockSpec((None, 2, GW // 2, 2 * S5_HALF), fixed),
                pl.BlockSpec((None, S5_LT, 8, 128), fixed),
                pl.BlockSpec((None, S5_LT, 8, 128), fixed),
                pl.BlockSpec((None, 2, 2 * S5_HALF, GW // 2), fixed)]
    args = [u3, w_bd, a_re8, a_im8, c_bd]
    if backward:
        in_specs += [pl.BlockSpec((B, S5_T, GW), blk),
                     pl.BlockSpec((1, GW), lambda k: (0, 0)),
                     pl.BlockSpec((GW, GW), lambda k: (0, 0)),
                     pl.BlockSpec((1, GW), lambda k: (0, 0))]
        args += list(glu)
    return pl.pallas_call(
        functools.partial(_s5_kernel, backward),
        grid=(S5_NCH,),
        in_specs=in_specs,
        out_specs=pl.BlockSpec((B, S5_T, GW), blk),
        out_shape=jax.ShapeDtypeStruct((B, S2, GW), BF16 if backward else F32),
        scratch_shapes=[pltpu.VMEM((S5_LT, 8, 128), F32),
                        pltpu.VMEM((S5_LT, 8, 128), F32),
                        pltpu.VMEM((2 * S5_LT, 8 * S5_T, 128), F32)],
        compiler_params=_cp(("arbitrary",), 48),
        name="s5_bwd_glu" if backward else "s5_fwd",
    )(*args)


def _outproj_kernel(nx, full, *refs):
    (a0_ref, a1_ref, a2_ref, p_ref, zp_ref, zn_ref, cw_ref, w_ref, mod_ref, g2_ref, wr_ref,
     x1_ref, h2_ref, aff_ref) = refs[nx:]
    tile = full(pl.program_id(0))
    acts = (a0_ref[...], a1_ref[...], a2_ref[...], _conv_tile(tile, p_ref, zp_ref, zn_ref, cw_ref))
    halves = []
    for c in (slice(0, D // 2), slice(D // 2, D)):
        o = jnp.dot(acts[0], w_ref[0:GW, c], preferred_element_type=F32)
        for m in range(1, 4):
            o = o + jnp.dot(acts[m], w_ref[m * GW:(m + 1) * GW, c], preferred_element_type=F32)
        halves.append(o)
    x = _stream_rows(refs[:nx], tile)
    x1 = x + mod_ref[2:3, :] * jnp.concatenate(halves, axis=1)
    x1_ref[...] = x1
    h2 = _modnorm(x1, g2_ref[...], mod_ref[3:4, :], mod_ref[4:5, :]).astype(BF16)
    h2_ref[...] = h2
    lg2 = lax.dot_general(wr_ref[...], h2, _NT_DIMS, preferred_element_type=F32)
    lg = lg2[:N_EXP] + lg2[N_EXP:]
    e = jnp.exp(lg - jnp.max(lg, axis=0, keepdims=True))
    aff_ref[...] = e / jnp.sum(e, axis=0, keepdims=True)


def _outproj(sgu_o, attn_o, ssm_o, p_conv, conv_w, w_out_bf, xs, mod, g2, wr2, li, with_ctx):
    ntiles = NT if with_ctx else NLT
    full = (lambda i: i) if with_ctx else _lat_tile
    frow = lambda i: (full(i), 0)
    orow = lambda i: (i, 0)
    return pl.pallas_call(
        functools.partial(_outproj_kernel, len(xs), full),
        grid=(ntiles,),
        in_specs=_stream_specs(xs)(full) + [
                  pl.BlockSpec((TM, GW), frow),
                  pl.BlockSpec((TM, GW), orow),
                  pl.BlockSpec((TM, GW), frow)] + _conv_specs(full) + [
                  pl.BlockSpec((None, D, D), lambda i: (li, 0, 0)),
                  pl.BlockSpec((None, 6, D), lambda i: (_seg_of_tile(full(i)), 0, 0)),
                  pl.BlockSpec((1, D), lambda i: (0, 0)),
                  pl.BlockSpec((2 * N_EXP, D), lambda i: (0, 0))],
        out_specs=[pl.BlockSpec((TM, D), orow),
                   pl.BlockSpec((TM, D), orow),
                   pl.BlockSpec((N_EXP, TM), lambda i: (0, i))],
        out_shape=[jax.ShapeDtypeStruct((ntiles * TM, D), F32),
                   jax.ShapeDtypeStruct((ntiles * TM, D), BF16),
                   jax.ShapeDtypeStruct((N_EXP, ntiles * TM), F32)],
        compiler_params=_cp(("arbitrary",), 48),
        name="outproj",
    )(*xs, sgu_o, attn_o, ssm_o, p_conv, p_conv, p_conv, conv_w, w_out_bf, mod, g2, wr2)


def _one_hot_rows(rank_row, cap):
    slot = lax.broadcasted_iota(jnp.int32, (cap, rank_row.shape[-1]), 0).astype(F32)
    return jnp.where(rank_row == slot, 1.0, 0.0).astype(BF16)


def _select_one(a, tri, cap, rank_ref, w_ref):
    ne, n = a.shape
    bits = pltpu.bitcast(a, jnp.int32)
    thr = jnp.zeros((ne, 1), jnp.int32)
    for bit in range(30, -1, -1):
        cand = thr | (1 << bit)
        cnt = jnp.sum(jnp.where(bits >= cand, 1.0, 0.0), axis=1, keepdims=True)
        thr = jnp.where(cnt >= cap, cand, thr)
    gt = jnp.where(bits > thr, 1.0, 0.0)
    eq = jnp.where(bits == thr, 1.0, 0.0)
    need = cap - jnp.sum(gt, axis=1, keepdims=True)
    eq_before = jnp.dot(eq.astype(BF16), tri, preferred_element_type=F32) - eq
    sel = gt + eq * jnp.where(eq_before < need, 1.0, 0.0)
    rank = jnp.dot(sel.astype(BF16), tri, preferred_element_type=F32) - 1.0
    rank = jnp.where(sel > 0.5, rank, -1.0)
    slot = lax.broadcasted_iota(jnp.int32, (cap, n), 0).astype(F32)
    for e in range(ne):
        rank_ref[e] = rank[e:e + 1, :]
        hit = rank[e:e + 1, :] == slot
        w_ref[e * cap:(e + 1) * cap, :] = jnp.sum(jnp.where(hit, a[e:e + 1, :], 0.0),
                                                  axis=1, keepdims=True)


def _select_kernel(with_ctx, a_ref, tri_ref, *out_refs):
    _select_one(a_ref[:, :SEQ], tri_ref[...], CAP, out_refs[0], out_refs[1])
    if with_ctx:
        _select_one(a_ref[:, SEQ:], tri_ref[:CTX, :CTX], CAP_C, out_refs[2], out_refs[3])


def _select(aff_t, tri, with_ctx):
    out_specs = [pl.BlockSpec((None, N_EXP, 1, SEQ), lambda b: (b, 0, 0, 0)),
                 pl.BlockSpec((None, N_EXP * CAP, 1), lambda b: (b, 0, 0))]
    out_shape = [jax.ShapeDtypeStruct((B, N_EXP, 1, SEQ), F32),
                 jax.ShapeDtypeStruct((B, N_EXP * CAP, 1), F32)]
    if with_ctx:
        out_specs += [pl.BlockSpec((None, N_EXP, 1, CTX), lambda b: (b, 0, 0, 0)),
                      pl.BlockSpec((None, N_EXP * CAP_C, 1), lambda b: (b, 0, 0))]
        out_shape += [jax.ShapeDtypeStruct((B, N_EXP, 1, CTX), F32),
                      jax.ShapeDtypeStruct((B, N_EXP * CAP_C, 1), F32)]
    return pl.pallas_call(
        functools.partial(_select_kernel, with_ctx),
        grid=(B,),
        in_specs=[pl.BlockSpec((N_EXP, S2 if with_ctx else SEQ), lambda b: (0, b)),
                  pl.BlockSpec((SEQ, SEQ), lambda b: (0, 0))],
        out_specs=out_specs,
        out_shape=out_shape,
        compiler_params=_cp(("arbitrary",), 48),
        name="select",
    )(aff_t, tri)


def _gather_kernel(cap, rank_ref, h_ref, o_ref):
    p = jnp.concatenate([_one_hot_rows(rank_ref[j], cap) for j in range(rank_ref.shape[0])], axis=0)
    o_ref[...] = jnp.dot(p, h_ref[...], preferred_element_type=F32).astype(BF16)


def _gather(rank, h3, ctx_only):
    if ctx_only:
        cap, n, ne = CAP_C, CTX, N_EXP
        hmap = lambda b, r: (b, TPB - 1, 0)
    else:
        cap, n, ne = CAP, SEQ, 2
        hmap = lambda b, r: (b, 0, 0)
    return pl.pallas_call(
        functools.partial(_gather_kernel, cap),
        grid=(B, N_EXP // ne),
        in_specs=[pl.BlockSpec((None, ne, 1, n), lambda b, r: (b, r, 0, 0)),
                  pl.BlockSpec((None, n, D), hmap)],
        out_specs=pl.BlockSpec((None, ne * cap, D), lambda b, r: (b, r, 0)),
        out_shape=jax.ShapeDtypeStruct((B, N_EXP * cap, D), BF16),
        compiler_params=_cp(("arbitrary", "arbitrary"), 48),
        name="gather_ctx" if ctx_only else "gather",
    )(rank, h3)


def _ffn_kernel(with_ctx, *refs):
    if with_ctx:
        (x_ref, xc_ref, wg_ref, wu_ref, wd_ref, ws_ref, wsc_ref,
         y_ref, yc_ref, acc, accc) = refs
    else:
        x_ref, wg_ref, wu_ref, wd_ref, ws_ref, y_ref, acc = refs
    f = pl.program_id(1)
    last = FF // FF_T - 1

    def run(kind, xr, wsr, yr, ac, rows, wg, wu, wd):
        x = xr[...].reshape(rows, D)
        gate = jnp.dot(x, wg, preferred_element_type=F32)
        up = jnp.dot(x, wu, preferred_element_type=F32)
        hid = (gate * jax.nn.sigmoid(gate) * up).astype(BF16)
        part = jnp.dot(hid, wd, preferred_element_type=F32)
        if kind == "first":
            ac[...] = part
        elif kind == "mid":
            ac[...] += part
        else:
            y = (ac[...] + part) * wsr[...].reshape(rows, 1)
            yr[...] = y.astype(BF16).reshape(yr.shape)

    def step(kind):
        wg = wg_ref[...].astype(BF16)
        wu = wu_ref[...].astype(BF16)
        wd = wd_ref[...].astype(BF16)
        run(kind, x_ref, ws_ref, y_ref, acc, B * CAP, wg, wu, wd)
        if with_ctx:
            run(kind, xc_ref, wsc_ref, yc_ref, accc, B * CAP_C, wg, wu, wd)

    pl.when(f == 0)(lambda: step("first"))
    pl.when(jnp.logical_and(f > 0, f < last))(lambda: step("mid"))
    pl.when(f == last)(lambda: step("last"))


def _ffn(xs, ws, xc, wsc, w_gate, w_up, w_down, li):
    with_ctx = xc is not None
    in_specs = [pl.BlockSpec((B, CAP, D), lambda e, f: (0, e, 0))]
    args = [xs]
    if with_ctx:
        in_specs.append(pl.BlockSpec((B, CAP_C, D), lambda e, f: (0, e, 0)))
        args.append(xc)
    in_specs += [pl.BlockSpec((None, None, D, FF_T), lambda e, f: (li, e, 0, f)),
                 pl.BlockSpec((None, None, D, FF_T), lambda e, f: (li, e, 0, f)),
                 pl.BlockSpec((None, None, FF_T, D), lambda e, f: (li, e, f, 0)),
                 pl.BlockSpec((B, CAP, 1), lambda e, f: (0, e, 0))]
    args += [w_gate, w_up, w_down, ws]
    out_specs = [pl.BlockSpec((B, CAP, D), lambda e, f: (0, e, 0))]
    out_shape = [jax.ShapeDtypeStruct((B, N_EXP * CAP, D), BF16)]
    scratch = [pltpu.VMEM((B * CAP, D), F32)]
    if with_ctx:
        in_specs.append(pl.BlockSpec((B, CAP_C, 1), lambda e, f: (0, e, 0)))
        args.append(wsc)
        out_specs.append(pl.BlockSpec((B, CAP_C, D), lambda e, f: (0, e, 0)))
        out_shape.append(jax.ShapeDtypeStruct((B, N_EXP * CAP_C, D), BF16))
        scratch.append(pltpu.VMEM((B * CAP_C, D), F32))
    return pl.pallas_call(
        functools.partial(_ffn_kernel, with_ctx),
        grid=(N_EXP, FF // FF_T),
        in_specs=in_specs,
        out_specs=out_specs,
        out_shape=out_shape,
        scratch_shapes=scratch,
        compiler_params=_cp(("arbitrary", "arbitrary"), 56),
        name="ffn",
    )(*args)


_TN_DIMS = (((0,), (0,)), ((), ()))


def _scatter_kernel(with_ctx, final, *refs):
    refs = list(refs)
    p_ref, y_ref = refs[:2]
    pc_ref, yc_ref = refs[2:4] if with_ctx else (None, None)
    rest = refs[4:] if with_ctx else refs[2:]
    x_ref, mod_ref = rest[:2]
    gf_ref = rest[2] if final else None
    o_ref = rest[-1]

    def finish(rank_ref, yr, cap):
        p = jnp.concatenate([_one_hot_rows(rank_ref[e], cap) for e in range(N_EXP)], axis=0)
        upd = lax.dot_general(p, yr[...], _TN_DIMS, preferred_element_type=F32)
        x = x_ref[...] + mod_ref[5:6, :] * upd
        o_ref[...] = _rms(x, gf_ref[...]) if final else x

    if with_ctx:
        t = pl.program_id(1)
        pl.when(t < LPB)(lambda: finish(p_ref, y_ref, CAP))
        pl.when(t == LPB)(lambda: finish(pc_ref, yc_ref, CAP_C))
    else:
        finish(p_ref, y_ref, CAP)


def _scatter(rank, y, rank_c, yc, x3, mod, final_g):
    with_ctx = rank_c is not None
    final = final_g is not None
    tpb = TPB if with_ctx else LPB
    in_specs = [pl.BlockSpec((None, N_EXP, 1, TM), lambda b, t: (b, 0, 0, jnp.minimum(t, LPB - 1))),
                pl.BlockSpec((None, N_EXP * CAP, D), lambda b, t: (b, 0, 0))]
    args = [rank, y]
    if with_ctx:
        in_specs += [pl.BlockSpec((None, N_EXP, 1, CTX), lambda b, t: (b, 0, 0, 0)),
                     pl.BlockSpec((None, N_EXP * CAP_C, D), lambda b, t: (b, 0, 0))]
        args += [rank_c, yc]
    in_specs += [pl.BlockSpec((None, TM, D), lambda b, t: (b, t, 0)),
                 pl.BlockSpec((None, 6, D), lambda b, t: (jnp.where(t == LPB, B, b), 0, 0))]
    args += [x3, mod]
    if final:
        in_specs.append(pl.BlockSpec((1, D), lambda b, t: (0, 0)))
        args.append(final_g)
    return pl.pallas_call(
        functools.partial(_scatter_kernel, with_ctx, final),
        grid=(B, tpb),
        in_specs=in_specs,
        out_specs=pl.BlockSpec((None, TM, D), lambda b, t: (b, t, 0)),
        out_shape=jax.ShapeDtypeStruct((B, tpb * TM, D), F32),
        compiler_params=_cp(("arbitrary", "arbitrary"), 56),
        name="scatter",
    )(*args)


def _rope_tables():
    n_freq = QK_ROPE // 4
    grid_w = 64
    pos = jnp.arange(SEQ, dtype=F32)
    inv_freq = 10000.0 ** (-jnp.arange(n_freq, dtype=F32) / n_freq)
    ang_r = jnp.floor(pos / grid_w)[:, None] * inv_freq
    ang_c = (pos - grid_w * jnp.floor(pos / grid_w))[:, None] * inv_freq
    cr, sr, cc, sc = jnp.cos(ang_r), jnp.sin(ang_r), jnp.cos(ang_c), jnp.sin(ang_c)
    cos = jnp.concatenate([cr, cr, cc, cc], axis=1)
    sin = jnp.concatenate([-sr, sr, -sc, sc], axis=1)
    cos = jnp.concatenate([cos, jnp.ones((CTX, QK_ROPE), F32)], axis=0)
    sin = jnp.concatenate([sin, jnp.zeros((CTX, QK_ROPE), F32)], axis=0)
    z = jnp.zeros((S2, QK_ROPE), F32)
    tq1 = jnp.concatenate([jnp.full((S2, QK_NOPE), ATT_SCALE, F32), cos * ATT_SCALE, z], axis=1)
    tq2 = jnp.concatenate([jnp.zeros((S2, QK_NOPE), F32), sin * ATT_SCALE, z], axis=1)
    tk1 = jnp.concatenate([cos, z], axis=1)
    tk2 = jnp.concatenate([sin, z], axis=1)
    return tq1, tq2, tk1, tk2


def _pair_swap(w):
    return jnp.concatenate([w[..., 16:32], w[..., 0:16], w[..., 48:64], w[..., 32:48]], axis=-1)


def _s5_operators(a_re, a_im, log_dt, b_re, b_im, c_re, c_im):
    a = lax.complex(jnp.minimum(a_re.astype(F32), -1e-4), a_im.astype(F32))
    dt = jnp.exp(log_dt.astype(F32))[..., None]
    abar = jnp.exp(a * dt)
    bbar = ((abar - 1.0) / a)[..., None] * lax.complex(b_re.astype(F32), b_im.astype(F32))
    hg = S5_G // 2
    eye = jnp.eye(hg, dtype=F32)

    def in_op(m):
        m = jnp.transpose(m, (0, 1, 3, 2)).reshape(2, 2, hg, S5_CH, S5_N)
        return jnp.einsum("zhgcn,gk->zhgckn", m, eye).reshape(2, GW, S5_HALF)

    def out_op(m):
        m = m.reshape(2, 2, hg, S5_CH, S5_N)
        return jnp.einsum("zhgcn,gk->zgnhkc", m, eye).reshape(2, S5_HALF, GW)

    w_bd = jnp.concatenate([in_op(jnp.real(bbar)), in_op(jnp.imag(bbar))], axis=-1)
    w_bd = w_bd.reshape(2, 2, GW // 2, 2 * S5_HALF).astype(BF16)
    c_bd = jnp.concatenate([out_op(c_re.astype(F32)), out_op(-c_im.astype(F32))], axis=1)
    c_bd = jnp.transpose(c_bd.reshape(2, 2 * S5_HALF, 2, GW // 2), (0, 2, 1, 3)).astype(BF16)

    def rows8(m):
        m = m.reshape(2, 1, 2, S5_LT, 128)
        m = jnp.broadcast_to(m, (2, B, 2, S5_LT, 128)).reshape(2, 2 * B, S5_LT, 128)
        return jnp.transpose(m, (0, 2, 1, 3))

    return w_bd, rows8(jnp.real(abar)), rows8(jnp.imag(abar)), c_bd


def kernel(x, c, ctx, c_ctx, norm1_g, norm2_g, w_ada, b_ada, w_in, w_out, sgu_norm_g, sgu_w,
           sgu_b, mla_q_norm_g, mla_w_uq, mla_kv_norm_g, mla_w_ukv, s5_a_re, s5_a_im, s5_log_dt,
           s5_b_re, s5_b_im, s5_c_re, s5_c_im, s5_d, s5_w_glu, s5_b_glu, conv_w, moe_w_router,
           moe_w_gate, moe_w_up, moe_w_down, final_norm_g):
    c8 = jnp.concatenate([c, c_ctx[None, :], jnp.zeros((3, D), F32)], axis=0)
    mod_all = _modulation(c8, w_ada, b_ada).reshape(DEPTH, 8, 6, D)
    rope_t = _rope_tables()
    tri = jnp.triu(jnp.ones((SEQ, SEQ), BF16))
    w_in_r = _winprep(jnp.transpose(w_in, (0, 2, 1)))
    w_out_bf = w_out.astype(BF16)
    xs = (x.reshape(B * SEQ, D), ctx.reshape(B * CTX, D))

    for i in range(DEPTH):
        last = i == DEPTH - 1
        mod = mod_all[i]
        if i > 0:
            xs = (x3.reshape(B * S2, D),)

        wq = mla_w_uq[i].reshape(Q_LORA, MLA_HEADS, QK_NOPE + QK_ROPE)
        wq_r = wq[:, :, QK_NOPE:]
        wq_ext = jnp.concatenate([wq[:, :, :QK_NOPE], wq_r, _pair_swap(wq_r)], axis=2)
        wq_ext = wq_ext.reshape(Q_LORA, MLA_HEADS * QK_PAD).astype(BF16)
        wkv = mla_w_ukv[i].reshape(KV_LORA, MLA_HEADS, 2 * QK_NOPE)
        wkv_ext = jnp.concatenate([wkv[:, :, :QK_NOPE].reshape(KV_LORA, -1),
                                   wkv[:, :, QK_NOPE:].reshape(KV_LORA, -1)], axis=1).astype(BF16)
        sgu_p = (sgu_norm_g[i][None, :], sgu_w[i].astype(BF16),
                 jnp.repeat(jnp.swapaxes(sgu_b[i], 0, 1), 128, axis=1))
        mla_p = (mla_q_norm_g[i][None, :], mla_kv_norm_g[i][None, :], wq_ext, wkv_ext) + rope_t
        sgu_o, q, kc, v, p_s5, p_conv = _inproj(xs, mod, norm1_g[i][None, :], w_in_r, i,
                                                sgu_p, mla_p)

        attn_o = _attention(q, kc.reshape(B, S2, MLA_HEADS * QK_PAD), v.reshape(B, S2, GW), not last)

        w_bd, a_re8, a_im8, c_bd = _s5_operators(s5_a_re[i], s5_a_im[i], s5_log_dt[i], s5_b_re[i],
                                                 s5_b_im[i], s5_c_re[i], s5_c_im[i])
        u3 = p_s5.reshape(B, S2, GW)
        y_fwd = _s5_pass(u3, w_bd, a_re8, a_im8, c_bd, None)
        ssm_o = _s5_pass(u3, w_bd, a_re8, a_im8, c_bd,
                         (y_fwd, s5_d[i][None, :], s5_w_glu[i].astype(BF16), s5_b_glu[i][None, :]))
        ssm_o = ssm_o.reshape(B * S2, GW)

        wr_t = jnp.transpose(moe_w_router[i])
        wr_hi = wr_t.astype(BF16)
        wr2 = jnp.concatenate([wr_hi, (wr_t - wr_hi.astype(F32)).astype(BF16)], axis=0)
        x1, h2, aff_t = _outproj(sgu_o, attn_o, ssm_o, p_conv, conv_w[i], w_out_bf, xs, mod,
                                 norm2_g[i][None, :], wr2, i, not last)

        rows_b = SEQ if last else S2
        sel = _select(aff_t, tri, not last)
        h3 = h2.reshape(B, rows_b, D)
        xs = _gather(sel[0], h3, ctx_only=False)
        xc = _gather(sel[2], h3, ctx_only=True) if not last else None
        ys = _ffn(xs, sel[1], xc, sel[3] if not last else None, moe_w_gate, moe_w_up, moe_w_down, i)
        x1_3 = x1.reshape(B, rows_b, D)
        if last:
            x3 = _scatter(sel[0], ys[0], None, None, x1_3, mod, final_norm_g[None, :])
        else:
            x3 = _scatter(sel[0], ys[0], sel[2], ys[1], x1_3, mod, None)

    return x3
```

```python
import functools

import jax
import jax.numpy as jnp
from jax import lax
from jax.experimental import pallas as pl
from jax.experimental.pallas import tpu as pltpu

F32 = jnp.float32
BF16 = jnp.bfloat16

D = 2048
B = 4
SEQ = 2048
CTX = 256
S2 = SEQ + CTX
DEPTH = 2
GW = 512
EPS = 1e-6

TM = 256
TPB = S2 // TM
LPB = SEQ // TM
NT = B * TPB
NLT = B * LPB

SGU_HEADS = 4
CHUNK = 128
MLA_HEADS = 4
QK_NOPE = 128
QK_ROPE = 64
QK_PAD = 256
Q_LORA = 384
KV_LORA = 256
ATT_SCALE = (QK_NOPE + QK_ROPE) ** -0.5
LOG2E = 1.4426950408889634

S5_G = 32
S5_N = 64
S5_CH = 16
S5_T = 128
S5_NCH = S2 // S5_T
S5_HALF = (S5_G // 2) * S5_N
S5_LT = S5_HALF // 128

N_EXP = 16
FF = D // 2
CAP = 2 * SEQ // N_EXP
CAP_C = 2 * CTX // N_EXP
FF_T = 256

IN_W = 3840
MIB = 1024 * 1024


def _cp(sem, vmem_mb):
    return pltpu.CompilerParams(dimension_semantics=sem, vmem_limit_bytes=vmem_mb * MIB)


def _lat_tile(i):
    return (i // LPB) * TPB + i % LPB


def _seg_of_tile(t):
    return jnp.where(t % TPB == TPB - 1, B, t // TPB)


def _rms(x, g):
    return x * lax.rsqrt(jnp.mean(x * x, axis=-1, keepdims=True) + EPS) * g


def _modnorm(x, g, shift, scale):
    return _rms(x, g) * (1.0 + scale) + shift


def _mod_kernel(c_ref, w_ref, b_ref, o_ref):
    a = c_ref[...]
    a = a * jax.nn.sigmoid(a)
    o_ref[...] = jnp.dot(a.astype(BF16), w_ref[...].astype(BF16),
                         preferred_element_type=F32) + b_ref[...]


def _modulation(c8, w_ada, b_ada):
    tn = 1024
    return pl.pallas_call(
        _mod_kernel,
        grid=(DEPTH, 6 * D // tn),
        in_specs=[pl.BlockSpec((8, D), lambda l, j: (0, 0)),
                  pl.BlockSpec((None, D, tn), lambda l, j: (l, 0, j)),
                  pl.BlockSpec((None, 1, tn), lambda l, j: (l, 0, j))],
        out_specs=pl.BlockSpec((None, 8, tn), lambda l, j: (l, 0, j)),
        out_shape=jax.ShapeDtypeStruct((DEPTH, 8, 6 * D), F32),
        compiler_params=_cp(("arbitrary", "arbitrary"), 40),
        name="modulation",
    )(c8, w_ada, b_ada.reshape(DEPTH, 1, 6 * D))


IN_RAW = 3776
KR0 = 2 * GW + Q_LORA + KV_LORA


WP_T = 256
WP_SWAP = KR0 // WP_T


def _winprep_kernel(prev_ref, cur_ref, o_ref):
    j = pl.program_id(1)
    keep = WP_T - QK_ROPE

    @pl.when(j < WP_SWAP)
    def _():
        o_ref[...] = cur_ref[...].astype(BF16)

    @pl.when(j == WP_SWAP)
    def _():
        cur = cur_ref[...]
        o_ref[:keep, :] = cur[:keep, :].astype(BF16)
        kr = cur[keep - QK_ROPE:keep, :]
        sw = jnp.concatenate([kr[16:32], kr[0:16], kr[48:64], kr[32:48]], axis=0)
        o_ref[keep:, :] = sw.astype(BF16)

    @pl.when(j > WP_SWAP)
    def _():
        o_ref[:QK_ROPE, :] = prev_ref[...].astype(BF16)
        o_ref[QK_ROPE:, :] = cur_ref[:keep, :].astype(BF16)


def _winprep(w_in_t):
    assert KR0 + QK_ROPE == (WP_SWAP + 1) * WP_T - QK_ROPE
    sub = WP_T // QK_ROPE
    return pl.pallas_call(
        _winprep_kernel,
        grid=(DEPTH, IN_W // WP_T),
        in_specs=[pl.BlockSpec((None, QK_ROPE, D), lambda l, j: (l, jnp.maximum(sub * j - 1, 0), 0)),
                  pl.BlockSpec((None, WP_T, D), lambda l, j: (l, j, 0))],
        out_specs=pl.BlockSpec((None, WP_T, D), lambda l, j: (l, j, 0)),
        out_shape=jax.ShapeDtypeStruct((DEPTH, IN_W, D), BF16),
        compiler_params=_cp(("arbitrary", "arbitrary"), 32),
        name="winprep",
    )(w_in_t, w_in_t)


def _stream_rows(refs, tile):
    if len(refs) == 1:
        return refs[0][...]
    return jnp.where(tile % TPB == TPB - 1, refs[1][...], refs[0][...])


def _stream_specs(xs):
    if len(xs) == 1:
        return lambda full: [pl.BlockSpec((TM, D), lambda i: (full(i), 0))]
    lat = lambda t: (t // TPB) * LPB + jnp.minimum(t % TPB, LPB - 1)
    return lambda full: [pl.BlockSpec((TM, D), lambda i: (lat(full(i)), 0)),
                         pl.BlockSpec((CTX, D), lambda i: (full(i) // TPB, 0))]


def _inproj_kernel(nx, *refs):
    (mod_ref, g_ref, w_ref, sg_ref, sw_ref, sb_ref,
     gq_ref, gkv_ref, wq_ref, wkv_ref, tq1_ref, tq2_ref, tk1_ref, tk2_ref,
     sgu_ref, q_ref, kc_ref, v_ref, s5_ref, conv_ref) = refs[nx:]
    x = _stream_rows(refs[:nx], pl.program_id(0))
    h = _modnorm(x, g_ref[...], mod_ref[0:1, :], mod_ref[1:2, :]).astype(BF16)

    def mm(a, b):
        return lax.dot_general(h, w_ref[a:b, :], _NT_DIMS, preferred_element_type=F32)

    p = jax.nn.gelu(mm(0, 2 * GW))
    u = p[:, :GW]
    vb = _rms(p[:, GW:], sg_ref[...]).astype(BF16)
    for ck in range(TM // CHUNK):
        r = slice(ck * CHUNK, (ck + 1) * CHUNK)
        for hd in range(SGU_HEADS):
            c = slice(hd * 128, (hd + 1) * 128)
            m = jnp.dot(sw_ref[hd], vb[r, c], preferred_element_type=F32)
            sgu_ref[r, c] = (u[r, c] * (m + sb_ref[:, c])).astype(BF16)

    pm = mm(2 * GW, 2 * GW + 768)
    cq = _rms(pm[:, :Q_LORA], gq_ref[...]).astype(BF16)
    q = jnp.dot(cq, wq_ref[...], preferred_element_type=F32)
    tq1 = tq1_ref[...]
    tq2 = tq2_ref[...]
    for hd in range(MLA_HEADS):
        c = slice(hd * QK_PAD, (hd + 1) * QK_PAD)
        blk = q[:, c]
        q_ref[:, c] = (blk * tq1 + pltpu.roll(blk, QK_PAD - QK_ROPE, 1) * tq2).astype(BF16)
    ckv = _rms(pm[:, Q_LORA:Q_LORA + KV_LORA], gkv_ref[...]).astype(BF16)
    kv = jnp.dot(ckv, wkv_ref[...], preferred_element_type=F32)
    ones = jnp.ones((TM, 128), BF16)
    for hd in range(MLA_HEADS):
        v_ref[:, hd * QK_PAD:hd * QK_PAD + 128] = kv[:, GW + hd * 128:GW + (hd + 1) * 128].astype(BF16)
        v_ref[:, hd * QK_PAD + 128:(hd + 1) * QK_PAD] = ones
    kt = pm[:, Q_LORA + KV_LORA:]
    kr = (kt * tk1_ref[...] + pltpu.roll(kt, QK_ROPE, 1) * tk2_ref[...]).astype(BF16)
    for hd in range(MLA_HEADS):
        kc_ref[:, hd * QK_PAD:hd * QK_PAD + QK_NOPE] = kv[:, hd * 128:(hd + 1) * 128].astype(BF16)
        kc_ref[:, hd * QK_PAD + QK_NOPE:(hd + 1) * QK_PAD] = kr

    s5_ref[...] = mm(1792, 2304)
    conv_ref[:, :GW] = mm(2304, 2816)
    conv_ref[:, GW:] = mm(2816, 3328) * mm(3328, 3840)


def _inproj(xs, mod, g1, w_in_r, li, sgu_p, mla_p):
    fix2 = lambda i: (0, 0)
    pos = lambda i: (i % TPB, 0)
    row = lambda i: (i, 0)
    qkw = MLA_HEADS * QK_PAD
    return pl.pallas_call(
        functools.partial(_inproj_kernel, len(xs)),
        grid=(NT,),
        in_specs=_stream_specs(xs)(lambda i: i) + [
                  pl.BlockSpec((None, 6, D), lambda i: (_seg_of_tile(i), 0, 0)),
                  pl.BlockSpec((1, D), fix2),
                  pl.BlockSpec((None, IN_W, D), lambda i: (li, 0, 0)),
                  pl.BlockSpec((1, GW), fix2),
                  pl.BlockSpec((SGU_HEADS, CHUNK, CHUNK), lambda i: (0, 0, 0)),
                  pl.BlockSpec((CHUNK, GW), fix2),
                  pl.BlockSpec((1, Q_LORA), fix2),
                  pl.BlockSpec((1, KV_LORA), fix2),
                  pl.BlockSpec((Q_LORA, qkw), fix2),
                  pl.BlockSpec((KV_LORA, 2 * GW), fix2),
                  pl.BlockSpec((TM, QK_PAD), pos),
                  pl.BlockSpec((TM, QK_PAD), pos),
                  pl.BlockSpec((TM, 128), pos),
                  pl.BlockSpec((TM, 128), pos)],
        out_specs=[pl.BlockSpec((TM, GW), row),
                   pl.BlockSpec((TM, qkw), row),
                   pl.BlockSpec((TM, qkw), row),
                   pl.BlockSpec((TM, qkw), row),
                   pl.BlockSpec((TM, GW), row),
                   pl.BlockSpec((TM, 2 * GW), row)],
        out_shape=[jax.ShapeDtypeStruct((B * S2, GW), BF16),
                   jax.ShapeDtypeStruct((B * S2, qkw), BF16),
                   jax.ShapeDtypeStruct((B * S2, qkw), BF16),
                   jax.ShapeDtypeStruct((B * S2, qkw), BF16),
                   jax.ShapeDtypeStruct((B * S2, GW), F32),
                   jax.ShapeDtypeStruct((B * S2, 2 * GW), F32)],
        compiler_params=_cp(("arbitrary",), 56),
        name="inproj",
    )(*xs, mod, g1, w_in_r, *sgu_p, *mla_p)


def _conv_tile(tile, p_ref, zp_ref, zn_ref, w_ref):
    r = tile % TPB
    bg = p_ref[:, :GW]
    z = p_ref[:, GW:]
    row = lax.broadcasted_iota(jnp.int32, (TM, GW), 0)
    has_prev = jnp.logical_and(r != 0, r != TPB - 1)
    has_next = r < LPB - 1
    prev_row = zp_ref[7:8, :] * has_prev.astype(F32)
    next_row = zn_ref[0:1, :] * has_next.astype(F32)
    zm = jnp.where(row == 0, prev_row, pltpu.roll(z, 1, 0))
    zp = jnp.where(row == TM - 1, next_row, pltpu.roll(z, TM - 1, 0))
    y = w_ref[0:1, :] * zm + w_ref[1:2, :] * z + w_ref[2:3, :] * zp
    return (bg * y).astype(BF16)


def _conv_specs(full):
    rb = TM // 8
    nrb = B * S2 // 8
    return [pl.BlockSpec((TM, 2 * GW), lambda i: (full(i), 0)),
            pl.BlockSpec((8, GW), lambda i: (jnp.maximum(full(i) * rb - 1, 0), 1)),
            pl.BlockSpec((8, GW), lambda i: (jnp.minimum((full(i) + 1) * rb, nrb - 1), 1)),
            pl.BlockSpec((3, GW), lambda i: (0, 0))]


_NT_DIMS = (((1,), (1,)), ((), ()))


def _attn_kernel(with_ctx, q_ref, kc_ref, v_ref, o_ref):
    def run(k0):
        for hd in range(MLA_HEADS):
            c = slice(hd * 128, (hd + 1) * 128)
            cq = slice(hd * QK_PAD, (hd + 1) * QK_PAD)
            s = lax.dot_general(q_ref[:, cq], kc_ref[k0:, cq], _NT_DIMS, preferred_element_type=F32)
            e = jnp.exp2(s - jnp.max(s, axis=-1, keepdims=True)).astype(BF16)
            o = jnp.dot(e, v_ref[k0:, cq], preferred_element_type=F32)
            o_ref[:, c] = (o[:, :128] / o[:, 128:129]).astype(BF16)

    if with_ctx:
        t = pl.program_id(1)
        pl.when(t < LPB)(lambda: run(0))
        pl.when(t == LPB)(lambda: run(SEQ))
    else:
        run(0)


def _attention(q, kc3, v3, with_ctx):
    tpb = TPB if with_ctx else LPB
    qkw = MLA_HEADS * QK_PAD
    return pl.pallas_call(
        functools.partial(_attn_kernel, with_ctx),
        grid=(B, tpb),
        in_specs=[pl.BlockSpec((TM, qkw), lambda b, t: (b * TPB + t, 0)),
                  pl.BlockSpec((None, S2, qkw), lambda b, t: (b, 0, 0)),
                  pl.BlockSpec((None, S2, qkw), lambda b, t: (b, 0, 0))],
        out_specs=pl.BlockSpec((TM, GW), lambda b, t: (b * tpb + t, 0)),
        out_shape=jax.ShapeDtypeStruct((B * tpb * TM, GW), BF16),
        compiler_params=_cp(("arbitrary", "arbitrary"), 48),
        name="attn",
    )(q, kc3, v3)


def _s5_kernel(backward, *refs):
    if backward:
        (u_ref, w_ref, are_ref, aim_ref, c_ref, yf_ref, d_ref, wg_ref, bg_ref,
         o_ref, st_re, st_im, buf) = refs
    else:
        u_ref, w_ref, are_ref, aim_ref, c_ref, o_ref, st_re, st_im, buf = refs
    k = pl.program_id(0)
    hw = GW // 2
    rows = B * S5_T
    nlt = S5_LT

    @pl.when(k == 0)
    def _():
        st_re[...] = jnp.zeros_like(st_re)
        st_im[...] = jnp.zeros_like(st_im)

    u = u_ref[...].reshape(rows, GW)
    ub = u.astype(BF16)
    for h in range(2):
        bu = jnp.dot(ub[:, h * hw:(h + 1) * hw], w_ref[h], preferred_element_type=F32)
        for b in range(B):
            for c in range(2 * nlt):
                buf[c, pl.ds(2 * b + h, S5_T, stride=8), :] = (
                    bu[b * S5_T:(b + 1) * S5_T, c * 128:(c + 1) * 128])
    a_re = are_ref[...]
    a_im = aim_ref[...]

    def step(j, carry):
        sr, si = carry
        r0 = pl.multiple_of((S5_T - 1 - j if backward else j) * 8, 8)
        br = buf[0:nlt, pl.ds(r0, 8), :]
        bi = buf[nlt:2 * nlt, pl.ds(r0, 8), :]
        nr = a_re * sr - a_im * si + br
        ni = a_re * si + a_im * sr + bi
        buf[0:nlt, pl.ds(r0, 8), :] = nr
        buf[nlt:2 * nlt, pl.ds(r0, 8), :] = ni
        return nr, ni

    sr, si = lax.fori_loop(0, S5_T, step, (st_re[...], st_im[...]), unroll=4)
    st_re[...] = sr
    st_im[...] = si

    ys = []
    for h in range(2):
        s = jnp.concatenate(
            [jnp.concatenate([buf[c, pl.ds(2 * b + h, S5_T, stride=8), :] for c in range(2 * nlt)],
                             axis=1) for b in range(B)], axis=0)
        ys.append(jnp.dot(s.astype(BF16), c_ref[h], preferred_element_type=F32))
    y = jnp.concatenate(ys, axis=1)
    if backward:
        y = y + yf_ref[...].reshape(rows, GW) + d_ref[...] * u
        g = jax.nn.gelu(y)
        z = jnp.dot(g.astype(BF16), wg_ref[...], preferred_element_type=F32) + bg_ref[...]
        o_ref[...] = (g * jax.nn.sigmoid(z)).astype(BF16).reshape(B, S5_T, GW)
    else:
        o_ref[...] = y.reshape(B, S5_T, GW)


def _s5_pass(u3, ops, li, glu):
    backward = glu is not None
    dr = 1 if backward else 0
    if backward:
        blk = lambda k: (0, S5_NCH - 1 - k, 0)
    else:
        blk = lambda k: (0, (k + SEQ // S5_T) % S5_NCH, 0)
    w_bd, a_re8, a_im8, c_bd = ops
    fixed = lambda k: (li, dr, 0, 0, 0)
    in_specs = [pl.BlockSpec((B, S5_T, GW), blk),
                pl.BlockSpec((None, None, 2, GW // 2, 2 * S5_HALF), fixed),
                pl.BlockSpec((None, None, S5_LT, 8, 128), fixed),
                pl.BlockSpec((None, None, S5_LT, 8, 128), fixed),
                pl.BlockSpec((None, None, 2, 2 * S5_HALF, GW // 2), fixed)]
    args = [u3, w_bd, a_re8, a_im8, c_bd]
    if backward:
        in_specs += [pl.BlockSpec((B, S5_T, GW), blk),
                     pl.BlockSpec((1, GW), lambda k: (0, 0)),
                     pl.BlockSpec((GW, GW), lambda k: (0, 0)),
                     pl.BlockSpec((1, GW), lambda k: (0, 0))]
        args += list(glu)
    return pl.pallas_call(
        functools.partial(_s5_kernel, backward),
        grid=(S5_NCH,),
        in_specs=in_specs,
        out_specs=pl.BlockSpec((B, S5_T, GW), blk),
        out_shape=jax.ShapeDtypeStruct((B, S2, GW), BF16 if backward else F32),
        scratch_shapes=[pltpu.VMEM((S5_LT, 8, 128), F32),
                        pltpu.VMEM((S5_LT, 8, 128), F32),
                        pltpu.VMEM((2 * S5_LT, 8 * S5_T, 128), F32)],
        compiler_params=_cp(("arbitrary",), 48),
        name="s5_bwd_glu" if backward else "s5_fwd",
    )(*args)


def _outproj_kernel(nx, full, *refs):
    (a0_ref, a1_ref, a2_ref, p_ref, zp_ref, zn_ref, cw_ref, w_ref, mod_ref, g2_ref, wr_ref,
     x1_ref, h2_ref, aff_ref) = refs[nx:]
    tile = full(pl.program_id(0))
    acts = (a0_ref[...], a1_ref[...], a2_ref[...], _conv_tile(tile, p_ref, zp_ref, zn_ref, cw_ref))
    halves = []
    for c in (slice(0, D // 2), slice(D // 2, D)):
        o = jnp.dot(acts[0], w_ref[0:GW, c], preferred_element_type=F32)
        for m in range(1, 4):
            o = o + jnp.dot(acts[m], w_ref[m * GW:(m + 1) * GW, c], preferred_element_type=F32)
        halves.append(o)
    x = _stream_rows(refs[:nx], tile)
    x1 = x + mod_ref[2:3, :] * jnp.concatenate(halves, axis=1)
    x1_ref[...] = x1
    h2 = _modnorm(x1, g2_ref[...], mod_ref[3:4, :], mod_ref[4:5, :]).astype(BF16)
    h2_ref[...] = h2
    lg2 = lax.dot_general(wr_ref[...], h2, _NT_DIMS, preferred_element_type=F32)
    lg = lg2[:N_EXP] + lg2[N_EXP:]
    e = jnp.exp(lg - jnp.max(lg, axis=0, keepdims=True))
    aff_ref[...] = e / jnp.sum(e, axis=0, keepdims=True)


def _outproj(sgu_o, attn_o, ssm_o, p_conv, conv_w, w_out_bf, xs, mod, g2, wr2, li, with_ctx):
    ntiles = NT if with_ctx else NLT
    full = (lambda i: i) if with_ctx else _lat_tile
    frow = lambda i: (full(i), 0)
    orow = lambda i: (i, 0)
    return pl.pallas_call(
        functools.partial(_outproj_kernel, len(xs), full),
        grid=(ntiles,),
        in_specs=_stream_specs(xs)(full) + [
                  pl.BlockSpec((TM, GW), frow),
                  pl.BlockSpec((TM, GW), orow),
                  pl.BlockSpec((TM, GW), frow)] + _conv_specs(full) + [
                  pl.BlockSpec((None, D, D), lambda i: (li, 0, 0)),
                  pl.BlockSpec((None, 6, D), lambda i: (_seg_of_tile(full(i)), 0, 0)),
                  pl.BlockSpec((1, D), lambda i: (0, 0)),
                  pl.BlockSpec((2 * N_EXP, D), lambda i: (0, 0))],
        out_specs=[pl.BlockSpec((TM, D), orow),
                   pl.BlockSpec((TM, D), orow),
                   pl.BlockSpec((N_EXP, TM), lambda i: (0, i))],
        out_shape=[jax.ShapeDtypeStruct((ntiles * TM, D), F32),
                   jax.ShapeDtypeStruct((ntiles * TM, D), BF16),
                   jax.ShapeDtypeStruct((N_EXP, ntiles * TM), F32)],
        compiler_params=_cp(("arbitrary",), 48),
        name="outproj",
    )(*xs, sgu_o, attn_o, ssm_o, p_conv, p_conv, p_conv, conv_w, w_out_bf, mod, g2, wr2)


def _one_hot_rows(rank_row, cap):
    slot = lax.broadcasted_iota(jnp.int32, (cap, rank_row.shape[-1]), 0).astype(F32)
    return jnp.where(rank_row == slot, 1.0, 0.0).astype(BF16)


def _select_one(a, tri, cap, rank_ref, w_ref):
    ne, n = a.shape
    bits = pltpu.bitcast(a, jnp.int32)
    thr = jnp.zeros((ne, 1), jnp.int32)
    for bit in range(30, -1, -1):
        cand = thr | (1 << bit)
        cnt = jnp.sum(jnp.where(bits >= cand, 1.0, 0.0), axis=1, keepdims=True)
        thr = jnp.where(cnt >= cap, cand, thr)
    gt = jnp.where(bits > thr, 1.0, 0.0)
    eq = jnp.where(bits == thr, 1.0, 0.0)
    need = cap - jnp.sum(gt, axis=1, keepdims=True)
    eq_before = jnp.dot(eq.astype(BF16), tri, preferred_element_type=F32) - eq
    sel = gt + eq * jnp.where(eq_before < need, 1.0, 0.0)
    rank = jnp.dot(sel.astype(BF16), tri, preferred_element_type=F32) - 1.0
    rank = jnp.where(sel > 0.5, rank, -1.0)
    slot = lax.broadcasted_iota(jnp.int32, (cap, n), 0).astype(F32)
    for e in range(ne):
        rank_ref[e] = rank[e:e + 1, :]
        hit = rank[e:e + 1, :] == slot
        w_ref[e * cap:(e + 1) * cap, :] = jnp.sum(jnp.where(hit, a[e:e + 1, :], 0.0),
                                                  axis=1, keepdims=True)


def _select_kernel(with_ctx, a_ref, tri_ref, *out_refs):
    _select_one(a_ref[:, :SEQ], tri_ref[...], CAP, out_refs[0], out_refs[1])
    if with_ctx:
        _select_one(a_ref[:, SEQ:], tri_ref[:CTX, :CTX], CAP_C, out_refs[2], out_refs[3])


def _select(aff_t, tri, with_ctx):
    out_specs = [pl.BlockSpec((None, N_EXP, 1, SEQ), lambda b: (b, 0, 0, 0)),
                 pl.BlockSpec((None, N_EXP * CAP, 1), lambda b: (b, 0, 0))]
    out_shape = [jax.ShapeDtypeStruct((B, N_EXP, 1, SEQ), F32),
                 jax.ShapeDtypeStruct((B, N_EXP * CAP, 1), F32)]
    if with_ctx:
        out_specs += [pl.BlockSpec((None, N_EXP, 1, CTX), lambda b: (b, 0, 0, 0)),
                      pl.BlockSpec((None, N_EXP * CAP_C, 1), lambda b: (b, 0, 0))]
        out_shape += [jax.ShapeDtypeStruct((B, N_EXP, 1, CTX), F32),
                      jax.ShapeDtypeStruct((B, N_EXP * CAP_C, 1), F32)]
    return pl.pallas_call(
        functools.partial(_select_kernel, with_ctx),
        grid=(B,),
        in_specs=[pl.BlockSpec((N_EXP, S2 if with_ctx else SEQ), lambda b: (0, b)),
                  pl.BlockSpec((SEQ, SEQ), lambda b: (0, 0))],
        out_specs=out_specs,
        out_shape=out_shape,
        compiler_params=_cp(("arbitrary",), 48),
        name="select",
    )(aff_t, tri)


def _gather_kernel(cap, rank_ref, h_ref, o_ref):
    p = jnp.concatenate([_one_hot_rows(rank_ref[j], cap) for j in range(rank_ref.shape[0])], axis=0)
    o_ref[...] = jnp.dot(p, h_ref[...], preferred_element_type=F32).astype(BF16)


def _gather(rank, h3, ctx_only):
    if ctx_only:
        cap, n, ne = CAP_C, CTX, N_EXP
        hmap = lambda b, r: (b, TPB - 1, 0)
    else:
        cap, n, ne = CAP, SEQ, 2
        hmap = lambda b, r: (b, 0, 0)
    return pl.pallas_call(
        functools.partial(_gather_kernel, cap),
        grid=(B, N_EXP // ne),
        in_specs=[pl.BlockSpec((None, ne, 1, n), lambda b, r: (b, r, 0, 0)),
                  pl.BlockSpec((None, n, D), hmap)],
        out_specs=pl.BlockSpec((None, ne * cap, D), lambda b, r: (b, r, 0)),
        out_shape=jax.ShapeDtypeStruct((B, N_EXP * cap, D), BF16),
        compiler_params=_cp(("arbitrary", "arbitrary"), 48),
        name="gather_ctx" if ctx_only else "gather",
    )(rank, h3)


def _ffn_kernel(with_ctx, *refs):
    if with_ctx:
        (x_ref, xc_ref, wg_ref, wu_ref, wd_ref, ws_ref, wsc_ref,
         y_ref, yc_ref, acc, accc) = refs
    else:
        x_ref, wg_ref, wu_ref, wd_ref, ws_ref, y_ref, acc = refs
    f = pl.program_id(1)
    last = FF // FF_T - 1

    def run(kind, xr, wsr, yr, ac, rows, wg, wu, wd):
        x = xr[...].reshape(rows, D)
        gate = jnp.dot(x, wg, preferred_element_type=F32)
        up = jnp.dot(x, wu, preferred_element_type=F32)
        hid = (gate * jax.nn.sigmoid(gate) * up).astype(BF16)
        part = jnp.dot(hid, wd, preferred_element_type=F32)
        if kind == "first":
            ac[...] = part
        elif kind == "mid":
            ac[...] += part
        else:
            y = (ac[...] + part) * wsr[...].reshape(rows, 1)
            yr[...] = y.astype(BF16).reshape(yr.shape)

    def step(kind):
        wg = wg_ref[...].astype(BF16)
        wu = wu_ref[...].astype(BF16)
        wd = wd_ref[...].astype(BF16)
        run(kind, x_ref, ws_ref, y_ref, acc, B * CAP, wg, wu, wd)
        if with_ctx:
            run(kind, xc_ref, wsc_ref, yc_ref, accc, B * CAP_C, wg, wu, wd)

    pl.when(f == 0)(lambda: step("first"))
    pl.when(jnp.logical_and(f > 0, f < last))(lambda: step("mid"))
    pl.when(f == last)(lambda: step("last"))


def _ffn(xs, ws, xc, wsc, w_gate, w_up, w_down, li):
    with_ctx = xc is not None
    in_specs = [pl.BlockSpec((B, CAP, D), lambda e, f: (0, e, 0))]
    args = [xs]
    if with_ctx:
        in_specs.append(pl.BlockSpec((B, CAP_C, D), lambda e, f: (0, e, 0)))
        args.append(xc)
    in_specs += [pl.BlockSpec((None, None, D, FF_T), lambda e, f: (li, e, 0, f)),
                 pl.BlockSpec((None, None, D, FF_T), lambda e, f: (li, e, 0, f)),
                 pl.BlockSpec((None, None, FF_T, D), lambda e, f: (li, e, f, 0)),
                 pl.BlockSpec((B, CAP, 1), lambda e, f: (0, e, 0))]
    args += [w_gate, w_up, w_down, ws]
    out_specs = [pl.BlockSpec((B, CAP, D), lambda e, f: (0, e, 0))]
    out_shape = [jax.ShapeDtypeStruct((B, N_EXP * CAP, D), BF16)]
    scratch = [pltpu.VMEM((B * CAP, D), F32)]
    if with_ctx:
        in_specs.append(pl.BlockSpec((B, CAP_C, 1), lambda e, f: (0, e, 0)))
        args.append(wsc)
        out_specs.append(pl.BlockSpec((B, CAP_C, D), lambda e, f: (0, e, 0)))
        out_shape.append(jax.ShapeDtypeStruct((B, N_EXP * CAP_C, D), BF16))
        scratch.append(pltpu.VMEM((B * CAP_C, D), F32))
    return pl.pallas_call(
        functools.partial(_ffn_kernel, with_ctx),
        grid=(N_EXP, FF // FF_T),
        in_specs=in_specs,
        out_specs=out_specs,
        out_shape=out_shape,
        scratch_shapes=scratch,
        compiler_params=_cp(("arbitrary", "arbitrary"), 56),
        name="ffn",
    )(*args)


_TN_DIMS = (((0,), (0,)), ((), ()))


def _scatter_kernel(with_ctx, final, *refs):
    refs = list(refs)
    p_ref, y_ref = refs[:2]
    pc_ref, yc_ref = refs[2:4] if with_ctx else (None, None)
    rest = refs[4:] if with_ctx else refs[2:]
    x_ref, mod_ref = rest[:2]
    gf_ref = rest[2] if final else None
    o_ref = rest[-1]

    def finish(rank_ref, yr, cap):
        p = jnp.concatenate([_one_hot_rows(rank_ref[e], cap) for e in range(N_EXP)], axis=0)
        upd = lax.dot_general(p, yr[...], _TN_DIMS, preferred_element_type=F32)
        x = x_ref[...] + mod_ref[5:6, :] * upd
        o_ref[...] = _rms(x, gf_ref[...]) if final else x

    if with_ctx:
        t = pl.program_id(1)
        pl.when(t < LPB)(lambda: finish(p_ref, y_ref, CAP))
        pl.when(t == LPB)(lambda: finish(pc_ref, yc_ref, CAP_C))
    else:
        finish(p_ref, y_ref, CAP)


def _scatter(rank, y, rank_c, yc, x3, mod, final_g):
    with_ctx = rank_c is not None
    final = final_g is not None
    tpb = TPB if with_ctx else LPB
    in_specs = [pl.BlockSpec((None, N_EXP, 1, TM), lambda b, t: (b, 0, 0, jnp.minimum(t, LPB - 1))),
                pl.BlockSpec((None, N_EXP * CAP, D), lambda b, t: (b, 0, 0))]
    args = [rank, y]
    if with_ctx:
        in_specs += [pl.BlockSpec((None, N_EXP, 1, CTX), lambda b, t: (b, 0, 0, 0)),
                     pl.BlockSpec((None, N_EXP * CAP_C, D), lambda b, t: (b, 0, 0))]
        args += [rank_c, yc]
    in_specs += [pl.BlockSpec((None, TM, D), lambda b, t: (b, t, 0)),
                 pl.BlockSpec((None, 6, D), lambda b, t: (jnp.where(t == LPB, B, b), 0, 0))]
    args += [x3, mod]
    if final:
        in_specs.append(pl.BlockSpec((1, D), lambda b, t: (0, 0)))
        args.append(final_g)
    return pl.pallas_call(
        functools.partial(_scatter_kernel, with_ctx, final),
        grid=(B, tpb),
        in_specs=in_specs,
        out_specs=pl.BlockSpec((None, TM, D), lambda b, t: (b, t, 0)),
        out_shape=jax.ShapeDtypeStruct((B, tpb * TM, D), F32),
        compiler_params=_cp(("arbitrary", "arbitrary"), 56),
        name="scatter",
    )(*args)


def _rope_tables():
    n_freq = QK_ROPE // 4
    grid_w = 64
    pos = jnp.arange(SEQ, dtype=F32)
    inv_freq = 10000.0 ** (-jnp.arange(n_freq, dtype=F32) / n_freq)
    ang_r = jnp.floor(pos / grid_w)[:, None] * inv_freq
    ang_c = (pos - grid_w * jnp.floor(pos / grid_w))[:, None] * inv_freq
    cr, sr, cc, sc = jnp.cos(ang_r), jnp.sin(ang_r), jnp.cos(ang_c), jnp.sin(ang_c)
    cos = jnp.concatenate([cr, cr, cc, cc], axis=1)
    sin = jnp.concatenate([-sr, sr, -sc, sc], axis=1)
    cos = jnp.concatenate([cos, jnp.ones((CTX, QK_ROPE), F32)], axis=0)
    sin = jnp.concatenate([sin, jnp.zeros((CTX, QK_ROPE), F32)], axis=0)
    z = jnp.zeros((S2, QK_ROPE), F32)
    qs = ATT_SCALE * LOG2E
    tq1 = jnp.concatenate([jnp.full((S2, QK_NOPE), qs, F32), cos * qs, z], axis=1)
    tq2 = jnp.concatenate([jnp.zeros((S2, QK_NOPE), F32), sin * qs, z], axis=1)
    tk1 = jnp.concatenate([cos, z], axis=1)
    tk2 = jnp.concatenate([sin, z], axis=1)
    return tq1, tq2, tk1, tk2


def _pair_swap(w):
    return jnp.concatenate([w[..., 16:32], w[..., 0:16], w[..., 48:64], w[..., 32:48]], axis=-1)


def _s5_place_kernel(bre_ref, bim_ref, cre_ref, cim_ref, w_ref, c_ref):
    hg = S5_G // 2
    w_ref[...] = jnp.zeros_like(w_ref)
    c_ref[...] = jnp.zeros_like(c_ref)
    for g in range(hg):
        r = slice(g * S5_CH, (g + 1) * S5_CH)
        for part, (b_ref, k_ref, sign) in enumerate(((bre_ref, cre_ref, 1.0), (bim_ref, cim_ref, -1.0))):
            s = slice(part * S5_HALF + g * S5_N, part * S5_HALF + (g + 1) * S5_N)
            w_ref[r, s] = b_ref[g].astype(BF16)
            c_ref[s, r] = (sign * k_ref[g]).astype(BF16)


def _s5_operators(a_re, a_im, log_dt, b_re, b_im, c_re, c_im):
    a = lax.complex(jnp.minimum(a_re.astype(F32), -1e-4), a_im.astype(F32))
    dt = jnp.exp(log_dt.astype(F32))[..., None]
    abar = jnp.exp(a * dt)
    bbar = ((abar - 1.0) / a)[..., None] * lax.complex(b_re.astype(F32), b_im.astype(F32))
    hg = S5_G // 2

    def per_group(m):
        return m.reshape(DEPTH, 2, 2, hg, m.shape[-2], m.shape[-1])

    bt = jnp.swapaxes(bbar, -1, -2)
    ct_re = jnp.swapaxes(c_re.astype(F32), -1, -2)
    ct_im = jnp.swapaxes(c_im.astype(F32), -1, -2)
    bspec = pl.BlockSpec((None, None, None, hg, S5_CH, S5_N), lambda l, z, h: (l, z, h, 0, 0, 0))
    cspec = pl.BlockSpec((None, None, None, hg, S5_N, S5_CH), lambda l, z, h: (l, z, h, 0, 0, 0))
    w_bd, c_bd = pl.pallas_call(
        _s5_place_kernel,
        grid=(DEPTH, 2, 2),
        in_specs=[bspec, bspec, cspec, cspec],
        out_specs=[pl.BlockSpec((None, None, None, GW // 2, 2 * S5_HALF), lambda l, z, h: (l, z, h, 0, 0)),
                   pl.BlockSpec((None, None, None, 2 * S5_HALF, GW // 2), lambda l, z, h: (l, z, h, 0, 0))],
        out_shape=[jax.ShapeDtypeStruct((DEPTH, 2, 2, GW // 2, 2 * S5_HALF), BF16),
                   jax.ShapeDtypeStruct((DEPTH, 2, 2, 2 * S5_HALF, GW // 2), BF16)],
        compiler_params=_cp(("arbitrary",) * 3, 32),
        name="s5_place",
    )(per_group(jnp.real(bt)), per_group(jnp.imag(bt)), per_group(ct_re), per_group(ct_im))

    def rows8(m):
        m = m.reshape(DEPTH, 2, 1, 2, S5_LT, 128)
        m = jnp.broadcast_to(m, (DEPTH, 2, B, 2, S5_LT, 128)).reshape(DEPTH, 2, 2 * B, S5_LT, 128)
        return jnp.transpose(m, (0, 1, 3, 2, 4))

    return w_bd, rows8(jnp.real(abar)), rows8(jnp.imag(abar)), c_bd


def kernel(x, c, ctx, c_ctx, norm1_g, norm2_g, w_ada, b_ada, w_in, w_out, sgu_norm_g, sgu_w,
           sgu_b, mla_q_norm_g, mla_w_uq, mla_kv_norm_g, mla_w_ukv, s5_a_re, s5_a_im, s5_log_dt,
           s5_b_re, s5_b_im, s5_c_re, s5_c_im, s5_d, s5_w_glu, s5_b_glu, conv_w, moe_w_router,
           moe_w_gate, moe_w_up, moe_w_down, final_norm_g):
    c8 = jnp.concatenate([c, c_ctx[None, :], jnp.zeros((3, D), F32)], axis=0)
    mod_all = _modulation(c8, w_ada, b_ada).reshape(DEPTH, 8, 6, D)
    rope_t = _rope_tables()
    tri = jnp.triu(jnp.ones((SEQ, SEQ), BF16))
    w_in_r = _winprep(jnp.transpose(w_in, (0, 2, 1)))
    w_out_bf = w_out.astype(BF16)
    s5_ops = _s5_operators(s5_a_re, s5_a_im, s5_log_dt, s5_b_re, s5_b_im, s5_c_re, s5_c_im)
    xs = (x.reshape(B * SEQ, D), ctx.reshape(B * CTX, D))

    for i in range(DEPTH):
        last = i == DEPTH - 1
        mod = mod_all[i]
        if i > 0:
            xs = (x3.reshape(B * S2, D),)

        wq = mla_w_uq[i].reshape(Q_LORA, MLA_HEADS, QK_NOPE + QK_ROPE)
        wq_r = wq[:, :, QK_NOPE:]
        wq_ext = jnp.concatenate([wq[:, :, :QK_NOPE], wq_r, _pair_swap(wq_r)], axis=2)
        wq_ext = wq_ext.reshape(Q_LORA, MLA_HEADS * QK_PAD).astype(BF16)
        wkv = mla_w_ukv[i].reshape(KV_LORA, MLA_HEADS, 2 * QK_NOPE)
        wkv_ext = jnp.concatenate([wkv[:, :, :QK_NOPE].reshape(KV_LORA, -1),
                                   wkv[:, :, QK_NOPE:].reshape(KV_LORA, -1)], axis=1).astype(BF16)
        sgu_p = (sgu_norm_g[i][None, :], sgu_w[i].astype(BF16),
                 jnp.repeat(jnp.swapaxes(sgu_b[i], 0, 1), 128, axis=1))
        mla_p = (mla_q_norm_g[i][None, :], mla_kv_norm_g[i][None, :], wq_ext, wkv_ext) + rope_t
        sgu_o, q, kc, v, p_s5, p_conv = _inproj(xs, mod, norm1_g[i][None, :], w_in_r, i,
                                                sgu_p, mla_p)

        attn_o = _attention(q, kc.reshape(B, S2, MLA_HEADS * QK_PAD), v.reshape(B, S2, MLA_HEADS * QK_PAD), not last)

        u3 = p_s5.reshape(B, S2, GW)
        y_fwd = _s5_pass(u3, s5_ops, i, None)
        ssm_o = _s5_pass(u3, s5_ops, i,
                         (y_fwd, s5_d[i][None, :], s5_w_glu[i].astype(BF16), s5_b_glu[i][None, :]))
        ssm_o = ssm_o.reshape(B * S2, GW)

        wr_t = jnp.transpose(moe_w_router[i])
        wr_hi = wr_t.astype(BF16)
        wr2 = jnp.concatenate([wr_hi, (wr_t - wr_hi.astype(F32)).astype(BF16)], axis=0)
        x1, h2, aff_t = _outproj(sgu_o, attn_o, ssm_o, p_conv, conv_w[i], w_out_bf, xs, mod,
                                 norm2_g[i][None, :], wr2, i, not last)

        rows_b = SEQ if last else S2
        sel = _select(aff_t, tri, not last)
        h3 = h2.reshape(B, rows_b, D)
        xs = _gather(sel[0], h3, ctx_only=False)
        xc = _gather(sel[2], h3, ctx_only=True) if not last else None
        ys = _ffn(xs, sel[1], xc, sel[3] if not last else None, moe_w_gate, moe_w_up, moe_w_down, i)
        x1_3 = x1.reshape(B, rows_b, D)
        if last:
            x3 = _scatter(sel[0], ys[0], None, None, x1_3, mod, final_norm_g[None, :])
        else:
            x3 = _scatter(sel[0], ys[0], sel[2], ys[1], x1_3, mod, None)

    return x3
```

```python
import functools

import jax
import jax.numpy as jnp
from jax import lax
from jax.experimental import pallas as pl
from jax.experimental.pallas import tpu as pltpu

F32 = jnp.float32
BF16 = jnp.bfloat16

D = 2048
B = 4
SEQ = 2048
CTX = 256
S2 = SEQ + CTX
DEPTH = 2
GW = 512
EPS = 1e-6

TM = 256
TPB = S2 // TM
LPB = SEQ // TM
NT = B * TPB
NLT = B * LPB

SGU_HEADS = 4
CHUNK = 128
MLA_HEADS = 4
QK_NOPE = 128
QK_ROPE = 64
QK_PAD = 256
Q_LORA = 384
KV_LORA = 256
ATT_SCALE = (QK_NOPE + QK_ROPE) ** -0.5
LOG2E = 1.4426950408889634

S5_G = 32
S5_N = 64
S5_CH = 16
S5_T = 128
S5_NCH = S2 // S5_T
S5_HALF = (S5_G // 2) * S5_N
S5_LT = S5_HALF // 128

N_EXP = 16
FF = D // 2
CAP = 2 * SEQ // N_EXP
CAP_C = 2 * CTX // N_EXP
FF_T = 256

IN_W = 3840
MIB = 1024 * 1024


def _cp(sem, vmem_mb):
    return pltpu.CompilerParams(dimension_semantics=sem, vmem_limit_bytes=vmem_mb * MIB)


def _lat_tile(i):
    return (i // LPB) * TPB + i % LPB


def _seg_of_tile(t):
    return jnp.where(t % TPB == TPB - 1, B, t // TPB)


def _rms(x, g):
    return x * lax.rsqrt(jnp.mean(x * x, axis=-1, keepdims=True) + EPS) * g


def _modnorm(x, g, shift, scale):
    return _rms(x, g) * (1.0 + scale) + shift


def _mod_kernel(c_ref, w_ref, b_ref, o_ref):
    a = c_ref[...]
    a = a * jax.nn.sigmoid(a)
    o_ref[...] = jnp.dot(a.astype(BF16), w_ref[...].astype(BF16),
                         preferred_element_type=F32) + b_ref[...]


def _modulation(c8, w_ada, b_ada):
    tn = 1024
    return pl.pallas_call(
        _mod_kernel,
        grid=(DEPTH, 6 * D // tn),
        in_specs=[pl.BlockSpec((8, D), lambda l, j: (0, 0)),
                  pl.BlockSpec((None, D, tn), lambda l, j: (l, 0, j)),
                  pl.BlockSpec((None, 1, tn), lambda l, j: (l, 0, j))],
        out_specs=pl.BlockSpec((None, 8, tn), lambda l, j: (l, 0, j)),
        out_shape=jax.ShapeDtypeStruct((DEPTH, 8, 6 * D), F32),
        compiler_params=_cp(("arbitrary", "arbitrary"), 40),
        name="modulation",
    )(c8, w_ada, b_ada.reshape(DEPTH, 1, 6 * D))


IN_RAW = 3776
KR0 = 2 * GW + Q_LORA + KV_LORA


WP_T = 256
WP_SWAP = KR0 // WP_T


def _winprep_kernel(prev_ref, cur_ref, o_ref):
    j = pl.program_id(1)
    keep = WP_T - QK_ROPE

    @pl.when(j < WP_SWAP)
    def _():
        o_ref[...] = cur_ref[...].astype(BF16)

    @pl.when(j == WP_SWAP)
    def _():
        cur = cur_ref[...]
        o_ref[:keep, :] = cur[:keep, :].astype(BF16)
        kr = cur[keep - QK_ROPE:keep, :]
        sw = jnp.concatenate([kr[16:32], kr[0:16], kr[48:64], kr[32:48]], axis=0)
        o_ref[keep:, :] = sw.astype(BF16)

    @pl.when(j > WP_SWAP)
    def _():
        o_ref[:QK_ROPE, :] = prev_ref[...].astype(BF16)
        o_ref[QK_ROPE:, :] = cur_ref[:keep, :].astype(BF16)


def _winprep(w_in_t):
    assert KR0 + QK_ROPE == (WP_SWAP + 1) * WP_T - QK_ROPE
    sub = WP_T // QK_ROPE
    return pl.pallas_call(
        _winprep_kernel,
        grid=(DEPTH, IN_W // WP_T),
        in_specs=[pl.BlockSpec((None, QK_ROPE, D), lambda l, j: (l, jnp.maximum(sub * j - 1, 0), 0)),
                  pl.BlockSpec((None, WP_T, D), lambda l, j: (l, j, 0))],
        out_specs=pl.BlockSpec((None, WP_T, D), lambda l, j: (l, j, 0)),
        out_shape=jax.ShapeDtypeStruct((DEPTH, IN_W, D), BF16),
        compiler_params=_cp(("arbitrary", "arbitrary"), 32),
        name="winprep",
    )(w_in_t, w_in_t)


def _stream_rows(refs, tile):
    if len(refs) == 1:
        return refs[0][...]
    return jnp.where(tile % TPB == TPB - 1, refs[1][...], refs[0][...])


def _stream_specs(xs):
    if len(xs) == 1:
        return lambda full: [pl.BlockSpec((TM, D), lambda i: (full(i), 0))]
    lat = lambda t: (t // TPB) * LPB + jnp.minimum(t % TPB, LPB - 1)
    return lambda full: [pl.BlockSpec((TM, D), lambda i: (lat(full(i)), 0)),
                         pl.BlockSpec((CTX, D), lambda i: (full(i) // TPB, 0))]


def _inproj_kernel(nx, *refs):
    (mod_ref, g_ref, w_ref, sg_ref, sw_ref, sb_ref,
     gq_ref, gkv_ref, wq_ref, wkv_ref, tq1_ref, tq2_ref, tk1_ref, tk2_ref,
     sgu_ref, q_ref, kc_ref, v_ref, s5_ref, conv_ref) = refs[nx:]
    x = _stream_rows(refs[:nx], pl.program_id(0))
    h = _modnorm(x, g_ref[...], mod_ref[0:1, :], mod_ref[1:2, :]).astype(BF16)

    def mm(a, b):
        return lax.dot_general(h, w_ref[a:b, :], _NT_DIMS, preferred_element_type=F32)

    p_a = mm(0, 2 * GW)
    pm = mm(2 * GW, 2 * GW + 768)

    p = jax.nn.gelu(p_a)
    u = p[:, :GW]
    vb = _rms(p[:, GW:], sg_ref[...]).astype(BF16)
    for ck in range(TM // CHUNK):
        r = slice(ck * CHUNK, (ck + 1) * CHUNK)
        for hd in range(SGU_HEADS):
            c = slice(hd * 128, (hd + 1) * 128)
            m = jnp.dot(sw_ref[hd], vb[r, c], preferred_element_type=F32)
            sgu_ref[r, c] = (u[r, c] * (m + sb_ref[:, c])).astype(BF16)

    s5_ref[...] = mm(1792, 2304)
    conv_ref[:, :GW] = mm(2304, 2816)

    cq = _rms(pm[:, :Q_LORA], gq_ref[...]).astype(BF16)
    q = jnp.dot(cq, wq_ref[...], preferred_element_type=F32)
    tq1 = tq1_ref[...]
    tq2 = tq2_ref[...]
    for hd in range(MLA_HEADS):
        c = slice(hd * QK_PAD, (hd + 1) * QK_PAD)
        blk = q[:, c]
        q_ref[:, c] = (blk * tq1 + pltpu.roll(blk, QK_PAD - QK_ROPE, 1) * tq2).astype(BF16)
    ckv = _rms(pm[:, Q_LORA:Q_LORA + KV_LORA], gkv_ref[...]).astype(BF16)
    kv = jnp.dot(ckv, wkv_ref[...], preferred_element_type=F32)
    ones = jnp.ones((TM, 128), BF16)
    for hd in range(MLA_HEADS):
        v_ref[:, hd * QK_PAD:hd * QK_PAD + 128] = kv[:, GW + hd * 128:GW + (hd + 1) * 128].astype(BF16)
        v_ref[:, hd * QK_PAD + 128:(hd + 1) * QK_PAD] = ones
    kt = pm[:, Q_LORA + KV_LORA:]
    kr = (kt * tk1_ref[...] + pltpu.roll(kt, QK_ROPE, 1) * tk2_ref[...]).astype(BF16)
    for hd in range(MLA_HEADS):
        kc_ref[:, hd * QK_PAD:hd * QK_PAD + QK_NOPE] = kv[:, hd * 128:(hd + 1) * 128].astype(BF16)
        kc_ref[:, hd * QK_PAD + QK_NOPE:(hd + 1) * QK_PAD] = kr

    conv_ref[:, GW:] = mm(2816, 3328) * mm(3328, 3840)


def _inproj(xs, mod, g1, w_in_r, li, sgu_p, mla_p):
    fix2 = lambda i: (0, 0)
    pos = lambda i: (i % TPB, 0)
    row = lambda i: (i, 0)
    qkw = MLA_HEADS * QK_PAD
    return pl.pallas_call(
        functools.partial(_inproj_kernel, len(xs)),
        grid=(NT,),
        in_specs=_stream_specs(xs)(lambda i: i) + [
                  pl.BlockSpec((None, 6, D), lambda i: (_seg_of_tile(i), 0, 0)),
                  pl.BlockSpec((1, D), fix2),
                  pl.BlockSpec((None, IN_W, D), lambda i: (li, 0, 0)),
                  pl.BlockSpec((1, GW), fix2),
                  pl.BlockSpec((SGU_HEADS, CHUNK, CHUNK), lambda i: (0, 0, 0)),
                  pl.BlockSpec((CHUNK, GW), fix2),
                  pl.BlockSpec((1, Q_LORA), fix2),
                  pl.BlockSpec((1, KV_LORA), fix2),
                  pl.BlockSpec((Q_LORA, qkw), fix2),
                  pl.BlockSpec((KV_LORA, 2 * GW), fix2),
                  pl.BlockSpec((TM, QK_PAD), pos),
                  pl.BlockSpec((TM, QK_PAD), pos),
                  pl.BlockSpec((TM, 128), pos),
                  pl.BlockSpec((TM, 128), pos)],
        out_specs=[pl.BlockSpec((TM, GW), row),
                   pl.BlockSpec((TM, qkw), row),
                   pl.BlockSpec((TM, qkw), row),
                   pl.BlockSpec((TM, qkw), row),
                   pl.BlockSpec((TM, GW), row),
                   pl.BlockSpec((TM, 2 * GW), row)],
        out_shape=[jax.ShapeDtypeStruct((B * S2, GW), BF16),
                   jax.ShapeDtypeStruct((B * S2, qkw), BF16),
                   jax.ShapeDtypeStruct((B * S2, qkw), BF16),
                   jax.ShapeDtypeStruct((B * S2, qkw), BF16),
                   jax.ShapeDtypeStruct((B * S2, GW), F32),
                   jax.ShapeDtypeStruct((B * S2, 2 * GW), F32)],
        compiler_params=_cp(("arbitrary",), 56),
        name="inproj",
    )(*xs, mod, g1, w_in_r, *sgu_p, *mla_p)


def _conv_tile(tile, p_ref, zp_ref, zn_ref, w_ref):
    r = tile % TPB
    bg = p_ref[:, :GW]
    z = p_ref[:, GW:]
    row = lax.broadcasted_iota(jnp.int32, (TM, GW), 0)
    has_prev = jnp.logical_and(r != 0, r != TPB - 1)
    has_next = r < LPB - 1
    prev_row = zp_ref[7:8, :] * has_prev.astype(F32)
    next_row = zn_ref[0:1, :] * has_next.astype(F32)
    zm = jnp.where(row == 0, prev_row, pltpu.roll(z, 1, 0))
    zp = jnp.where(row == TM - 1, next_row, pltpu.roll(z, TM - 1, 0))
    y = w_ref[0:1, :] * zm + w_ref[1:2, :] * z + w_ref[2:3, :] * zp
    return (bg * y).astype(BF16)


def _conv_specs(full):
    rb = TM // 8
    nrb = B * S2 // 8
    return [pl.BlockSpec((TM, 2 * GW), lambda i: (full(i), 0)),
            pl.BlockSpec((8, GW), lambda i: (jnp.maximum(full(i) * rb - 1, 0), 1)),
            pl.BlockSpec((8, GW), lambda i: (jnp.minimum((full(i) + 1) * rb, nrb - 1), 1)),
            pl.BlockSpec((3, GW), lambda i: (0, 0))]


_NT_DIMS = (((1,), (1,)), ((), ()))


def _attn_kernel(with_ctx, q_ref, kc_ref, v_ref, o_ref):
    def run(k0):
        def scores(hd):
            cq = slice(hd * QK_PAD, (hd + 1) * QK_PAD)
            return lax.dot_general(q_ref[:, cq], kc_ref[k0:, cq], _NT_DIMS, preferred_element_type=F32)

        def weights(s):
            return jnp.exp2(s - jnp.max(s, axis=-1, keepdims=True)).astype(BF16)

        def values(hd, e):
            o = jnp.dot(e, v_ref[k0:, hd * QK_PAD:(hd + 1) * QK_PAD], preferred_element_type=F32)
            o_ref[:, hd * 128:(hd + 1) * 128] = (o[:, :128] / o[:, 128:129]).astype(BF16)

        s_next = scores(0)
        e_prev = None
        for hd in range(MLA_HEADS):
            s_cur = s_next
            if hd + 1 < MLA_HEADS:
                s_next = scores(hd + 1)
            e_cur = weights(s_cur)
            if e_prev is not None:
                values(hd - 1, e_prev)
            e_prev = e_cur
        values(MLA_HEADS - 1, e_prev)

    if with_ctx:
        t = pl.program_id(1)
        pl.when(t < LPB)(lambda: run(0))
        pl.when(t == LPB)(lambda: run(SEQ))
    else:
        run(0)


def _attention(q, kc3, v3, with_ctx):
    tpb = TPB if with_ctx else LPB
    qkw = MLA_HEADS * QK_PAD
    return pl.pallas_call(
        functools.partial(_attn_kernel, with_ctx),
        grid=(B, tpb),
        in_specs=[pl.BlockSpec((TM, qkw), lambda b, t: (b * TPB + t, 0)),
                  pl.BlockSpec((None, S2, qkw), lambda b, t: (b, 0, 0)),
                  pl.BlockSpec((None, S2, qkw), lambda b, t: (b, 0, 0))],
        out_specs=pl.BlockSpec((TM, GW), lambda b, t: (b * tpb + t, 0)),
        out_shape=jax.ShapeDtypeStruct((B * tpb * TM, GW), BF16),
        compiler_params=_cp(("arbitrary", "arbitrary"), 48),
        name="attn",
    )(q, kc3, v3)


def _s5_kernel(backward, *refs):
    if backward:
        (u_ref, w_ref, are_ref, aim_ref, c_ref, yf_ref, d_ref, wg_ref, bg_ref,
         o_ref, st_re, st_im, buf) = refs
    else:
        u_ref, w_ref, are_ref, aim_ref, c_ref, o_ref, st_re, st_im, buf = refs
    k = pl.program_id(0)
    hw = GW // 2
    rows = B * S5_T
    nlt = S5_LT

    @pl.when(k == 0)
    def _():
        st_re[...] = jnp.zeros_like(st_re)
        st_im[...] = jnp.zeros_like(st_im)

    u = u_ref[...].reshape(rows, GW)
    ub = u.astype(BF16)
    for h in range(2):
        bu = jnp.dot(ub[:, h * hw:(h + 1) * hw], w_ref[h], preferred_element_type=F32)
        for b in range(B):
            for c in range(2 * nlt):
                buf[c, pl.ds(2 * b + h, S5_T, stride=8), :] = (
                    bu[b * S5_T:(b + 1) * S5_T, c * 128:(c + 1) * 128])
    a_re = are_ref[...]
    a_im = aim_ref[...]

    def step(j, carry):
        sr, si = carry
        r0 = pl.multiple_of((S5_T - 1 - j if backward else j) * 8, 8)
        br = buf[0:nlt, pl.ds(r0, 8), :]
        bi = buf[nlt:2 * nlt, pl.ds(r0, 8), :]
        nr = a_re * sr - a_im * si + br
        ni = a_re * si + a_im * sr + bi
        buf[0:nlt, pl.ds(r0, 8), :] = nr
        buf[nlt:2 * nlt, pl.ds(r0, 8), :] = ni
        return nr, ni

    sr, si = lax.fori_loop(0, S5_T, step, (st_re[...], st_im[...]), unroll=4)
    st_re[...] = sr
    st_im[...] = si

    ys = []
    for h in range(2):
        s = jnp.concatenate(
            [jnp.concatenate([buf[c, pl.ds(2 * b + h, S5_T, stride=8), :] for c in range(2 * nlt)],
                             axis=1) for b in range(B)], axis=0)
        ys.append(jnp.dot(s.astype(BF16), c_ref[h], preferred_element_type=F32))
    y = jnp.concatenate(ys, axis=1)
    if backward:
        y = y + yf_ref[...].reshape(rows, GW) + d_ref[...] * u
        g = jax.nn.gelu(y)
        z = jnp.dot(g.astype(BF16), wg_ref[...], preferred_element_type=F32) + bg_ref[...]
        o_ref[...] = (g * jax.nn.sigmoid(z)).astype(BF16).reshape(B, S5_T, GW)
    else:
        o_ref[...] = y.reshape(B, S5_T, GW)


def _s5_pass(u3, ops, li, glu):
    backward = glu is not None
    dr = 1 if backward else 0
    if backward:
        blk = lambda k: (0, S5_NCH - 1 - k, 0)
    else:
        blk = lambda k: (0, (k + SEQ // S5_T) % S5_NCH, 0)
    w_bd, a_re8, a_im8, c_bd = ops
    fixed = lambda k: (li, dr, 0, 0, 0)
    in_specs = [pl.BlockSpec((B, S5_T, GW), blk),
                pl.BlockSpec((None, None, 2, GW // 2, 2 * S5_HALF), fixed),
                pl.BlockSpec((None, None, S5_LT, 8, 128), fixed),
                pl.BlockSpec((None, None, S5_LT, 8, 128), fixed),
                pl.BlockSpec((None, None, 2, 2 * S5_HALF, GW // 2), fixed)]
    args = [u3, w_bd, a_re8, a_im8, c_bd]
    if backward:
        in_specs += [pl.BlockSpec((B, S5_T, GW), blk),
                     pl.BlockSpec((1, GW), lambda k: (0, 0)),
                     pl.BlockSpec((GW, GW), lambda k: (0, 0)),
                     pl.BlockSpec((1, GW), lambda k: (0, 0))]
        args += list(glu)
    return pl.pallas_call(
        functools.partial(_s5_kernel, backward),
        grid=(S5_NCH,),
        in_specs=in_specs,
        out_specs=pl.BlockSpec((B, S5_T, GW), blk),
        out_shape=jax.ShapeDtypeStruct((B, S2, GW), BF16 if backward else F32),
        scratch_shapes=[pltpu.VMEM((S5_LT, 8, 128), F32),
                        pltpu.VMEM((S5_LT, 8, 128), F32),
                        pltpu.VMEM((2 * S5_LT, 8 * S5_T, 128), F32)],
        compiler_params=_cp(("arbitrary",), 48),
        name="s5_bwd_glu" if backward else "s5_fwd",
    )(*args)


OP_SUB = 2


def _outproj_kernel(nx, full, *refs):
    per = nx + 7
    cw_ref, w_ref, g2_ref, wr_ref, x1_ref, h2_ref, aff_ref = refs[OP_SUB * per:]
    tiles = [full(OP_SUB * pl.program_id(0) + s) for s in range(OP_SUB)]
    acts_all = []
    for s in range(OP_SUB):
        a0_ref, a1_ref, a2_ref, p_ref, zp_ref, zn_ref, _ = refs[s * per + nx:(s + 1) * per]
        acts_all.append((a0_ref[...], a1_ref[...], a2_ref[...],
                         _conv_tile(tiles[s], p_ref, zp_ref, zn_ref, cw_ref)))
    outs = []
    for s in range(OP_SUB):
        acts = acts_all[s]
        halves = []
        for c in (slice(0, D // 2), slice(D // 2, D)):
            o = jnp.dot(acts[0], w_ref[0:GW, c], preferred_element_type=F32)
            for m in range(1, 4):
                o = o + jnp.dot(acts[m], w_ref[m * GW:(m + 1) * GW, c], preferred_element_type=F32)
            halves.append(o)
        outs.append(jnp.concatenate(halves, axis=1))
    for s in range(OP_SUB):
        tr = refs[s * per:(s + 1) * per]
        mod_ref = tr[-1]
        tile = tiles[s]
        rows = slice(s * TM, (s + 1) * TM)
        x = _stream_rows(tr[:nx], tile)
        x1 = x + mod_ref[2:3, :] * outs[s]
        x1_ref[rows, :] = x1
        h2 = _modnorm(x1, g2_ref[...], mod_ref[3:4, :], mod_ref[4:5, :]).astype(BF16)
        h2_ref[rows, :] = h2
        lg2 = lax.dot_general(wr_ref[...], h2, _NT_DIMS, preferred_element_type=F32)
        lg = lg2[:N_EXP] + lg2[N_EXP:]
        e = jnp.exp(lg - jnp.max(lg, axis=0, keepdims=True))
        aff_ref[:, rows] = e / jnp.sum(e, axis=0, keepdims=True)


def _outproj(sgu_o, attn_o, ssm_o, p_conv, conv_w, w_out_bf, xs, mod, g2, wr2, li, with_ctx):
    ntiles = NT if with_ctx else NLT
    full = (lambda i: i) if with_ctx else _lat_tile
    in_specs, args = [], []
    for s in range(OP_SUB):
        out_t = lambda i, s=s: OP_SUB * i + s
        full_t = lambda i, s=s: full(OP_SUB * i + s)
        frow = lambda i, f=full_t: (f(i), 0)
        orow = lambda i, f=out_t: (f(i), 0)
        in_specs += _stream_specs(xs)(full_t) + [
            pl.BlockSpec((TM, GW), frow),
            pl.BlockSpec((TM, GW), orow),
            pl.BlockSpec((TM, GW), frow)] + _conv_specs(full_t)[:3] + [
            pl.BlockSpec((None, 6, D), lambda i, f=full_t: (_seg_of_tile(f(i)), 0, 0))]
        args += [*xs, sgu_o, attn_o, ssm_o, p_conv, p_conv, p_conv, mod]
    in_specs += [_conv_specs(full)[3],
                 pl.BlockSpec((None, D, D), lambda i: (li, 0, 0)),
                 pl.BlockSpec((1, D), lambda i: (0, 0)),
                 pl.BlockSpec((2 * N_EXP, D), lambda i: (0, 0))]
    args += [conv_w, w_out_bf, g2, wr2]
    blk = OP_SUB * TM
    return pl.pallas_call(
        functools.partial(_outproj_kernel, len(xs), full),
        grid=(ntiles // OP_SUB,),
        in_specs=in_specs,
        out_specs=[pl.BlockSpec((blk, D), lambda i: (i, 0)),
                   pl.BlockSpec((blk, D), lambda i: (i, 0)),
                   pl.BlockSpec((N_EXP, blk), lambda i: (0, i))],
        out_shape=[jax.ShapeDtypeStruct((ntiles * TM, D), F32),
                   jax.ShapeDtypeStruct((ntiles * TM, D), BF16),
                   jax.ShapeDtypeStruct((N_EXP, ntiles * TM), F32)],
        compiler_params=_cp(("arbitrary",), 56),
        name="outproj",
    )(*args)


def _one_hot_rows(rank_row, cap):
    slot = lax.broadcasted_iota(jnp.int32, (cap, rank_row.shape[-1]), 0).astype(F32)
    return jnp.where(rank_row == slot, 1.0, 0.0).astype(BF16)


def _select_one(a, tri, cap, rank_ref, w_ref):
    ne, n = a.shape
    bits = pltpu.bitcast(a, jnp.int32)
    thr = jnp.zeros((ne, 1), jnp.int32)
    for bit in range(30, -1, -1):
        cand = thr | (1 << bit)
        cnt = jnp.sum(jnp.where(bits >= cand, 1.0, 0.0), axis=1, keepdims=True)
        thr = jnp.where(cnt >= cap, cand, thr)
    gt = jnp.where(bits > thr, 1.0, 0.0)
    eq = jnp.where(bits == thr, 1.0, 0.0)
    need = cap - jnp.sum(gt, axis=1, keepdims=True)
    eq_before = jnp.dot(eq.astype(BF16), tri, preferred_element_type=F32) - eq
    sel = gt + eq * jnp.where(eq_before < need, 1.0, 0.0)
    rank = jnp.dot(sel.astype(BF16), tri, preferred_element_type=F32) - 1.0
    rank = jnp.where(sel > 0.5, rank, -1.0)
    slot = lax.broadcasted_iota(jnp.int32, (cap, n), 0).astype(F32)
    for e in range(ne):
        rank_ref[e] = rank[e:e + 1, :]
        hit = rank[e:e + 1, :] == slot
        w_ref[e * cap:(e + 1) * cap, :] = jnp.sum(jnp.where(hit, a[e:e + 1, :], 0.0),
                                                  axis=1, keepdims=True)


def _select_kernel(with_ctx, a_ref, tri_ref, *out_refs):
    _select_one(a_ref[:, :SEQ], tri_ref[...], CAP, out_refs[0], out_refs[1])
    if with_ctx:
        _select_one(a_ref[:, SEQ:], tri_ref[:CTX, :CTX], CAP_C, out_refs[2], out_refs[3])


def _select(aff_t, tri, with_ctx):
    out_specs = [pl.BlockSpec((None, N_EXP, 1, SEQ), lambda b: (b, 0, 0, 0)),
                 pl.BlockSpec((None, N_EXP * CAP, 1), lambda b: (b, 0, 0))]
    out_shape = [jax.ShapeDtypeStruct((B, N_EXP, 1, SEQ), F32),
                 jax.ShapeDtypeStruct((B, N_EXP * CAP, 1), F32)]
    if with_ctx:
        out_specs += [pl.BlockSpec((None, N_EXP, 1, CTX), lambda b: (b, 0, 0, 0)),
                      pl.BlockSpec((None, N_EXP * CAP_C, 1), lambda b: (b, 0, 0))]
        out_shape += [jax.ShapeDtypeStruct((B, N_EXP, 1, CTX), F32),
                      jax.ShapeDtypeStruct((B, N_EXP * CAP_C, 1), F32)]
    return pl.pallas_call(
        functools.partial(_select_kernel, with_ctx),
        grid=(B,),
        in_specs=[pl.BlockSpec((N_EXP, S2 if with_ctx else SEQ), lambda b: (0, b)),
                  pl.BlockSpec((SEQ, SEQ), lambda b: (0, 0))],
        out_specs=out_specs,
        out_shape=out_shape,
        compiler_params=_cp(("arbitrary",), 48),
        name="select",
    )(aff_t, tri)


def _gather_kernel(cap, rank_ref, h_ref, o_ref):
    p = jnp.concatenate([_one_hot_rows(rank_ref[j], cap) for j in range(rank_ref.shape[0])], axis=0)
    o_ref[...] = jnp.dot(p, h_ref[...], preferred_element_type=F32).astype(BF16)


def _gather(rank, h3, ctx_only):
    if ctx_only:
        cap, n, ne = CAP_C, CTX, N_EXP
        hmap = lambda b, r: (b, TPB - 1, 0)
    else:
        cap, n, ne = CAP, SEQ, 2
        hmap = lambda b, r: (b, 0, 0)
    return pl.pallas_call(
        functools.partial(_gather_kernel, cap),
        grid=(B, N_EXP // ne),
        in_specs=[pl.BlockSpec((None, ne, 1, n), lambda b, r: (b, r, 0, 0)),
                  pl.BlockSpec((None, n, D), hmap)],
        out_specs=pl.BlockSpec((None, ne * cap, D), lambda b, r: (b, r, 0)),
        out_shape=jax.ShapeDtypeStruct((B, N_EXP * cap, D), BF16),
        compiler_params=_cp(("arbitrary", "arbitrary"), 48),
        name="gather_ctx" if ctx_only else "gather",
    )(rank, h3)


def _ffn_kernel(with_ctx, *refs):
    if with_ctx:
        (x_ref, xc_ref, wg_ref, wu_ref, wd_ref, ws_ref, wsc_ref,
         y_ref, yc_ref, acc, accc) = refs
    else:
        x_ref, wg_ref, wu_ref, wd_ref, ws_ref, y_ref, acc = refs
    f = pl.program_id(1)
    last = FF // FF_T - 1

    def run(kind, xr, wsr, yr, ac, rows, wg, wu, wd):
        x = xr[...].reshape(rows, D)
        gate = jnp.dot(x, wg, preferred_element_type=F32)
        up = jnp.dot(x, wu, preferred_element_type=F32)
        hid = (gate * jax.nn.sigmoid(gate) * up).astype(BF16)
        part = jnp.dot(hid, wd, preferred_element_type=F32)
        if kind == "first":
            ac[...] = part
        elif kind == "mid":
            ac[...] += part
        else:
            y = (ac[...] + part) * wsr[...].reshape(rows, 1)
            yr[...] = y.astype(BF16).reshape(yr.shape)

    def step(kind):
        wg = wg_ref[...].astype(BF16)
        wu = wu_ref[...].astype(BF16)
        wd = wd_ref[...].astype(BF16)
        run(kind, x_ref, ws_ref, y_ref, acc, B * CAP, wg, wu, wd)
        if with_ctx:
            run(kind, xc_ref, wsc_ref, yc_ref, accc, B * CAP_C, wg, wu, wd)

    pl.when(f == 0)(lambda: step("first"))
    pl.when(jnp.logical_and(f > 0, f < last))(lambda: step("mid"))
    pl.when(f == last)(lambda: step("last"))


def _ffn(xs, ws, xc, wsc, w_gate, w_up, w_down, li):
    with_ctx = xc is not None
    in_specs = [pl.BlockSpec((B, CAP, D), lambda e, f: (0, e, 0))]
    args = [xs]
    if with_ctx:
        in_specs.append(pl.BlockSpec((B, CAP_C, D), lambda e, f: (0, e, 0)))
        args.append(xc)
    in_specs += [pl.BlockSpec((None, None, D, FF_T), lambda e, f: (li, e, 0, f)),
                 pl.BlockSpec((None, None, D, FF_T), lambda e, f: (li, e, 0, f)),
                 pl.BlockSpec((None, None, FF_T, D), lambda e, f: (li, e, f, 0)),
                 pl.BlockSpec((B, CAP, 1), lambda e, f: (0, e, 0))]
    args += [w_gate, w_up, w_down, ws]
    out_specs = [pl.BlockSpec((B, CAP, D), lambda e, f: (0, e, 0))]
    out_shape = [jax.ShapeDtypeStruct((B, N_EXP * CAP, D), BF16)]
    scratch = [pltpu.VMEM((B * CAP, D), F32)]
    if with_ctx:
        in_specs.append(pl.BlockSpec((B, CAP_C, 1), lambda e, f: (0, e, 0)))
        args.append(wsc)
        out_specs.append(pl.BlockSpec((B, CAP_C, D), lambda e, f: (0, e, 0)))
        out_shape.append(jax.ShapeDtypeStruct((B, N_EXP * CAP_C, D), BF16))
        scratch.append(pltpu.VMEM((B * CAP_C, D), F32))
    return pl.pallas_call(
        functools.partial(_ffn_kernel, with_ctx),
        grid=(N_EXP, FF // FF_T),
        in_specs=in_specs,
        out_specs=out_specs,
        out_shape=out_shape,
        scratch_shapes=scratch,
        compiler_params=_cp(("arbitrary", "arbitrary"), 56),
        name="ffn",
    )(*args)


_TN_DIMS = (((0,), (0,)), ((), ()))


def _scatter_kernel(with_ctx, final, *refs):
    refs = list(refs)
    p_ref, y_ref = refs[:2]
    pc_ref, yc_ref = refs[2:4] if with_ctx else (None, None)
    rest = refs[4:] if with_ctx else refs[2:]
    x_ref, mod_ref = rest[:2]
    gf_ref = rest[2] if final else None
    o_ref = rest[-1]

    def finish(rank_ref, yr, cap):
        p = jnp.concatenate([_one_hot_rows(rank_ref[e], cap) for e in range(N_EXP)], axis=0)
        upd = lax.dot_general(p, yr[...], _TN_DIMS, preferred_element_type=F32)
        x = x_ref[...] + mod_ref[5:6, :] * upd
        o_ref[...] = _rms(x, gf_ref[...]) if final else x

    if with_ctx:
        t = pl.program_id(1)
        pl.when(t < LPB)(lambda: finish(p_ref, y_ref, CAP))
        pl.when(t == LPB)(lambda: finish(pc_ref, yc_ref, CAP_C))
    else:
        finish(p_ref, y_ref, CAP)


def _scatter(rank, y, rank_c, yc, x3, mod, final_g):
    with_ctx = rank_c is not None
    final = final_g is not None
    tpb = TPB if with_ctx else LPB
    in_specs = [pl.BlockSpec((None, N_EXP, 1, TM), lambda b, t: (b, 0, 0, jnp.minimum(t, LPB - 1))),
                pl.BlockSpec((None, N_EXP * CAP, D), lambda b, t: (b, 0, 0))]
    args = [rank, y]
    if with_ctx:
        in_specs += [pl.BlockSpec((None, N_EXP, 1, CTX), lambda b, t: (b, 0, 0, 0)),
                     pl.BlockSpec((None, N_EXP * CAP_C, D), lambda b, t: (b, 0, 0))]
        args += [rank_c, yc]
    in_specs += [pl.BlockSpec((None, TM, D), lambda b, t: (b, t, 0)),
                 pl.BlockSpec((None, 6, D), lambda b, t: (jnp.where(t == LPB, B, b), 0, 0))]
    args += [x3, mod]
    if final:
        in_specs.append(pl.BlockSpec((1, D), lambda b, t: (0, 0)))
        args.append(final_g)
    return pl.pallas_call(
        functools.partial(_scatter_kernel, with_ctx, final),
        grid=(B, tpb),
        in_specs=in_specs,
        out_specs=pl.BlockSpec((None, TM, D), lambda b, t: (b, t, 0)),
        out_shape=jax.ShapeDtypeStruct((B, tpb * TM, D), F32),
        compiler_params=_cp(("arbitrary", "arbitrary"), 56),
        name="scatter",
    )(*args)


def _rope_tables():
    n_freq = QK_ROPE // 4
    grid_w = 64
    pos = jnp.arange(SEQ, dtype=F32)
    inv_freq = 10000.0 ** (-jnp.arange(n_freq, dtype=F32) / n_freq)
    ang_r = jnp.floor(pos / grid_w)[:, None] * inv_freq
    ang_c = (pos - grid_w * jnp.floor(pos / grid_w))[:, None] * inv_freq
    cr, sr, cc, sc = jnp.cos(ang_r), jnp.sin(ang_r), jnp.cos(ang_c), jnp.sin(ang_c)
    cos = jnp.concatenate([cr, cr, cc, cc], axis=1)
    sin = jnp.concatenate([-sr, sr, -sc, sc], axis=1)
    cos = jnp.concatenate([cos, jnp.ones((CTX, QK_ROPE), F32)], axis=0)
    sin = jnp.concatenate([sin, jnp.zeros((CTX, QK_ROPE), F32)], axis=0)
    z = jnp.zeros((S2, QK_ROPE), F32)
    qs = ATT_SCALE * LOG2E
    tq1 = jnp.concatenate([jnp.full((S2, QK_NOPE), qs, F32), cos * qs, z], axis=1)
    tq2 = jnp.concatenate([jnp.zeros((S2, QK_NOPE), F32), sin * qs, z], axis=1)
    tk1 = jnp.concatenate([cos, z], axis=1)
    tk2 = jnp.concatenate([sin, z], axis=1)
    return tq1, tq2, tk1, tk2


def _pair_swap(w):
    return jnp.concatenate([w[..., 16:32], w[..., 0:16], w[..., 48:64], w[..., 32:48]], axis=-1)


def _s5_place_kernel(bre_ref, bim_ref, cre_ref, cim_ref, w_ref, c_ref):
    hg = S5_G // 2
    w_ref[...] = jnp.zeros_like(w_ref)
    c_ref[...] = jnp.zeros_like(c_ref)
    for g in range(hg):
        r = slice(g * S5_CH, (g + 1) * S5_CH)
        for part, (b_ref, k_ref, sign) in enumerate(((bre_ref, cre_ref, 1.0), (bim_ref, cim_ref, -1.0))):
            s = slice(part * S5_HALF + g * S5_N, part * S5_HALF + (g + 1) * S5_N)
            w_ref[r, s] = b_ref[g].astype(BF16)
            c_ref[s, r] = (sign * k_ref[g]).astype(BF16)


def _s5_operators(a_re, a_im, log_dt, b_re, b_im, c_re, c_im):
    a = lax.complex(jnp.minimum(a_re.astype(F32), -1e-4), a_im.astype(F32))
    dt = jnp.exp(log_dt.astype(F32))[..., None]
    abar = jnp.exp(a * dt)
    bbar = ((abar - 1.0) / a)[..., None] * lax.complex(b_re.astype(F32), b_im.astype(F32))
    hg = S5_G // 2

    def per_group(m):
        return m.reshape(DEPTH, 2, 2, hg, m.shape[-2], m.shape[-1])

    bt = jnp.swapaxes(bbar, -1, -2)
    ct_re = jnp.swapaxes(c_re.astype(F32), -1, -2)
    ct_im = jnp.swapaxes(c_im.astype(F32), -1, -2)
    bspec = pl.BlockSpec((None, None, None, hg, S5_CH, S5_N), lambda l, z, h: (l, z, h, 0, 0, 0))
    cspec = pl.BlockSpec((None, None, None, hg, S5_N, S5_CH), lambda l, z, h: (l, z, h, 0, 0, 0))
    w_bd, c_bd = pl.pallas_call(
        _s5_place_kernel,
        grid=(DEPTH, 2, 2),
        in_specs=[bspec, bspec, cspec, cspec],
        out_specs=[pl.BlockSpec((None, None, None, GW // 2, 2 * S5_HALF), lambda l, z, h: (l, z, h, 0, 0)),
                   pl.BlockSpec((None, None, None, 2 * S5_HALF, GW // 2), lambda l, z, h: (l, z, h, 0, 0))],
        out_shape=[jax.ShapeDtypeStruct((DEPTH, 2, 2, GW // 2, 2 * S5_HALF), BF16),
                   jax.ShapeDtypeStruct((DEPTH, 2, 2, 2 * S5_HALF, GW // 2), BF16)],
        compiler_params=_cp(("arbitrary",) * 3, 32),
        name="s5_place",
    )(per_group(jnp.real(bt)), per_group(jnp.imag(bt)), per_group(ct_re), per_group(ct_im))

    def rows8(m):
        m = m.reshape(DEPTH, 2, 1, 2, S5_LT, 128)
        m = jnp.broadcast_to(m, (DEPTH, 2, B, 2, S5_LT, 128)).reshape(DEPTH, 2, 2 * B, S5_LT, 128)
        return jnp.transpose(m, (0, 1, 3, 2, 4))

    return w_bd, rows8(jnp.real(abar)), rows8(jnp.imag(abar)), c_bd


def kernel(x, c, ctx, c_ctx, norm1_g, norm2_g, w_ada, b_ada, w_in, w_out, sgu_norm_g, sgu_w,
           sgu_b, mla_q_norm_g, mla_w_uq, mla_kv_norm_g, mla_w_ukv, s5_a_re, s5_a_im, s5_log_dt,
           s5_b_re, s5_b_im, s5_c_re, s5_c_im, s5_d, s5_w_glu, s5_b_glu, conv_w, moe_w_router,
           moe_w_gate, moe_w_up, moe_w_down, final_norm_g):
    c8 = jnp.concatenate([c, c_ctx[None, :], jnp.zeros((3, D), F32)], axis=0)
    mod_all = _modulation(c8, w_ada, b_ada).reshape(DEPTH, 8, 6, D)
    rope_t = _rope_tables()
    tri = jnp.triu(jnp.ones((SEQ, SEQ), BF16))
    w_in_r = _winprep(jnp.transpose(w_in, (0, 2, 1)))
    w_out_bf = w_out.astype(BF16)
    s5_ops = _s5_operators(s5_a_re, s5_a_im, s5_log_dt, s5_b_re, s5_b_im, s5_c_re, s5_c_im)
    xs = (x.reshape(B * SEQ, D), ctx.reshape(B * CTX, D))

    for i in range(DEPTH):
        last = i == DEPTH - 1
        mod = mod_all[i]
        if i > 0:
            xs = (x3.reshape(B * S2, D),)

        wq = mla_w_uq[i].reshape(Q_LORA, MLA_HEADS, QK_NOPE + QK_ROPE)
        wq_r = wq[:, :, QK_NOPE:]
        wq_ext = jnp.concatenate([wq[:, :, :QK_NOPE], wq_r, _pair_swap(wq_r)], axis=2)
        wq_ext = wq_ext.reshape(Q_LORA, MLA_HEADS * QK_PAD).astype(BF16)
        wkv = mla_w_ukv[i].reshape(KV_LORA, MLA_HEADS, 2 * QK_NOPE)
        wkv_ext = jnp.concatenate([wkv[:, :, :QK_NOPE].reshape(KV_LORA, -1),
                                   wkv[:, :, QK_NOPE:].reshape(KV_LORA, -1)], axis=1).astype(BF16)
        sgu_p = (sgu_norm_g[i][None, :], sgu_w[i].astype(BF16),
                 jnp.repeat(jnp.swapaxes(sgu_b[i], 0, 1), 128, axis=1))
        mla_p = (mla_q_norm_g[i][None, :], mla_kv_norm_g[i][None, :], wq_ext, wkv_ext) + rope_t
        sgu_o, q, kc, v, p_s5, p_conv = _inproj(xs, mod, norm1_g[i][None, :], w_in_r, i,
                                                sgu_p, mla_p)

        attn_o = _attention(q, kc.reshape(B, S2, MLA_HEADS * QK_PAD), v.reshape(B, S2, MLA_HEADS * QK_PAD), not last)

        u3 = p_s5.reshape(B, S2, GW)
        y_fwd = _s5_pass(u3, s5_ops, i, None)
        ssm_o = _s5_pass(u3, s5_ops, i,
                         (y_fwd, s5_d[i][None, :], s5_w_glu[i].astype(BF16), s5_b_glu[i][None, :]))
        ssm_o = ssm_o.reshape(B * S2, GW)

        wr_t = jnp.transpose(moe_w_router[i])
        wr_hi = wr_t.astype(BF16)
        wr2 = jnp.concatenate([wr_hi, (wr_t - wr_hi.astype(F32)).astype(BF16)], axis=0)
        x1, h2, aff_t = _outproj(sgu_o, attn_o, ssm_o, p_conv, conv_w[i], w_out_bf, xs, mod,
                                 norm2_g[i][None, :], wr2, i, not last)

        rows_b = SEQ if last else S2
        sel = _select(aff_t, tri, not last)
        h3 = h2.reshape(B, rows_b, D)
        xs = _gather(sel[0], h3, ctx_only=False)
        xc = _gather(sel[2], h3, ctx_only=True) if not last else None
        ys = _ffn(xs, sel[1], xc, sel[3] if not last else None, moe_w_gate, moe_w_up, moe_w_down, i)
        x1_3 = x1.reshape(B, rows_b, D)
        if last:
            x3 = _scatter(sel[0], ys[0], None, None, x1_3, mod, final_norm_g[None, :])
        else:
            x3 = _scatter(sel[0], ys[0], sel[2], ys[1], x1_3, mod, None)

    return x3
```

```python
import functools

import jax
import jax.numpy as jnp
from jax import lax
from jax.experimental import pallas as pl
from jax.experimental.pallas import tpu as pltpu

F32 = jnp.float32
BF16 = jnp.bfloat16

D = 2048
B = 4
SEQ = 2048
CTX = 256
S2 = SEQ + CTX
DEPTH = 2
GW = 512
EPS = 1e-6

TM = 256
TPB = S2 // TM
LPB = SEQ // TM
NT = B * TPB
NLT = B * LPB

SGU_HEADS = 4
CHUNK = 128
MLA_HEADS = 4
QK_NOPE = 128
QK_ROPE = 64
QK_PAD = 256
Q_LORA = 384
KV_LORA = 256
ATT_SCALE = (QK_NOPE + QK_ROPE) ** -0.5
LOG2E = 1.4426950408889634

S5_G = 32
S5_N = 64
S5_CH = 16
S5_T = 128
S5_NCH = S2 // S5_T
S5_HALF = (S5_G // 2) * S5_N
S5_LT = S5_HALF // 128

N_EXP = 16
FF = D // 2
CAP = 2 * SEQ // N_EXP
CAP_C = 2 * CTX // N_EXP
FF_T = 256

IN_W = 3840
MIB = 1024 * 1024


def _cp(sem, vmem_mb):
    return pltpu.CompilerParams(dimension_semantics=sem, vmem_limit_bytes=vmem_mb * MIB)


def _lat_tile(i):
    return (i // LPB) * TPB + i % LPB


def _seg_of_tile(t):
    return jnp.where(t % TPB == TPB - 1, B, t // TPB)


def _rms(x, g):
    return x * lax.rsqrt(jnp.mean(x * x, axis=-1, keepdims=True) + EPS) * g


def _modnorm(x, g, shift, scale):
    return _rms(x, g) * (1.0 + scale) + shift


def _mod_kernel(c_ref, w_ref, b_ref, o_ref):
    a = c_ref[...]
    a = a * jax.nn.sigmoid(a)
    o_ref[...] = jnp.dot(a.astype(BF16), w_ref[...].astype(BF16),
                         preferred_element_type=F32) + b_ref[...]


def _modulation(c8, w_ada, b_ada):
    tn = 1024
    return pl.pallas_call(
        _mod_kernel,
        grid=(DEPTH, 6 * D // tn),
        in_specs=[pl.BlockSpec((8, D), lambda l, j: (0, 0)),
                  pl.BlockSpec((None, D, tn), lambda l, j: (l, 0, j)),
                  pl.BlockSpec((None, 1, tn), lambda l, j: (l, 0, j))],
        out_specs=pl.BlockSpec((None, 8, tn), lambda l, j: (l, 0, j)),
        out_shape=jax.ShapeDtypeStruct((DEPTH, 8, 6 * D), F32),
        compiler_params=_cp(("arbitrary", "arbitrary"), 40),
        name="modulation",
    )(c8, w_ada, b_ada.reshape(DEPTH, 1, 6 * D))


IN_RAW = 3776
KR0 = 2 * GW + Q_LORA + KV_LORA


WP_T = 256
WP_SWAP = KR0 // WP_T


def _winprep_kernel(prev_ref, cur_ref, o_ref):
    j = pl.program_id(1)
    keep = WP_T - QK_ROPE

    @pl.when(j < WP_SWAP)
    def _():
        o_ref[...] = cur_ref[...].astype(BF16)

    @pl.when(j == WP_SWAP)
    def _():
        cur = cur_ref[...]
        o_ref[:keep, :] = cur[:keep, :].astype(BF16)
        kr = cur[keep - QK_ROPE:keep, :]
        sw = jnp.concatenate([kr[16:32], kr[0:16], kr[48:64], kr[32:48]], axis=0)
        o_ref[keep:, :] = sw.astype(BF16)

    @pl.when(j > WP_SWAP)
    def _():
        o_ref[:QK_ROPE, :] = prev_ref[...].astype(BF16)
        o_ref[QK_ROPE:, :] = cur_ref[:keep, :].astype(BF16)


def _winprep(w_in_t):
    assert KR0 + QK_ROPE == (WP_SWAP + 1) * WP_T - QK_ROPE
    sub = WP_T // QK_ROPE
    return pl.pallas_call(
        _winprep_kernel,
        grid=(DEPTH, IN_W // WP_T),
        in_specs=[pl.BlockSpec((None, QK_ROPE, D), lambda l, j: (l, jnp.maximum(sub * j - 1, 0), 0)),
                  pl.BlockSpec((None, WP_T, D), lambda l, j: (l, j, 0))],
        out_specs=pl.BlockSpec((None, WP_T, D), lambda l, j: (l, j, 0)),
        out_shape=jax.ShapeDtypeStruct((DEPTH, IN_W, D), BF16),
        compiler_params=_cp(("arbitrary", "arbitrary"), 32),
        name="winprep",
    )(w_in_t, w_in_t)


def _stream_rows(refs, tile):
    if len(refs) == 1:
        return refs[0][...]
    return jnp.where(tile % TPB == TPB - 1, refs[1][...], refs[0][...])


def _stream_specs(xs):
    if len(xs) == 1:
        return lambda full: [pl.BlockSpec((TM, D), lambda i: (full(i), 0))]
    lat = lambda t: (t // TPB) * LPB + jnp.minimum(t % TPB, LPB - 1)
    return lambda full: [pl.BlockSpec((TM, D), lambda i: (lat(full(i)), 0)),
                         pl.BlockSpec((CTX, D), lambda i: (full(i) // TPB, 0))]


def _inproj_kernel(nx, *refs):
    (mod_ref, g_ref, w_ref, sg_ref, sw_ref, sb_ref,
     gq_ref, gkv_ref, wq_ref, wkv_ref, tq1_ref, tq2_ref, tk1_ref, tk2_ref,
     sgu_ref, q_ref, kc_ref, v_ref, s5_ref, conv_ref) = refs[nx:]
    x = _stream_rows(refs[:nx], pl.program_id(0))
    h = _modnorm(x, g_ref[...], mod_ref[0:1, :], mod_ref[1:2, :]).astype(BF16)

    def mm(a, b):
        return lax.dot_general(h, w_ref[a:b, :], _NT_DIMS, preferred_element_type=F32)

    p_a = mm(0, 2 * GW)
    pm = mm(2 * GW, 2 * GW + 768)

    p = jax.nn.gelu(p_a)
    u = p[:, :GW]
    vb = _rms(p[:, GW:], sg_ref[...]).astype(BF16)
    for ck in range(TM // CHUNK):
        r = slice(ck * CHUNK, (ck + 1) * CHUNK)
        for hd in range(SGU_HEADS):
            c = slice(hd * 128, (hd + 1) * 128)
            m = jnp.dot(sw_ref[hd], vb[r, c], preferred_element_type=F32)
            sgu_ref[r, c] = (u[r, c] * (m + sb_ref[:, c])).astype(BF16)

    s5_ref[...] = mm(1792, 2304)
    conv_ref[:, :GW] = mm(2304, 2816)

    cq = _rms(pm[:, :Q_LORA], gq_ref[...]).astype(BF16)
    q = jnp.dot(cq, wq_ref[...], preferred_element_type=F32)
    tq1 = tq1_ref[...]
    tq2 = tq2_ref[...]
    for hd in range(MLA_HEADS):
        c = slice(hd * QK_PAD, (hd + 1) * QK_PAD)
        blk = q[:, c]
        q_ref[:, c] = (blk * tq1 + pltpu.roll(blk, QK_PAD - QK_ROPE, 1) * tq2).astype(BF16)
    ckv = _rms(pm[:, Q_LORA:Q_LORA + KV_LORA], gkv_ref[...]).astype(BF16)
    kv = jnp.dot(ckv, wkv_ref[...], preferred_element_type=F32)
    ones = jnp.ones((TM, 128), BF16)
    for hd in range(MLA_HEADS):
        v_ref[:, hd * QK_PAD:hd * QK_PAD + 128] = kv[:, GW + hd * 128:GW + (hd + 1) * 128].astype(BF16)
        v_ref[:, hd * QK_PAD + 128:(hd + 1) * QK_PAD] = ones
    kt = pm[:, Q_LORA + KV_LORA:]
    kr = (kt * tk1_ref[...] + pltpu.roll(kt, QK_ROPE, 1) * tk2_ref[...]).astype(BF16)
    for hd in range(MLA_HEADS):
        kc_ref[:, hd * QK_PAD:hd * QK_PAD + QK_NOPE] = kv[:, hd * 128:(hd + 1) * 128].astype(BF16)
        kc_ref[:, hd * QK_PAD + QK_NOPE:(hd + 1) * QK_PAD] = kr

    conv_ref[:, GW:] = mm(2816, 3328) * mm(3328, 3840)


def _inproj(xs, mod, g1, w_in_r, li, sgu_p, mla_p):
    fix2 = lambda i: (0, 0)
    pos = lambda i: (i % TPB, 0)
    row = lambda i: (i, 0)
    qkw = MLA_HEADS * QK_PAD
    return pl.pallas_call(
        functools.partial(_inproj_kernel, len(xs)),
        grid=(NT,),
        in_specs=_stream_specs(xs)(lambda i: i) + [
                  pl.BlockSpec((None, 6, D), lambda i: (_seg_of_tile(i), 0, 0)),
                  pl.BlockSpec((1, D), fix2),
                  pl.BlockSpec((None, IN_W, D), lambda i: (li, 0, 0)),
                  pl.BlockSpec((1, GW), fix2),
                  pl.BlockSpec((SGU_HEADS, CHUNK, CHUNK), lambda i: (0, 0, 0)),
                  pl.BlockSpec((CHUNK, GW), fix2),
                  pl.BlockSpec((1, Q_LORA), fix2),
                  pl.BlockSpec((1, KV_LORA), fix2),
                  pl.BlockSpec((Q_LORA, qkw), fix2),
                  pl.BlockSpec((KV_LORA, 2 * GW), fix2),
                  pl.BlockSpec((TM, QK_PAD), pos),
                  pl.BlockSpec((TM, QK_PAD), pos),
                  pl.BlockSpec((TM, 128), pos),
                  pl.BlockSpec((TM, 128), pos)],
        out_specs=[pl.BlockSpec((TM, GW), row),
                   pl.BlockSpec((TM, qkw), row),
                   pl.BlockSpec((TM, qkw), row),
                   pl.BlockSpec((TM, qkw), row),
                   pl.BlockSpec((TM, GW), row),
                   pl.BlockSpec((TM, 2 * GW), row)],
        out_shape=[jax.ShapeDtypeStruct((B * S2, GW), BF16),
                   jax.ShapeDtypeStruct((B * S2, qkw), BF16),
                   jax.ShapeDtypeStruct((B * S2, qkw), BF16),
                   jax.ShapeDtypeStruct((B * S2, qkw), BF16),
                   jax.ShapeDtypeStruct((B * S2, GW), F32),
                   jax.ShapeDtypeStruct((B * S2, 2 * GW), F32)],
        compiler_params=_cp(("arbitrary",), 56),
        name="inproj",
    )(*xs, mod, g1, w_in_r, *sgu_p, *mla_p)


def _conv_tile(tile, p_ref, zp_ref, zn_ref, w_ref):
    r = tile % TPB
    bg = p_ref[:, :GW]
    z = p_ref[:, GW:]
    row = lax.broadcasted_iota(jnp.int32, (TM, GW), 0)
    has_prev = jnp.logical_and(r != 0, r != TPB - 1)
    has_next = r < LPB - 1
    prev_row = zp_ref[7:8, :] * has_prev.astype(F32)
    next_row = zn_ref[0:1, :] * has_next.astype(F32)
    zm = jnp.where(row == 0, prev_row, pltpu.roll(z, 1, 0))
    zp = jnp.where(row == TM - 1, next_row, pltpu.roll(z, TM - 1, 0))
    y = w_ref[0:1, :] * zm + w_ref[1:2, :] * z + w_ref[2:3, :] * zp
    return (bg * y).astype(BF16)


def _conv_specs(full):
    rb = TM // 8
    nrb = B * S2 // 8
    return [pl.BlockSpec((TM, 2 * GW), lambda i: (full(i), 0)),
            pl.BlockSpec((8, GW), lambda i: (jnp.maximum(full(i) * rb - 1, 0), 1)),
            pl.BlockSpec((8, GW), lambda i: (jnp.minimum((full(i) + 1) * rb, nrb - 1), 1)),
            pl.BlockSpec((3, GW), lambda i: (0, 0))]


_NT_DIMS = (((1,), (1,)), ((), ()))


def _attn_kernel(with_ctx, q_ref, kc_ref, v_ref, o_ref):
    def run(k0):
        def scores(hd):
            cq = slice(hd * QK_PAD, (hd + 1) * QK_PAD)
            return lax.dot_general(q_ref[:, cq], kc_ref[k0:, cq], _NT_DIMS, preferred_element_type=F32)

        def weights(s):
            return jnp.exp2(s - jnp.max(s, axis=-1, keepdims=True)).astype(BF16)

        def values(hd, e):
            o = jnp.dot(e, v_ref[k0:, hd * QK_PAD:(hd + 1) * QK_PAD], preferred_element_type=F32)
            o_ref[:, hd * 128:(hd + 1) * 128] = (o[:, :128] / o[:, 128:129]).astype(BF16)

        s_next = scores(0)
        e_prev = None
        for hd in range(MLA_HEADS):
            s_cur = s_next
            if hd + 1 < MLA_HEADS:
                s_next = scores(hd + 1)
            e_cur = weights(s_cur)
            if e_prev is not None:
                values(hd - 1, e_prev)
            e_prev = e_cur
        values(MLA_HEADS - 1, e_prev)

    if with_ctx:
        t = pl.program_id(1)
        pl.when(t < LPB)(lambda: run(0))
        pl.when(t == LPB)(lambda: run(SEQ))
    else:
        run(0)


def _attention(q, kc3, v3, with_ctx):
    tpb = TPB if with_ctx else LPB
    qkw = MLA_HEADS * QK_PAD
    return pl.pallas_call(
        functools.partial(_attn_kernel, with_ctx),
        grid=(B, tpb),
        in_specs=[pl.BlockSpec((TM, qkw), lambda b, t: (b * TPB + t, 0)),
                  pl.BlockSpec((None, S2, qkw), lambda b, t: (b, 0, 0)),
                  pl.BlockSpec((None, S2, qkw), lambda b, t: (b, 0, 0))],
        out_specs=pl.BlockSpec((TM, GW), lambda b, t: (b * tpb + t, 0)),
        out_shape=jax.ShapeDtypeStruct((B * tpb * TM, GW), BF16),
        compiler_params=_cp(("arbitrary", "arbitrary"), 48),
        name="attn",
    )(q, kc3, v3)


def _s5_kernel(backward, *refs):
    if backward:
        (u_ref, w_ref, are_ref, aim_ref, c_ref, yf_ref, d_ref, wg_ref, bg_ref,
         o_ref, st_re, st_im, buf) = refs
    else:
        u_ref, w_ref, are_ref, aim_ref, c_ref, o_ref, st_re, st_im, buf = refs
    k = pl.program_id(0)
    hw = GW // 2
    rows = B * S5_T
    nlt = S5_LT

    @pl.when(k == 0)
    def _():
        st_re[...] = jnp.zeros_like(st_re)
        st_im[...] = jnp.zeros_like(st_im)

    u = u_ref[...].reshape(rows, GW)
    ub = u.astype(BF16)
    for h in range(2):
        bu = jnp.dot(ub[:, h * hw:(h + 1) * hw], w_ref[h], preferred_element_type=F32)
        for b in range(B):
            for c in range(2 * nlt):
                buf[c, pl.ds(2 * b + h, S5_T, stride=8), :] = (
                    bu[b * S5_T:(b + 1) * S5_T, c * 128:(c + 1) * 128])
    a_re = are_ref[...]
    a_im = aim_ref[...]

    def step(j, carry):
        sr, si = carry
        r0 = pl.multiple_of((S5_T - 1 - j if backward else j) * 8, 8)
        br = buf[0:nlt, pl.ds(r0, 8), :]
        bi = buf[nlt:2 * nlt, pl.ds(r0, 8), :]
        nr = a_re * sr - a_im * si + br
        ni = a_re * si + a_im * sr + bi
        buf[0:nlt, pl.ds(r0, 8), :] = nr
        buf[nlt:2 * nlt, pl.ds(r0, 8), :] = ni
        return nr, ni

    sr, si = lax.fori_loop(0, S5_T, step, (st_re[...], st_im[...]), unroll=4)
    st_re[...] = sr
    st_im[...] = si

    ys = []
    for h in range(2):
        s = jnp.concatenate(
            [jnp.concatenate([buf[c, pl.ds(2 * b + h, S5_T, stride=8), :] for c in range(2 * nlt)],
                             axis=1) for b in range(B)], axis=0)
        ys.append(jnp.dot(s.astype(BF16), c_ref[h], preferred_element_type=F32))
    y = jnp.concatenate(ys, axis=1)
    if backward:
        y = y + yf_ref[...].reshape(rows, GW) + d_ref[...] * u
        g = jax.nn.gelu(y)
        z = jnp.dot(g.astype(BF16), wg_ref[...], preferred_element_type=F32) + bg_ref[...]
        o_ref[...] = (g * jax.nn.sigmoid(z)).astype(BF16).reshape(B, S5_T, GW)
    else:
        o_ref[...] = y.reshape(B, S5_T, GW)


def _s5_pass(u3, ops, li, glu):
    backward = glu is not None
    dr = 1 if backward else 0
    if backward:
        blk = lambda k: (0, S5_NCH - 1 - k, 0)
    else:
        blk = lambda k: (0, (k + SEQ // S5_T) % S5_NCH, 0)
    w_bd, a_re8, a_im8, c_bd = ops
    fixed = lambda k: (li, dr, 0, 0, 0)
    in_specs = [pl.BlockSpec((B, S5_T, GW), blk),
                pl.BlockSpec((None, None, 2, GW // 2, 2 * S5_HALF), fixed),
                pl.BlockSpec((None, None, S5_LT, 8, 128), fixed),
                pl.BlockSpec((None, None, S5_LT, 8, 128), fixed),
                pl.BlockSpec((None, None, 2, 2 * S5_HALF, GW // 2), fixed)]
    args = [u3, w_bd, a_re8, a_im8, c_bd]
    if backward:
        in_specs += [pl.BlockSpec((B, S5_T, GW), blk),
                     pl.BlockSpec((1, GW), lambda k: (0, 0)),
                     pl.BlockSpec((GW, GW), lambda k: (0, 0)),
                     pl.BlockSpec((1, GW), lambda k: (0, 0))]
        args += list(glu)
    return pl.pallas_call(
        functools.partial(_s5_kernel, backward),
        grid=(S5_NCH,),
        in_specs=in_specs,
        out_specs=pl.BlockSpec((B, S5_T, GW), blk),
        out_shape=jax.ShapeDtypeStruct((B, S2, GW), BF16 if backward else F32),
        scratch_shapes=[pltpu.VMEM((S5_LT, 8, 128), F32),
                        pltpu.VMEM((S5_LT, 8, 128), F32),
                        pltpu.VMEM((2 * S5_LT, 8 * S5_T, 128), F32)],
        compiler_params=_cp(("arbitrary",), 48),
        name="s5_bwd_glu" if backward else "s5_fwd",
    )(*args)


OP_SUB = 2


def _outproj_kernel(nx, full, *refs):
    per = nx + 7
    cw_ref, w_ref, g2_ref, wr_ref, x1_ref, h2_ref, aff_ref = refs[OP_SUB * per:]
    tiles = [full(OP_SUB * pl.program_id(0) + s) for s in range(OP_SUB)]
    acts_all = []
    for s in range(OP_SUB):
        a0_ref, a1_ref, a2_ref, p_ref, zp_ref, zn_ref, _ = refs[s * per + nx:(s + 1) * per]
        acts_all.append((a0_ref[...], a1_ref[...], a2_ref[...],
                         _conv_tile(tiles[s], p_ref, zp_ref, zn_ref, cw_ref)))
    outs = []
    for s in range(OP_SUB):
        acts = acts_all[s]
        halves = []
        for c in (slice(0, D // 2), slice(D // 2, D)):
            o = jnp.dot(acts[0], w_ref[0:GW, c], preferred_element_type=F32)
            for m in range(1, 4):
                o = o + jnp.dot(acts[m], w_ref[m * GW:(m + 1) * GW, c], preferred_element_type=F32)
            halves.append(o)
        outs.append(jnp.concatenate(halves, axis=1))
    for s in range(OP_SUB):
        tr = refs[s * per:(s + 1) * per]
        mod_ref = tr[-1]
        tile = tiles[s]
        rows = slice(s * TM, (s + 1) * TM)
        x = _stream_rows(tr[:nx], tile)
        x1 = x + mod_ref[2:3, :] * outs[s]
        x1_ref[rows, :] = x1
        h2 = _modnorm(x1, g2_ref[...], mod_ref[3:4, :], mod_ref[4:5, :]).astype(BF16)
        h2_ref[rows, :] = h2
        lg2 = lax.dot_general(wr_ref[...], h2, _NT_DIMS, preferred_element_type=F32)
        lg = lg2[:N_EXP] + lg2[N_EXP:]
        e = jnp.exp(lg - jnp.max(lg, axis=0, keepdims=True))
        aff_ref[:, rows] = e / jnp.sum(e, axis=0, keepdims=True)


def _outproj(sgu_o, attn_o, ssm_o, p_conv, conv_w, w_out_bf, xs, mod, g2, wr2, li, with_ctx):
    ntiles = NT if with_ctx else NLT
    full = (lambda i: i) if with_ctx else _lat_tile
    in_specs, args = [], []
    for s in range(OP_SUB):
        out_t = lambda i, s=s: OP_SUB * i + s
        full_t = lambda i, s=s: full(OP_SUB * i + s)
        frow = lambda i, f=full_t: (f(i), 0)
        orow = lambda i, f=out_t: (f(i), 0)
        in_specs += _stream_specs(xs)(full_t) + [
            pl.BlockSpec((TM, GW), frow),
            pl.BlockSpec((TM, GW), orow),
            pl.BlockSpec((TM, GW), frow)] + _conv_specs(full_t)[:3] + [
            pl.BlockSpec((None, 6, D), lambda i, f=full_t: (_seg_of_tile(f(i)), 0, 0))]
        args += [*xs, sgu_o, attn_o, ssm_o, p_conv, p_conv, p_conv, mod]
    in_specs += [_conv_specs(full)[3],
                 pl.BlockSpec((None, D, D), lambda i: (li, 0, 0)),
                 pl.BlockSpec((1, D), lambda i: (0, 0)),
                 pl.BlockSpec((2 * N_EXP, D), lambda i: (0, 0))]
    args += [conv_w, w_out_bf, g2, wr2]
    blk = OP_SUB * TM
    return pl.pallas_call(
        functools.partial(_outproj_kernel, len(xs), full),
        grid=(ntiles // OP_SUB,),
        in_specs=in_specs,
        out_specs=[pl.BlockSpec((blk, D), lambda i: (i, 0)),
                   pl.BlockSpec((blk, D), lambda i: (i, 0)),
                   pl.BlockSpec((N_EXP, blk), lambda i: (0, i))],
        out_shape=[jax.ShapeDtypeStruct((ntiles * TM, D), F32),
                   jax.ShapeDtypeStruct((ntiles * TM, D), BF16),
                   jax.ShapeDtypeStruct((N_EXP, ntiles * TM), F32)],
        compiler_params=_cp(("arbitrary",), 56),
        name="outproj",
    )(*args)


def _one_hot_rows(rank_row, cap):
    slot = lax.broadcasted_iota(jnp.int32, (cap, rank_row.shape[-1]), 0).astype(F32)
    return jnp.where(rank_row == slot, 1.0, 0.0).astype(BF16)


def _select_one(a, tri, cap, rank_ref, w_ref):
    ne, n = a.shape
    bits = pltpu.bitcast(a, jnp.int32)
    thr = jnp.zeros((ne, 1), jnp.int32)
    for bit in range(30, -1, -1):
        cand = thr | (1 << bit)
        cnt = jnp.sum(jnp.where(bits >= cand, 1.0, 0.0), axis=1, keepdims=True)
        thr = jnp.where(cnt >= cap, cand, thr)
    gt = jnp.where(bits > thr, 1.0, 0.0)
    eq = jnp.where(bits == thr, 1.0, 0.0)
    need = cap - jnp.sum(gt, axis=1, keepdims=True)
    eq_before = jnp.dot(eq.astype(BF16), tri, preferred_element_type=F32) - eq
    sel = gt + eq * jnp.where(eq_before < need, 1.0, 0.0)
    rank = jnp.dot(sel.astype(BF16), tri, preferred_element_type=F32) - 1.0
    rank = jnp.where(sel > 0.5, rank, -1.0)
    slot = lax.broadcasted_iota(jnp.int32, (cap, n), 0).astype(F32)
    for e in range(ne):
        rank_ref[e] = rank[e:e + 1, :]
        hit = rank[e:e + 1, :] == slot
        w_ref[e * cap:(e + 1) * cap, :] = jnp.sum(jnp.where(hit, a[e:e + 1, :], 0.0),
                                                  axis=1, keepdims=True)


def _select_kernel(with_ctx, a_ref, tri_ref, *out_refs):
    _select_one(a_ref[:, :SEQ], tri_ref[...], CAP, out_refs[0], out_refs[1])
    if with_ctx:
        _select_one(a_ref[:, SEQ:], tri_ref[:CTX, :CTX], CAP_C, out_refs[2], out_refs[3])


def _select(aff_t, tri, with_ctx):
    out_specs = [pl.BlockSpec((None, N_EXP, 1, SEQ), lambda b: (b, 0, 0, 0)),
                 pl.BlockSpec((None, N_EXP * CAP, 1), lambda b: (b, 0, 0))]
    out_shape = [jax.ShapeDtypeStruct((B, N_EXP, 1, SEQ), F32),
                 jax.ShapeDtypeStruct((B, N_EXP * CAP, 1), F32)]
    if with_ctx:
        out_specs += [pl.BlockSpec((None, N_EXP, 1, CTX), lambda b: (b, 0, 0, 0)),
                      pl.BlockSpec((None, N_EXP * CAP_C, 1), lambda b: (b, 0, 0))]
        out_shape += [jax.ShapeDtypeStruct((B, N_EXP, 1, CTX), F32),
                      jax.ShapeDtypeStruct((B, N_EXP * CAP_C, 1), F32)]
    return pl.pallas_call(
        functools.partial(_select_kernel, with_ctx),
        grid=(B,),
        in_specs=[pl.BlockSpec((N_EXP, S2 if with_ctx else SEQ), lambda b: (0, b)),
                  pl.BlockSpec((SEQ, SEQ), lambda b: (0, 0))],
        out_specs=out_specs,
        out_shape=out_shape,
        compiler_params=_cp(("arbitrary",), 48),
        name="select",
    )(aff_t, tri)


def _gather_kernel(cap, rank_ref, h_ref, o_ref):
    p = jnp.concatenate([_one_hot_rows(rank_ref[j], cap) for j in range(rank_ref.shape[0])], axis=0)
    o_ref[...] = jnp.dot(p, h_ref[...], preferred_element_type=F32).astype(BF16)


def _gather(rank, h3, ctx_only):
    if ctx_only:
        cap, n, ne = CAP_C, CTX, N_EXP
        hmap = lambda b, r: (b, TPB - 1, 0)
    else:
        cap, n, ne = CAP, SEQ, 2
        hmap = lambda b, r: (b, 0, 0)
    return pl.pallas_call(
        functools.partial(_gather_kernel, cap),
        grid=(B, N_EXP // ne),
        in_specs=[pl.BlockSpec((None, ne, 1, n), lambda b, r: (b, r, 0, 0)),
                  pl.BlockSpec((None, n, D), hmap)],
        out_specs=pl.BlockSpec((None, ne * cap, D), lambda b, r: (b, r, 0)),
        out_shape=jax.ShapeDtypeStruct((B, N_EXP * cap, D), BF16),
        compiler_params=_cp(("arbitrary", "arbitrary"), 48),
        name="gather_ctx" if ctx_only else "gather",
    )(rank, h3)


def _ffn_kernel(with_ctx, *refs):
    if with_ctx:
        x_ref, xc_ref, wg_ref, wu_ref, wd_ref, ws_ref, wsc_ref, y_ref, yc_ref, acc = refs
    else:
        x_ref, wg_ref, wu_ref, wd_ref, ws_ref, y_ref, acc = refs
    f = pl.program_id(1)
    last = FF // FF_T - 1
    nl = B * CAP

    def step(kind):
        wg = wg_ref[...].astype(BF16)
        wu = wu_ref[...].astype(BF16)
        wd = wd_ref[...].astype(BF16)
        x = x_ref[...].reshape(nl, D)
        if with_ctx:
            x = jnp.concatenate([x, xc_ref[...].reshape(B * CAP_C, D)], axis=0)
        gate = jnp.dot(x, wg, preferred_element_type=F32)
        up = jnp.dot(x, wu, preferred_element_type=F32)
        hid = (gate * jax.nn.sigmoid(gate) * up).astype(BF16)
        part = jnp.dot(hid, wd, preferred_element_type=F32)
        if kind == "first":
            acc[...] = part
        elif kind == "mid":
            acc[...] += part
        else:
            y = acc[...] + part
            y_ref[...] = (y[:nl] * ws_ref[...].reshape(nl, 1)).astype(BF16).reshape(y_ref.shape)
            if with_ctx:
                yc = y[nl:] * wsc_ref[...].reshape(B * CAP_C, 1)
                yc_ref[...] = yc.astype(BF16).reshape(yc_ref.shape)

    pl.when(f == 0)(lambda: step("first"))
    pl.when(jnp.logical_and(f > 0, f < last))(lambda: step("mid"))
    pl.when(f == last)(lambda: step("last"))


def _ffn(xs, ws, xc, wsc, w_gate, w_up, w_down, li):
    with_ctx = xc is not None
    in_specs = [pl.BlockSpec((B, CAP, D), lambda e, f: (0, e, 0))]
    args = [xs]
    if with_ctx:
        in_specs.append(pl.BlockSpec((B, CAP_C, D), lambda e, f: (0, e, 0)))
        args.append(xc)
    in_specs += [pl.BlockSpec((None, None, D, FF_T), lambda e, f: (li, e, 0, f)),
                 pl.BlockSpec((None, None, D, FF_T), lambda e, f: (li, e, 0, f)),
                 pl.BlockSpec((None, None, FF_T, D), lambda e, f: (li, e, f, 0)),
                 pl.BlockSpec((B, CAP, 1), lambda e, f: (0, e, 0))]
    args += [w_gate, w_up, w_down, ws]
    out_specs = [pl.BlockSpec((B, CAP, D), lambda e, f: (0, e, 0))]
    out_shape = [jax.ShapeDtypeStruct((B, N_EXP * CAP, D), BF16)]
    scratch = [pltpu.VMEM((B * (CAP + CAP_C if with_ctx else CAP), D), F32)]
    if with_ctx:
        in_specs.append(pl.BlockSpec((B, CAP_C, 1), lambda e, f: (0, e, 0)))
        args.append(wsc)
        out_specs.append(pl.BlockSpec((B, CAP_C, D), lambda e, f: (0, e, 0)))
        out_shape.append(jax.ShapeDtypeStruct((B, N_EXP * CAP_C, D), BF16))
    return pl.pallas_call(
        functools.partial(_ffn_kernel, with_ctx),
        grid=(N_EXP, FF // FF_T),
        in_specs=in_specs,
        out_specs=out_specs,
        out_shape=out_shape,
        scratch_shapes=scratch,
        compiler_params=_cp(("arbitrary", "arbitrary"), 56),
        name="ffn",
    )(*args)


_TN_DIMS = (((0,), (0,)), ((), ()))


def _scatter_kernel(with_ctx, final, *refs):
    refs = list(refs)
    p_ref, y_ref = refs[:2]
    pc_ref, yc_ref = refs[2:4] if with_ctx else (None, None)
    rest = refs[4:] if with_ctx else refs[2:]
    x_ref, mod_ref = rest[:2]
    gf_ref = rest[2] if final else None
    o_ref = rest[-1]

    def finish(rank_ref, yr, cap):
        ng, ge = 4, N_EXP // 4

        def hot(j):
            return jnp.concatenate([_one_hot_rows(rank_ref[e], cap)
                                    for e in range(j * ge, (j + 1) * ge)], axis=0)

        upd = None
        p_next = hot(0)
        for j in range(ng):
            p_cur = p_next
            if j + 1 < ng:
                p_next = hot(j + 1)
            part = lax.dot_general(p_cur, yr[j * ge * cap:(j + 1) * ge * cap, :], _TN_DIMS,
                                   preferred_element_type=F32)
            upd = part if upd is None else upd + part
        x = x_ref[...] + mod_ref[5:6, :] * upd
        o_ref[...] = _rms(x, gf_ref[...]) if final else x

    if with_ctx:
        t = pl.program_id(1)
        pl.when(t < LPB)(lambda: finish(p_ref, y_ref, CAP))
        pl.when(t == LPB)(lambda: finish(pc_ref, yc_ref, CAP_C))
    else:
        finish(p_ref, y_ref, CAP)


def _scatter(rank, y, rank_c, yc, x3, mod, final_g):
    with_ctx = rank_c is not None
    final = final_g is not None
    tpb = TPB if with_ctx else LPB
    in_specs = [pl.BlockSpec((None, N_EXP, 1, TM), lambda b, t: (b, 0, 0, jnp.minimum(t, LPB - 1))),
                pl.BlockSpec((None, N_EXP * CAP, D), lambda b, t: (b, 0, 0))]
    args = [rank, y]
    if with_ctx:
        in_specs += [pl.BlockSpec((None, N_EXP, 1, CTX), lambda b, t: (b, 0, 0, 0)),
                     pl.BlockSpec((None, N_EXP * CAP_C, D), lambda b, t: (b, 0, 0))]
        args += [rank_c, yc]
    in_specs += [pl.BlockSpec((None, TM, D), lambda b, t: (b, t, 0)),
                 pl.BlockSpec((None, 6, D), lambda b, t: (jnp.where(t == LPB, B, b), 0, 0))]
    args += [x3, mod]
    if final:
        in_specs.append(pl.BlockSpec((1, D), lambda b, t: (0, 0)))
        args.append(final_g)
    return pl.pallas_call(
        functools.partial(_scatter_kernel, with_ctx, final),
        grid=(B, tpb),
        in_specs=in_specs,
        out_specs=pl.BlockSpec((None, TM, D), lambda b, t: (b, t, 0)),
        out_shape=jax.ShapeDtypeStruct((B, tpb * TM, D), F32),
        compiler_params=_cp(("arbitrary", "arbitrary"), 56),
        name="scatter",
    )(*args)


def _rope_tables():
    n_freq = QK_ROPE // 4
    grid_w = 64
    pos = jnp.arange(SEQ, dtype=F32)
    inv_freq = 10000.0 ** (-jnp.arange(n_freq, dtype=F32) / n_freq)
    ang_r = jnp.floor(pos / grid_w)[:, None] * inv_freq
    ang_c = (pos - grid_w * jnp.floor(pos / grid_w))[:, None] * inv_freq
    cr, sr, cc, sc = jnp.cos(ang_r), jnp.sin(ang_r), jnp.cos(ang_c), jnp.sin(ang_c)
    cos = jnp.concatenate([cr, cr, cc, cc], axis=1)
    sin = jnp.concatenate([-sr, sr, -sc, sc], axis=1)
    cos = jnp.concatenate([cos, jnp.ones((CTX, QK_ROPE), F32)], axis=0)
    sin = jnp.concatenate([sin, jnp.zeros((CTX, QK_ROPE), F32)], axis=0)
    z = jnp.zeros((S2, QK_ROPE), F32)
    qs = ATT_SCALE * LOG2E
    tq1 = jnp.concatenate([jnp.full((S2, QK_NOPE), qs, F32), cos * qs, z], axis=1)
    tq2 = jnp.concatenate([jnp.zeros((S2, QK_NOPE), F32), sin * qs, z], axis=1)
    tk1 = jnp.concatenate([cos, z], axis=1)
    tk2 = jnp.concatenate([sin, z], axis=1)
    return tq1, tq2, tk1, tk2


def _pair_swap(w):
    return jnp.concatenate([w[..., 16:32], w[..., 0:16], w[..., 48:64], w[..., 32:48]], axis=-1)


def _s5_place_kernel(bre_ref, bim_ref, cre_ref, cim_ref, w_ref, c_ref):
    hg = S5_G // 2
    w_ref[...] = jnp.zeros_like(w_ref)
    c_ref[...] = jnp.zeros_like(c_ref)
    for g in range(hg):
        r = slice(g * S5_CH, (g + 1) * S5_CH)
        for part, (b_ref, k_ref, sign) in enumerate(((bre_ref, cre_ref, 1.0), (bim_ref, cim_ref, -1.0))):
            s = slice(part * S5_HALF + g * S5_N, part * S5_HALF + (g + 1) * S5_N)
            w_ref[r, s] = b_ref[g].astype(BF16)
            c_ref[s, r] = (sign * k_ref[g]).astype(BF16)


def _s5_operators(a_re, a_im, log_dt, b_re, b_im, c_re, c_im):
    a = lax.complex(jnp.minimum(a_re.astype(F32), -1e-4), a_im.astype(F32))
    dt = jnp.exp(log_dt.astype(F32))[..., None]
    abar = jnp.exp(a * dt)
    bbar = ((abar - 1.0) / a)[..., None] * lax.complex(b_re.astype(F32), b_im.astype(F32))
    hg = S5_G // 2

    def per_group(m):
        return m.reshape(DEPTH, 2, 2, hg, m.shape[-2], m.shape[-1])

    bt = jnp.swapaxes(bbar, -1, -2)
    ct_re = jnp.swapaxes(c_re.astype(F32), -1, -2)
    ct_im = jnp.swapaxes(c_im.astype(F32), -1, -2)
    bspec = pl.BlockSpec((None, None, None, hg, S5_CH, S5_N), lambda l, z, h: (l, z, h, 0, 0, 0))
    cspec = pl.BlockSpec((None, None, None, hg, S5_N, S5_CH), lambda l, z, h: (l, z, h, 0, 0, 0))
    w_bd, c_bd = pl.pallas_call(
        _s5_place_kernel,
        grid=(DEPTH, 2, 2),
        in_specs=[bspec, bspec, cspec, cspec],
        out_specs=[pl.BlockSpec((None, None, None, GW // 2, 2 * S5_HALF), lambda l, z, h: (l, z, h, 0, 0)),
                   pl.BlockSpec((None, None, None, 2 * S5_HALF, GW // 2), lambda l, z, h: (l, z, h, 0, 0))],
        out_shape=[jax.ShapeDtypeStruct((DEPTH, 2, 2, GW // 2, 2 * S5_HALF), BF16),
                   jax.ShapeDtypeStruct((DEPTH, 2, 2, 2 * S5_HALF, GW // 2), BF16)],
        compiler_params=_cp(("arbitrary",) * 3, 32),
        name="s5_place",
    )(per_group(jnp.real(bt)), per_group(jnp.imag(bt)), per_group(ct_re), per_group(ct_im))

    def rows8(m):
        m = m.reshape(DEPTH, 2, 1, 2, S5_LT, 128)
        m = jnp.broadcast_to(m, (DEPTH, 2, B, 2, S5_LT, 128)).reshape(DEPTH, 2, 2 * B, S5_LT, 128)
        return jnp.transpose(m, (0, 1, 3, 2, 4))

    return w_bd, rows8(jnp.real(abar)), rows8(jnp.imag(abar)), c_bd


def kernel(x, c, ctx, c_ctx, norm1_g, norm2_g, w_ada, b_ada, w_in, w_out, sgu_norm_g, sgu_w,
           sgu_b, mla_q_norm_g, mla_w_uq, mla_kv_norm_g, mla_w_ukv, s5_a_re, s5_a_im, s5_log_dt,
           s5_b_re, s5_b_im, s5_c_re, s5_c_im, s5_d, s5_w_glu, s5_b_glu, conv_w, moe_w_router,
           moe_w_gate, moe_w_up, moe_w_down, final_norm_g):
    c8 = jnp.concatenate([c, c_ctx[None, :], jnp.zeros((3, D), F32)], axis=0)
    mod_all = _modulation(c8, w_ada, b_ada).reshape(DEPTH, 8, 6, D)
    rope_t = _rope_tables()
    tri = jnp.triu(jnp.ones((SEQ, SEQ), BF16))
    w_in_r = _winprep(jnp.transpose(w_in, (0, 2, 1)))
    w_out_bf = w_out.astype(BF16)
    s5_ops = _s5_operators(s5_a_re, s5_a_im, s5_log_dt, s5_b_re, s5_b_im, s5_c_re, s5_c_im)
    xs = (x.reshape(B * SEQ, D), ctx.reshape(B * CTX, D))

    for i in range(DEPTH):
        last = i == DEPTH - 1
        mod = mod_all[i]
        if i > 0:
            xs = (x3.reshape(B * S2, D),)

        wq = mla_w_uq[i].reshape(Q_LORA, MLA_HEADS, QK_NOPE + QK_ROPE)
        wq_r = wq[:, :, QK_NOPE:]
        wq_ext = jnp.concatenate([wq[:, :, :QK_NOPE], wq_r, _pair_swap(wq_r)], axis=2)
        wq_ext = wq_ext.reshape(Q_LORA, MLA_HEADS * QK_PAD).astype(BF16)
        wkv = mla_w_ukv[i].reshape(KV_LORA, MLA_HEADS, 2 * QK_NOPE)
        wkv_ext = jnp.concatenate([wkv[:, :, :QK_NOPE].reshape(KV_LORA, -1),
                                   wkv[:, :, QK_NOPE:].reshape(KV_LORA, -1)], axis=1).astype(BF16)
        sgu_p = (sgu_norm_g[i][None, :], sgu_w[i].astype(BF16),
                 jnp.repeat(jnp.swapaxes(sgu_b[i], 0, 1), 128, axis=1))
        mla_p = (mla_q_norm_g[i][None, :], mla_kv_norm_g[i][None, :], wq_ext, wkv_ext) + rope_t
        sgu_o, q, kc, v, p_s5, p_conv = _inproj(xs, mod, norm1_g[i][None, :], w_in_r, i,
                                                sgu_p, mla_p)

        attn_o = _attention(q, kc.reshape(B, S2, MLA_HEADS * QK_PAD), v.reshape(B, S2, MLA_HEADS * QK_PAD), not last)

        u3 = p_s5.reshape(B, S2, GW)
        y_fwd = _s5_pass(u3, s5_ops, i, None)
        ssm_o = _s5_pass(u3, s5_ops, i,
                         (y_fwd, s5_d[i][None, :], s5_w_glu[i].astype(BF16), s5_b_glu[i][None, :]))
        ssm_o = ssm_o.reshape(B * S2, GW)

        wr_t = jnp.transpose(moe_w_router[i])
        wr_hi = wr_t.astype(BF16)
        wr2 = jnp.concatenate([wr_hi, (wr_t - wr_hi.astype(F32)).astype(BF16)], axis=0)
        x1, h2, aff_t = _outproj(sgu_o, attn_o, ssm_o, p_conv, conv_w[i], w_out_bf, xs, mod,
                                 norm2_g[i][None, :], wr2, i, not last)

        rows_b = SEQ if last else S2
        sel = _select(aff_t, tri, not last)
        h3 = h2.reshape(B, rows_b, D)
        xs = _gather(sel[0], h3, ctx_only=False)
        xc = _gather(sel[2], h3, ctx_only=True) if not last else None
        ys = _ffn(xs, sel[1], xc, sel[3] if not last else None, moe_w_gate, moe_w_up, moe_w_down, i)
        x1_3 = x1.reshape(B, rows_b, D)
        if last:
            x3 = _scatter(sel[0], ys[0], None, None, x1_3, mod, final_norm_g[None, :])
        else:
            x3 = _scatter(sel[0], ys[0], sel[2], ys[1], x1_3, mod, None)

    return x3
```

```python
import functools

import jax
import jax.numpy as jnp
from jax import lax
from jax.experimental import pallas as pl
from jax.experimental.pallas import tpu as pltpu

F32 = jnp.float32
BF16 = jnp.bfloat16

D = 2048
B = 4
SEQ = 2048
CTX = 256
S2 = SEQ + CTX
DEPTH = 2
GW = 512
EPS = 1e-6

TM = 256
TPB = S2 // TM
LPB = SEQ // TM
NT = B * TPB
NLT = B * LPB

SGU_HEADS = 4
CHUNK = 128
MLA_HEADS = 4
QK_NOPE = 128
QK_ROPE = 64
QK_PAD = 256
Q_LORA = 384
KV_LORA = 256
ATT_SCALE = (QK_NOPE + QK_ROPE) ** -0.5
LOG2E = 1.4426950408889634

S5_G = 32
S5_N = 64
S5_CH = 16
S5_T = 128
S5_NCH = S2 // S5_T
S5_HALF = (S5_G // 2) * S5_N
S5_LT = S5_HALF // 128
S5_PITCH = 9

N_EXP = 16
FF = D // 2
CAP = 2 * SEQ // N_EXP
CAP_C = 2 * CTX // N_EXP
FF_T = 256

IN_W = 3840
MIB = 1024 * 1024


def _cp(sem, vmem_mb):
    return pltpu.CompilerParams(dimension_semantics=sem, vmem_limit_bytes=vmem_mb * MIB)


def _lat_tile(i):
    return (i // LPB) * TPB + i % LPB


def _seg_of_tile(t):
    return jnp.where(t % TPB == TPB - 1, B, t // TPB)


def _rms(x, g):
    return x * lax.rsqrt(jnp.mean(x * x, axis=-1, keepdims=True) + EPS) * g


def _modnorm(x, g, shift, scale):
    return _rms(x, g) * (1.0 + scale) + shift


def _mod_kernel(c_ref, w_ref, b_ref, o_ref):
    a = c_ref[...]
    a = a * jax.nn.sigmoid(a)
    o_ref[...] = jnp.dot(a.astype(BF16), w_ref[...].astype(BF16),
                         preferred_element_type=F32) + b_ref[...]


def _modulation(c8, w_ada, b_ada):
    tn = 1024
    return pl.pallas_call(
        _mod_kernel,
        grid=(DEPTH, 6 * D // tn),
        in_specs=[pl.BlockSpec((8, D), lambda l, j: (0, 0)),
                  pl.BlockSpec((None, D, tn), lambda l, j: (l, 0, j)),
                  pl.BlockSpec((None, 1, tn), lambda l, j: (l, 0, j))],
        out_specs=pl.BlockSpec((None, 8, tn), lambda l, j: (l, 0, j)),
        out_shape=jax.ShapeDtypeStruct((DEPTH, 8, 6 * D), F32),
        compiler_params=_cp(("arbitrary", "arbitrary"), 40),
        name="modulation",
    )(c8, w_ada, b_ada.reshape(DEPTH, 1, 6 * D))


IN_RAW = 3776
KR0 = 2 * GW + Q_LORA + KV_LORA


WP_T = 256
WP_SWAP = KR0 // WP_T


def _winprep_kernel(prev_ref, cur_ref, o_ref):
    j = pl.program_id(1)
    keep = WP_T - QK_ROPE

    @pl.when(j < WP_SWAP)
    def _():
        o_ref[...] = cur_ref[...].astype(BF16)

    @pl.when(j == WP_SWAP)
    def _():
        cur = cur_ref[...]
        o_ref[:keep, :] = cur[:keep, :].astype(BF16)
        kr = cur[keep - QK_ROPE:keep, :]
        sw = jnp.concatenate([kr[16:32], kr[0:16], kr[48:64], kr[32:48]], axis=0)
        o_ref[keep:, :] = sw.astype(BF16)

    @pl.when(j > WP_SWAP)
    def _():
        o_ref[:QK_ROPE, :] = prev_ref[...].astype(BF16)
        o_ref[QK_ROPE:, :] = cur_ref[:keep, :].astype(BF16)


def _winprep(w_in_t):
    assert KR0 + QK_ROPE == (WP_SWAP + 1) * WP_T - QK_ROPE
    sub = WP_T // QK_ROPE
    return pl.pallas_call(
        _winprep_kernel,
        grid=(DEPTH, IN_W // WP_T),
        in_specs=[pl.BlockSpec((None, QK_ROPE, D), lambda l, j: (l, jnp.maximum(sub * j - 1, 0), 0)),
                  pl.BlockSpec((None, WP_T, D), lambda l, j: (l, j, 0))],
        out_specs=pl.BlockSpec((None, WP_T, D), lambda l, j: (l, j, 0)),
        out_shape=jax.ShapeDtypeStruct((DEPTH, IN_W, D), BF16),
        compiler_params=_cp(("arbitrary", "arbitrary"), 32),
        name="winprep",
    )(w_in_t, w_in_t)


def _stream_rows(refs, tile):
    if len(refs) == 1:
        return refs[0][...]
    return jnp.where(tile % TPB == TPB - 1, refs[1][...], refs[0][...])


def _stream_specs(xs):
    if len(xs) == 1:
        return lambda full: [pl.BlockSpec((TM, D), lambda i: (full(i), 0))]
    lat = lambda t: (t // TPB) * LPB + jnp.minimum(t % TPB, LPB - 1)
    return lambda full: [pl.BlockSpec((TM, D), lambda i: (lat(full(i)), 0)),
                         pl.BlockSpec((CTX, D), lambda i: (full(i) // TPB, 0))]


def _inproj_kernel(nx, *refs):
    (mod_ref, g_ref, w_ref, sg_ref, sw_ref, sb_ref,
     gq_ref, gkv_ref, wq_ref, wkv_ref, tq1_ref, tq2_ref, tk1_ref, tk2_ref,
     sgu_ref, q_ref, kc_ref, v_ref, s5_ref, conv_ref) = refs[nx:]
    x = _stream_rows(refs[:nx], pl.program_id(0))
    h = _modnorm(x, g_ref[...], mod_ref[0:1, :], mod_ref[1:2, :]).astype(BF16)

    def mm(a, b):
        return lax.dot_general(h, w_ref[a:b, :], _NT_DIMS, preferred_element_type=F32)

    p_a = mm(0, 2 * GW)
    pm = mm(2 * GW, 2 * GW + 768)

    p = jax.nn.gelu(p_a)
    u = p[:, :GW]
    vb = _rms(p[:, GW:], sg_ref[...]).astype(BF16)
    for ck in range(TM // CHUNK):
        r = slice(ck * CHUNK, (ck + 1) * CHUNK)
        for hd in range(SGU_HEADS):
            c = slice(hd * 128, (hd + 1) * 128)
            m = jnp.dot(sw_ref[hd], vb[r, c], preferred_element_type=F32)
            sgu_ref[r, c] = (u[r, c] * (m + sb_ref[:, c])).astype(BF16)

    s5_ref[...] = mm(1792, 2304)
    conv_ref[:, :GW] = mm(2304, 2816)

    cq = _rms(pm[:, :Q_LORA], gq_ref[...]).astype(BF16)
    q = jnp.dot(cq, wq_ref[...], preferred_element_type=F32)
    tq1 = tq1_ref[...]
    tq2 = tq2_ref[...]
    for hd in range(MLA_HEADS):
        c = slice(hd * QK_PAD, (hd + 1) * QK_PAD)
        blk = q[:, c]
        q_ref[:, c] = (blk * tq1 + pltpu.roll(blk, QK_PAD - QK_ROPE, 1) * tq2).astype(BF16)
    ckv = _rms(pm[:, Q_LORA:Q_LORA + KV_LORA], gkv_ref[...]).astype(BF16)
    kv = jnp.dot(ckv, wkv_ref[...], preferred_element_type=F32)
    ones = jnp.ones((TM, 128), BF16)
    for hd in range(MLA_HEADS):
        v_ref[:, hd * QK_PAD:hd * QK_PAD + 128] = kv[:, GW + hd * 128:GW + (hd + 1) * 128].astype(BF16)
        v_ref[:, hd * QK_PAD + 128:(hd + 1) * QK_PAD] = ones
    kt = pm[:, Q_LORA + KV_LORA:]
    kr = (kt * tk1_ref[...] + pltpu.roll(kt, QK_ROPE, 1) * tk2_ref[...]).astype(BF16)
    for hd in range(MLA_HEADS):
        kc_ref[:, hd * QK_PAD:hd * QK_PAD + QK_NOPE] = kv[:, hd * 128:(hd + 1) * 128].astype(BF16)
        kc_ref[:, hd * QK_PAD + QK_NOPE:(hd + 1) * QK_PAD] = kr

    conv_ref[:, GW:] = mm(2816, 3328) * mm(3328, 3840)


def _inproj(xs, mod, g1, w_in_r, li, sgu_p, mla_p):
    fix2 = lambda i: (0, 0)
    pos = lambda i: (i % TPB, 0)
    row = lambda i: (i, 0)
    qkw = MLA_HEADS * QK_PAD
    return pl.pallas_call(
        functools.partial(_inproj_kernel, len(xs)),
        grid=(NT,),
        in_specs=_stream_specs(xs)(lambda i: i) + [
                  pl.BlockSpec((None, 6, D), lambda i: (_seg_of_tile(i), 0, 0)),
                  pl.BlockSpec((1, D), fix2),
                  pl.BlockSpec((None, IN_W, D), lambda i: (li, 0, 0)),
                  pl.BlockSpec((1, GW), fix2),
                  pl.BlockSpec((SGU_HEADS, CHUNK, CHUNK), lambda i: (0, 0, 0)),
                  pl.BlockSpec((CHUNK, GW), fix2),
                  pl.BlockSpec((1, Q_LORA), fix2),
                  pl.BlockSpec((1, KV_LORA), fix2),
                  pl.BlockSpec((Q_LORA, qkw), fix2),
                  pl.BlockSpec((KV_LORA, 2 * GW), fix2),
                  pl.BlockSpec((TM, QK_PAD), pos),
                  pl.BlockSpec((TM, QK_PAD), pos),
                  pl.BlockSpec((TM, 128), pos),
                  pl.BlockSpec((TM, 128), pos)],
        out_specs=[pl.BlockSpec((TM, GW), row),
                   pl.BlockSpec((TM, qkw), row),
                   pl.BlockSpec((TM, qkw), row),
                   pl.BlockSpec((TM, qkw), row),
                   pl.BlockSpec((TM, GW), row),
                   pl.BlockSpec((TM, 2 * GW), row)],
        out_shape=[jax.ShapeDtypeStruct((B * S2, GW), BF16),
                   jax.ShapeDtypeStruct((B * S2, qkw), BF16),
                   jax.ShapeDtypeStruct((B * S2, qkw), BF16),
                   jax.ShapeDtypeStruct((B * S2, qkw), BF16),
                   jax.ShapeDtypeStruct((B * S2, GW), F32),
                   jax.ShapeDtypeStruct((B * S2, 2 * GW), F32)],
        compiler_params=_cp(("arbitrary",), 56),
        name="inproj",
    )(*xs, mod, g1, w_in_r, *sgu_p, *mla_p)


def _conv_tile(tile, p_ref, zp_ref, zn_ref, w_ref):
    r = tile % TPB
    bg = p_ref[:, :GW]
    z = p_ref[:, GW:]
    row = lax.broadcasted_iota(jnp.int32, (TM, GW), 0)
    has_prev = jnp.logical_and(r != 0, r != TPB - 1)
    has_next = r < LPB - 1
    prev_row = zp_ref[7:8, :] * has_prev.astype(F32)
    next_row = zn_ref[0:1, :] * has_next.astype(F32)
    zm = jnp.where(row == 0, prev_row, pltpu.roll(z, 1, 0))
    zp = jnp.where(row == TM - 1, next_row, pltpu.roll(z, TM - 1, 0))
    y = w_ref[0:1, :] * zm + w_ref[1:2, :] * z + w_ref[2:3, :] * zp
    return (bg * y).astype(BF16)


def _conv_specs(full):
    rb = TM // 8
    nrb = B * S2 // 8
    return [pl.BlockSpec((TM, 2 * GW), lambda i: (full(i), 0)),
            pl.BlockSpec((8, GW), lambda i: (jnp.maximum(full(i) * rb - 1, 0), 1)),
            pl.BlockSpec((8, GW), lambda i: (jnp.minimum((full(i) + 1) * rb, nrb - 1), 1)),
            pl.BlockSpec((3, GW), lambda i: (0, 0))]


_NT_DIMS = (((1,), (1,)), ((), ()))


def _attn_kernel(with_ctx, q_ref, kc_ref, v_ref, o_ref):
    def run(k0):
        def scores(hd):
            cq = slice(hd * QK_PAD, (hd + 1) * QK_PAD)
            return lax.dot_general(q_ref[:, cq], kc_ref[k0:, cq], _NT_DIMS, preferred_element_type=F32)

        def weights(s):
            return jnp.exp2(s - jnp.max(s, axis=-1, keepdims=True)).astype(BF16)

        def values(hd, e):
            o = jnp.dot(e, v_ref[k0:, hd * QK_PAD:(hd + 1) * QK_PAD], preferred_element_type=F32)
            o_ref[:, hd * 128:(hd + 1) * 128] = (o[:, :128] / o[:, 128:129]).astype(BF16)

        s_next = scores(0)
        e_prev = None
        for hd in range(MLA_HEADS):
            s_cur = s_next
            if hd + 1 < MLA_HEADS:
                s_next = scores(hd + 1)
            e_cur = weights(s_cur)
            if e_prev is not None:
                values(hd - 1, e_prev)
            e_prev = e_cur
        values(MLA_HEADS - 1, e_prev)

    if with_ctx:
        t = pl.program_id(1)
        pl.when(t < LPB)(lambda: run(0))
        pl.when(t == LPB)(lambda: run(SEQ))
    else:
        run(0)


def _attention(q, kc3, v3, with_ctx):
    tpb = TPB if with_ctx else LPB
    qkw = MLA_HEADS * QK_PAD
    return pl.pallas_call(
        functools.partial(_attn_kernel, with_ctx),
        grid=(B, tpb),
        in_specs=[pl.BlockSpec((TM, qkw), lambda b, t: (b * TPB + t, 0)),
                  pl.BlockSpec((None, S2, qkw), lambda b, t: (b, 0, 0)),
                  pl.BlockSpec((None, S2, qkw), lambda b, t: (b, 0, 0))],
        out_specs=pl.BlockSpec((TM, GW), lambda b, t: (b * tpb + t, 0)),
        out_shape=jax.ShapeDtypeStruct((B * tpb * TM, GW), BF16),
        compiler_params=_cp(("arbitrary", "arbitrary"), 48),
        name="attn",
    )(q, kc3, v3)


def _s5_kernel(backward, *refs):
    if backward:
        (u_ref, w_ref, are_ref, aim_ref, c_ref, yf_ref, d_ref, wg_ref, bg_ref,
         o_ref, st_re, st_im, buf) = refs
    else:
        u_ref, w_ref, are_ref, aim_ref, c_ref, o_ref, st_re, st_im, buf = refs
    k = pl.program_id(0)
    hw = GW // 2
    rows = B * S5_T
    nlt = S5_LT

    @pl.when(k == 0)
    def _():
        st_re[...] = jnp.zeros_like(st_re)
        st_im[...] = jnp.zeros_like(st_im)

    u = u_ref[...].reshape(rows, GW)
    ub = u.astype(BF16)
    for h in range(2):
        bu = jnp.dot(ub[:, h * hw:(h + 1) * hw], w_ref[h], preferred_element_type=F32)
        for b in range(B):
            for c in range(2 * nlt):
                buf[c, pl.ds(2 * b + h, S5_T, stride=S5_PITCH), :] = (
                    bu[b * S5_T:(b + 1) * S5_T, c * 128:(c + 1) * 128])
    a_re = are_ref[...]
    a_im = aim_ref[...]

    def step(j, carry):
        sr, si = carry
        r0 = (S5_T - 1 - j if backward else j) * S5_PITCH
        br = buf[0:nlt, pl.ds(r0, 8), :]
        bi = buf[nlt:2 * nlt, pl.ds(r0, 8), :]
        nr = a_re * sr - a_im * si + br
        ni = a_re * si + a_im * sr + bi
        buf[0:nlt, pl.ds(r0, 8), :] = nr
        buf[nlt:2 * nlt, pl.ds(r0, 8), :] = ni
        return nr, ni

    sr, si = lax.fori_loop(0, S5_T, step, (st_re[...], st_im[...]), unroll=4)
    st_re[...] = sr
    st_im[...] = si

    ys = []
    for h in range(2):
        s = jnp.concatenate(
            [jnp.concatenate([buf[c, pl.ds(2 * b + h, S5_T, stride=S5_PITCH), :] for c in range(2 * nlt)],
                             axis=1) for b in range(B)], axis=0)
        ys.append(jnp.dot(s.astype(BF16), c_ref[h], preferred_element_type=F32))
    y = jnp.concatenate(ys, axis=1)
    if backward:
        y = y + yf_ref[...].reshape(rows, GW) + d_ref[...] * u
        g = jax.nn.gelu(y)
        z = jnp.dot(g.astype(BF16), wg_ref[...], preferred_element_type=F32) + bg_ref[...]
        o_ref[...] = (g * jax.nn.sigmoid(z)).astype(BF16).reshape(B, S5_T, GW)
    else:
        o_ref[...] = y.reshape(B, S5_T, GW)


def _s5_pass(u3, ops, li, glu):
    backward = glu is not None
    dr = 1 if backward else 0
    if backward:
        blk = lambda k: (0, S5_NCH - 1 - k, 0)
    else:
        blk = lambda k: (0, (k + SEQ // S5_T) % S5_NCH, 0)
    w_bd, a_re8, a_im8, c_bd = ops
    fixed = lambda k: (li, dr, 0, 0, 0)
    in_specs = [pl.BlockSpec((B, S5_T, GW), blk),
                pl.BlockSpec((None, None, 2, GW // 2, 2 * S5_HALF), fixed),
                pl.BlockSpec((None, None, S5_LT, 8, 128), fixed),
                pl.BlockSpec((None, None, S5_LT, 8, 128), fixed),
                pl.BlockSpec((None, None, 2, 2 * S5_HALF, GW // 2), fixed)]
    args = [u3, w_bd, a_re8, a_im8, c_bd]
    if backward:
        in_specs += [pl.BlockSpec((B, S5_T, GW), blk),
                     pl.BlockSpec((1, GW), lambda k: (0, 0)),
                     pl.BlockSpec((GW, GW), lambda k: (0, 0)),
                     pl.BlockSpec((1, GW), lambda k: (0, 0))]
        args += list(glu)
    return pl.pallas_call(
        functools.partial(_s5_kernel, backward),
        grid=(S5_NCH,),
        in_specs=in_specs,
        out_specs=pl.BlockSpec((B, S5_T, GW), blk),
        out_shape=jax.ShapeDtypeStruct((B, S2, GW), BF16 if backward else F32),
        scratch_shapes=[pltpu.VMEM((S5_LT, 8, 128), F32),
                        pltpu.VMEM((S5_LT, 8, 128), F32),
                        pltpu.VMEM((2 * S5_LT, S5_PITCH * S5_T, 128), F32)],
        compiler_params=_cp(("arbitrary",), 48),
        name="s5_bwd_glu" if backward else "s5_fwd",
    )(*args)


OP_SUB = 2


def _outproj_kernel(nx, full, *refs):
    per = nx + 7
    cw_ref, w_ref, g2_ref, wr_ref, x1_ref, h2_ref, aff_ref = refs[OP_SUB * per:]
    tiles = [full(OP_SUB * pl.program_id(0) + s) for s in range(OP_SUB)]
    acts_all = []
    for s in range(OP_SUB):
        a0_ref, a1_ref, a2_ref, p_ref, zp_ref, zn_ref, _ = refs[s * per + nx:(s + 1) * per]
        acts_all.append((a0_ref[...], a1_ref[...], a2_ref[...],
                         _conv_tile(tiles[s], p_ref, zp_ref, zn_ref, cw_ref)))
    outs = []
    for s in range(OP_SUB):
        acts = acts_all[s]
        halves = []
        for c in (slice(0, D // 2), slice(D // 2, D)):
            o = jnp.dot(acts[0], w_ref[0:GW, c], preferred_element_type=F32)
            for m in range(1, 4):
                o = o + jnp.dot(acts[m], w_ref[m * GW:(m + 1) * GW, c], preferred_element_type=F32)
            halves.append(o)
        outs.append(jnp.concatenate(halves, axis=1))
    for s in range(OP_SUB):
        tr = refs[s * per:(s + 1) * per]
        mod_ref = tr[-1]
        tile = tiles[s]
        rows = slice(s * TM, (s + 1) * TM)
        x = _stream_rows(tr[:nx], tile)
        x1 = x + mod_ref[2:3, :] * outs[s]
        x1_ref[rows, :] = x1
        h2 = _modnorm(x1, g2_ref[...], mod_ref[3:4, :], mod_ref[4:5, :]).astype(BF16)
        h2_ref[rows, :] = h2
        lg2 = lax.dot_general(wr_ref[...], h2, _NT_DIMS, preferred_element_type=F32)
        lg = lg2[:N_EXP] + lg2[N_EXP:]
        e = jnp.exp(lg - jnp.max(lg, axis=0, keepdims=True))
        aff_ref[:, rows] = e / jnp.sum(e, axis=0, keepdims=True)


def _outproj(sgu_o, attn_o, ssm_o, p_conv, conv_w, w_out_bf, xs, mod, g2, wr2, li, with_ctx):
    ntiles = NT if with_ctx else NLT
    full = (lambda i: i) if with_ctx else _lat_tile
    in_specs, args = [], []
    for s in range(OP_SUB):
        out_t = lambda i, s=s: OP_SUB * i + s
        full_t = lambda i, s=s: full(OP_SUB * i + s)
        frow = lambda i, f=full_t: (f(i), 0)
        orow = lambda i, f=out_t: (f(i), 0)
        in_specs += _stream_specs(xs)(full_t) + [
            pl.BlockSpec((TM, GW), frow),
            pl.BlockSpec((TM, GW), orow),
            pl.BlockSpec((TM, GW), frow)] + _conv_specs(full_t)[:3] + [
            pl.BlockSpec((None, 6, D), lambda i, f=full_t: (_seg_of_tile(f(i)), 0, 0))]
        args += [*xs, sgu_o, attn_o, ssm_o, p_conv, p_conv, p_conv, mod]
    in_specs += [_conv_specs(full)[3],
                 pl.BlockSpec((None, D, D), lambda i: (li, 0, 0)),
                 pl.BlockSpec((1, D), lambda i: (0, 0)),
                 pl.BlockSpec((2 * N_EXP, D), lambda i: (0, 0))]
    args += [conv_w, w_out_bf, g2, wr2]
    blk = OP_SUB * TM
    return pl.pallas_call(
        functools.partial(_outproj_kernel, len(xs), full),
        grid=(ntiles // OP_SUB,),
        in_specs=in_specs,
        out_specs=[pl.BlockSpec((blk, D), lambda i: (i, 0)),
                   pl.BlockSpec((blk, D), lambda i: (i, 0)),
                   pl.BlockSpec((N_EXP, blk), lambda i: (0, i))],
        out_shape=[jax.ShapeDtypeStruct((ntiles * TM, D), F32),
                   jax.ShapeDtypeStruct((ntiles * TM, D), BF16),
                   jax.ShapeDtypeStruct((N_EXP, ntiles * TM), F32)],
        compiler_params=_cp(("arbitrary",), 56),
        name="outproj",
    )(*args)


def _one_hot_rows(rank_row, cap):
    slot = lax.broadcasted_iota(jnp.int32, (cap, rank_row.shape[-1]), 0).astype(F32)
    return jnp.where(rank_row == slot, 1.0, 0.0).astype(BF16)


def _select_one(a, tri, cap, rank_ref, w_ref):
    ne, n = a.shape
    bits = pltpu.bitcast(a, jnp.int32)
    thr = jnp.zeros((ne, 1), jnp.int32)
    for bit in range(30, -1, -1):
        cand = thr | (1 << bit)
        cnt = jnp.sum(jnp.where(bits >= cand, 1.0, 0.0), axis=1, keepdims=True)
        thr = jnp.where(cnt >= cap, cand, thr)
    gt = jnp.where(bits > thr, 1.0, 0.0)
    eq = jnp.where(bits == thr, 1.0, 0.0)
    need = cap - jnp.sum(gt, axis=1, keepdims=True)
    eq_before = jnp.dot(eq.astype(BF16), tri, preferred_element_type=F32) - eq
    sel = gt + eq * jnp.where(eq_before < need, 1.0, 0.0)
    rank = jnp.dot(sel.astype(BF16), tri, preferred_element_type=F32) - 1.0
    rank = jnp.where(sel > 0.5, rank, -1.0)
    slot = lax.broadcasted_iota(jnp.int32, (cap, n), 0).astype(F32)
    for e in range(ne):
        rank_ref[e] = rank[e:e + 1, :]
        hit = rank[e:e + 1, :] == slot
        w_ref[e * cap:(e + 1) * cap, :] = jnp.sum(jnp.where(hit, a[e:e + 1, :], 0.0),
                                                  axis=1, keepdims=True)


def _select_kernel(with_ctx, a_ref, tri_ref, *out_refs):
    _select_one(a_ref[:, :SEQ], tri_ref[...], CAP, out_refs[0], out_refs[1])
    if with_ctx:
        _select_one(a_ref[:, SEQ:], tri_ref[:CTX, :CTX], CAP_C, out_refs[2], out_refs[3])


def _select(aff_t, tri, with_ctx):
    out_specs = [pl.BlockSpec((None, N_EXP, 1, SEQ), lambda b: (b, 0, 0, 0)),
                 pl.BlockSpec((None, N_EXP * CAP, 1), lambda b: (b, 0, 0))]
    out_shape = [jax.ShapeDtypeStruct((B, N_EXP, 1, SEQ), F32),
                 jax.ShapeDtypeStruct((B, N_EXP * CAP, 1), F32)]
    if with_ctx:
        out_specs += [pl.BlockSpec((None, N_EXP, 1, CTX), lambda b: (b, 0, 0, 0)),
                      pl.BlockSpec((None, N_EXP * CAP_C, 1), lambda b: (b, 0, 0))]
        out_shape += [jax.ShapeDtypeStruct((B, N_EXP, 1, CTX), F32),
                      jax.ShapeDtypeStruct((B, N_EXP * CAP_C, 1), F32)]
    return pl.pallas_call(
        functools.partial(_select_kernel, with_ctx),
        grid=(B,),
        in_specs=[pl.BlockSpec((N_EXP, S2 if with_ctx else SEQ), lambda b: (0, b)),
                  pl.BlockSpec((SEQ, SEQ), lambda b: (0, 0))],
        out_specs=out_specs,
        out_shape=out_shape,
        compiler_params=_cp(("arbitrary",), 48),
        name="select",
    )(aff_t, tri)


def _gather_kernel(cap, rank_ref, h_ref, o_ref):
    p = jnp.concatenate([_one_hot_rows(rank_ref[j], cap) for j in range(rank_ref.shape[0])], axis=0)
    o_ref[...] = jnp.dot(p, h_ref[...], preferred_element_type=F32).astype(BF16)


def _gather(rank, h3, ctx_only):
    if ctx_only:
        cap, n, ne = CAP_C, CTX, N_EXP
        hmap = lambda b, r: (b, TPB - 1, 0)
    else:
        cap, n, ne = CAP, SEQ, 2
        hmap = lambda b, r: (b, 0, 0)
    return pl.pallas_call(
        functools.partial(_gather_kernel, cap),
        grid=(B, N_EXP // ne),
        in_specs=[pl.BlockSpec((None, ne, 1, n), lambda b, r: (b, r, 0, 0)),
                  pl.BlockSpec((None, n, D), hmap)],
        out_specs=pl.BlockSpec((None, ne * cap, D), lambda b, r: (b, r, 0)),
        out_shape=jax.ShapeDtypeStruct((B, N_EXP * cap, D), BF16),
        compiler_params=_cp(("arbitrary", "arbitrary"), 48),
        name="gather_ctx" if ctx_only else "gather",
    )(rank, h3)


def _ffn_kernel(with_ctx, *refs):
    if with_ctx:
        x_ref, xc_ref, wg_ref, wu_ref, wd_ref, ws_ref, wsc_ref, y_ref, yc_ref, acc = refs
    else:
        x_ref, wg_ref, wu_ref, wd_ref, ws_ref, y_ref, acc = refs
    f = pl.program_id(1)
    last = FF // FF_T - 1
    nl = B * CAP

    def step(kind):
        wg = wg_ref[...].astype(BF16)
        wu = wu_ref[...].astype(BF16)
        wd = wd_ref[...].astype(BF16)
        x = x_ref[...].reshape(nl, D)
        if with_ctx:
            x = jnp.concatenate([x, xc_ref[...].reshape(B * CAP_C, D)], axis=0)
        gate = jnp.dot(x, wg, preferred_element_type=F32)
        up = jnp.dot(x, wu, preferred_element_type=F32)
        hid = (gate * jax.nn.sigmoid(gate) * up).astype(BF16)
        part = jnp.dot(hid, wd, preferred_element_type=F32)
        if kind == "first":
            acc[...] = part
        elif kind == "mid":
            acc[...] += part
        else:
            y = acc[...] + part
            y_ref[...] = (y[:nl] * ws_ref[...].reshape(nl, 1)).astype(BF16).reshape(y_ref.shape)
            if with_ctx:
                yc = y[nl:] * wsc_ref[...].reshape(B * CAP_C, 1)
                yc_ref[...] = yc.astype(BF16).reshape(yc_ref.shape)

    pl.when(f == 0)(lambda: step("first"))
    pl.when(jnp.logical_and(f > 0, f < last))(lambda: step("mid"))
    pl.when(f == last)(lambda: step("last"))


def _ffn(xs, ws, xc, wsc, w_gate, w_up, w_down, li):
    with_ctx = xc is not None
    in_specs = [pl.BlockSpec((B, CAP, D), lambda e, f: (0, e, 0))]
    args = [xs]
    if with_ctx:
        in_specs.append(pl.BlockSpec((B, CAP_C, D), lambda e, f: (0, e, 0)))
        args.append(xc)
    in_specs += [pl.BlockSpec((None, None, D, FF_T), lambda e, f: (li, e, 0, f)),
                 pl.BlockSpec((None, None, D, FF_T), lambda e, f: (li, e, 0, f)),
                 pl.BlockSpec((None, None, FF_T, D), lambda e, f: (li, e, f, 0)),
                 pl.BlockSpec((B, CAP, 1), lambda e, f: (0, e, 0))]
    args += [w_gate, w_up, w_down, ws]
    out_specs = [pl.BlockSpec((B, CAP, D), lambda e, f: (0, e, 0))]
    out_shape = [jax.ShapeDtypeStruct((B, N_EXP * CAP, D), BF16)]
    scratch = [pltpu.VMEM((B * (CAP + CAP_C if with_ctx else CAP), D), F32)]
    if with_ctx:
        in_specs.append(pl.BlockSpec((B, CAP_C, 1), lambda e, f: (0, e, 0)))
        args.append(wsc)
        out_specs.append(pl.BlockSpec((B, CAP_C, D), lambda e, f: (0, e, 0)))
        out_shape.append(jax.ShapeDtypeStruct((B, N_EXP * CAP_C, D), BF16))
    return pl.pallas_call(
        functools.partial(_ffn_kernel, with_ctx),
        grid=(N_EXP, FF // FF_T),
        in_specs=in_specs,
        out_specs=out_specs,
        out_shape=out_shape,
        scratch_shapes=scratch,
        compiler_params=_cp(("arbitrary", "arbitrary"), 56),
        name="ffn",
    )(*args)


_TN_DIMS = (((0,), (0,)), ((), ()))


def _scatter_kernel(with_ctx, final, *refs):
    refs = list(refs)
    p_ref, y_ref = refs[:2]
    pc_ref, yc_ref = refs[2:4] if with_ctx else (None, None)
    rest = refs[4:] if with_ctx else refs[2:]
    x_ref, mod_ref = rest[:2]
    gf_ref = rest[2] if final else None
    o_ref = rest[-1]

    def finish(rank_ref, yr, cap):
        ng, ge = 4, N_EXP // 4

        def hot(j):
            return jnp.concatenate([_one_hot_rows(rank_ref[e], cap)
                                    for e in range(j * ge, (j + 1) * ge)], axis=0)

        upd = None
        p_next = hot(0)
        for j in range(ng):
            p_cur = p_next
            if j + 1 < ng:
                p_next = hot(j + 1)
            part = lax.dot_general(p_cur, yr[j * ge * cap:(j + 1) * ge * cap, :], _TN_DIMS,
                                   preferred_element_type=F32)
            upd = part if upd is None else upd + part
        x = x_ref[...] + mod_ref[5:6, :] * upd
        o_ref[...] = _rms(x, gf_ref[...]) if final else x

    if with_ctx:
        t = pl.program_id(1)
        pl.when(t < LPB)(lambda: finish(p_ref, y_ref, CAP))
        pl.when(t == LPB)(lambda: finish(pc_ref, yc_ref, CAP_C))
    else:
        finish(p_ref, y_ref, CAP)


def _scatter(rank, y, rank_c, yc, x3, mod, final_g):
    with_ctx = rank_c is not None
    final = final_g is not None
    tpb = TPB if with_ctx else LPB
    in_specs = [pl.BlockSpec((None, N_EXP, 1, TM), lambda b, t: (b, 0, 0, jnp.minimum(t, LPB - 1))),
                pl.BlockSpec((None, N_EXP * CAP, D), lambda b, t: (b, 0, 0))]
    args = [rank, y]
    if with_ctx:
        in_specs += [pl.BlockSpec((None, N_EXP, 1, CTX), lambda b, t: (b, 0, 0, 0)),
                     pl.BlockSpec((None, N_EXP * CAP_C, D), lambda b, t: (b, 0, 0))]
        args += [rank_c, yc]
    in_specs += [pl.BlockSpec((None, TM, D), lambda b, t: (b, t, 0)),
                 pl.BlockSpec((None, 6, D), lambda b, t: (jnp.where(t == LPB, B, b), 0, 0))]
    args += [x3, mod]
    if final:
        in_specs.append(pl.BlockSpec((1, D), lambda b, t: (0, 0)))
        args.append(final_g)
    return pl.pallas_call(
        functools.partial(_scatter_kernel, with_ctx, final),
        grid=(B, tpb),
        in_specs=in_specs,
        out_specs=pl.BlockSpec((None, TM, D), lambda b, t: (b, t, 0)),
        out_shape=jax.ShapeDtypeStruct((B, tpb * TM, D), F32),
        compiler_params=_cp(("arbitrary", "arbitrary"), 56),
        name="scatter",
    )(*args)


def _rope_tables():
    n_freq = QK_ROPE // 4
    grid_w = 64
    pos = jnp.arange(SEQ, dtype=F32)
    inv_freq = 10000.0 ** (-jnp.arange(n_freq, dtype=F32) / n_freq)
    ang_r = jnp.floor(pos / grid_w)[:, None] * inv_freq
    ang_c = (pos - grid_w * jnp.floor(pos / grid_w))[:, None] * inv_freq
    cr, sr, cc, sc = jnp.cos(ang_r), jnp.sin(ang_r), jnp.cos(ang_c), jnp.sin(ang_c)
    cos = jnp.concatenate([cr, cr, cc, cc], axis=1)
    sin = jnp.concatenate([-sr, sr, -sc, sc], axis=1)
    cos = jnp.concatenate([cos, jnp.ones((CTX, QK_ROPE), F32)], axis=0)
    sin = jnp.concatenate([sin, jnp.zeros((CTX, QK_ROPE), F32)], axis=0)
    z = jnp.zeros((S2, QK_ROPE), F32)
    qs = ATT_SCALE * LOG2E
    tq1 = jnp.concatenate([jnp.full((S2, QK_NOPE), qs, F32), cos * qs, z], axis=1)
    tq2 = jnp.concatenate([jnp.zeros((S2, QK_NOPE), F32), sin * qs, z], axis=1)
    tk1 = jnp.concatenate([cos, z], axis=1)
    tk2 = jnp.concatenate([sin, z], axis=1)
    return tq1, tq2, tk1, tk2


def _pair_swap(w):
    return jnp.concatenate([w[..., 16:32], w[..., 0:16], w[..., 48:64], w[..., 32:48]], axis=-1)


def _s5_place_kernel(bre_ref, bim_ref, cre_ref, cim_ref, w_ref, c_ref):
    hg = S5_G // 2
    w_ref[...] = jnp.zeros_like(w_ref)
    c_ref[...] = jnp.zeros_like(c_ref)
    for g in range(hg):
        r = slice(g * S5_CH, (g + 1) * S5_CH)
        for part, (b_ref, k_ref, sign) in enumerate(((bre_ref, cre_ref, 1.0), (bim_ref, cim_ref, -1.0))):
            s = slice(part * S5_HALF + g * S5_N, part * S5_HALF + (g + 1) * S5_N)
            w_ref[r, s] = b_ref[g].astype(BF16)
            c_ref[s, r] = (sign * k_ref[g]).astype(BF16)


def _s5_operators(a_re, a_im, log_dt, b_re, b_im, c_re, c_im):
    a = lax.complex(jnp.minimum(a_re.astype(F32), -1e-4), a_im.astype(F32))
    dt = jnp.exp(log_dt.astype(F32))[..., None]
    abar = jnp.exp(a * dt)
    bbar = ((abar - 1.0) / a)[..., None] * lax.complex(b_re.astype(F32), b_im.astype(F32))
    hg = S5_G // 2

    def per_group(m):
        return m.reshape(DEPTH, 2, 2, hg, m.shape[-2], m.shape[-1])

    bt = jnp.swapaxes(bbar, -1, -2)
    ct_re = jnp.swapaxes(c_re.astype(F32), -1, -2)
    ct_im = jnp.swapaxes(c_im.astype(F32), -1, -2)
    bspec = pl.BlockSpec((None, None, None, hg, S5_CH, S5_N), lambda l, z, h: (l, z, h, 0, 0, 0))
    cspec = pl.BlockSpec((None, None, None, hg, S5_N, S5_CH), lambda l, z, h: (l, z, h, 0, 0, 0))
    w_bd, c_bd = pl.pallas_call(
        _s5_place_kernel,
        grid=(DEPTH, 2, 2),
        in_specs=[bspec, bspec, cspec, cspec],
        out_specs=[pl.BlockSpec((None, None, None, GW // 2, 2 * S5_HALF), lambda l, z, h: (l, z, h, 0, 0)),
                   pl.BlockSpec((None, None, None, 2 * S5_HALF, GW // 2), lambda l, z, h: (l, z, h, 0, 0))],
        out_shape=[jax.ShapeDtypeStruct((DEPTH, 2, 2, GW // 2, 2 * S5_HALF), BF16),
                   jax.ShapeDtypeStruct((DEPTH, 2, 2, 2 * S5_HALF, GW // 2), BF16)],
        compiler_params=_cp(("arbitrary",) * 3, 32),
        name="s5_place",
    )(per_group(jnp.real(bt)), per_group(jnp.imag(bt)), per_group(ct_re), per_group(ct_im))

    def rows8(m):
        m = m.reshape(DEPTH, 2, 1, 2, S5_LT, 128)
        m = jnp.broadcast_to(m, (DEPTH, 2, B, 2, S5_LT, 128)).reshape(DEPTH, 2, 2 * B, S5_LT, 128)
        return jnp.transpose(m, (0, 1, 3, 2, 4))

    return w_bd, rows8(jnp.real(abar)), rows8(jnp.imag(abar)), c_bd


def kernel(x, c, ctx, c_ctx, norm1_g, norm2_g, w_ada, b_ada, w_in, w_out, sgu_norm_g, sgu_w,
           sgu_b, mla_q_norm_g, mla_w_uq, mla_kv_norm_g, mla_w_ukv, s5_a_re, s5_a_im, s5_log_dt,
           s5_b_re, s5_b_im, s5_c_re, s5_c_im, s5_d, s5_w_glu, s5_b_glu, conv_w, moe_w_router,
           moe_w_gate, moe_w_up, moe_w_down, final_norm_g):
    c8 = jnp.concatenate([c, c_ctx[None, :], jnp.zeros((3, D), F32)], axis=0)
    mod_all = _modulation(c8, w_ada, b_ada).reshape(DEPTH, 8, 6, D)
    rope_t = _rope_tables()
    tri = jnp.triu(jnp.ones((SEQ, SEQ), BF16))
    w_in_r = _winprep(jnp.transpose(w_in, (0, 2, 1)))
    w_out_bf = w_out.astype(BF16)
    s5_ops = _s5_operators(s5_a_re, s5_a_im, s5_log_dt, s5_b_re, s5_b_im, s5_c_re, s5_c_im)
    xs = (x.reshape(B * SEQ, D), ctx.reshape(B * CTX, D))

    for i in range(DEPTH):
        last = i == DEPTH - 1
        mod = mod_all[i]
        if i > 0:
            xs = (x3.reshape(B * S2, D),)

        wq = mla_w_uq[i].reshape(Q_LORA, MLA_HEADS, QK_NOPE + QK_ROPE)
        wq_r = wq[:, :, QK_NOPE:]
        wq_ext = jnp.concatenate([wq[:, :, :QK_NOPE], wq_r, _pair_swap(wq_r)], axis=2)
        wq_ext = wq_ext.reshape(Q_LORA, MLA_HEADS * QK_PAD).astype(BF16)
        wkv = mla_w_ukv[i].reshape(KV_LORA, MLA_HEADS, 2 * QK_NOPE)
        wkv_ext = jnp.concatenate([wkv[:, :, :QK_NOPE].reshape(KV_LORA, -1),
                                   wkv[:, :, QK_NOPE:].reshape(KV_LORA, -1)], axis=1).astype(BF16)
        sgu_p = (sgu_norm_g[i][None, :], sgu_w[i].astype(BF16),
                 jnp.repeat(jnp.swapaxes(sgu_b[i], 0, 1), 128, axis=1))
        mla_p = (mla_q_norm_g[i][None, :], mla_kv_norm_g[i][None, :], wq_ext, wkv_ext) + rope_t
        sgu_o, q, kc, v, p_s5, p_conv = _inproj(xs, mod, norm1_g[i][None, :], w_in_r, i,
                                                sgu_p, mla_p)

        attn_o = _attention(q, kc.reshape(B, S2, MLA_HEADS * QK_PAD), v.reshape(B, S2, MLA_HEADS * QK_PAD), not last)

        u3 = p_s5.reshape(B, S2, GW)
        y_fwd = _s5_pass(u3, s5_ops, i, None)
        ssm_o = _s5_pass(u3, s5_ops, i,
                         (y_fwd, s5_d[i][None, :], s5_w_glu[i].astype(BF16), s5_b_glu[i][None, :]))
        ssm_o = ssm_o.reshape(B * S2, GW)

        wr_t = jnp.transpose(moe_w_router[i])
        wr_hi = wr_t.astype(BF16)
        wr2 = jnp.concatenate([wr_hi, (wr_t - wr_hi.astype(F32)).astype(BF16)], axis=0)
        x1, h2, aff_t = _outproj(sgu_o, attn_o, ssm_o, p_conv, conv_w[i], w_out_bf, xs, mod,
                                 norm2_g[i][None, :], wr2, i, not last)

        rows_b = SEQ if last else S2
        sel = _select(aff_t, tri, not last)
        h3 = h2.reshape(B, rows_b, D)
        xs = _gather(sel[0], h3, ctx_only=False)
        xc = _gather(sel[2], h3, ctx_only=True) if not last else None
        ys = _ffn(xs, sel[1], xc, sel[3] if not last else None, moe_w_gate, moe_w_up, moe_w_down, i)
        x1_3 = x1.reshape(B, rows_b, D)
        if last:
            x3 = _scatter(sel[0], ys[0], None, None, x1_3, mod, final_norm_g[None, :])
        else:
            x3 = _scatter(sel[0], ys[0], sel[2], ys[1], x1_3, mod, None)

    return x3
```

```python
import functools

import jax
import jax.numpy as jnp
from jax import lax
from jax.experimental import pallas as pl
from jax.experimental.pallas import tpu as pltpu

F32 = jnp.float32
BF16 = jnp.bfloat16

D = 2048
B = 4
SEQ = 2048
CTX = 256
S2 = SEQ + CTX
DEPTH = 2
GW = 512
EPS = 1e-6

TM = 256
TPB = S2 // TM
LPB = SEQ // TM
NT = B * TPB
NLT = B * LPB

SGU_HEADS = 4
CHUNK = 128
MLA_HEADS = 4
QK_NOPE = 128
QK_ROPE = 64
QK_PAD = 256
Q_LORA = 384
KV_LORA = 256
ATT_SCALE = (QK_NOPE + QK_ROPE) ** -0.5
LOG2E = 1.4426950408889634

S5_G = 32
S5_N = 64
S5_CH = 16
S5_T = 128
S5_NCH = S2 // S5_T
S5_HALF = (S5_G // 2) * S5_N
S5_LT = S5_HALF // 128
S5_PITCH = 9

N_EXP = 16
FF = D // 2
CAP = 2 * SEQ // N_EXP
CAP_C = 2 * CTX // N_EXP
FF_T = 256

IN_W = 3840
MIB = 1024 * 1024


def _cp(sem, vmem_mb):
    return pltpu.CompilerParams(dimension_semantics=sem, vmem_limit_bytes=vmem_mb * MIB)


def _lat_tile(i):
    return (i // LPB) * TPB + i % LPB


def _seg_of_tile(t):
    return jnp.where(t % TPB == TPB - 1, B, t // TPB)


def _rms(x, g):
    return x * lax.rsqrt(jnp.mean(x * x, axis=-1, keepdims=True) + EPS) * g


def _modnorm(x, g, shift, scale):
    return _rms(x, g) * (1.0 + scale) + shift


def _mod_kernel(c_ref, w_ref, b_ref, o_ref):
    a = c_ref[...]
    a = a * jax.nn.sigmoid(a)
    o_ref[...] = jnp.dot(a.astype(BF16), w_ref[...].astype(BF16),
                         preferred_element_type=F32) + b_ref[...]


def _modulation(c8, w_ada, b_ada):
    tn = 1024
    return pl.pallas_call(
        _mod_kernel,
        grid=(DEPTH, 6 * D // tn),
        in_specs=[pl.BlockSpec((8, D), lambda l, j: (0, 0)),
                  pl.BlockSpec((None, D, tn), lambda l, j: (l, 0, j)),
                  pl.BlockSpec((None, 1, tn), lambda l, j: (l, 0, j))],
        out_specs=pl.BlockSpec((None, 8, tn), lambda l, j: (l, 0, j)),
        out_shape=jax.ShapeDtypeStruct((DEPTH, 8, 6 * D), F32),
        compiler_params=_cp(("arbitrary", "arbitrary"), 40),
        name="modulation",
    )(c8, w_ada, b_ada.reshape(DEPTH, 1, 6 * D))


IN_RAW = 3776
KR0 = 2 * GW + Q_LORA + KV_LORA


WP_T = 256
WP_SWAP = KR0 // WP_T


def _winprep_kernel(prev_ref, cur_ref, o_ref):
    j = pl.program_id(1)
    keep = WP_T - QK_ROPE

    @pl.when(j < WP_SWAP)
    def _():
        o_ref[...] = cur_ref[...].astype(BF16)

    @pl.when(j == WP_SWAP)
    def _():
        cur = cur_ref[...]
        o_ref[:keep, :] = cur[:keep, :].astype(BF16)
        kr = cur[keep - QK_ROPE:keep, :]
        sw = jnp.concatenate([kr[16:32], kr[0:16], kr[48:64], kr[32:48]], axis=0)
        o_ref[keep:, :] = sw.astype(BF16)

    @pl.when(j > WP_SWAP)
    def _():
        o_ref[:QK_ROPE, :] = prev_ref[...].astype(BF16)
        o_ref[QK_ROPE:, :] = cur_ref[:keep, :].astype(BF16)


def _winprep(w_in_t):
    assert KR0 + QK_ROPE == (WP_SWAP + 1) * WP_T - QK_ROPE
    sub = WP_T // QK_ROPE
    return pl.pallas_call(
        _winprep_kernel,
        grid=(DEPTH, IN_W // WP_T),
        in_specs=[pl.BlockSpec((None, QK_ROPE, D), lambda l, j: (l, jnp.maximum(sub * j - 1, 0), 0)),
                  pl.BlockSpec((None, WP_T, D), lambda l, j: (l, j, 0))],
        out_specs=pl.BlockSpec((None, WP_T, D), lambda l, j: (l, j, 0)),
        out_shape=jax.ShapeDtypeStruct((DEPTH, IN_W, D), BF16),
        compiler_params=_cp(("arbitrary", "arbitrary"), 32),
        name="winprep",
    )(w_in_t, w_in_t)


def _stream_rows(refs, tile):
    if len(refs) == 1:
        return refs[0][...]
    return jnp.where(tile % TPB == TPB - 1, refs[1][...], refs[0][...])


def _stream_specs(xs):
    if len(xs) == 1:
        return lambda full: [pl.BlockSpec((TM, D), lambda i: (full(i), 0))]
    lat = lambda t: (t // TPB) * LPB + jnp.minimum(t % TPB, LPB - 1)
    return lambda full: [pl.BlockSpec((TM, D), lambda i: (lat(full(i)), 0)),
                         pl.BlockSpec((CTX, D), lambda i: (full(i) // TPB, 0))]


def _inproj_kernel(nx, *refs):
    (mod_ref, g_ref, w_ref, sg_ref, sw_ref, sb_ref,
     gq_ref, gkv_ref, wq_ref, wkv_ref, tq1_ref, tq2_ref, tk1_ref, tk2_ref,
     sgu_ref, q_ref, kc_ref, v_ref, s5_ref, conv_ref) = refs[nx:]
    x = _stream_rows(refs[:nx], pl.program_id(0))
    h = _modnorm(x, g_ref[...], mod_ref[0:1, :], mod_ref[1:2, :]).astype(BF16)

    def mm(a, b):
        return lax.dot_general(h, w_ref[a:b, :], _NT_DIMS, preferred_element_type=F32)

    p_a = mm(0, 2 * GW)
    pm = mm(2 * GW, 2 * GW + 768)

    p = jax.nn.gelu(p_a)
    u = p[:, :GW]
    vb = _rms(p[:, GW:], sg_ref[...]).astype(BF16)
    for ck in range(TM // CHUNK):
        r = slice(ck * CHUNK, (ck + 1) * CHUNK)
        for hd in range(SGU_HEADS):
            c = slice(hd * 128, (hd + 1) * 128)
            m = jnp.dot(sw_ref[hd], vb[r, c], preferred_element_type=F32)
            sgu_ref[r, c] = (u[r, c] * (m + sb_ref[:, c])).astype(BF16)

    s5_ref[...] = mm(1792, 2304)
    conv_ref[:, :GW] = mm(2304, 2816)

    cq = _rms(pm[:, :Q_LORA], gq_ref[...]).astype(BF16)
    q = jnp.dot(cq, wq_ref[...], preferred_element_type=F32)
    tq1 = tq1_ref[...]
    tq2 = tq2_ref[...]
    for hd in range(MLA_HEADS):
        c = slice(hd * QK_PAD, (hd + 1) * QK_PAD)
        blk = q[:, c]
        q_ref[:, c] = (blk * tq1 + pltpu.roll(blk, QK_PAD - QK_ROPE, 1) * tq2).astype(BF16)
    ckv = _rms(pm[:, Q_LORA:Q_LORA + KV_LORA], gkv_ref[...]).astype(BF16)
    kv = jnp.dot(ckv, wkv_ref[...], preferred_element_type=F32)
    ones = jnp.ones((TM, 128), BF16)
    for hd in range(MLA_HEADS):
        v_ref[:, hd * QK_PAD:hd * QK_PAD + 128] = kv[:, GW + hd * 128:GW + (hd + 1) * 128].astype(BF16)
        v_ref[:, hd * QK_PAD + 128:(hd + 1) * QK_PAD] = ones
    kt = pm[:, Q_LORA + KV_LORA:]
    kr = (kt * tk1_ref[...] + pltpu.roll(kt, QK_ROPE, 1) * tk2_ref[...]).astype(BF16)
    for hd in range(MLA_HEADS):
        kc_ref[:, hd * QK_PAD:hd * QK_PAD + QK_NOPE] = kv[:, hd * 128:(hd + 1) * 128].astype(BF16)
        kc_ref[:, hd * QK_PAD + QK_NOPE:(hd + 1) * QK_PAD] = kr

    conv_ref[:, GW:] = mm(2816, 3328) * mm(3328, 3840)


def _inproj(xs, mod, g1, w_in_r, li, sgu_p, mla_p):
    fix2 = lambda i: (0, 0)
    pos = lambda i: (i % TPB, 0)
    row = lambda i: (i, 0)
    qkw = MLA_HEADS * QK_PAD
    return pl.pallas_call(
        functools.partial(_inproj_kernel, len(xs)),
        grid=(NT,),
        in_specs=_stream_specs(xs)(lambda i: i) + [
                  pl.BlockSpec((None, 6, D), lambda i: (_seg_of_tile(i), 0, 0)),
                  pl.BlockSpec((1, D), fix2),
                  pl.BlockSpec((None, IN_W, D), lambda i: (li, 0, 0)),
                  pl.BlockSpec((1, GW), fix2),
                  pl.BlockSpec((SGU_HEADS, CHUNK, CHUNK), lambda i: (0, 0, 0)),
                  pl.BlockSpec((CHUNK, GW), fix2),
                  pl.BlockSpec((1, Q_LORA), fix2),
                  pl.BlockSpec((1, KV_LORA), fix2),
                  pl.BlockSpec((Q_LORA, qkw), fix2),
                  pl.BlockSpec((KV_LORA, 2 * GW), fix2),
                  pl.BlockSpec((TM, QK_PAD), pos),
                  pl.BlockSpec((TM, QK_PAD), pos),
                  pl.BlockSpec((TM, 128), pos),
                  pl.BlockSpec((TM, 128), pos)],
        out_specs=[pl.BlockSpec((TM, GW), row),
                   pl.BlockSpec((TM, qkw), row),
                   pl.BlockSpec((TM, qkw), row),
                   pl.BlockSpec((TM, qkw), row),
                   pl.BlockSpec((TM, GW), row),
                   pl.BlockSpec((TM, 2 * GW), row)],
        out_shape=[jax.ShapeDtypeStruct((B * S2, GW), BF16),
                   jax.ShapeDtypeStruct((B * S2, qkw), BF16),
                   jax.ShapeDtypeStruct((B * S2, qkw), BF16),
                   jax.ShapeDtypeStruct((B * S2, qkw), BF16),
                   jax.ShapeDtypeStruct((B * S2, GW), F32),
                   jax.ShapeDtypeStruct((B * S2, 2 * GW), F32)],
        compiler_params=_cp(("arbitrary",), 56),
        name="inproj",
    )(*xs, mod, g1, w_in_r, *sgu_p, *mla_p)


def _conv_tile(tile, p_ref, zp_ref, zn_ref, w_ref):
    r = tile % TPB
    bg = p_ref[:, :GW]
    z = p_ref[:, GW:]
    row = lax.broadcasted_iota(jnp.int32, (TM, GW), 0)
    has_prev = jnp.logical_and(r != 0, r != TPB - 1)
    has_next = r < LPB - 1
    prev_row = zp_ref[7:8, :] * has_prev.astype(F32)
    next_row = zn_ref[0:1, :] * has_next.astype(F32)
    zm = jnp.where(row == 0, prev_row, pltpu.roll(z, 1, 0))
    zp = jnp.where(row == TM - 1, next_row, pltpu.roll(z, TM - 1, 0))
    y = w_ref[0:1, :] * zm + w_ref[1:2, :] * z + w_ref[2:3, :] * zp
    return (bg * y).astype(BF16)


def _conv_specs(full):
    rb = TM // 8
    nrb = B * S2 // 8
    return [pl.BlockSpec((TM, 2 * GW), lambda i: (full(i), 0)),
            pl.BlockSpec((8, GW), lambda i: (jnp.maximum(full(i) * rb - 1, 0), 1)),
            pl.BlockSpec((8, GW), lambda i: (jnp.minimum((full(i) + 1) * rb, nrb - 1), 1)),
            pl.BlockSpec((3, GW), lambda i: (0, 0))]


_NT_DIMS = (((1,), (1,)), ((), ()))


def _attn_kernel(with_ctx, q_ref, kc_ref, v_ref, o_ref):
    def run(k0):
        def scores(hd):
            cq = slice(hd * QK_PAD, (hd + 1) * QK_PAD)
            return lax.dot_general(q_ref[:, cq], kc_ref[k0:, cq], _NT_DIMS, preferred_element_type=F32)

        def weights(s):
            return jnp.exp2(s - jnp.max(s, axis=-1, keepdims=True)).astype(BF16)

        def values(hd, e):
            o = jnp.dot(e, v_ref[k0:, hd * QK_PAD:(hd + 1) * QK_PAD], preferred_element_type=F32)
            o_ref[:, hd * 128:(hd + 1) * 128] = (o[:, :128] / o[:, 128:129]).astype(BF16)

        s_next = scores(0)
        e_prev = None
        for hd in range(MLA_HEADS):
            s_cur = s_next
            if hd + 1 < MLA_HEADS:
                s_next = scores(hd + 1)
            e_cur = weights(s_cur)
            if e_prev is not None:
                values(hd - 1, e_prev)
            e_prev = e_cur
        values(MLA_HEADS - 1, e_prev)

    if with_ctx:
        t = pl.program_id(1)
        pl.when(t < LPB)(lambda: run(0))
        pl.when(t == LPB)(lambda: run(SEQ))
    else:
        run(0)


def _attention(q, kc3, v3, with_ctx):
    tpb = TPB if with_ctx else LPB
    qkw = MLA_HEADS * QK_PAD
    return pl.pallas_call(
        functools.partial(_attn_kernel, with_ctx),
        grid=(B, tpb),
        in_specs=[pl.BlockSpec((TM, qkw), lambda b, t: (b * TPB + t, 0)),
                  pl.BlockSpec((None, S2, qkw), lambda b, t: (b, 0, 0)),
                  pl.BlockSpec((None, S2, qkw), lambda b, t: (b, 0, 0))],
        out_specs=pl.BlockSpec((TM, GW), lambda b, t: (b * tpb + t, 0)),
        out_shape=jax.ShapeDtypeStruct((B * tpb * TM, GW), BF16),
        compiler_params=_cp(("arbitrary", "arbitrary"), 48),
        name="attn",
    )(q, kc3, v3)


def _s5_kernel(backward, *refs):
    if backward:
        (u_ref, w_ref, are_ref, aim_ref, c_ref, yf_ref, d_ref, wg_ref, bg_ref,
         o_ref, st_re, st_im, buf0, buf1) = refs
    else:
        u_ref, w_ref, are_ref, aim_ref, c_ref, o_ref, st_re, st_im, buf0, buf1 = refs
    bufs = (buf0, buf1)
    k = pl.program_id(0)
    hw = GW // 2
    nlt = S5_LT
    ht = S5_T // 2
    npc = nlt

    @pl.when(k == 0)
    def _():
        st_re[...] = jnp.zeros_like(st_re)
        st_im[...] = jnp.zeros_like(st_im)

    def rows_of(ref, hf):
        return jnp.concatenate([ref[b, hf * ht:(hf + 1) * ht, :] for b in range(B)], axis=0)

    def expand_pieces(hf):
        ub = rows_of(u_ref, hf).astype(BF16)
        buf = bufs[hf]

        def piece(h, n):
            bu = jnp.dot(ub[:, h * hw:(h + 1) * hw], w_ref[h, :, n * 256:(n + 1) * 256],
                         preferred_element_type=F32)
            for b in range(B):
                for cc in range(2):
                    buf[2 * n + cc, pl.ds(2 * b + h, ht, stride=S5_PITCH), :] = (
                        bu[b * ht:(b + 1) * ht, cc * 128:(cc + 1) * 128])

        return [functools.partial(piece, h, n) for h in range(2) for n in range(npc)]

    a_re = are_ref[...]
    a_im = aim_ref[...]
    state = [st_re[...], st_im[...]]

    def scan_pieces(hf, per):
        order = list(range(ht - 1, -1, -1)) if backward else list(range(ht))
        buf = bufs[hf]

        def piece(steps):
            sr, si = state
            for j in steps:
                r0 = S5_PITCH * j
                br = buf[0:nlt, r0:r0 + 8, :]
                bi = buf[nlt:2 * nlt, r0:r0 + 8, :]
                sr, si = a_re * sr - a_im * si + br, a_re * si + a_im * sr + bi
                buf[0:nlt, r0:r0 + 8, :] = sr
                buf[nlt:2 * nlt, r0:r0 + 8, :] = si
            state[0], state[1] = sr, si

        return [functools.partial(piece, order[i:i + per]) for i in range(0, ht, per)]

    acc = {}

    def readout_pieces(hf):
        buf = bufs[hf]

        def piece(h, n):
            s = jnp.concatenate(
                [jnp.concatenate([buf[2 * n + cc, pl.ds(2 * b + h, ht, stride=S5_PITCH), :]
                                  for cc in range(2)], axis=1) for b in range(B)], axis=0)
            part = jnp.dot(s.astype(BF16), c_ref[h, n * 256:(n + 1) * 256, :],
                           preferred_element_type=F32)
            acc[(hf, h)] = part if n == 0 else acc[(hf, h)] + part

        return [functools.partial(piece, h, n) for h in range(2) for n in range(npc)]

    def finish(hf):
        y = jnp.concatenate([acc[(hf, 0)], acc[(hf, 1)]], axis=1)
        if backward:
            y = y + rows_of(yf_ref, hf) + d_ref[...] * rows_of(u_ref, hf)
            g = jax.nn.gelu(y)
            z = jnp.dot(g.astype(BF16), wg_ref[...], preferred_element_type=F32) + bg_ref[...]
            y = (g * jax.nn.sigmoid(z)).astype(BF16)
        for b in range(B):
            o_ref[b, hf * ht:(hf + 1) * ht, :] = y[b * ht:(b + 1) * ht]

    def interleave(mxu, vpu):
        for i, m in enumerate(mxu):
            m()
            if i < len(vpu):
                vpu[i]()

    first, second = (1, 0) if backward else (0, 1)
    per = ht // (2 * npc)
    for m in expand_pieces(first):
        m()
    interleave(expand_pieces(second), scan_pieces(first, per))
    interleave(readout_pieces(first), scan_pieces(second, per))
    st_re[...] = state[0]
    st_im[...] = state[1]
    finish(first)
    for m in readout_pieces(second):
        m()
    finish(second)


def _s5_pass(u3, ops, li, glu):
    backward = glu is not None
    dr = 1 if backward else 0
    if backward:
        blk = lambda k: (0, S5_NCH - 1 - k, 0)
    else:
        blk = lambda k: (0, (k + SEQ // S5_T) % S5_NCH, 0)
    w_bd, a_re8, a_im8, c_bd = ops
    fixed = lambda k: (li, dr, 0, 0, 0)
    in_specs = [pl.BlockSpec((B, S5_T, GW), blk),
                pl.BlockSpec((None, None, 2, GW // 2, 2 * S5_HALF), fixed),
                pl.BlockSpec((None, None, S5_LT, 8, 128), fixed),
                pl.BlockSpec((None, None, S5_LT, 8, 128), fixed),
                pl.BlockSpec((None, None, 2, 2 * S5_HALF, GW // 2), fixed)]
    args = [u3, w_bd, a_re8, a_im8, c_bd]
    if backward:
        in_specs += [pl.BlockSpec((B, S5_T, GW), blk),
                     pl.BlockSpec((1, GW), lambda k: (0, 0)),
                     pl.BlockSpec((GW, GW), lambda k: (0, 0)),
                     pl.BlockSpec((1, GW), lambda k: (0, 0))]
        args += list(glu)
    return pl.pallas_call(
        functools.partial(_s5_kernel, backward),
        grid=(S5_NCH,),
        in_specs=in_specs,
        out_specs=pl.BlockSpec((B, S5_T, GW), blk),
        out_shape=jax.ShapeDtypeStruct((B, S2, GW), BF16 if backward else F32),
        scratch_shapes=[pltpu.VMEM((S5_LT, 8, 128), F32),
                        pltpu.VMEM((S5_LT, 8, 128), F32),
                        pltpu.VMEM((2 * S5_LT, S5_PITCH * S5_T // 2, 128), F32),
                        pltpu.VMEM((2 * S5_LT, S5_PITCH * S5_T // 2, 128), F32)],
        compiler_params=_cp(("arbitrary",), 48),
        name="s5_bwd_glu" if backward else "s5_fwd",
    )(*args)


OP_SUB = 2


def _outproj_kernel(nx, full, *refs):
    per = nx + 7
    cw_ref, w_ref, g2_ref, wr_ref, x1_ref, h2_ref, aff_ref = refs[OP_SUB * per:]
    tiles = [full(OP_SUB * pl.program_id(0) + s) for s in range(OP_SUB)]
    acts_all = []
    for s in range(OP_SUB):
        a0_ref, a1_ref, a2_ref, p_ref, zp_ref, zn_ref, _ = refs[s * per + nx:(s + 1) * per]
        acts_all.append((a0_ref[...], a1_ref[...], a2_ref[...],
                         _conv_tile(tiles[s], p_ref, zp_ref, zn_ref, cw_ref)))
    outs = []
    for s in range(OP_SUB):
        acts = acts_all[s]
        halves = []
        for c in (slice(0, D // 2), slice(D // 2, D)):
            o = jnp.dot(acts[0], w_ref[0:GW, c], preferred_element_type=F32)
            for m in range(1, 4):
                o = o + jnp.dot(acts[m], w_ref[m * GW:(m + 1) * GW, c], preferred_element_type=F32)
            halves.append(o)
        outs.append(jnp.concatenate(halves, axis=1))
    for s in range(OP_SUB):
        tr = refs[s * per:(s + 1) * per]
        mod_ref = tr[-1]
        tile = tiles[s]
        rows = slice(s * TM, (s + 1) * TM)
        x = _stream_rows(tr[:nx], tile)
        x1 = x + mod_ref[2:3, :] * outs[s]
        x1_ref[rows, :] = x1
        h2 = _modnorm(x1, g2_ref[...], mod_ref[3:4, :], mod_ref[4:5, :]).astype(BF16)
        h2_ref[rows, :] = h2
        lg2 = lax.dot_general(wr_ref[...], h2, _NT_DIMS, preferred_element_type=F32)
        lg = lg2[:N_EXP] + lg2[N_EXP:]
        e = jnp.exp(lg - jnp.max(lg, axis=0, keepdims=True))
        aff_ref[:, rows] = e / jnp.sum(e, axis=0, keepdims=True)


def _outproj(sgu_o, attn_o, ssm_o, p_conv, conv_w, w_out_bf, xs, mod, g2, wr2, li, with_ctx):
    ntiles = NT if with_ctx else NLT
    full = (lambda i: i) if with_ctx else _lat_tile
    in_specs, args = [], []
    for s in range(OP_SUB):
        out_t = lambda i, s=s: OP_SUB * i + s
        full_t = lambda i, s=s: full(OP_SUB * i + s)
        frow = lambda i, f=full_t: (f(i), 0)
        orow = lambda i, f=out_t: (f(i), 0)
        in_specs += _stream_specs(xs)(full_t) + [
            pl.BlockSpec((TM, GW), frow),
            pl.BlockSpec((TM, GW), orow),
            pl.BlockSpec((TM, GW), frow)] + _conv_specs(full_t)[:3] + [
            pl.BlockSpec((None, 6, D), lambda i, f=full_t: (_seg_of_tile(f(i)), 0, 0))]
        args += [*xs, sgu_o, attn_o, ssm_o, p_conv, p_conv, p_conv, mod]
    in_specs += [_conv_specs(full)[3],
                 pl.BlockSpec((None, D, D), lambda i: (li, 0, 0)),
                 pl.BlockSpec((1, D), lambda i: (0, 0)),
                 pl.BlockSpec((2 * N_EXP, D), lambda i: (0, 0))]
    args += [conv_w, w_out_bf, g2, wr2]
    blk = OP_SUB * TM
    return pl.pallas_call(
        functools.partial(_outproj_kernel, len(xs), full),
        grid=(ntiles // OP_SUB,),
        in_specs=in_specs,
        out_specs=[pl.BlockSpec((blk, D), lambda i: (i, 0)),
                   pl.BlockSpec((blk, D), lambda i: (i, 0)),
                   pl.BlockSpec((N_EXP, blk), lambda i: (0, i))],
        out_shape=[jax.ShapeDtypeStruct((ntiles * TM, D), F32),
                   jax.ShapeDtypeStruct((ntiles * TM, D), BF16),
                   jax.ShapeDtypeStruct((N_EXP, ntiles * TM), F32)],
        compiler_params=_cp(("arbitrary",), 56),
        name="outproj",
    )(*args)


def _one_hot_rows(rank_row, cap):
    slot = lax.broadcasted_iota(jnp.int32, (cap, rank_row.shape[-1]), 0).astype(F32)
    return jnp.where(rank_row == slot, 1.0, 0.0).astype(BF16)


def _select_one(a, tri, cap, rank_ref, w_ref):
    ne, n = a.shape
    bits = pltpu.bitcast(a, jnp.int32)
    thr = jnp.zeros((ne, 1), jnp.int32)
    for bit in range(30, -1, -1):
        cand = thr | (1 << bit)
        cnt = jnp.sum(jnp.where(bits >= cand, 1.0, 0.0), axis=1, keepdims=True)
        thr = jnp.where(cnt >= cap, cand, thr)
    gt = jnp.where(bits > thr, 1.0, 0.0)
    eq = jnp.where(bits == thr, 1.0, 0.0)
    need = cap - jnp.sum(gt, axis=1, keepdims=True)
    eq_before = jnp.dot(eq.astype(BF16), tri, preferred_element_type=F32) - eq
    sel = gt + eq * jnp.where(eq_before < need, 1.0, 0.0)
    rank = jnp.dot(sel.astype(BF16), tri, preferred_element_type=F32) - 1.0
    rank = jnp.where(sel > 0.5, rank, -1.0)
    slot = lax.broadcasted_iota(jnp.int32, (cap, n), 0).astype(F32)
    for e in range(ne):
        rank_ref[e] = rank[e:e + 1, :]
        hit = rank[e:e + 1, :] == slot
        w_ref[e * cap:(e + 1) * cap, :] = jnp.sum(jnp.where(hit, a[e:e + 1, :], 0.0),
                                                  axis=1, keepdims=True)


def _select_kernel(with_ctx, a_ref, tri_ref, *out_refs):
    _select_one(a_ref[:, :SEQ], tri_ref[...], CAP, out_refs[0], out_refs[1])
    if with_ctx:
        _select_one(a_ref[:, SEQ:], tri_ref[:CTX, :CTX], CAP_C, out_refs[2], out_refs[3])


def _select(aff_t, tri, with_ctx):
    out_specs = [pl.BlockSpec((None, N_EXP, 1, SEQ), lambda b: (b, 0, 0, 0)),
                 pl.BlockSpec((None, N_EXP * CAP, 1), lambda b: (b, 0, 0))]
    out_shape = [jax.ShapeDtypeStruct((B, N_EXP, 1, SEQ), F32),
                 jax.ShapeDtypeStruct((B, N_EXP * CAP, 1), F32)]
    if with_ctx:
        out_specs += [pl.BlockSpec((None, N_EXP, 1, CTX), lambda b: (b, 0, 0, 0)),
                      pl.BlockSpec((None, N_EXP * CAP_C, 1), lambda b: (b, 0, 0))]
        out_shape += [jax.ShapeDtypeStruct((B, N_EXP, 1, CTX), F32),
                      jax.ShapeDtypeStruct((B, N_EXP * CAP_C, 1), F32)]
    return pl.pallas_call(
        functools.partial(_select_kernel, with_ctx),
        grid=(B,),
        in_specs=[pl.BlockSpec((N_EXP, S2 if with_ctx else SEQ), lambda b: (0, b)),
                  pl.BlockSpec((SEQ, SEQ), lambda b: (0, 0))],
        out_specs=out_specs,
        out_shape=out_shape,
        compiler_params=_cp(("arbitrary",), 48),
        name="select",
    )(aff_t, tri)


def _gather_kernel(cap, rank_ref, h_ref, o_ref):
    p = jnp.concatenate([_one_hot_rows(rank_ref[j], cap) for j in range(rank_ref.shape[0])], axis=0)
    o_ref[...] = jnp.dot(p, h_ref[...], preferred_element_type=F32).astype(BF16)


def _gather(rank, h3, ctx_only):
    if ctx_only:
        cap, n, ne = CAP_C, CTX, N_EXP
        hmap = lambda b, r: (b, TPB - 1, 0)
    else:
        cap, n, ne = CAP, SEQ, 2
        hmap = lambda b, r: (b, 0, 0)
    return pl.pallas_call(
        functools.partial(_gather_kernel, cap),
        grid=(B, N_EXP // ne),
        in_specs=[pl.BlockSpec((None, ne, 1, n), lambda b, r: (b, r, 0, 0)),
                  pl.BlockSpec((None, n, D), hmap)],
        out_specs=pl.BlockSpec((None, ne * cap, D), lambda b, r: (b, r, 0)),
        out_shape=jax.ShapeDtypeStruct((B, N_EXP * cap, D), BF16),
        compiler_params=_cp(("arbitrary", "arbitrary"), 48),
        name="gather_ctx" if ctx_only else "gather",
    )(rank, h3)


def _ffn_kernel(with_ctx, *refs):
    if with_ctx:
        x_ref, xc_ref, wg_ref, wu_ref, wd_ref, ws_ref, wsc_ref, y_ref, yc_ref, acc = refs
    else:
        x_ref, wg_ref, wu_ref, wd_ref, ws_ref, y_ref, acc = refs
    f = pl.program_id(1)
    last = FF // FF_T - 1
    nl = B * CAP

    def step(kind):
        wg = wg_ref[...].astype(BF16)
        wu = wu_ref[...].astype(BF16)
        wd = wd_ref[...].astype(BF16)
        x = x_ref[...].reshape(nl, D)
        if with_ctx:
            x = jnp.concatenate([x, xc_ref[...].reshape(B * CAP_C, D)], axis=0)
        gate = jnp.dot(x, wg, preferred_element_type=F32)
        up = jnp.dot(x, wu, preferred_element_type=F32)
        hid = (gate * jax.nn.sigmoid(gate) * up).astype(BF16)
        part = jnp.dot(hid, wd, preferred_element_type=F32)
        if kind == "first":
            acc[...] = part
        elif kind == "mid":
            acc[...] += part
        else:
            y = acc[...] + part
            y_ref[...] = (y[:nl] * ws_ref[...].reshape(nl, 1)).astype(BF16).reshape(y_ref.shape)
            if with_ctx:
                yc = y[nl:] * wsc_ref[...].reshape(B * CAP_C, 1)
                yc_ref[...] = yc.astype(BF16).reshape(yc_ref.shape)

    pl.when(f == 0)(lambda: step("first"))
    pl.when(jnp.logical_and(f > 0, f < last))(lambda: step("mid"))
    pl.when(f == last)(lambda: step("last"))


def _ffn(xs, ws, xc, wsc, w_gate, w_up, w_down, li):
    with_ctx = xc is not None
    in_specs = [pl.BlockSpec((B, CAP, D), lambda e, f: (0, e, 0))]
    args = [xs]
    if with_ctx:
        in_specs.append(pl.BlockSpec((B, CAP_C, D), lambda e, f: (0, e, 0)))
        args.append(xc)
    in_specs += [pl.BlockSpec((None, None, D, FF_T), lambda e, f: (li, e, 0, f)),
                 pl.BlockSpec((None, None, D, FF_T), lambda e, f: (li, e, 0, f)),
                 pl.BlockSpec((None, None, FF_T, D), lambda e, f: (li, e, f, 0)),
                 pl.BlockSpec((B, CAP, 1), lambda e, f: (0, e, 0))]
    args += [w_gate, w_up, w_down, ws]
    out_specs = [pl.BlockSpec((B, CAP, D), lambda e, f: (0, e, 0))]
    out_shape = [jax.ShapeDtypeStruct((B, N_EXP * CAP, D), BF16)]
    scratch = [pltpu.VMEM((B * (CAP + CAP_C if with_ctx else CAP), D), F32)]
    if with_ctx:
        in_specs.append(pl.BlockSpec((B, CAP_C, 1), lambda e, f: (0, e, 0)))
        args.append(wsc)
        out_specs.append(pl.BlockSpec((B, CAP_C, D), lambda e, f: (0, e, 0)))
        out_shape.append(jax.ShapeDtypeStruct((B, N_EXP * CAP_C, D), BF16))
    return pl.pallas_call(
        functools.partial(_ffn_kernel, with_ctx),
        grid=(N_EXP, FF // FF_T),
        in_specs=in_specs,
        out_specs=out_specs,
        out_shape=out_shape,
        scratch_shapes=scratch,
        compiler_params=_cp(("arbitrary", "arbitrary"), 56),
        name="ffn",
    )(*args)


_TN_DIMS = (((0,), (0,)), ((), ()))


def _scatter_kernel(with_ctx, final, *refs):
    refs = list(refs)
    p_ref, y_ref = refs[:2]
    pc_ref, yc_ref = refs[2:4] if with_ctx else (None, None)
    rest = refs[4:] if with_ctx else refs[2:]
    x_ref, mod_ref = rest[:2]
    gf_ref = rest[2] if final else None
    o_ref = rest[-1]

    def finish(rank_ref, yr, cap):
        ng, ge = 4, N_EXP // 4

        def hot(j):
            return jnp.concatenate([_one_hot_rows(rank_ref[e], cap)
                                    for e in range(j * ge, (j + 1) * ge)], axis=0)

        upd = None
        p_next = hot(0)
        for j in range(ng):
            p_cur = p_next
            if j + 1 < ng:
                p_next = hot(j + 1)
            part = lax.dot_general(p_cur, yr[j * ge * cap:(j + 1) * ge * cap, :], _TN_DIMS,
                                   preferred_element_type=F32)
            upd = part if upd is None else upd + part
        x = x_ref[...] + mod_ref[5:6, :] * upd
        o_ref[...] = _rms(x, gf_ref[...]) if final else x

    if with_ctx:
        t = pl.program_id(1)
        pl.when(t < LPB)(lambda: finish(p_ref, y_ref, CAP))
        pl.when(t == LPB)(lambda: finish(pc_ref, yc_ref, CAP_C))
    else:
        finish(p_ref, y_ref, CAP)


def _scatter(rank, y, rank_c, yc, x3, mod, final_g):
    with_ctx = rank_c is not None
    final = final_g is not None
    tpb = TPB if with_ctx else LPB
    in_specs = [pl.BlockSpec((None, N_EXP, 1, TM), lambda b, t: (b, 0, 0, jnp.minimum(t, LPB - 1))),
                pl.BlockSpec((None, N_EXP * CAP, D), lambda b, t: (b, 0, 0))]
    args = [rank, y]
    if with_ctx:
        in_specs += [pl.BlockSpec((None, N_EXP, 1, CTX), lambda b, t: (b, 0, 0, 0)),
                     pl.BlockSpec((None, N_EXP * CAP_C, D), lambda b, t: (b, 0, 0))]
        args += [rank_c, yc]
    in_specs += [pl.BlockSpec((None, TM, D), lambda b, t: (b, t, 0)),
                 pl.BlockSpec((None, 6, D), lambda b, t: (jnp.where(t == LPB, B, b), 0, 0))]
    args += [x3, mod]
    if final:
        in_specs.append(pl.BlockSpec((1, D), lambda b, t: (0, 0)))
        args.append(final_g)
    return pl.pallas_call(
        functools.partial(_scatter_kernel, with_ctx, final),
        grid=(B, tpb),
        in_specs=in_specs,
        out_specs=pl.BlockSpec((None, TM, D), lambda b, t: (b, t, 0)),
        out_shape=jax.ShapeDtypeStruct((B, tpb * TM, D), F32),
        compiler_params=_cp(("arbitrary", "arbitrary"), 56),
        name="scatter",
    )(*args)


def _rope_tables():
    n_freq = QK_ROPE // 4
    grid_w = 64
    pos = jnp.arange(SEQ, dtype=F32)
    inv_freq = 10000.0 ** (-jnp.arange(n_freq, dtype=F32) / n_freq)
    ang_r = jnp.floor(pos / grid_w)[:, None] * inv_freq
    ang_c = (pos - grid_w * jnp.floor(pos / grid_w))[:, None] * inv_freq
    cr, sr, cc, sc = jnp.cos(ang_r), jnp.sin(ang_r), jnp.cos(ang_c), jnp.sin(ang_c)
    cos = jnp.concatenate([cr, cr, cc, cc], axis=1)
    sin = jnp.concatenate([-sr, sr, -sc, sc], axis=1)
    cos = jnp.concatenate([cos, jnp.ones((CTX, QK_ROPE), F32)], axis=0)
    sin = jnp.concatenate([sin, jnp.zeros((CTX, QK_ROPE), F32)], axis=0)
    z = jnp.zeros((S2, QK_ROPE), F32)
    qs = ATT_SCALE * LOG2E
    tq1 = jnp.concatenate([jnp.full((S2, QK_NOPE), qs, F32), cos * qs, z], axis=1)
    tq2 = jnp.concatenate([jnp.zeros((S2, QK_NOPE), F32), sin * qs, z], axis=1)
    tk1 = jnp.concatenate([cos, z], axis=1)
    tk2 = jnp.concatenate([sin, z], axis=1)
    return tq1, tq2, tk1, tk2


def _pair_swap(w):
    return jnp.concatenate([w[..., 16:32], w[..., 0:16], w[..., 48:64], w[..., 32:48]], axis=-1)


def _s5_place_kernel(bre_ref, bim_ref, cre_ref, cim_ref, w_ref, c_ref):
    hg = S5_G // 2
    w_ref[...] = jnp.zeros_like(w_ref)
    c_ref[...] = jnp.zeros_like(c_ref)
    for g in range(hg):
        r = slice(g * S5_CH, (g + 1) * S5_CH)
        for part, (b_ref, k_ref, sign) in enumerate(((bre_ref, cre_ref, 1.0), (bim_ref, cim_ref, -1.0))):
            s = slice(part * S5_HALF + g * S5_N, part * S5_HALF + (g + 1) * S5_N)
            w_ref[r, s] = b_ref[g].astype(BF16)
            c_ref[s, r] = (sign * k_ref[g]).astype(BF16)


def _s5_operators(a_re, a_im, log_dt, b_re, b_im, c_re, c_im):
    a = lax.complex(jnp.minimum(a_re.astype(F32), -1e-4), a_im.astype(F32))
    dt = jnp.exp(log_dt.astype(F32))[..., None]
    abar = jnp.exp(a * dt)
    bbar = ((abar - 1.0) / a)[..., None] * lax.complex(b_re.astype(F32), b_im.astype(F32))
    hg = S5_G // 2

    def per_group(m):
        return m.reshape(DEPTH, 2, 2, hg, m.shape[-2], m.shape[-1])

    bt = jnp.swapaxes(bbar, -1, -2)
    ct_re = jnp.swapaxes(c_re.astype(F32), -1, -2)
    ct_im = jnp.swapaxes(c_im.astype(F32), -1, -2)
    bspec = pl.BlockSpec((None, None, None, hg, S5_CH, S5_N), lambda l, z, h: (l, z, h, 0, 0, 0))
    cspec = pl.BlockSpec((None, None, None, hg, S5_N, S5_CH), lambda l, z, h: (l, z, h, 0, 0, 0))
    w_bd, c_bd = pl.pallas_call(
        _s5_place_kernel,
        grid=(DEPTH, 2, 2),
        in_specs=[bspec, bspec, cspec, cspec],
        out_specs=[pl.BlockSpec((None, None, None, GW // 2, 2 * S5_HALF), lambda l, z, h: (l, z, h, 0, 0)),
                   pl.BlockSpec((None, None, None, 2 * S5_HALF, GW // 2), lambda l, z, h: (l, z, h, 0, 0))],
        out_shape=[jax.ShapeDtypeStruct((DEPTH, 2, 2, GW // 2, 2 * S5_HALF), BF16),
                   jax.ShapeDtypeStruct((DEPTH, 2, 2, 2 * S5_HALF, GW // 2), BF16)],
        compiler_params=_cp(("arbitrary",) * 3, 32),
        name="s5_place",
    )(per_group(jnp.real(bt)), per_group(jnp.imag(bt)), per_group(ct_re), per_group(ct_im))

    def rows8(m):
        m = m.reshape(DEPTH, 2, 1, 2, S5_LT, 128)
        m = jnp.broadcast_to(m, (DEPTH, 2, B, 2, S5_LT, 128)).reshape(DEPTH, 2, 2 * B, S5_LT, 128)
        return jnp.transpose(m, (0, 1, 3, 2, 4))

    return w_bd, rows8(jnp.real(abar)), rows8(jnp.imag(abar)), c_bd


def kernel(x, c, ctx, c_ctx, norm1_g, norm2_g, w_ada, b_ada, w_in, w_out, sgu_norm_g, sgu_w,
           sgu_b, mla_q_norm_g, mla_w_uq, mla_kv_norm_g, mla_w_ukv, s5_a_re, s5_a_im, s5_log_dt,
           s5_b_re, s5_b_im, s5_c_re, s5_c_im, s5_d, s5_w_glu, s5_b_glu, conv_w, moe_w_router,
           moe_w_gate, moe_w_up, moe_w_down, final_norm_g):
    c8 = jnp.concatenate([c, c_ctx[None, :], jnp.zeros((3, D), F32)], axis=0)
    mod_all = _modulation(c8, w_ada, b_ada).reshape(DEPTH, 8, 6, D)
    rope_t = _rope_tables()
    tri = jnp.triu(jnp.ones((SEQ, SEQ), BF16))
    w_in_r = _winprep(jnp.transpose(w_in, (0, 2, 1)))
    w_out_bf = w_out.astype(BF16)
    s5_ops = _s5_operators(s5_a_re, s5_a_im, s5_log_dt, s5_b_re, s5_b_im, s5_c_re, s5_c_im)
    xs = (x.reshape(B * SEQ, D), ctx.reshape(B * CTX, D))

    for i in range(DEPTH):
        last = i == DEPTH - 1
        mod = mod_all[i]
        if i > 0:
            xs = (x3.reshape(B * S2, D),)

        wq = mla_w_uq[i].reshape(Q_LORA, MLA_HEADS, QK_NOPE + QK_ROPE)
        wq_r = wq[:, :, QK_NOPE:]
        wq_ext = jnp.concatenate([wq[:, :, :QK_NOPE], wq_r, _pair_swap(wq_r)], axis=2)
        wq_ext = wq_ext.reshape(Q_LORA, MLA_HEADS * QK_PAD).astype(BF16)
        wkv = mla_w_ukv[i].reshape(KV_LORA, MLA_HEADS, 2 * QK_NOPE)
        wkv_ext = jnp.concatenate([wkv[:, :, :QK_NOPE].reshape(KV_LORA, -1),
                                   wkv[:, :, QK_NOPE:].reshape(KV_LORA, -1)], axis=1).astype(BF16)
        sgu_p = (sgu_norm_g[i][None, :], sgu_w[i].astype(BF16),
                 jnp.repeat(jnp.swapaxes(sgu_b[i], 0, 1), 128, axis=1))
        mla_p = (mla_q_norm_g[i][None, :], mla_kv_norm_g[i][None, :], wq_ext, wkv_ext) + rope_t
        sgu_o, q, kc, v, p_s5, p_conv = _inproj(xs, mod, norm1_g[i][None, :], w_in_r, i,
                                                sgu_p, mla_p)

        attn_o = _attention(q, kc.reshape(B, S2, MLA_HEADS * QK_PAD), v.reshape(B, S2, MLA_HEADS * QK_PAD), not last)

        u3 = p_s5.reshape(B, S2, GW)
        y_fwd = _s5_pass(u3, s5_ops, i, None)
        ssm_o = _s5_pass(u3, s5_ops, i,
                         (y_fwd, s5_d[i][None, :], s5_w_glu[i].astype(BF16), s5_b_glu[i][None, :]))
        ssm_o = ssm_o.reshape(B * S2, GW)

        wr_t = jnp.transpose(moe_w_router[i])
        wr_hi = wr_t.astype(BF16)
        wr2 = jnp.concatenate([wr_hi, (wr_t - wr_hi.astype(F32)).astype(BF16)], axis=0)
        x1, h2, aff_t = _outproj(sgu_o, attn_o, ssm_o, p_conv, conv_w[i], w_out_bf, xs, mod,
                                 norm2_g[i][None, :], wr2, i, not last)

        rows_b = SEQ if last else S2
        sel = _select(aff_t, tri, not last)
        h3 = h2.reshape(B, rows_b, D)
        xs = _gather(sel[0], h3, ctx_only=False)
        xc = _gather(sel[2], h3, ctx_only=True) if not last else None
        ys = _ffn(xs, sel[1], xc, sel[3] if not last else None, moe_w_gate, moe_w_up, moe_w_down, i)
        x1_3 = x1.reshape(B, rows_b, D)
        if last:
            x3 = _scatter(sel[0], ys[0], None, None, x1_3, mod, final_norm_g[None, :])
        else:
            x3 = _scatter(sel[0], ys[0], sel[2], ys[1], x1_3, mod, None)

    return x3
```

```python
import functools

import jax
import jax.numpy as jnp
from jax import lax
from jax.experimental import pallas as pl
from jax.experimental.pallas import tpu as pltpu

F32 = jnp.float32
BF16 = jnp.bfloat16

D = 2048
B = 4
SEQ = 2048
CTX = 256
S2 = SEQ + CTX
DEPTH = 2
GW = 512
EPS = 1e-6

TM = 256
TPB = S2 // TM
LPB = SEQ // TM
NT = B * TPB
NLT = B * LPB

SGU_HEADS = 4
CHUNK = 128
MLA_HEADS = 4
QK_NOPE = 128
QK_ROPE = 64
QK_PAD = 256
Q_LORA = 384
KV_LORA = 256
ATT_SCALE = (QK_NOPE + QK_ROPE) ** -0.5
LOG2E = 1.4426950408889634

S5_G = 32
S5_N = 64
S5_CH = 16
S5_T = 128
S5_NCH = S2 // S5_T
S5_HALF = (S5_G // 2) * S5_N
S5_LT = S5_HALF // 128
S5_PITCH = 9

N_EXP = 16
FF = D // 2
CAP = 2 * SEQ // N_EXP
CAP_C = 2 * CTX // N_EXP
FF_T = 256

COL_MLA = 2 * GW
MLA_W = Q_LORA + KV_LORA + 2 * QK_ROPE
COL_S5 = COL_MLA + MLA_W
COL_CONV = COL_S5 + GW
IN_W = COL_CONV + 3 * GW
MIB = 1024 * 1024


def _cp(sem, vmem_mb):
    return pltpu.CompilerParams(dimension_semantics=sem, vmem_limit_bytes=vmem_mb * MIB)


def _lat_tile(i):
    return (i // LPB) * TPB + i % LPB


def _seg_of_tile(t):
    return jnp.where(t % TPB == TPB - 1, B, t // TPB)


def _rms(x, g):
    return x * lax.rsqrt(jnp.mean(x * x, axis=-1, keepdims=True) + EPS) * g


def _modnorm(x, g, shift, scale):
    return _rms(x, g) * (1.0 + scale) + shift


def _mod_kernel(c_ref, w_ref, b_ref, o_ref):
    a = c_ref[...]
    a = a * jax.nn.sigmoid(a)
    o_ref[...] = jnp.dot(a.astype(BF16), w_ref[...].astype(BF16),
                         preferred_element_type=F32) + b_ref[...]


def _modulation(c8, w_ada, b_ada):
    tn = 1024
    return pl.pallas_call(
        _mod_kernel,
        grid=(DEPTH, 6 * D // tn),
        in_specs=[pl.BlockSpec((8, D), lambda l, j: (0, 0)),
                  pl.BlockSpec((None, D, tn), lambda l, j: (l, 0, j)),
                  pl.BlockSpec((None, 1, tn), lambda l, j: (l, 0, j))],
        out_specs=pl.BlockSpec((None, 8, tn), lambda l, j: (l, 0, j)),
        out_shape=jax.ShapeDtypeStruct((DEPTH, 8, 6 * D), F32),
        compiler_params=_cp(("arbitrary", "arbitrary"), 40),
        name="modulation",
    )(c8, w_ada, b_ada.reshape(DEPTH, 1, 6 * D))


KR0 = COL_MLA + Q_LORA + KV_LORA
IN_RAW = IN_W - QK_ROPE


WP_T = 256
WP_SWAP = KR0 // WP_T


def _winprep_kernel(prev_ref, cur_ref, o_ref):
    j = pl.program_id(1)
    keep = WP_T - QK_ROPE

    @pl.when(j < WP_SWAP)
    def _():
        o_ref[...] = cur_ref[...].astype(BF16)

    @pl.when(j == WP_SWAP)
    def _():
        cur = cur_ref[...]
        o_ref[:keep, :] = cur[:keep, :].astype(BF16)
        kr = cur[keep - QK_ROPE:keep, :]
        sw = jnp.concatenate([kr[16:32], kr[0:16], kr[48:64], kr[32:48]], axis=0)
        o_ref[keep:, :] = sw.astype(BF16)

    @pl.when(j > WP_SWAP)
    def _():
        o_ref[:QK_ROPE, :] = prev_ref[...].astype(BF16)
        o_ref[QK_ROPE:, :] = cur_ref[:keep, :].astype(BF16)


def _winprep(w_in_t):
    assert KR0 + QK_ROPE == (WP_SWAP + 1) * WP_T - QK_ROPE
    sub = WP_T // QK_ROPE
    return pl.pallas_call(
        _winprep_kernel,
        grid=(DEPTH, IN_W // WP_T),
        in_specs=[pl.BlockSpec((None, QK_ROPE, D), lambda l, j: (l, jnp.maximum(sub * j - 1, 0), 0)),
                  pl.BlockSpec((None, WP_T, D), lambda l, j: (l, j, 0))],
        out_specs=pl.BlockSpec((None, WP_T, D), lambda l, j: (l, j, 0)),
        out_shape=jax.ShapeDtypeStruct((DEPTH, IN_W, D), BF16),
        compiler_params=_cp(("arbitrary", "arbitrary"), 32),
        name="winprep",
    )(w_in_t, w_in_t)


def _stream_rows(refs, tile):
    if len(refs) == 1:
        return refs[0][...]
    return jnp.where(tile % TPB == TPB - 1, refs[1][...], refs[0][...])


def _stream_specs(xs):
    if len(xs) == 1:
        return lambda full: [pl.BlockSpec((TM, D), lambda i: (full(i), 0))]
    lat = lambda t: (t // TPB) * LPB + jnp.minimum(t % TPB, LPB - 1)
    return lambda full: [pl.BlockSpec((TM, D), lambda i: (lat(full(i)), 0)),
                         pl.BlockSpec((CTX, D), lambda i: (full(i) // TPB, 0))]


def _inproj_kernel(nx, *refs):
    (mod_ref, g_ref, w_ref, sg_ref, sw_ref, sb_ref,
     gq_ref, gkv_ref, wq_ref, wkv_ref, tq1_ref, tq2_ref, tk1_ref, tk2_ref,
     sgu_ref, q_ref, kc_ref, v_ref, s5_ref, conv_ref) = refs[nx:]
    x = _stream_rows(refs[:nx], pl.program_id(0))
    h = _modnorm(x, g_ref[...], mod_ref[0:1, :], mod_ref[1:2, :]).astype(BF16)

    def mm(a, b):
        return lax.dot_general(h, w_ref[a:b, :], _NT_DIMS, preferred_element_type=F32)

    p_a = mm(0, COL_MLA)
    pm = mm(COL_MLA, COL_S5)

    p = jax.nn.gelu(p_a)
    u = p[:, :GW]
    vb = _rms(p[:, GW:], sg_ref[...]).astype(BF16)
    for ck in range(TM // CHUNK):
        r = slice(ck * CHUNK, (ck + 1) * CHUNK)
        for hd in range(SGU_HEADS):
            c = slice(hd * 128, (hd + 1) * 128)
            m = jnp.dot(sw_ref[hd], vb[r, c], preferred_element_type=F32)
            sgu_ref[r, c] = (u[r, c] * (m + sb_ref[:, c])).astype(BF16)

    s5_ref[...] = mm(COL_S5, COL_CONV)
    conv_ref[:, :GW] = mm(COL_CONV, COL_CONV + GW)

    cq = _rms(pm[:, :Q_LORA], gq_ref[...]).astype(BF16)
    q = jnp.dot(cq, wq_ref[...], preferred_element_type=F32)
    tq1 = tq1_ref[...]
    tq2 = tq2_ref[...]
    for hd in range(MLA_HEADS):
        c = slice(hd * QK_PAD, (hd + 1) * QK_PAD)
        blk = q[:, c]
        q_ref[:, c] = (blk * tq1 + pltpu.roll(blk, QK_PAD - QK_ROPE, 1) * tq2).astype(BF16)
    ckv = _rms(pm[:, Q_LORA:Q_LORA + KV_LORA], gkv_ref[...]).astype(BF16)
    kv = jnp.dot(ckv, wkv_ref[...], preferred_element_type=F32)
    ones = jnp.ones((TM, 128), BF16)
    for hd in range(MLA_HEADS):
        v_ref[:, hd * QK_PAD:hd * QK_PAD + 128] = kv[:, GW + hd * 128:GW + (hd + 1) * 128].astype(BF16)
        v_ref[:, hd * QK_PAD + 128:(hd + 1) * QK_PAD] = ones
    kt = pm[:, Q_LORA + KV_LORA:]
    kr = (kt * tk1_ref[...] + pltpu.roll(kt, QK_ROPE, 1) * tk2_ref[...]).astype(BF16)
    for hd in range(MLA_HEADS):
        kc_ref[:, hd * QK_PAD:hd * QK_PAD + QK_NOPE] = kv[:, hd * 128:(hd + 1) * 128].astype(BF16)
        kc_ref[:, hd * QK_PAD + QK_NOPE:(hd + 1) * QK_PAD] = kr

    conv_ref[:, GW:] = mm(COL_CONV + GW, COL_CONV + 2 * GW) * mm(COL_CONV + 2 * GW, IN_W)


def _inproj(xs, mod, g1, w_in_r, li, sgu_p, mla_p):
    fix2 = lambda i: (0, 0)
    pos = lambda i: (i % TPB, 0)
    row = lambda i: (i, 0)
    qkw = MLA_HEADS * QK_PAD
    return pl.pallas_call(
        functools.partial(_inproj_kernel, len(xs)),
        grid=(NT,),
        in_specs=_stream_specs(xs)(lambda i: i) + [
                  pl.BlockSpec((None, 6, D), lambda i: (_seg_of_tile(i), 0, 0)),
                  pl.BlockSpec((1, D), fix2),
                  pl.BlockSpec((None, IN_W, D), lambda i: (li, 0, 0)),
                  pl.BlockSpec((1, GW), fix2),
                  pl.BlockSpec((SGU_HEADS, CHUNK, CHUNK), lambda i: (0, 0, 0)),
                  pl.BlockSpec((CHUNK, GW), fix2),
                  pl.BlockSpec((1, Q_LORA), fix2),
                  pl.BlockSpec((1, KV_LORA), fix2),
                  pl.BlockSpec((Q_LORA, qkw), fix2),
                  pl.BlockSpec((KV_LORA, 2 * GW), fix2),
                  pl.BlockSpec((TM, QK_PAD), pos),
                  pl.BlockSpec((TM, QK_PAD), pos),
                  pl.BlockSpec((TM, 128), pos),
                  pl.BlockSpec((TM, 128), pos)],
        out_specs=[pl.BlockSpec((TM, GW), row),
                   pl.BlockSpec((TM, qkw), row),
                   pl.BlockSpec((TM, qkw), row),
                   pl.BlockSpec((TM, qkw), row),
                   pl.BlockSpec((TM, GW), row),
                   pl.BlockSpec((TM, 2 * GW), row)],
        out_shape=[jax.ShapeDtypeStruct((B * S2, GW), BF16),
                   jax.ShapeDtypeStruct((B * S2, qkw), BF16),
                   jax.ShapeDtypeStruct((B * S2, qkw), BF16),
                   jax.ShapeDtypeStruct((B * S2, qkw), BF16),
                   jax.ShapeDtypeStruct((B * S2, GW), F32),
                   jax.ShapeDtypeStruct((B * S2, 2 * GW), F32)],
        compiler_params=_cp(("arbitrary",), 56),
        name="inproj",
    )(*xs, mod, g1, w_in_r, *sgu_p, *mla_p)


def _conv_tile(tile, p_ref, zp_ref, zn_ref, w_ref):
    r = tile % TPB
    bg = p_ref[:, :GW]
    z = p_ref[:, GW:]
    row = lax.broadcasted_iota(jnp.int32, (TM, GW), 0)
    has_prev = jnp.logical_and(r != 0, r != TPB - 1)
    has_next = r < LPB - 1
    prev_row = zp_ref[7:8, :] * has_prev.astype(F32)
    next_row = zn_ref[0:1, :] * has_next.astype(F32)
    zm = jnp.where(row == 0, prev_row, pltpu.roll(z, 1, 0))
    zp = jnp.where(row == TM - 1, next_row, pltpu.roll(z, TM - 1, 0))
    y = w_ref[0:1, :] * zm + w_ref[1:2, :] * z + w_ref[2:3, :] * zp
    return (bg * y).astype(BF16)


def _conv_specs(full):
    rb = TM // 8
    nrb = B * S2 // 8
    return [pl.BlockSpec((TM, 2 * GW), lambda i: (full(i), 0)),
            pl.BlockSpec((8, GW), lambda i: (jnp.maximum(full(i) * rb - 1, 0), 1)),
            pl.BlockSpec((8, GW), lambda i: (jnp.minimum((full(i) + 1) * rb, nrb - 1), 1)),
            pl.BlockSpec((3, GW), lambda i: (0, 0))]


_NT_DIMS = (((1,), (1,)), ((), ()))


def _attn_kernel(with_ctx, q_ref, kc_ref, v_ref, o_ref):
    def run(k0):
        def scores(hd):
            cq = slice(hd * QK_PAD, (hd + 1) * QK_PAD)
            return lax.dot_general(q_ref[:, cq], kc_ref[k0:, cq], _NT_DIMS, preferred_element_type=F32)

        def weights(s):
            return jnp.exp2(s - jnp.max(s, axis=-1, keepdims=True)).astype(BF16)

        def values(hd, e):
            o = jnp.dot(e, v_ref[k0:, hd * QK_PAD:(hd + 1) * QK_PAD], preferred_element_type=F32)
            o_ref[:, hd * 128:(hd + 1) * 128] = (o[:, :128] / o[:, 128:129]).astype(BF16)

        s_next = scores(0)
        e_prev = None
        for hd in range(MLA_HEADS):
            s_cur = s_next
            if hd + 1 < MLA_HEADS:
                s_next = scores(hd + 1)
            e_cur = weights(s_cur)
            if e_prev is not None:
                values(hd - 1, e_prev)
            e_prev = e_cur
        values(MLA_HEADS - 1, e_prev)

    if with_ctx:
        t = pl.program_id(1)
        pl.when(t < LPB)(lambda: run(0))
        pl.when(t == LPB)(lambda: run(SEQ))
    else:
        run(0)


def _attention(q, kc3, v3, with_ctx):
    tpb = TPB if with_ctx else LPB
    qkw = MLA_HEADS * QK_PAD
    return pl.pallas_call(
        functools.partial(_attn_kernel, with_ctx),
        grid=(B, tpb),
        in_specs=[pl.BlockSpec((TM, qkw), lambda b, t: (b * TPB + t, 0)),
                  pl.BlockSpec((None, S2, qkw), lambda b, t: (b, 0, 0)),
                  pl.BlockSpec((None, S2, qkw), lambda b, t: (b, 0, 0))],
        out_specs=pl.BlockSpec((TM, GW), lambda b, t: (b * tpb + t, 0)),
        out_shape=jax.ShapeDtypeStruct((B * tpb * TM, GW), BF16),
        compiler_params=_cp(("arbitrary", "arbitrary"), 48),
        name="attn",
    )(q, kc3, v3)


def _s5_kernel(backward, *refs):
    if backward:
        (u_ref, w_ref, are_ref, aim_ref, c_ref, yf_ref, d_ref, wg_ref, bg_ref,
         o_ref, st_re, st_im, buf0, buf1) = refs
    else:
        u_ref, w_ref, are_ref, aim_ref, c_ref, o_ref, st_re, st_im, buf0, buf1 = refs
    bufs = (buf0, buf1)
    k = pl.program_id(0)
    hw = GW // 2
    nlt = S5_LT
    ht = S5_T // 2
    npc = nlt

    @pl.when(k == 0)
    def _():
        st_re[...] = jnp.zeros_like(st_re)
        st_im[...] = jnp.zeros_like(st_im)

    def rows_of(ref, hf):
        return jnp.concatenate([ref[b, hf * ht:(hf + 1) * ht, :] for b in range(B)], axis=0)

    def expand_pieces(hf):
        ub = rows_of(u_ref, hf).astype(BF16)
        buf = bufs[hf]

        def piece(h, n):
            bu = jnp.dot(ub[:, h * hw:(h + 1) * hw], w_ref[h, :, n * 256:(n + 1) * 256],
                         preferred_element_type=F32)
            for b in range(B):
                for cc in range(2):
                    buf[2 * n + cc, pl.ds(2 * b + h, ht, stride=S5_PITCH), :] = (
                        bu[b * ht:(b + 1) * ht, cc * 128:(cc + 1) * 128])

        return [functools.partial(piece, h, n) for h in range(2) for n in range(npc)]

    a_re = are_ref[...]
    a_im = aim_ref[...]
    state = [st_re[...], st_im[...]]

    def scan_pieces(hf, per):
        order = list(range(ht - 1, -1, -1)) if backward else list(range(ht))
        buf = bufs[hf]

        def piece(steps):
            sr, si = state
            for j in steps:
                r0 = S5_PITCH * j
                br = buf[0:nlt, r0:r0 + 8, :]
                bi = buf[nlt:2 * nlt, r0:r0 + 8, :]
                sr, si = a_re * sr - a_im * si + br, a_re * si + a_im * sr + bi
                buf[0:nlt, r0:r0 + 8, :] = sr
                buf[nlt:2 * nlt, r0:r0 + 8, :] = si
            state[0], state[1] = sr, si

        return [functools.partial(piece, order[i:i + per]) for i in range(0, ht, per)]

    acc = {}

    def readout_pieces(hf):
        buf = bufs[hf]

        def piece(h, n):
            s = jnp.concatenate(
                [jnp.concatenate([buf[2 * n + cc, pl.ds(2 * b + h, ht, stride=S5_PITCH), :]
                                  for cc in range(2)], axis=1) for b in range(B)], axis=0)
            part = jnp.dot(s.astype(BF16), c_ref[h, n * 256:(n + 1) * 256, :],
                           preferred_element_type=F32)
            acc[(hf, h)] = part if n == 0 else acc[(hf, h)] + part

        return [functools.partial(piece, h, n) for h in range(2) for n in range(npc)]

    def finish(hf):
        y = jnp.concatenate([acc[(hf, 0)], acc[(hf, 1)]], axis=1)
        if backward:
            y = y + rows_of(yf_ref, hf) + d_ref[...] * rows_of(u_ref, hf)
            g = jax.nn.gelu(y)
            z = jnp.dot(g.astype(BF16), wg_ref[...], preferred_element_type=F32) + bg_ref[...]
            y = (g * jax.nn.sigmoid(z)).astype(BF16)
        for b in range(B):
            o_ref[b, hf * ht:(hf + 1) * ht, :] = y[b * ht:(b + 1) * ht]

    def interleave(mxu, vpu):
        for i, m in enumerate(mxu):
            m()
            if i < len(vpu):
                vpu[i]()

    first, second = (1, 0) if backward else (0, 1)
    per = ht // (2 * npc)
    for m in expand_pieces(first):
        m()
    interleave(expand_pieces(second), scan_pieces(first, per))
    interleave(readout_pieces(first), scan_pieces(second, per))
    st_re[...] = state[0]
    st_im[...] = state[1]
    finish(first)
    for m in readout_pieces(second):
        m()
    finish(second)


def _s5_pass(u3, ops, li, glu):
    backward = glu is not None
    dr = 1 if backward else 0
    if backward:
        blk = lambda k: (0, S5_NCH - 1 - k, 0)
    else:
        blk = lambda k: (0, (k + SEQ // S5_T) % S5_NCH, 0)
    w_bd, a_re8, a_im8, c_bd = ops
    fixed = lambda k: (li, dr, 0, 0, 0)
    in_specs = [pl.BlockSpec((B, S5_T, GW), blk),
                pl.BlockSpec((None, None, 2, GW // 2, 2 * S5_HALF), fixed),
                pl.BlockSpec((None, None, S5_LT, 8, 128), fixed),
                pl.BlockSpec((None, None, S5_LT, 8, 128), fixed),
                pl.BlockSpec((None, None, 2, 2 * S5_HALF, GW // 2), fixed)]
    args = [u3, w_bd, a_re8, a_im8, c_bd]
    if backward:
        in_specs += [pl.BlockSpec((B, S5_T, GW), blk),
                     pl.BlockSpec((1, GW), lambda k: (0, 0)),
                     pl.BlockSpec((GW, GW), lambda k: (0, 0)),
                     pl.BlockSpec((1, GW), lambda k: (0, 0))]
        args += list(glu)
    return pl.pallas_call(
        functools.partial(_s5_kernel, backward),
        grid=(S5_NCH,),
        in_specs=in_specs,
        out_specs=pl.BlockSpec((B, S5_T, GW), blk),
        out_shape=jax.ShapeDtypeStruct((B, S2, GW), BF16 if backward else F32),
        scratch_shapes=[pltpu.VMEM((S5_LT, 8, 128), F32),
                        pltpu.VMEM((S5_LT, 8, 128), F32),
                        pltpu.VMEM((2 * S5_LT, S5_PITCH * S5_T // 2, 128), F32),
                        pltpu.VMEM((2 * S5_LT, S5_PITCH * S5_T // 2, 128), F32)],
        compiler_params=_cp(("arbitrary",), 48),
        name="s5_bwd_glu" if backward else "s5_fwd",
    )(*args)


OP_SUB = 2


def _outproj_kernel(nx, full, *refs):
    per = nx + 7
    cw_ref, w_ref, g2_ref, wr_ref, x1_ref, h2_ref, aff_ref = refs[OP_SUB * per:]
    tiles = [full(OP_SUB * pl.program_id(0) + s) for s in range(OP_SUB)]
    acts_all = []
    for s in range(OP_SUB):
        a0_ref, a1_ref, a2_ref, p_ref, zp_ref, zn_ref, _ = refs[s * per + nx:(s + 1) * per]
        acts_all.append((a0_ref[...], a1_ref[...], a2_ref[...],
                         _conv_tile(tiles[s], p_ref, zp_ref, zn_ref, cw_ref)))
    outs = []
    for s in range(OP_SUB):
        acts = acts_all[s]
        halves = []
        for c in (slice(0, D // 2), slice(D // 2, D)):
            o = jnp.dot(acts[0], w_ref[0:GW, c], preferred_element_type=F32)
            for m in range(1, 4):
                o = o + jnp.dot(acts[m], w_ref[m * GW:(m + 1) * GW, c], preferred_element_type=F32)
            halves.append(o)
        outs.append(jnp.concatenate(halves, axis=1))
    for s in range(OP_SUB):
        tr = refs[s * per:(s + 1) * per]
        mod_ref = tr[-1]
        tile = tiles[s]
        rows = slice(s * TM, (s + 1) * TM)
        x = _stream_rows(tr[:nx], tile)
        x1 = x + mod_ref[2:3, :] * outs[s]
        x1_ref[rows, :] = x1
        h2 = _modnorm(x1, g2_ref[...], mod_ref[3:4, :], mod_ref[4:5, :]).astype(BF16)
        h2_ref[rows, :] = h2
        lg2 = lax.dot_general(wr_ref[...], h2, _NT_DIMS, preferred_element_type=F32)
        lg = lg2[:N_EXP] + lg2[N_EXP:]
        e = jnp.exp(lg - jnp.max(lg, axis=0, keepdims=True))
        aff_ref[:, rows] = e / jnp.sum(e, axis=0, keepdims=True)


def _outproj(sgu_o, attn_o, ssm_o, p_conv, conv_w, w_out_bf, xs, mod, g2, wr2, li, with_ctx):
    ntiles = NT if with_ctx else NLT
    full = (lambda i: i) if with_ctx else _lat_tile
    in_specs, args = [], []
    for s in range(OP_SUB):
        out_t = lambda i, s=s: OP_SUB * i + s
        full_t = lambda i, s=s: full(OP_SUB * i + s)
        frow = lambda i, f=full_t: (f(i), 0)
        orow = lambda i, f=out_t: (f(i), 0)
        in_specs += _stream_specs(xs)(full_t) + [
            pl.BlockSpec((TM, GW), frow),
            pl.BlockSpec((TM, GW), orow),
            pl.BlockSpec((TM, GW), frow)] + _conv_specs(full_t)[:3] + [
            pl.BlockSpec((None, 6, D), lambda i, f=full_t: (_seg_of_tile(f(i)), 0, 0))]
        args += [*xs, sgu_o, attn_o, ssm_o, p_conv, p_conv, p_conv, mod]
    in_specs += [_conv_specs(full)[3],
                 pl.BlockSpec((None, D, D), lambda i: (li, 0, 0)),
                 pl.BlockSpec((1, D), lambda i: (0, 0)),
                 pl.BlockSpec((2 * N_EXP, D), lambda i: (0, 0))]
    args += [conv_w, w_out_bf, g2, wr2]
    blk = OP_SUB * TM
    return pl.pallas_call(
        functools.partial(_outproj_kernel, len(xs), full),
        grid=(ntiles // OP_SUB,),
        in_specs=in_specs,
        out_specs=[pl.BlockSpec((blk, D), lambda i: (i, 0)),
                   pl.BlockSpec((blk, D), lambda i: (i, 0)),
                   pl.BlockSpec((N_EXP, blk), lambda i: (0, i))],
        out_shape=[jax.ShapeDtypeStruct((ntiles * TM, D), F32),
                   jax.ShapeDtypeStruct((ntiles * TM, D), BF16),
                   jax.ShapeDtypeStruct((N_EXP, ntiles * TM), F32)],
        compiler_params=_cp(("arbitrary",), 56),
        name="outproj",
    )(*args)


def _one_hot_rows(rank_row, cap):
    slot = lax.broadcasted_iota(jnp.int32, (cap, rank_row.shape[-1]), 0).astype(F32)
    return jnp.where(rank_row == slot, 1.0, 0.0).astype(BF16)


def _cap_thresholds(bits_caps):
    thrs = [jnp.zeros((bits.shape[0], 1), jnp.int32) for bits, _ in bits_caps]
    for bit in range(30, -1, -1):
        for i, (bits, cap) in enumerate(bits_caps):
            cand = thrs[i] | (1 << bit)
            cnt = jnp.sum(jnp.where(bits >= cand, 1.0, 0.0), axis=1, keepdims=True)
            thrs[i] = jnp.where(cnt >= cap, cand, thrs[i])
    return thrs


def _select_one(a, bits, thr, tri, cap, rank_ref, w_ref):
    ne, n = a.shape
    gt = jnp.where(bits > thr, 1.0, 0.0)
    eq = jnp.where(bits == thr, 1.0, 0.0)
    need = cap - jnp.sum(gt, axis=1, keepdims=True)
    eq_before = jnp.dot(eq.astype(BF16), tri, preferred_element_type=F32) - eq
    sel = gt + eq * jnp.where(eq_before < need, 1.0, 0.0)
    rank = jnp.dot(sel.astype(BF16), tri, preferred_element_type=F32) - 1.0
    rank = jnp.where(sel > 0.5, rank, -1.0)
    slot = lax.broadcasted_iota(jnp.int32, (cap, n), 0).astype(F32)
    for e in range(ne):
        rank_ref[e] = rank[e:e + 1, :]
        hit = rank[e:e + 1, :] == slot
        w_ref[e * cap:(e + 1) * cap, :] = jnp.sum(jnp.where(hit, a[e:e + 1, :], 0.0),
                                                  axis=1, keepdims=True)


def _select_kernel(with_ctx, a_ref, tri_ref, *out_refs):
    sets = [(a_ref[:, :SEQ], CAP, tri_ref[...])]
    if with_ctx:
        sets.append((a_ref[:, SEQ:], CAP_C, tri_ref[:CTX, :CTX]))
    bits = [pltpu.bitcast(a, jnp.int32) for a, _, _ in sets]
    thrs = _cap_thresholds([(b, cap) for b, (_, cap, _) in zip(bits, sets)])
    for i, (a, cap, tri) in enumerate(sets):
        _select_one(a, bits[i], thrs[i], tri, cap, out_refs[2 * i], out_refs[2 * i + 1])


def _select(aff_t, tri, with_ctx):
    out_specs = [pl.BlockSpec((None, N_EXP, 1, SEQ), lambda b: (b, 0, 0, 0)),
                 pl.BlockSpec((None, N_EXP * CAP, 1), lambda b: (b, 0, 0))]
    out_shape = [jax.ShapeDtypeStruct((B, N_EXP, 1, SEQ), F32),
                 jax.ShapeDtypeStruct((B, N_EXP * CAP, 1), F32)]
    if with_ctx:
        out_specs += [pl.BlockSpec((None, N_EXP, 1, CTX), lambda b: (b, 0, 0, 0)),
                      pl.BlockSpec((None, N_EXP * CAP_C, 1), lambda b: (b, 0, 0))]
        out_shape += [jax.ShapeDtypeStruct((B, N_EXP, 1, CTX), F32),
                      jax.ShapeDtypeStruct((B, N_EXP * CAP_C, 1), F32)]
    return pl.pallas_call(
        functools.partial(_select_kernel, with_ctx),
        grid=(B,),
        in_specs=[pl.BlockSpec((N_EXP, S2 if with_ctx else SEQ), lambda b: (0, b)),
                  pl.BlockSpec((SEQ, SEQ), lambda b: (0, 0))],
        out_specs=out_specs,
        out_shape=out_shape,
        compiler_params=_cp(("arbitrary",), 48),
        name="select",
    )(aff_t, tri)


def _gather_kernel(cap, rank_ref, h_ref, o_ref):
    p = jnp.concatenate([_one_hot_rows(rank_ref[j], cap) for j in range(rank_ref.shape[0])], axis=0)
    o_ref[...] = jnp.dot(p, h_ref[...], preferred_element_type=F32).astype(BF16)


def _gather(rank, h3, ctx_only):
    if ctx_only:
        cap, n, ne = CAP_C, CTX, N_EXP
        hmap = lambda b, r: (b, TPB - 1, 0)
    else:
        cap, n, ne = CAP, SEQ, 2
        hmap = lambda b, r: (b, 0, 0)
    return pl.pallas_call(
        functools.partial(_gather_kernel, cap),
        grid=(B, N_EXP // ne),
        in_specs=[pl.BlockSpec((None, ne, 1, n), lambda b, r: (b, r, 0, 0)),
                  pl.BlockSpec((None, n, D), hmap)],
        out_specs=pl.BlockSpec((None, ne * cap, D), lambda b, r: (b, r, 0)),
        out_shape=jax.ShapeDtypeStruct((B, N_EXP * cap, D), BF16),
        compiler_params=_cp(("arbitrary", "arbitrary"), 48),
        name="gather_ctx" if ctx_only else "gather",
    )(rank, h3)


def _ffn_kernel(with_ctx, *refs):
    if with_ctx:
        x_ref, xc_ref, wg_ref, wu_ref, wd_ref, ws_ref, wsc_ref, y_ref, yc_ref, acc = refs
    else:
        x_ref, wg_ref, wu_ref, wd_ref, ws_ref, y_ref, acc = refs
    f = pl.program_id(1)
    last = FF // FF_T - 1
    nl = B * CAP

    def step(kind):
        wg = wg_ref[...].astype(BF16)
        wu = wu_ref[...].astype(BF16)
        wd = wd_ref[...].astype(BF16)
        x = x_ref[...].reshape(nl, D)
        if with_ctx:
            x = jnp.concatenate([x, xc_ref[...].reshape(B * CAP_C, D)], axis=0)
        gate = jnp.dot(x, wg, preferred_element_type=F32)
        up = jnp.dot(x, wu, preferred_element_type=F32)
        hid = (gate * jax.nn.sigmoid(gate) * up).astype(BF16)
        part = jnp.dot(hid, wd, preferred_element_type=F32)
        if kind == "first":
            acc[...] = part
        elif kind == "mid":
            acc[...] += part
        else:
            y = acc[...] + part
            y_ref[...] = (y[:nl] * ws_ref[...].reshape(nl, 1)).astype(BF16).reshape(y_ref.shape)
            if with_ctx:
                yc = y[nl:] * wsc_ref[...].reshape(B * CAP_C, 1)
                yc_ref[...] = yc.astype(BF16).reshape(yc_ref.shape)

    pl.when(f == 0)(lambda: step("first"))
    pl.when(jnp.logical_and(f > 0, f < last))(lambda: step("mid"))
    pl.when(f == last)(lambda: step("last"))


def _ffn(xs, ws, xc, wsc, w_gate, w_up, w_down, li):
    with_ctx = xc is not None
    in_specs = [pl.BlockSpec((B, CAP, D), lambda e, f: (0, e, 0))]
    args = [xs]
    if with_ctx:
        in_specs.append(pl.BlockSpec((B, CAP_C, D), lambda e, f: (0, e, 0)))
        args.append(xc)
    in_specs += [pl.BlockSpec((None, None, D, FF_T), lambda e, f: (li, e, 0, f)),
                 pl.BlockSpec((None, None, D, FF_T), lambda e, f: (li, e, 0, f)),
                 pl.BlockSpec((None, None, FF_T, D), lambda e, f: (li, e, f, 0)),
                 pl.BlockSpec((B, CAP, 1), lambda e, f: (0, e, 0))]
    args += [w_gate, w_up, w_down, ws]
    out_specs = [pl.BlockSpec((B, CAP, D), lambda e, f: (0, e, 0))]
    out_shape = [jax.ShapeDtypeStruct((B, N_EXP * CAP, D), BF16)]
    scratch = [pltpu.VMEM((B * (CAP + CAP_C if with_ctx else CAP), D), F32)]
    if with_ctx:
        in_specs.append(pl.BlockSpec((B, CAP_C, 1), lambda e, f: (0, e, 0)))
        args.append(wsc)
        out_specs.append(pl.BlockSpec((B, CAP_C, D), lambda e, f: (0, e, 0)))
        out_shape.append(jax.ShapeDtypeStruct((B, N_EXP * CAP_C, D), BF16))
    return pl.pallas_call(
        functools.partial(_ffn_kernel, with_ctx),
        grid=(N_EXP, FF // FF_T),
        in_specs=in_specs,
        out_specs=out_specs,
        out_shape=out_shape,
        scratch_shapes=scratch,
        compiler_params=_cp(("arbitrary", "arbitrary"), 56),
        name="ffn",
    )(*args)


_TN_DIMS = (((0,), (0,)), ((), ()))


def _scatter_kernel(with_ctx, final, *refs):
    refs = list(refs)
    p_ref, y_ref = refs[:2]
    pc_ref, yc_ref = refs[2:4] if with_ctx else (None, None)
    rest = refs[4:] if with_ctx else refs[2:]
    x_ref, mod_ref = rest[:2]
    gf_ref = rest[2] if final else None
    o_ref = rest[-1]

    def finish(rank_ref, yr, cap):
        ng, ge = 4, N_EXP // 4

        def hot(j):
            return jnp.concatenate([_one_hot_rows(rank_ref[e], cap)
                                    for e in range(j * ge, (j + 1) * ge)], axis=0)

        upd = None
        p_next = hot(0)
        for j in range(ng):
            p_cur = p_next
            if j + 1 < ng:
                p_next = hot(j + 1)
            part = lax.dot_general(p_cur, yr[j * ge * cap:(j + 1) * ge * cap, :], _TN_DIMS,
                                   preferred_element_type=F32)
            upd = part if upd is None else upd + part
        x = x_ref[...] + mod_ref[5:6, :] * upd
        o_ref[...] = _rms(x, gf_ref[...]) if final else x

    if with_ctx:
        t = pl.program_id(1)
        pl.when(t < LPB)(lambda: finish(p_ref, y_ref, CAP))
        pl.when(t == LPB)(lambda: finish(pc_ref, yc_ref, CAP_C))
    else:
        finish(p_ref, y_ref, CAP)


def _scatter(rank, y, rank_c, yc, x3, mod, final_g):
    with_ctx = rank_c is not None
    final = final_g is not None
    tpb = TPB if with_ctx else LPB
    in_specs = [pl.BlockSpec((None, N_EXP, 1, TM), lambda b, t: (b, 0, 0, jnp.minimum(t, LPB - 1))),
                pl.BlockSpec((None, N_EXP * CAP, D), lambda b, t: (b, 0, 0))]
    args = [rank, y]
    if with_ctx:
        in_specs += [pl.BlockSpec((None, N_EXP, 1, CTX), lambda b, t: (b, 0, 0, 0)),
                     pl.BlockSpec((None, N_EXP * CAP_C, D), lambda b, t: (b, 0, 0))]
        args += [rank_c, yc]
    in_specs += [pl.BlockSpec((None, TM, D), lambda b, t: (b, t, 0)),
                 pl.BlockSpec((None, 6, D), lambda b, t: (jnp.where(t == LPB, B, b), 0, 0))]
    args += [x3, mod]
    if final:
        in_specs.append(pl.BlockSpec((1, D), lambda b, t: (0, 0)))
        args.append(final_g)
    return pl.pallas_call(
        functools.partial(_scatter_kernel, with_ctx, final),
        grid=(B, tpb),
        in_specs=in_specs,
        out_specs=pl.BlockSpec((None, TM, D), lambda b, t: (b, t, 0)),
        out_shape=jax.ShapeDtypeStruct((B, tpb * TM, D), F32),
        compiler_params=_cp(("arbitrary", "arbitrary"), 56),
        name="scatter",
    )(*args)


def _rope_tables():
    n_freq = QK_ROPE // 4
    grid_w = 64
    pos = jnp.arange(SEQ, dtype=F32)
    inv_freq = 10000.0 ** (-jnp.arange(n_freq, dtype=F32) / n_freq)
    ang_r = jnp.floor(pos / grid_w)[:, None] * inv_freq
    ang_c = (pos - grid_w * jnp.floor(pos / grid_w))[:, None] * inv_freq
    cr, sr, cc, sc = jnp.cos(ang_r), jnp.sin(ang_r), jnp.cos(ang_c), jnp.sin(ang_c)
    cos = jnp.concatenate([cr, cr, cc, cc], axis=1)
    sin = jnp.concatenate([-sr, sr, -sc, sc], axis=1)
    cos = jnp.concatenate([cos, jnp.ones((CTX, QK_ROPE), F32)], axis=0)
    sin = jnp.concatenate([sin, jnp.zeros((CTX, QK_ROPE), F32)], axis=0)
    z = jnp.zeros((S2, QK_ROPE), F32)
    qs = ATT_SCALE * LOG2E
    tq1 = jnp.concatenate([jnp.full((S2, QK_NOPE), qs, F32), cos * qs, z], axis=1)
    tq2 = jnp.concatenate([jnp.zeros((S2, QK_NOPE), F32), sin * qs, z], axis=1)
    tk1 = jnp.concatenate([cos, z], axis=1)
    tk2 = jnp.concatenate([sin, z], axis=1)
    return tq1, tq2, tk1, tk2


def _pair_swap(w):
    return jnp.concatenate([w[..., 16:32], w[..., 0:16], w[..., 48:64], w[..., 32:48]], axis=-1)


def _s5_place_kernel(bre_ref, bim_ref, cre_ref, cim_ref, w_ref, c_ref):
    hg = S5_G // 2
    w_ref[...] = jnp.zeros_like(w_ref)
    c_ref[...] = jnp.zeros_like(c_ref)
    for g in range(hg):
        r = slice(g * S5_CH, (g + 1) * S5_CH)
        for part, (b_ref, k_ref, sign) in enumerate(((bre_ref, cre_ref, 1.0), (bim_ref, cim_ref, -1.0))):
            s = slice(part * S5_HALF + g * S5_N, part * S5_HALF + (g + 1) * S5_N)
            w_ref[r, s] = b_ref[g].astype(BF16)
            c_ref[s, r] = (sign * k_ref[g]).astype(BF16)


def _s5_operators(a_re, a_im, log_dt, b_re, b_im, c_re, c_im):
    a = lax.complex(jnp.minimum(a_re.astype(F32), -1e-4), a_im.astype(F32))
    dt = jnp.exp(log_dt.astype(F32))[..., None]
    abar = jnp.exp(a * dt)
    bbar = ((abar - 1.0) / a)[..., None] * lax.complex(b_re.astype(F32), b_im.astype(F32))
    hg = S5_G // 2

    def per_group(m):
        return m.reshape(DEPTH, 2, 2, hg, m.shape[-2], m.shape[-1])

    bt = jnp.swapaxes(bbar, -1, -2)
    ct_re = jnp.swapaxes(c_re.astype(F32), -1, -2)
    ct_im = jnp.swapaxes(c_im.astype(F32), -1, -2)
    bspec = pl.BlockSpec((None, None, None, hg, S5_CH, S5_N), lambda l, z, h: (l, z, h, 0, 0, 0))
    cspec = pl.BlockSpec((None, None, None, hg, S5_N, S5_CH), lambda l, z, h: (l, z, h, 0, 0, 0))
    w_bd, c_bd = pl.pallas_call(
        _s5_place_kernel,
        grid=(DEPTH, 2, 2),
        in_specs=[bspec, bspec, cspec, cspec],
        out_specs=[pl.BlockSpec((None, None, None, GW // 2, 2 * S5_HALF), lambda l, z, h: (l, z, h, 0, 0)),
                   pl.BlockSpec((None, None, None, 2 * S5_HALF, GW // 2), lambda l, z, h: (l, z, h, 0, 0))],
        out_shape=[jax.ShapeDtypeStruct((DEPTH, 2, 2, GW // 2, 2 * S5_HALF), BF16),
                   jax.ShapeDtypeStruct((DEPTH, 2, 2, 2 * S5_HALF, GW // 2), BF16)],
        compiler_params=_cp(("arbitrary",) * 3, 32),
        name="s5_place",
    )(per_group(jnp.real(bt)), per_group(jnp.imag(bt)), per_group(ct_re), per_group(ct_im))

    def rows8(m):
        m = m.reshape(DEPTH, 2, 1, 2, S5_LT, 128)
        m = jnp.broadcast_to(m, (DEPTH, 2, B, 2, S5_LT, 128)).reshape(DEPTH, 2, 2 * B, S5_LT, 128)
        return jnp.transpose(m, (0, 1, 3, 2, 4))

    return w_bd, rows8(jnp.real(abar)), rows8(jnp.imag(abar)), c_bd


def kernel(x, c, ctx, c_ctx, norm1_g, norm2_g, w_ada, b_ada, w_in, w_out, sgu_norm_g, sgu_w,
           sgu_b, mla_q_norm_g, mla_w_uq, mla_kv_norm_g, mla_w_ukv, s5_a_re, s5_a_im, s5_log_dt,
           s5_b_re, s5_b_im, s5_c_re, s5_c_im, s5_d, s5_w_glu, s5_b_glu, conv_w, moe_w_router,
           moe_w_gate, moe_w_up, moe_w_down, final_norm_g):
    c8 = jnp.concatenate([c, c_ctx[None, :], jnp.zeros((3, D), F32)], axis=0)
    mod_all = _modulation(c8, w_ada, b_ada).reshape(DEPTH, 8, 6, D)
    rope_t = _rope_tables()
    tri = jnp.triu(jnp.ones((SEQ, SEQ), BF16))
    w_in_r = _winprep(jnp.transpose(w_in, (0, 2, 1)))
    w_out_bf = w_out.astype(BF16)
    s5_ops = _s5_operators(s5_a_re, s5_a_im, s5_log_dt, s5_b_re, s5_b_im, s5_c_re, s5_c_im)
    xs = (x.reshape(B * SEQ, D), ctx.reshape(B * CTX, D))

    for i in range(DEPTH):
        last = i == DEPTH - 1
        mod = mod_all[i]
        if i > 0:
            xs = (x3.reshape(B * S2, D),)

        wq = mla_w_uq[i].reshape(Q_LORA, MLA_HEADS, QK_NOPE + QK_ROPE)
        wq_r = wq[:, :, QK_NOPE:]
        wq_ext = jnp.concatenate([wq[:, :, :QK_NOPE], wq_r, _pair_swap(wq_r)], axis=2)
        wq_ext = wq_ext.reshape(Q_LORA, MLA_HEADS * QK_PAD).astype(BF16)
        wkv = mla_w_ukv[i].reshape(KV_LORA, MLA_HEADS, 2 * QK_NOPE)
        wkv_ext = jnp.concatenate([wkv[:, :, :QK_NOPE].reshape(KV_LORA, -1),
                                   wkv[:, :, QK_NOPE:].reshape(KV_LORA, -1)], axis=1).astype(BF16)
        sgu_p = (sgu_norm_g[i][None, :], sgu_w[i].astype(BF16),
                 jnp.repeat(jnp.swapaxes(sgu_b[i], 0, 1), 128, axis=1))
        mla_p = (mla_q_norm_g[i][None, :], mla_kv_norm_g[i][None, :], wq_ext, wkv_ext) + rope_t
        sgu_o, q, kc, v, p_s5, p_conv = _inproj(xs, mod, norm1_g[i][None, :], w_in_r, i,
                                                sgu_p, mla_p)

        attn_o = _attention(q, kc.reshape(B, S2, MLA_HEADS * QK_PAD), v.reshape(B, S2, MLA_HEADS * QK_PAD), not last)

        u3 = p_s5.reshape(B, S2, GW)
        y_fwd = _s5_pass(u3, s5_ops, i, None)
        ssm_o = _s5_pass(u3, s5_ops, i,
                         (y_fwd, s5_d[i][None, :], s5_w_glu[i].astype(BF16), s5_b_glu[i][None, :]))
        ssm_o = ssm_o.reshape(B * S2, GW)

        wr_t = jnp.transpose(moe_w_router[i])
        wr_hi = wr_t.astype(BF16)
        wr2 = jnp.concatenate([wr_hi, (wr_t - wr_hi.astype(F32)).astype(BF16)], axis=0)
        x1, h2, aff_t = _outproj(sgu_o, attn_o, ssm_o, p_conv, conv_w[i], w_out_bf, xs, mod,
                                 norm2_g[i][None, :], wr2, i, not last)

        rows_b = SEQ if last else S2
        sel = _select(aff_t, tri, not last)
        h3 = h2.reshape(B, rows_b, D)
        xs = _gather(sel[0], h3, ctx_only=False)
        xc = _gather(sel[2], h3, ctx_only=True) if not last else None
        ys = _ffn(xs, sel[1], xc, sel[3] if not last else None, moe_w_gate, moe_w_up, moe_w_down, i)
        x1_3 = x1.reshape(B, rows_b, D)
        if last:
            x3 = _scatter(sel[0], ys[0], None, None, x1_3, mod, final_norm_g[None, :])
        else:
            x3 = _scatter(sel[0], ys[0], sel[2], ys[1], x1_3, mod, None)

    return x3
```

```python
import functools

import jax
import jax.numpy as jnp
from jax import lax
from jax.experimental import pallas as pl
from jax.experimental.pallas import tpu as pltpu

F32 = jnp.float32
BF16 = jnp.bfloat16

D = 2048
B = 4
SEQ = 2048
CTX = 256
S2 = SEQ + CTX
DEPTH = 2
GW = 512
EPS = 1e-6

TM = 256
TPB = S2 // TM
LPB = SEQ // TM
NT = B * TPB
NLT = B * LPB

SGU_HEADS = 4
CHUNK = 128
MLA_HEADS = 4
QK_NOPE = 128
QK_ROPE = 64
QK_PAD = 256
Q_LORA = 384
KV_LORA = 256
ATT_SCALE = (QK_NOPE + QK_ROPE) ** -0.5
LOG2E = 1.4426950408889634

S5_G = 32
S5_N = 64
S5_CH = 16
S5_T = 128
S5_NCH = S2 // S5_T
S5_HALF = (S5_G // 2) * S5_N
S5_LT = S5_HALF // 128
S5_PITCH = 9

N_EXP = 16
FF = D // 2
CAP = 2 * SEQ // N_EXP
CAP_C = 2 * CTX // N_EXP
FF_T = 256

COL_MLA = 2 * GW
MLA_W = Q_LORA + KV_LORA + 2 * QK_ROPE
COL_S5 = COL_MLA + MLA_W
COL_CONV = COL_S5 + GW
IN_W = COL_CONV + 3 * GW
MIB = 1024 * 1024


def _cp(sem, vmem_mb):
    return pltpu.CompilerParams(dimension_semantics=sem, vmem_limit_bytes=vmem_mb * MIB)


def _lat_tile(i):
    return (i // LPB) * TPB + i % LPB


def _seg_of_tile(t):
    return jnp.where(t % TPB == TPB - 1, B, t // TPB)


def _rms(x, g):
    return x * lax.rsqrt(jnp.mean(x * x, axis=-1, keepdims=True) + EPS) * g


def _modnorm(x, g, shift, scale):
    return _rms(x, g) * (1.0 + scale) + shift


def _mod_kernel(c_ref, w_ref, b_ref, o_ref):
    a = c_ref[...]
    a = a * jax.nn.sigmoid(a)
    o_ref[...] = jnp.dot(a.astype(BF16), w_ref[...].astype(BF16),
                         preferred_element_type=F32) + b_ref[...]


def _modulation(c8, w_ada, b_ada):
    tn = 1024
    return pl.pallas_call(
        _mod_kernel,
        grid=(DEPTH, 6 * D // tn),
        in_specs=[pl.BlockSpec((8, D), lambda l, j: (0, 0)),
                  pl.BlockSpec((None, D, tn), lambda l, j: (l, 0, j)),
                  pl.BlockSpec((None, 1, tn), lambda l, j: (l, 0, j))],
        out_specs=pl.BlockSpec((None, 8, tn), lambda l, j: (l, 0, j)),
        out_shape=jax.ShapeDtypeStruct((DEPTH, 8, 6 * D), F32),
        compiler_params=_cp(("arbitrary", "arbitrary"), 40),
        name="modulation",
    )(c8, w_ada, b_ada.reshape(DEPTH, 1, 6 * D))


KR0 = COL_MLA + Q_LORA + KV_LORA
IN_RAW = IN_W - QK_ROPE


WP_T = 256
WP_SWAP = KR0 // WP_T


def _winprep_kernel(prev_ref, cur_ref, o_ref):
    j = pl.program_id(1)
    keep = WP_T - QK_ROPE

    @pl.when(j < WP_SWAP)
    def _():
        o_ref[...] = cur_ref[...].astype(BF16)

    @pl.when(j == WP_SWAP)
    def _():
        cur = cur_ref[...]
        o_ref[:keep, :] = cur[:keep, :].astype(BF16)
        kr = cur[keep - QK_ROPE:keep, :]
        sw = jnp.concatenate([kr[16:32], kr[0:16], kr[48:64], kr[32:48]], axis=0)
        o_ref[keep:, :] = sw.astype(BF16)

    @pl.when(j > WP_SWAP)
    def _():
        o_ref[:QK_ROPE, :] = prev_ref[...].astype(BF16)
        o_ref[QK_ROPE:, :] = cur_ref[:keep, :].astype(BF16)


def _winprep(w_in_t):
    assert KR0 + QK_ROPE == (WP_SWAP + 1) * WP_T - QK_ROPE
    sub = WP_T // QK_ROPE
    return pl.pallas_call(
        _winprep_kernel,
        grid=(DEPTH, IN_W // WP_T),
        in_specs=[pl.BlockSpec((None, QK_ROPE, D), lambda l, j: (l, jnp.maximum(sub * j - 1, 0), 0)),
                  pl.BlockSpec((None, WP_T, D), lambda l, j: (l, j, 0))],
        out_specs=pl.BlockSpec((None, WP_T, D), lambda l, j: (l, j, 0)),
        out_shape=jax.ShapeDtypeStruct((DEPTH, IN_W, D), BF16),
        compiler_params=_cp(("arbitrary", "arbitrary"), 32),
        name="winprep",
    )(w_in_t, w_in_t)


def _stream_rows(refs, tile):
    if len(refs) == 1:
        return refs[0][...]
    return jnp.where(tile % TPB == TPB - 1, refs[1][...], refs[0][...])


def _stream_specs(xs):
    if len(xs) == 1:
        return lambda full: [pl.BlockSpec((TM, D), lambda i: (full(i), 0))]
    lat = lambda t: (t // TPB) * LPB + jnp.minimum(t % TPB, LPB - 1)
    return lambda full: [pl.BlockSpec((TM, D), lambda i: (lat(full(i)), 0)),
                         pl.BlockSpec((CTX, D), lambda i: (full(i) // TPB, 0))]


def _inproj_kernel(nx, *refs):
    (mod_ref, g_ref, w_ref, sg_ref, sw_ref, sb_ref,
     gq_ref, gkv_ref, wq_ref, wkv_ref, tq1_ref, tq2_ref, tk1_ref, tk2_ref,
     sgu_ref, q_ref, kc_ref, v_ref, s5_ref, conv_ref) = refs[nx:]
    x = _stream_rows(refs[:nx], pl.program_id(0))
    h = _modnorm(x, g_ref[...], mod_ref[0:1, :], mod_ref[1:2, :]).astype(BF16)

    def mm(a, b):
        return lax.dot_general(h, w_ref[a:b, :], _NT_DIMS, preferred_element_type=F32)

    p_a = mm(0, COL_MLA)
    pm = mm(COL_MLA, COL_S5)

    p = jax.nn.gelu(p_a)
    u = p[:, :GW]
    vb = _rms(p[:, GW:], sg_ref[...]).astype(BF16)
    for ck in range(TM // CHUNK):
        r = slice(ck * CHUNK, (ck + 1) * CHUNK)
        for hd in range(SGU_HEADS):
            c = slice(hd * 128, (hd + 1) * 128)
            m = jnp.dot(sw_ref[hd], vb[r, c], preferred_element_type=F32)
            sgu_ref[r, c] = (u[r, c] * (m + sb_ref[:, c])).astype(BF16)

    s5_ref[...] = mm(COL_S5, COL_CONV)
    conv_ref[:, :GW] = mm(COL_CONV, COL_CONV + GW)

    cq = _rms(pm[:, :Q_LORA], gq_ref[...]).astype(BF16)
    q = jnp.dot(cq, wq_ref[...], preferred_element_type=F32)
    tq1 = tq1_ref[...]
    tq2 = tq2_ref[...]
    for hd in range(MLA_HEADS):
        c = slice(hd * QK_PAD, (hd + 1) * QK_PAD)
        blk = q[:, c]
        q_ref[:, c] = (blk * tq1 + pltpu.roll(blk, QK_PAD - QK_ROPE, 1) * tq2).astype(BF16)
    ckv = _rms(pm[:, Q_LORA:Q_LORA + KV_LORA], gkv_ref[...]).astype(BF16)
    kv = jnp.dot(ckv, wkv_ref[...], preferred_element_type=F32)
    ones = jnp.ones((TM, 128), BF16)
    for hd in range(MLA_HEADS):
        v_ref[:, hd * QK_PAD:hd * QK_PAD + 128] = kv[:, GW + hd * 128:GW + (hd + 1) * 128].astype(BF16)
        v_ref[:, hd * QK_PAD + 128:(hd + 1) * QK_PAD] = ones
    kt = pm[:, Q_LORA + KV_LORA:]
    kr = (kt * tk1_ref[...] + pltpu.roll(kt, QK_ROPE, 1) * tk2_ref[...]).astype(BF16)
    for hd in range(MLA_HEADS):
        kc_ref[:, hd * QK_PAD:hd * QK_PAD + QK_NOPE] = kv[:, hd * 128:(hd + 1) * 128].astype(BF16)
        kc_ref[:, hd * QK_PAD + QK_NOPE:(hd + 1) * QK_PAD] = kr

    conv_ref[:, GW:] = mm(COL_CONV + GW, COL_CONV + 2 * GW) * mm(COL_CONV + 2 * GW, IN_W)


def _inproj(xs, mod, g1, w_in_r, li, sgu_p, mla_p):
    fix2 = lambda i: (0, 0)
    pos = lambda i: (i % TPB, 0)
    row = lambda i: (i, 0)
    qkw = MLA_HEADS * QK_PAD
    return pl.pallas_call(
        functools.partial(_inproj_kernel, len(xs)),
        grid=(NT,),
        in_specs=_stream_specs(xs)(lambda i: i) + [
                  pl.BlockSpec((None, 6, D), lambda i: (_seg_of_tile(i), 0, 0)),
                  pl.BlockSpec((1, D), fix2),
                  pl.BlockSpec((None, IN_W, D), lambda i: (li, 0, 0)),
                  pl.BlockSpec((1, GW), fix2),
                  pl.BlockSpec((SGU_HEADS, CHUNK, CHUNK), lambda i: (0, 0, 0)),
                  pl.BlockSpec((CHUNK, GW), fix2),
                  pl.BlockSpec((1, Q_LORA), fix2),
                  pl.BlockSpec((1, KV_LORA), fix2),
                  pl.BlockSpec((Q_LORA, qkw), fix2),
                  pl.BlockSpec((KV_LORA, 2 * GW), fix2),
                  pl.BlockSpec((TM, QK_PAD), pos),
                  pl.BlockSpec((TM, QK_PAD), pos),
                  pl.BlockSpec((TM, 128), pos),
                  pl.BlockSpec((TM, 128), pos)],
        out_specs=[pl.BlockSpec((TM, GW), row),
                   pl.BlockSpec((TM, qkw), row),
                   pl.BlockSpec((TM, qkw), row),
                   pl.BlockSpec((TM, qkw), row),
                   pl.BlockSpec((TM, GW), row),
                   pl.BlockSpec((TM, 2 * GW), row)],
        out_shape=[jax.ShapeDtypeStruct((B * S2, GW), BF16),
                   jax.ShapeDtypeStruct((B * S2, qkw), BF16),
                   jax.ShapeDtypeStruct((B * S2, qkw), BF16),
                   jax.ShapeDtypeStruct((B * S2, qkw), BF16),
                   jax.ShapeDtypeStruct((B * S2, GW), F32),
                   jax.ShapeDtypeStruct((B * S2, 2 * GW), F32)],
        compiler_params=_cp(("arbitrary",), 56),
        name="inproj",
    )(*xs, mod, g1, w_in_r, *sgu_p, *mla_p)


def _conv_tile(tile, p_ref, zp_ref, zn_ref, w_ref):
    r = tile % TPB
    bg = p_ref[:, :GW]
    z = p_ref[:, GW:]
    row = lax.broadcasted_iota(jnp.int32, (TM, GW), 0)
    has_prev = jnp.logical_and(r != 0, r != TPB - 1)
    has_next = r < LPB - 1
    prev_row = zp_ref[7:8, :] * has_prev.astype(F32)
    next_row = zn_ref[0:1, :] * has_next.astype(F32)
    zm = jnp.where(row == 0, prev_row, pltpu.roll(z, 1, 0))
    zp = jnp.where(row == TM - 1, next_row, pltpu.roll(z, TM - 1, 0))
    y = w_ref[0:1, :] * zm + w_ref[1:2, :] * z + w_ref[2:3, :] * zp
    return (bg * y).astype(BF16)


def _conv_specs(full):
    rb = TM // 8
    nrb = B * S2 // 8
    return [pl.BlockSpec((TM, 2 * GW), lambda i: (full(i), 0)),
            pl.BlockSpec((8, GW), lambda i: (jnp.maximum(full(i) * rb - 1, 0), 1)),
            pl.BlockSpec((8, GW), lambda i: (jnp.minimum((full(i) + 1) * rb, nrb - 1), 1)),
            pl.BlockSpec((3, GW), lambda i: (0, 0))]


_NT_DIMS = (((1,), (1,)), ((), ()))


AT_SUB = 2


def _attn_kernel(with_ctx, q_ref, kc_ref, v_ref, o_ref):
    def run(k0, nsub):
        def scores(it):
            r, hd = it
            cq = slice(hd * QK_PAD, (hd + 1) * QK_PAD)
            return lax.dot_general(q_ref[r * TM:(r + 1) * TM, cq], kc_ref[k0:, cq], _NT_DIMS,
                                   preferred_element_type=F32)

        def weights(s):
            return jnp.exp2(s - jnp.max(s, axis=-1, keepdims=True)).astype(BF16)

        def values(it, e):
            r, hd = it
            o = jnp.dot(e, v_ref[k0:, hd * QK_PAD:(hd + 1) * QK_PAD], preferred_element_type=F32)
            o_ref[r * TM:(r + 1) * TM, hd * 128:(hd + 1) * 128] = (
                (o[:, :128] / o[:, 128:129]).astype(BF16))

        items = [(r, hd) for r in range(nsub) for hd in range(MLA_HEADS)]
        s_next = scores(items[0])
        e_prev = None
        for n, it in enumerate(items):
            s_cur = s_next
            if n + 1 < len(items):
                s_next = scores(items[n + 1])
            e_cur = weights(s_cur)
            if e_prev is not None:
                values(items[n - 1], e_prev)
            e_prev = e_cur
        values(items[-1], e_prev)

    if with_ctx:
        t = pl.program_id(1)
        pl.when(t < LPB // AT_SUB)(lambda: run(0, AT_SUB))
        pl.when(t == LPB // AT_SUB)(lambda: run(SEQ, 1))
    else:
        run(0, AT_SUB)


def _attention(q3, kc3, v3, with_ctx):
    steps = LPB // AT_SUB + (1 if with_ctx else 0)
    qkw = MLA_HEADS * QK_PAD
    blk = AT_SUB * TM
    return pl.pallas_call(
        functools.partial(_attn_kernel, with_ctx),
        grid=(B, steps),
        in_specs=[pl.BlockSpec((None, blk, qkw), lambda b, t: (b, t, 0)),
                  pl.BlockSpec((None, S2, qkw), lambda b, t: (b, 0, 0)),
                  pl.BlockSpec((None, S2, qkw), lambda b, t: (b, 0, 0))],
        out_specs=pl.BlockSpec((None, blk, GW), lambda b, t: (b, t, 0)),
        out_shape=jax.ShapeDtypeStruct((B, S2 if with_ctx else SEQ, GW), BF16),
        compiler_params=_cp(("arbitrary", "arbitrary"), 48),
        name="attn",
    )(q3, kc3, v3)


def _s5_kernel(backward, *refs):
    if backward:
        (u_ref, w_ref, are_ref, aim_ref, c_ref, yf_ref, d_ref, wg_ref, bg_ref,
         o_ref, st_re, st_im, buf0, buf1) = refs
    else:
        u_ref, w_ref, are_ref, aim_ref, c_ref, o_ref, st_re, st_im, buf0, buf1 = refs
    bufs = (buf0, buf1)
    k = pl.program_id(0)
    hw = GW // 2
    nlt = S5_LT
    ht = S5_T // 2
    npc = nlt

    @pl.when(k == 0)
    def _():
        st_re[...] = jnp.zeros_like(st_re)
        st_im[...] = jnp.zeros_like(st_im)

    def rows_of(ref, hf):
        return jnp.concatenate([ref[b, hf * ht:(hf + 1) * ht, :] for b in range(B)], axis=0)

    def expand_pieces(hf):
        ub = rows_of(u_ref, hf).astype(BF16)
        buf = bufs[hf]

        def piece(h, n):
            bu = jnp.dot(ub[:, h * hw:(h + 1) * hw], w_ref[h, :, n * 256:(n + 1) * 256],
                         preferred_element_type=F32)
            for b in range(B):
                for cc in range(2):
                    buf[2 * n + cc, pl.ds(2 * b + h, ht, stride=S5_PITCH), :] = (
                        bu[b * ht:(b + 1) * ht, cc * 128:(cc + 1) * 128])

        return [functools.partial(piece, h, n) for h in range(2) for n in range(npc)]

    a_re = are_ref[...]
    a_im = aim_ref[...]
    state = [st_re[...], st_im[...]]

    def scan_pieces(hf, per):
        order = list(range(ht - 1, -1, -1)) if backward else list(range(ht))
        buf = bufs[hf]

        def piece(steps):
            sr, si = state
            for j in steps:
                r0 = S5_PITCH * j
                br = buf[0:nlt, r0:r0 + 8, :]
                bi = buf[nlt:2 * nlt, r0:r0 + 8, :]
                sr, si = a_re * sr - a_im * si + br, a_re * si + a_im * sr + bi
                buf[0:nlt, r0:r0 + 8, :] = sr
                buf[nlt:2 * nlt, r0:r0 + 8, :] = si
            state[0], state[1] = sr, si

        return [functools.partial(piece, order[i:i + per]) for i in range(0, ht, per)]

    acc = {}

    def readout_pieces(hf):
        buf = bufs[hf]

        def piece(h, n):
            s = jnp.concatenate(
                [jnp.concatenate([buf[2 * n + cc, pl.ds(2 * b + h, ht, stride=S5_PITCH), :]
                                  for cc in range(2)], axis=1) for b in range(B)], axis=0)
            part = jnp.dot(s.astype(BF16), c_ref[h, n * 256:(n + 1) * 256, :],
                           preferred_element_type=F32)
            acc[(hf, h)] = part if n == 0 else acc[(hf, h)] + part

        return [functools.partial(piece, h, n) for h in range(2) for n in range(npc)]

    def finish(hf):
        y = jnp.concatenate([acc[(hf, 0)], acc[(hf, 1)]], axis=1)
        if backward:
            y = y + rows_of(yf_ref, hf) + d_ref[...] * rows_of(u_ref, hf)
            g = jax.nn.gelu(y)
            z = jnp.dot(g.astype(BF16), wg_ref[...], preferred_element_type=F32) + bg_ref[...]
            y = (g * jax.nn.sigmoid(z)).astype(BF16)
        for b in range(B):
            o_ref[b, hf * ht:(hf + 1) * ht, :] = y[b * ht:(b + 1) * ht]

    def interleave(mxu, vpu):
        for i, m in enumerate(mxu):
            m()
            if i < len(vpu):
                vpu[i]()

    first, second = (1, 0) if backward else (0, 1)
    per = ht // (2 * npc)
    for m in expand_pieces(first):
        m()
    interleave(expand_pieces(second), scan_pieces(first, per))
    interleave(readout_pieces(first), scan_pieces(second, per))
    st_re[...] = state[0]
    st_im[...] = state[1]
    finish(first)
    for m in readout_pieces(second):
        m()
    finish(second)


def _s5_pass(u3, ops, li, glu):
    backward = glu is not None
    dr = 1 if backward else 0
    if backward:
        blk = lambda k: (0, S5_NCH - 1 - k, 0)
    else:
        blk = lambda k: (0, (k + SEQ // S5_T) % S5_NCH, 0)
    w_bd, a_re8, a_im8, c_bd = ops
    fixed = lambda k: (li, dr, 0, 0, 0)
    in_specs = [pl.BlockSpec((B, S5_T, GW), blk),
                pl.BlockSpec((None, None, 2, GW // 2, 2 * S5_HALF), fixed),
                pl.BlockSpec((None, None, S5_LT, 8, 128), fixed),
                pl.BlockSpec((None, None, S5_LT, 8, 128), fixed),
                pl.BlockSpec((None, None, 2, 2 * S5_HALF, GW // 2), fixed)]
    args = [u3, w_bd, a_re8, a_im8, c_bd]
    if backward:
        in_specs += [pl.BlockSpec((B, S5_T, GW), blk),
                     pl.BlockSpec((1, GW), lambda k: (0, 0)),
                     pl.BlockSpec((GW, GW), lambda k: (0, 0)),
                     pl.BlockSpec((1, GW), lambda k: (0, 0))]
        args += list(glu)
    return pl.pallas_call(
        functools.partial(_s5_kernel, backward),
        grid=(S5_NCH,),
        in_specs=in_specs,
        out_specs=pl.BlockSpec((B, S5_T, GW), blk),
        out_shape=jax.ShapeDtypeStruct((B, S2, GW), BF16 if backward else F32),
        scratch_shapes=[pltpu.VMEM((S5_LT, 8, 128), F32),
                        pltpu.VMEM((S5_LT, 8, 128), F32),
                        pltpu.VMEM((2 * S5_LT, S5_PITCH * S5_T // 2, 128), F32),
                        pltpu.VMEM((2 * S5_LT, S5_PITCH * S5_T // 2, 128), F32)],
        compiler_params=_cp(("arbitrary",), 48),
        name="s5_bwd_glu" if backward else "s5_fwd",
    )(*args)


OP_SUB = 2


def _outproj_kernel(nx, full, *refs):
    per = nx + 7
    cw_ref, w_ref, g2_ref, wr_ref, x1_ref, h2_ref, aff_ref = refs[OP_SUB * per:]
    tiles = [full(OP_SUB * pl.program_id(0) + s) for s in range(OP_SUB)]
    acts_all = []
    for s in range(OP_SUB):
        a0_ref, a1_ref, a2_ref, p_ref, zp_ref, zn_ref, _ = refs[s * per + nx:(s + 1) * per]
        acts_all.append((a0_ref[...], a1_ref[...], a2_ref[...],
                         _conv_tile(tiles[s], p_ref, zp_ref, zn_ref, cw_ref)))
    outs = []
    for s in range(OP_SUB):
        acts = acts_all[s]
        halves = []
        for c in (slice(0, D // 2), slice(D // 2, D)):
            o = jnp.dot(acts[0], w_ref[0:GW, c], preferred_element_type=F32)
            for m in range(1, 4):
                o = o + jnp.dot(acts[m], w_ref[m * GW:(m + 1) * GW, c], preferred_element_type=F32)
            halves.append(o)
        outs.append(jnp.concatenate(halves, axis=1))
    for s in range(OP_SUB):
        tr = refs[s * per:(s + 1) * per]
        mod_ref = tr[-1]
        tile = tiles[s]
        rows = slice(s * TM, (s + 1) * TM)
        x = _stream_rows(tr[:nx], tile)
        x1 = x + mod_ref[2:3, :] * outs[s]
        x1_ref[rows, :] = x1
        h2 = _modnorm(x1, g2_ref[...], mod_ref[3:4, :], mod_ref[4:5, :]).astype(BF16)
        h2_ref[rows, :] = h2
        lg2 = lax.dot_general(wr_ref[...], h2, _NT_DIMS, preferred_element_type=F32)
        lg = lg2[:N_EXP] + lg2[N_EXP:]
        e = jnp.exp(lg - jnp.max(lg, axis=0, keepdims=True))
        aff_ref[:, rows] = e / jnp.sum(e, axis=0, keepdims=True)


def _outproj(sgu_o, attn_o, ssm_o, p_conv, conv_w, w_out_bf, xs, mod, g2, wr2, li, with_ctx):
    ntiles = NT if with_ctx else NLT
    full = (lambda i: i) if with_ctx else _lat_tile
    in_specs, args = [], []
    for s in range(OP_SUB):
        out_t = lambda i, s=s: OP_SUB * i + s
        full_t = lambda i, s=s: full(OP_SUB * i + s)
        frow = lambda i, f=full_t: (f(i), 0)
        orow = lambda i, f=out_t: (f(i), 0)
        in_specs += _stream_specs(xs)(full_t) + [
            pl.BlockSpec((TM, GW), frow),
            pl.BlockSpec((TM, GW), orow),
            pl.BlockSpec((TM, GW), frow)] + _conv_specs(full_t)[:3] + [
            pl.BlockSpec((None, 6, D), lambda i, f=full_t: (_seg_of_tile(f(i)), 0, 0))]
        args += [*xs, sgu_o, attn_o, ssm_o, p_conv, p_conv, p_conv, mod]
    in_specs += [_conv_specs(full)[3],
                 pl.BlockSpec((None, D, D), lambda i: (li, 0, 0)),
                 pl.BlockSpec((1, D), lambda i: (0, 0)),
                 pl.BlockSpec((2 * N_EXP, D), lambda i: (0, 0))]
    args += [conv_w, w_out_bf, g2, wr2]
    blk = OP_SUB * TM
    return pl.pallas_call(
        functools.partial(_outproj_kernel, len(xs), full),
        grid=(ntiles // OP_SUB,),
        in_specs=in_specs,
        out_specs=[pl.BlockSpec((blk, D), lambda i: (i, 0)),
                   pl.BlockSpec((blk, D), lambda i: (i, 0)),
                   pl.BlockSpec((N_EXP, blk), lambda i: (0, i))],
        out_shape=[jax.ShapeDtypeStruct((ntiles * TM, D), F32),
                   jax.ShapeDtypeStruct((ntiles * TM, D), BF16),
                   jax.ShapeDtypeStruct((N_EXP, ntiles * TM), F32)],
        compiler_params=_cp(("arbitrary",), 56),
        name="outproj",
    )(*args)


def _one_hot_rows(rank_row, cap):
    slot = lax.broadcasted_iota(jnp.int32, (cap, rank_row.shape[-1]), 0).astype(F32)
    return jnp.where(rank_row == slot, 1.0, 0.0).astype(BF16)


def _cap_thresholds(bits_caps):
    thrs = [jnp.zeros((bits.shape[0], 1), jnp.int32) for bits, _ in bits_caps]
    for bit in range(30, -1, -1):
        for i, (bits, cap) in enumerate(bits_caps):
            cand = thrs[i] | (1 << bit)
            cnt = jnp.sum(jnp.where(bits >= cand, 1.0, 0.0), axis=1, keepdims=True)
            thrs[i] = jnp.where(cnt >= cap, cand, thrs[i])
    return thrs


def _select_one(a, bits, thr, tri, cap, rank_ref, w_ref):
    ne, n = a.shape
    gt = jnp.where(bits > thr, 1.0, 0.0)
    eq = jnp.where(bits == thr, 1.0, 0.0)
    need = cap - jnp.sum(gt, axis=1, keepdims=True)
    eq_before = jnp.dot(eq.astype(BF16), tri, preferred_element_type=F32) - eq
    sel = gt + eq * jnp.where(eq_before < need, 1.0, 0.0)
    rank = jnp.dot(sel.astype(BF16), tri, preferred_element_type=F32) - 1.0
    rank = jnp.where(sel > 0.5, rank, -1.0)
    slot = lax.broadcasted_iota(jnp.int32, (cap, n), 0).astype(F32)
    for e in range(ne):
        rank_ref[e] = rank[e:e + 1, :]
        hit = rank[e:e + 1, :] == slot
        w_ref[e * cap:(e + 1) * cap, :] = jnp.sum(jnp.where(hit, a[e:e + 1, :], 0.0),
                                                  axis=1, keepdims=True)


def _select_kernel(with_ctx, a_ref, tri_ref, *out_refs):
    sets = [(a_ref[:, :SEQ], CAP, tri_ref[...])]
    if with_ctx:
        sets.append((a_ref[:, SEQ:], CAP_C, tri_ref[:CTX, :CTX]))
    bits = [pltpu.bitcast(a, jnp.int32) for a, _, _ in sets]
    thrs = _cap_thresholds([(b, cap) for b, (_, cap, _) in zip(bits, sets)])
    for i, (a, cap, tri) in enumerate(sets):
        _select_one(a, bits[i], thrs[i], tri, cap, out_refs[2 * i], out_refs[2 * i + 1])


def _select(aff_t, tri, with_ctx):
    out_specs = [pl.BlockSpec((None, N_EXP, 1, SEQ), lambda b: (b, 0, 0, 0)),
                 pl.BlockSpec((None, N_EXP * CAP, 1), lambda b: (b, 0, 0))]
    out_shape = [jax.ShapeDtypeStruct((B, N_EXP, 1, SEQ), F32),
                 jax.ShapeDtypeStruct((B, N_EXP * CAP, 1), F32)]
    if with_ctx:
        out_specs += [pl.BlockSpec((None, N_EXP, 1, CTX), lambda b: (b, 0, 0, 0)),
                      pl.BlockSpec((None, N_EXP * CAP_C, 1), lambda b: (b, 0, 0))]
        out_shape += [jax.ShapeDtypeStruct((B, N_EXP, 1, CTX), F32),
                      jax.ShapeDtypeStruct((B, N_EXP * CAP_C, 1), F32)]
    return pl.pallas_call(
        functools.partial(_select_kernel, with_ctx),
        grid=(B,),
        in_specs=[pl.BlockSpec((N_EXP, S2 if with_ctx else SEQ), lambda b: (0, b)),
                  pl.BlockSpec((SEQ, SEQ), lambda b: (0, 0))],
        out_specs=out_specs,
        out_shape=out_shape,
        compiler_params=_cp(("arbitrary",), 48),
        name="select",
    )(aff_t, tri)


def _gather_kernel(cap, rank_ref, h_ref, o_ref):
    p = jnp.concatenate([_one_hot_rows(rank_ref[j], cap) for j in range(rank_ref.shape[0])], axis=0)
    o_ref[...] = jnp.dot(p, h_ref[...], preferred_element_type=F32).astype(BF16)


def _gather(rank, h3, ctx_only):
    if ctx_only:
        cap, n, ne = CAP_C, CTX, N_EXP
        hmap = lambda b, r: (b, TPB - 1, 0)
    else:
        cap, n, ne = CAP, SEQ, 2
        hmap = lambda b, r: (b, 0, 0)
    return pl.pallas_call(
        functools.partial(_gather_kernel, cap),
        grid=(B, N_EXP // ne),
        in_specs=[pl.BlockSpec((None, ne, 1, n), lambda b, r: (b, r, 0, 0)),
                  pl.BlockSpec((None, n, D), hmap)],
        out_specs=pl.BlockSpec((None, ne * cap, D), lambda b, r: (b, r, 0)),
        out_shape=jax.ShapeDtypeStruct((B, N_EXP * cap, D), BF16),
        compiler_params=_cp(("arbitrary", "arbitrary"), 48),
        name="gather_ctx" if ctx_only else "gather",
    )(rank, h3)


def _ffn_kernel(with_ctx, *refs):
    if with_ctx:
        x_ref, xc_ref, wg_ref, wu_ref, wd_ref, ws_ref, wsc_ref, y_ref, yc_ref, acc = refs
    else:
        x_ref, wg_ref, wu_ref, wd_ref, ws_ref, y_ref, acc = refs
    f = pl.program_id(1)
    last = FF // FF_T - 1
    nl = B * CAP

    def step(kind):
        wg = wg_ref[...].astype(BF16)
        wu = wu_ref[...].astype(BF16)
        wd = wd_ref[...].astype(BF16)
        x = x_ref[...].reshape(nl, D)
        if with_ctx:
            x = jnp.concatenate([x, xc_ref[...].reshape(B * CAP_C, D)], axis=0)
        gate = jnp.dot(x, wg, preferred_element_type=F32)
        up = jnp.dot(x, wu, preferred_element_type=F32)
        hid = (gate * jax.nn.sigmoid(gate) * up).astype(BF16)
        part = jnp.dot(hid, wd, preferred_element_type=F32)
        if kind == "first":
            acc[...] = part
        elif kind == "mid":
            acc[...] += part
        else:
            y = acc[...] + part
            y_ref[...] = (y[:nl] * ws_ref[...].reshape(nl, 1)).astype(BF16).reshape(y_ref.shape)
            if with_ctx:
                yc = y[nl:] * wsc_ref[...].reshape(B * CAP_C, 1)
                yc_ref[...] = yc.astype(BF16).reshape(yc_ref.shape)

    pl.when(f == 0)(lambda: step("first"))
    pl.when(jnp.logical_and(f > 0, f < last))(lambda: step("mid"))
    pl.when(f == last)(lambda: step("last"))


def _ffn(xs, ws, xc, wsc, w_gate, w_up, w_down, li):
    with_ctx = xc is not None
    in_specs = [pl.BlockSpec((B, CAP, D), lambda e, f: (0, e, 0))]
    args = [xs]
    if with_ctx:
        in_specs.append(pl.BlockSpec((B, CAP_C, D), lambda e, f: (0, e, 0)))
        args.append(xc)
    in_specs += [pl.BlockSpec((None, None, D, FF_T), lambda e, f: (li, e, 0, f)),
                 pl.BlockSpec((None, None, D, FF_T), lambda e, f: (li, e, 0, f)),
                 pl.BlockSpec((None, None, FF_T, D), lambda e, f: (li, e, f, 0)),
                 pl.BlockSpec((B, CAP, 1), lambda e, f: (0, e, 0))]
    args += [w_gate, w_up, w_down, ws]
    out_specs = [pl.BlockSpec((B, CAP, D), lambda e, f: (0, e, 0))]
    out_shape = [jax.ShapeDtypeStruct((B, N_EXP * CAP, D), BF16)]
    scratch = [pltpu.VMEM((B * (CAP + CAP_C if with_ctx else CAP), D), F32)]
    if with_ctx:
        in_specs.append(pl.BlockSpec((B, CAP_C, 1), lambda e, f: (0, e, 0)))
        args.append(wsc)
        out_specs.append(pl.BlockSpec((B, CAP_C, D), lambda e, f: (0, e, 0)))
        out_shape.append(jax.ShapeDtypeStruct((B, N_EXP * CAP_C, D), BF16))
    return pl.pallas_call(
        functools.partial(_ffn_kernel, with_ctx),
        grid=(N_EXP, FF // FF_T),
        in_specs=in_specs,
        out_specs=out_specs,
        out_shape=out_shape,
        scratch_shapes=scratch,
        compiler_params=_cp(("arbitrary", "arbitrary"), 56),
        name="ffn",
    )(*args)


_TN_DIMS = (((0,), (0,)), ((), ()))


def _scatter_kernel(with_ctx, final, *refs):
    refs = list(refs)
    p_ref, y_ref = refs[:2]
    pc_ref, yc_ref = refs[2:4] if with_ctx else (None, None)
    rest = refs[4:] if with_ctx else refs[2:]
    x_ref, mod_ref = rest[:2]
    gf_ref = rest[2] if final else None
    o_ref = rest[-1]

    def finish(rank_ref, yr, cap):
        ng, ge = 4, N_EXP // 4

        def hot(j):
            return jnp.concatenate([_one_hot_rows(rank_ref[e], cap)
                                    for e in range(j * ge, (j + 1) * ge)], axis=0)

        upd = None
        p_next = hot(0)
        for j in range(ng):
            p_cur = p_next
            if j + 1 < ng:
                p_next = hot(j + 1)
            part = lax.dot_general(p_cur, yr[j * ge * cap:(j + 1) * ge * cap, :], _TN_DIMS,
                                   preferred_element_type=F32)
            upd = part if upd is None else upd + part
        x = x_ref[...] + mod_ref[5:6, :] * upd
        o_ref[...] = _rms(x, gf_ref[...]) if final else x

    if with_ctx:
        t = pl.program_id(1)
        pl.when(t < LPB)(lambda: finish(p_ref, y_ref, CAP))
        pl.when(t == LPB)(lambda: finish(pc_ref, yc_ref, CAP_C))
    else:
        finish(p_ref, y_ref, CAP)


def _scatter(rank, y, rank_c, yc, x3, mod, final_g):
    with_ctx = rank_c is not None
    final = final_g is not None
    tpb = TPB if with_ctx else LPB
    in_specs = [pl.BlockSpec((None, N_EXP, 1, TM), lambda b, t: (b, 0, 0, jnp.minimum(t, LPB - 1))),
                pl.BlockSpec((None, N_EXP * CAP, D), lambda b, t: (b, 0, 0))]
    args = [rank, y]
    if with_ctx:
        in_specs += [pl.BlockSpec((None, N_EXP, 1, CTX), lambda b, t: (b, 0, 0, 0)),
                     pl.BlockSpec((None, N_EXP * CAP_C, D), lambda b, t: (b, 0, 0))]
        args += [rank_c, yc]
    in_specs += [pl.BlockSpec((None, TM, D), lambda b, t: (b, t, 0)),
                 pl.BlockSpec((None, 6, D), lambda b, t: (jnp.where(t == LPB, B, b), 0, 0))]
    args += [x3, mod]
    if final:
        in_specs.append(pl.BlockSpec((1, D), lambda b, t: (0, 0)))
        args.append(final_g)
    return pl.pallas_call(
        functools.partial(_scatter_kernel, with_ctx, final),
        grid=(B, tpb),
        in_specs=in_specs,
        out_specs=pl.BlockSpec((None, TM, D), lambda b, t: (b, t, 0)),
        out_shape=jax.ShapeDtypeStruct((B, tpb * TM, D), F32),
        compiler_params=_cp(("arbitrary", "arbitrary"), 56),
        name="scatter",
    )(*args)


def _rope_tables():
    n_freq = QK_ROPE // 4
    grid_w = 64
    pos = jnp.arange(SEQ, dtype=F32)
    inv_freq = 10000.0 ** (-jnp.arange(n_freq, dtype=F32) / n_freq)
    ang_r = jnp.floor(pos / grid_w)[:, None] * inv_freq
    ang_c = (pos - grid_w * jnp.floor(pos / grid_w))[:, None] * inv_freq
    cr, sr, cc, sc = jnp.cos(ang_r), jnp.sin(ang_r), jnp.cos(ang_c), jnp.sin(ang_c)
    cos = jnp.concatenate([cr, cr, cc, cc], axis=1)
    sin = jnp.concatenate([-sr, sr, -sc, sc], axis=1)
    cos = jnp.concatenate([cos, jnp.ones((CTX, QK_ROPE), F32)], axis=0)
    sin = jnp.concatenate([sin, jnp.zeros((CTX, QK_ROPE), F32)], axis=0)
    z = jnp.zeros((S2, QK_ROPE), F32)
    qs = ATT_SCALE * LOG2E
    tq1 = jnp.concatenate([jnp.full((S2, QK_NOPE), qs, F32), cos * qs, z], axis=1)
    tq2 = jnp.concatenate([jnp.zeros((S2, QK_NOPE), F32), sin * qs, z], axis=1)
    tk1 = jnp.concatenate([cos, z], axis=1)
    tk2 = jnp.concatenate([sin, z], axis=1)
    return tq1, tq2, tk1, tk2


def _pair_swap(w):
    return jnp.concatenate([w[..., 16:32], w[..., 0:16], w[..., 48:64], w[..., 32:48]], axis=-1)


def _s5_place_kernel(bre_ref, bim_ref, cre_ref, cim_ref, w_ref, c_ref):
    hg = S5_G // 2
    w_ref[...] = jnp.zeros_like(w_ref)
    c_ref[...] = jnp.zeros_like(c_ref)
    for g in range(hg):
        r = slice(g * S5_CH, (g + 1) * S5_CH)
        for part, (b_ref, k_ref, sign) in enumerate(((bre_ref, cre_ref, 1.0), (bim_ref, cim_ref, -1.0))):
            s = slice(part * S5_HALF + g * S5_N, part * S5_HALF + (g + 1) * S5_N)
            w_ref[r, s] = b_ref[g].astype(BF16)
            c_ref[s, r] = (sign * k_ref[g]).astype(BF16)


def _s5_operators(a_re, a_im, log_dt, b_re, b_im, c_re, c_im):
    a = lax.complex(jnp.minimum(a_re.astype(F32), -1e-4), a_im.astype(F32))
    dt = jnp.exp(log_dt.astype(F32))[..., None]
    abar = jnp.exp(a * dt)
    bbar = ((abar - 1.0) / a)[..., None] * lax.complex(b_re.astype(F32), b_im.astype(F32))
    hg = S5_G // 2

    def per_group(m):
        return m.reshape(DEPTH, 2, 2, hg, m.shape[-2], m.shape[-1])

    bt = jnp.swapaxes(bbar, -1, -2)
    ct_re = jnp.swapaxes(c_re.astype(F32), -1, -2)
    ct_im = jnp.swapaxes(c_im.astype(F32), -1, -2)
    bspec = pl.BlockSpec((None, None, None, hg, S5_CH, S5_N), lambda l, z, h: (l, z, h, 0, 0, 0))
    cspec = pl.BlockSpec((None, None, None, hg, S5_N, S5_CH), lambda l, z, h: (l, z, h, 0, 0, 0))
    w_bd, c_bd = pl.pallas_call(
        _s5_place_kernel,
        grid=(DEPTH, 2, 2),
        in_specs=[bspec, bspec, cspec, cspec],
        out_specs=[pl.BlockSpec((None, None, None, GW // 2, 2 * S5_HALF), lambda l, z, h: (l, z, h, 0, 0)),
                   pl.BlockSpec((None, None, None, 2 * S5_HALF, GW // 2), lambda l, z, h: (l, z, h, 0, 0))],
        out_shape=[jax.ShapeDtypeStruct((DEPTH, 2, 2, GW // 2, 2 * S5_HALF), BF16),
                   jax.ShapeDtypeStruct((DEPTH, 2, 2, 2 * S5_HALF, GW // 2), BF16)],
        compiler_params=_cp(("arbitrary",) * 3, 32),
        name="s5_place",
    )(per_group(jnp.real(bt)), per_group(jnp.imag(bt)), per_group(ct_re), per_group(ct_im))

    def rows8(m):
        m = m.reshape(DEPTH, 2, 1, 2, S5_LT, 128)
        m = jnp.broadcast_to(m, (DEPTH, 2, B, 2, S5_LT, 128)).reshape(DEPTH, 2, 2 * B, S5_LT, 128)
        return jnp.transpose(m, (0, 1, 3, 2, 4))

    return w_bd, rows8(jnp.real(abar)), rows8(jnp.imag(abar)), c_bd


def kernel(x, c, ctx, c_ctx, norm1_g, norm2_g, w_ada, b_ada, w_in, w_out, sgu_norm_g, sgu_w,
           sgu_b, mla_q_norm_g, mla_w_uq, mla_kv_norm_g, mla_w_ukv, s5_a_re, s5_a_im, s5_log_dt,
           s5_b_re, s5_b_im, s5_c_re, s5_c_im, s5_d, s5_w_glu, s5_b_glu, conv_w, moe_w_router,
           moe_w_gate, moe_w_up, moe_w_down, final_norm_g):
    c8 = jnp.concatenate([c, c_ctx[None, :], jnp.zeros((3, D), F32)], axis=0)
    mod_all = _modulation(c8, w_ada, b_ada).reshape(DEPTH, 8, 6, D)
    rope_t = _rope_tables()
    tri = jnp.triu(jnp.ones((SEQ, SEQ), BF16))
    w_in_r = _winprep(jnp.transpose(w_in, (0, 2, 1)))
    w_out_bf = w_out.astype(BF16)
    s5_ops = _s5_operators(s5_a_re, s5_a_im, s5_log_dt, s5_b_re, s5_b_im, s5_c_re, s5_c_im)
    xs = (x.reshape(B * SEQ, D), ctx.reshape(B * CTX, D))

    for i in range(DEPTH):
        last = i == DEPTH - 1
        mod = mod_all[i]
        if i > 0:
            xs = (x3.reshape(B * S2, D),)

        wq = mla_w_uq[i].reshape(Q_LORA, MLA_HEADS, QK_NOPE + QK_ROPE)
        wq_r = wq[:, :, QK_NOPE:]
        wq_ext = jnp.concatenate([wq[:, :, :QK_NOPE], wq_r, _pair_swap(wq_r)], axis=2)
        wq_ext = wq_ext.reshape(Q_LORA, MLA_HEADS * QK_PAD).astype(BF16)
        wkv = mla_w_ukv[i].reshape(KV_LORA, MLA_HEADS, 2 * QK_NOPE)
        wkv_ext = jnp.concatenate([wkv[:, :, :QK_NOPE].reshape(KV_LORA, -1),
                                   wkv[:, :, QK_NOPE:].reshape(KV_LORA, -1)], axis=1).astype(BF16)
        sgu_p = (sgu_norm_g[i][None, :], sgu_w[i].astype(BF16),
                 jnp.repeat(jnp.swapaxes(sgu_b[i], 0, 1), 128, axis=1))
        mla_p = (mla_q_norm_g[i][None, :], mla_kv_norm_g[i][None, :], wq_ext, wkv_ext) + rope_t
        sgu_o, q, kc, v, p_s5, p_conv = _inproj(xs, mod, norm1_g[i][None, :], w_in_r, i,
                                                sgu_p, mla_p)

        qk3 = (B, S2, MLA_HEADS * QK_PAD)
        attn_o = _attention(q.reshape(qk3), kc.reshape(qk3), v.reshape(qk3), not last)
        attn_o = attn_o.reshape(-1, GW)

        u3 = p_s5.reshape(B, S2, GW)
        y_fwd = _s5_pass(u3, s5_ops, i, None)
        ssm_o = _s5_pass(u3, s5_ops, i,
                         (y_fwd, s5_d[i][None, :], s5_w_glu[i].astype(BF16), s5_b_glu[i][None, :]))
        ssm_o = ssm_o.reshape(B * S2, GW)

        wr_t = jnp.transpose(moe_w_router[i])
        wr_hi = wr_t.astype(BF16)
        wr2 = jnp.concatenate([wr_hi, (wr_t - wr_hi.astype(F32)).astype(BF16)], axis=0)
        x1, h2, aff_t = _outproj(sgu_o, attn_o, ssm_o, p_conv, conv_w[i], w_out_bf, xs, mod,
                                 norm2_g[i][None, :], wr2, i, not last)

        rows_b = SEQ if last else S2
        sel = _select(aff_t, tri, not last)
        h3 = h2.reshape(B, rows_b, D)
        xs = _gather(sel[0], h3, ctx_only=False)
        xc = _gather(sel[2], h3, ctx_only=True) if not last else None
        ys = _ffn(xs, sel[1], xc, sel[3] if not last else None, moe_w_gate, moe_w_up, moe_w_down, i)
        x1_3 = x1.reshape(B, rows_b, D)
        if last:
            x3 = _scatter(sel[0], ys[0], None, None, x1_3, mod, final_norm_g[None, :])
        else:
            x3 = _scatter(sel[0], ys[0], sel[2], ys[1], x1_3, mod, None)

    return x3
```

```python
import functools

import jax
import jax.numpy as jnp
from jax import lax
from jax.experimental import pallas as pl
from jax.experimental.pallas import tpu as pltpu

F32 = jnp.float32
BF16 = jnp.bfloat16

D = 2048
B = 4
SEQ = 2048
CTX = 256
S2 = SEQ + CTX
DEPTH = 2
GW = 512
EPS = 1e-6

TM = 256
TPB = S2 // TM
LPB = SEQ // TM
NT = B * TPB
NLT = B * LPB

SGU_HEADS = 4
CHUNK = 128
MLA_HEADS = 4
QK_NOPE = 128
QK_ROPE = 64
QK_PAD = 256
Q_LORA = 384
KV_LORA = 256
ATT_SCALE = (QK_NOPE + QK_ROPE) ** -0.5
LOG2E = 1.4426950408889634

S5_G = 32
S5_N = 64
S5_CH = 16
S5_MAX_RE = -1e-4
S5_T = 128
S5_NCH = S2 // S5_T
S5_HALF = (S5_G // 2) * S5_N
S5_LT = S5_HALF // 128
S5_PITCH = 9

N_EXP = 16
FF = D // 2
CAP = 2 * SEQ // N_EXP
CAP_C = 2 * CTX // N_EXP
FF_T = 256

COL_MLA = 2 * GW
MLA_W = Q_LORA + KV_LORA + 2 * QK_ROPE
COL_S5 = COL_MLA + MLA_W
COL_CONV = COL_S5 + GW
IN_W = COL_CONV + 3 * GW
MIB = 1024 * 1024


def _cp(sem, vmem_mb):
    return pltpu.CompilerParams(dimension_semantics=sem, vmem_limit_bytes=vmem_mb * MIB)


def _lat_tile(i):
    return (i // LPB) * TPB + i % LPB


def _seg_of_tile(t):
    return jnp.where(t % TPB == TPB - 1, B, t // TPB)


def _rms(x, g):
    return x * lax.rsqrt(jnp.mean(x * x, axis=-1, keepdims=True) + EPS) * g


def _modnorm(x, g, shift, scale):
    return _rms(x, g) * (1.0 + scale) + shift


def _mod_kernel(c_ref, w_ref, b_ref, o_ref):
    a = c_ref[...]
    a = a * jax.nn.sigmoid(a)
    o_ref[...] = jnp.dot(a.astype(BF16), w_ref[...].astype(BF16),
                         preferred_element_type=F32) + b_ref[...]


def _modulation(c8, w_ada, b_ada):
    tn = 1024
    return pl.pallas_call(
        _mod_kernel,
        grid=(DEPTH, 6 * D // tn),
        in_specs=[pl.BlockSpec((8, D), lambda l, j: (0, 0)),
                  pl.BlockSpec((None, D, tn), lambda l, j: (l, 0, j)),
                  pl.BlockSpec((None, 1, tn), lambda l, j: (l, 0, j))],
        out_specs=pl.BlockSpec((None, 8, tn), lambda l, j: (l, 0, j)),
        out_shape=jax.ShapeDtypeStruct((DEPTH, 8, 6 * D), F32),
        compiler_params=_cp(("arbitrary", "arbitrary"), 40),
        name="modulation",
    )(c8, w_ada, b_ada.reshape(DEPTH, 1, 6 * D))


KR0 = COL_MLA + Q_LORA + KV_LORA
IN_RAW = IN_W - QK_ROPE


WP_T = 256
WP_SWAP = KR0 // WP_T


def _winprep_kernel(prev_ref, cur_ref, o_ref):
    j = pl.program_id(1)
    keep = WP_T - QK_ROPE

    @pl.when(j < WP_SWAP)
    def _():
        o_ref[...] = cur_ref[...].astype(BF16)

    @pl.when(j == WP_SWAP)
    def _():
        cur = cur_ref[...]
        o_ref[:keep, :] = cur[:keep, :].astype(BF16)
        kr = cur[keep - QK_ROPE:keep, :]
        sw = jnp.concatenate([kr[16:32], kr[0:16], kr[48:64], kr[32:48]], axis=0)
        o_ref[keep:, :] = sw.astype(BF16)

    @pl.when(j > WP_SWAP)
    def _():
        o_ref[:QK_ROPE, :] = prev_ref[...].astype(BF16)
        o_ref[QK_ROPE:, :] = cur_ref[:keep, :].astype(BF16)


def _winprep(w_in_t):
    assert KR0 + QK_ROPE == (WP_SWAP + 1) * WP_T - QK_ROPE
    sub = WP_T // QK_ROPE
    return pl.pallas_call(
        _winprep_kernel,
        grid=(DEPTH, IN_W // WP_T),
        in_specs=[pl.BlockSpec((None, QK_ROPE, D), lambda l, j: (l, jnp.maximum(sub * j - 1, 0), 0)),
                  pl.BlockSpec((None, WP_T, D), lambda l, j: (l, j, 0))],
        out_specs=pl.BlockSpec((None, WP_T, D), lambda l, j: (l, j, 0)),
        out_shape=jax.ShapeDtypeStruct((DEPTH, IN_W, D), BF16),
        compiler_params=_cp(("arbitrary", "arbitrary"), 32),
        name="winprep",
    )(w_in_t, w_in_t)


def _stream_rows(refs, tile):
    if len(refs) == 1:
        return refs[0][...]
    return jnp.where(tile % TPB == TPB - 1, refs[1][...], refs[0][...])


def _stream_specs(xs):
    if len(xs) == 1:
        return lambda full: [pl.BlockSpec((TM, D), lambda i: (full(i), 0))]
    lat = lambda t: (t // TPB) * LPB + jnp.minimum(t % TPB, LPB - 1)
    return lambda full: [pl.BlockSpec((TM, D), lambda i: (lat(full(i)), 0)),
                         pl.BlockSpec((CTX, D), lambda i: (full(i) // TPB, 0))]


def _inproj_kernel(nx, *refs):
    (mod_ref, g_ref, w_ref, sg_ref, sw_ref, sb_ref,
     gq_ref, gkv_ref, wq_ref, wkv_ref, tq1_ref, tq2_ref, tk1_ref, tk2_ref,
     sgu_ref, q_ref, kc_ref, v_ref, s5_ref, conv_ref) = refs[nx:]
    x = _stream_rows(refs[:nx], pl.program_id(0))
    h = _modnorm(x, g_ref[...], mod_ref[0:1, :], mod_ref[1:2, :]).astype(BF16)

    def mm(a, b):
        return lax.dot_general(h, w_ref[a:b, :], _NT_DIMS, preferred_element_type=F32)

    p_a = mm(0, COL_MLA)
    pm = mm(COL_MLA, COL_S5)

    p = jax.nn.gelu(p_a)
    u = p[:, :GW]
    vb = _rms(p[:, GW:], sg_ref[...]).astype(BF16)
    for ck in range(TM // CHUNK):
        r = slice(ck * CHUNK, (ck + 1) * CHUNK)
        for hd in range(SGU_HEADS):
            c = slice(hd * 128, (hd + 1) * 128)
            m = jnp.dot(sw_ref[hd], vb[r, c], preferred_element_type=F32)
            sgu_ref[r, c] = (u[r, c] * (m + sb_ref[:, c])).astype(BF16)

    s5_ref[...] = mm(COL_S5, COL_CONV)
    conv_ref[:, :GW] = mm(COL_CONV, COL_CONV + GW)

    cq = _rms(pm[:, :Q_LORA], gq_ref[...]).astype(BF16)
    q = jnp.dot(cq, wq_ref[...], preferred_element_type=F32)
    tq1 = tq1_ref[...]
    tq2 = tq2_ref[...]
    for hd in range(MLA_HEADS):
        c = slice(hd * QK_PAD, (hd + 1) * QK_PAD)
        blk = q[:, c]
        q_ref[:, c] = (blk * tq1 + pltpu.roll(blk, QK_PAD - QK_ROPE, 1) * tq2).astype(BF16)
    ckv = _rms(pm[:, Q_LORA:Q_LORA + KV_LORA], gkv_ref[...]).astype(BF16)
    kv = jnp.dot(ckv, wkv_ref[...], preferred_element_type=F32)
    ones = jnp.ones((TM, 128), BF16)
    for hd in range(MLA_HEADS):
        v_ref[:, hd * QK_PAD:hd * QK_PAD + 128] = kv[:, GW + hd * 128:GW + (hd + 1) * 128].astype(BF16)
        v_ref[:, hd * QK_PAD + 128:(hd + 1) * QK_PAD] = ones
    kt = pm[:, Q_LORA + KV_LORA:]
    kr = (kt * tk1_ref[...] + pltpu.roll(kt, QK_ROPE, 1) * tk2_ref[...]).astype(BF16)
    for hd in range(MLA_HEADS):
        kc_ref[:, hd * QK_PAD:hd * QK_PAD + QK_NOPE] = kv[:, hd * 128:(hd + 1) * 128].astype(BF16)
        kc_ref[:, hd * QK_PAD + QK_NOPE:(hd + 1) * QK_PAD] = kr

    conv_ref[:, GW:] = mm(COL_CONV + GW, COL_CONV + 2 * GW) * mm(COL_CONV + 2 * GW, IN_W)


def _inproj(xs, mod, g1, w_in_r, li, sgu_p, mla_p):
    fix2 = lambda i: (0, 0)
    pos = lambda i: (i % TPB, 0)
    row = lambda i: (i, 0)
    qkw = MLA_HEADS * QK_PAD
    return pl.pallas_call(
        functools.partial(_inproj_kernel, len(xs)),
        grid=(NT,),
        in_specs=_stream_specs(xs)(lambda i: i) + [
                  pl.BlockSpec((None, 6, D), lambda i: (_seg_of_tile(i), 0, 0)),
                  pl.BlockSpec((1, D), fix2),
                  pl.BlockSpec((None, IN_W, D), lambda i: (li, 0, 0)),
                  pl.BlockSpec((1, GW), fix2),
                  pl.BlockSpec((SGU_HEADS, CHUNK, CHUNK), lambda i: (0, 0, 0)),
                  pl.BlockSpec((CHUNK, GW), fix2),
                  pl.BlockSpec((1, Q_LORA), fix2),
                  pl.BlockSpec((1, KV_LORA), fix2),
                  pl.BlockSpec((Q_LORA, qkw), fix2),
                  pl.BlockSpec((KV_LORA, 2 * GW), fix2),
                  pl.BlockSpec((TM, QK_PAD), pos),
                  pl.BlockSpec((TM, QK_PAD), pos),
                  pl.BlockSpec((TM, 128), pos),
                  pl.BlockSpec((TM, 128), pos)],
        out_specs=[pl.BlockSpec((TM, GW), row),
                   pl.BlockSpec((TM, qkw), row),
                   pl.BlockSpec((TM, qkw), row),
                   pl.BlockSpec((TM, qkw), row),
                   pl.BlockSpec((TM, GW), row),
                   pl.BlockSpec((TM, 2 * GW), row)],
        out_shape=[jax.ShapeDtypeStruct((B * S2, GW), BF16),
                   jax.ShapeDtypeStruct((B * S2, qkw), BF16),
                   jax.ShapeDtypeStruct((B * S2, qkw), BF16),
                   jax.ShapeDtypeStruct((B * S2, qkw), BF16),
                   jax.ShapeDtypeStruct((B * S2, GW), F32),
                   jax.ShapeDtypeStruct((B * S2, 2 * GW), F32)],
        compiler_params=_cp(("arbitrary",), 56),
        name="inproj",
    )(*xs, mod, g1, w_in_r, *sgu_p, *mla_p)


def _conv_tile(tile, p_ref, zp_ref, zn_ref, w_ref):
    r = tile % TPB
    bg = p_ref[:, :GW]
    z = p_ref[:, GW:]
    row = lax.broadcasted_iota(jnp.int32, (TM, GW), 0)
    has_prev = jnp.logical_and(r != 0, r != TPB - 1)
    has_next = r < LPB - 1
    prev_row = zp_ref[7:8, :] * has_prev.astype(F32)
    next_row = zn_ref[0:1, :] * has_next.astype(F32)
    zm = jnp.where(row == 0, prev_row, pltpu.roll(z, 1, 0))
    zp = jnp.where(row == TM - 1, next_row, pltpu.roll(z, TM - 1, 0))
    y = w_ref[0:1, :] * zm + w_ref[1:2, :] * z + w_ref[2:3, :] * zp
    return (bg * y).astype(BF16)


def _conv_specs(full):
    rb = TM // 8
    nrb = B * S2 // 8
    return [pl.BlockSpec((TM, 2 * GW), lambda i: (full(i), 0)),
            pl.BlockSpec((8, GW), lambda i: (jnp.maximum(full(i) * rb - 1, 0), 1)),
            pl.BlockSpec((8, GW), lambda i: (jnp.minimum((full(i) + 1) * rb, nrb - 1), 1)),
            pl.BlockSpec((3, GW), lambda i: (0, 0))]


_NT_DIMS = (((1,), (1,)), ((), ()))


AT_SUB = 4


def _attn_kernel(with_ctx, q_ref, kc_ref, v_ref, o_ref):
    def run(k0, nsub):
        def scores(it):
            r, hd = it
            cq = slice(hd * QK_PAD, (hd + 1) * QK_PAD)
            return lax.dot_general(q_ref[r * TM:(r + 1) * TM, cq], kc_ref[k0:, cq], _NT_DIMS,
                                   preferred_element_type=F32)

        def weights(s):
            return jnp.exp2(s - jnp.max(s, axis=-1, keepdims=True)).astype(BF16)

        def values(it, e):
            r, hd = it
            o = jnp.dot(e, v_ref[k0:, hd * QK_PAD:(hd + 1) * QK_PAD], preferred_element_type=F32)
            o_ref[r * TM:(r + 1) * TM, hd * 128:(hd + 1) * 128] = (
                (o[:, :128] / o[:, 128:129]).astype(BF16))

        items = [(r, hd) for r in range(nsub) for hd in range(MLA_HEADS)]
        s_next = scores(items[0])
        e_prev = None
        for n, it in enumerate(items):
            s_cur = s_next
            if n + 1 < len(items):
                s_next = scores(items[n + 1])
            e_cur = weights(s_cur)
            if e_prev is not None:
                values(items[n - 1], e_prev)
            e_prev = e_cur
        values(items[-1], e_prev)

    if with_ctx:
        t = pl.program_id(1)
        pl.when(t < LPB // AT_SUB)(lambda: run(0, AT_SUB))
        pl.when(t == LPB // AT_SUB)(lambda: run(SEQ, 1))
    else:
        run(0, AT_SUB)


def _attention(q3, kc3, v3, with_ctx):
    steps = LPB // AT_SUB + (1 if with_ctx else 0)
    qkw = MLA_HEADS * QK_PAD
    blk = AT_SUB * TM
    return pl.pallas_call(
        functools.partial(_attn_kernel, with_ctx),
        grid=(B, steps),
        in_specs=[pl.BlockSpec((None, blk, qkw), lambda b, t: (b, t, 0)),
                  pl.BlockSpec((None, S2, qkw), lambda b, t: (b, 0, 0)),
                  pl.BlockSpec((None, S2, qkw), lambda b, t: (b, 0, 0))],
        out_specs=pl.BlockSpec((None, blk, GW), lambda b, t: (b, t, 0)),
        out_shape=jax.ShapeDtypeStruct((B, S2 if with_ctx else SEQ, GW), BF16),
        compiler_params=_cp(("arbitrary", "arbitrary"), 48),
        name="attn",
    )(q3, kc3, v3)


def _s5_kernel(backward, *refs):
    if backward:
        (u_ref, w_ref, are_ref, aim_ref, c_ref, yf_ref, d_ref, wg_ref, bg_ref,
         o_ref, st_re, st_im, buf0, buf1) = refs
    else:
        u_ref, w_ref, are_ref, aim_ref, c_ref, o_ref, st_re, st_im, buf0, buf1 = refs
    bufs = (buf0, buf1)
    k = pl.program_id(0)
    hw = GW // 2
    nlt = S5_LT
    ht = S5_T // 2
    npc = nlt

    @pl.when(k == 0)
    def _():
        st_re[...] = jnp.zeros_like(st_re)
        st_im[...] = jnp.zeros_like(st_im)

    def rows_of(ref, hf):
        return jnp.concatenate([ref[b, hf * ht:(hf + 1) * ht, :] for b in range(B)], axis=0)

    def expand_pieces(hf):
        ub = rows_of(u_ref, hf).astype(BF16)
        buf = bufs[hf]

        def piece(h, n):
            bu = jnp.dot(ub[:, h * hw:(h + 1) * hw], w_ref[h, :, n * 256:(n + 1) * 256],
                         preferred_element_type=F32)
            for b in range(B):
                for cc in range(2):
                    buf[2 * n + cc, pl.ds(2 * b + h, ht, stride=S5_PITCH), :] = (
                        bu[b * ht:(b + 1) * ht, cc * 128:(cc + 1) * 128])

        return [functools.partial(piece, h, n) for h in range(2) for n in range(npc)]

    a_re = are_ref[...]
    a_im = aim_ref[...]
    state = [st_re[...], st_im[...]]

    def scan_pieces(hf, per):
        order = list(range(ht - 1, -1, -1)) if backward else list(range(ht))
        buf = bufs[hf]

        def piece(steps):
            sr, si = state
            for j in steps:
                r0 = S5_PITCH * j
                br = buf[0:nlt, r0:r0 + 8, :]
                bi = buf[nlt:2 * nlt, r0:r0 + 8, :]
                sr, si = a_re * sr - a_im * si + br, a_re * si + a_im * sr + bi
                buf[0:nlt, r0:r0 + 8, :] = sr
                buf[nlt:2 * nlt, r0:r0 + 8, :] = si
            state[0], state[1] = sr, si

        return [functools.partial(piece, order[i:i + per]) for i in range(0, ht, per)]

    acc = {}

    def readout_pieces(hf):
        buf = bufs[hf]

        def piece(h, n):
            s = jnp.concatenate(
                [jnp.concatenate([buf[2 * n + cc, pl.ds(2 * b + h, ht, stride=S5_PITCH), :]
                                  for cc in range(2)], axis=1) for b in range(B)], axis=0)
            part = jnp.dot(s.astype(BF16), c_ref[h, n * 256:(n + 1) * 256, :],
                           preferred_element_type=F32)
            acc[(hf, h)] = part if n == 0 else acc[(hf, h)] + part

        return [functools.partial(piece, h, n) for h in range(2) for n in range(npc)]

    def finish(hf):
        y = jnp.concatenate([acc[(hf, 0)], acc[(hf, 1)]], axis=1)
        if backward:
            y = y + rows_of(yf_ref, hf) + d_ref[...] * rows_of(u_ref, hf)
            g = jax.nn.gelu(y)
            z = jnp.dot(g.astype(BF16), wg_ref[...], preferred_element_type=F32) + bg_ref[...]
            y = (g * jax.nn.sigmoid(z)).astype(BF16)
        for b in range(B):
            o_ref[b, hf * ht:(hf + 1) * ht, :] = y[b * ht:(b + 1) * ht]

    def interleave(mxu, vpu):
        for i, m in enumerate(mxu):
            m()
            if i < len(vpu):
                vpu[i]()

    first, second = (1, 0) if backward else (0, 1)
    per = ht // (2 * npc)
    for m in expand_pieces(first):
        m()
    interleave(expand_pieces(second), scan_pieces(first, per))
    interleave(readout_pieces(first), scan_pieces(second, per))
    st_re[...] = state[0]
    st_im[...] = state[1]
    finish(first)
    for m in readout_pieces(second):
        m()
    finish(second)


def _s5_pass(u3, ops, li, glu):
    backward = glu is not None
    dr = 1 if backward else 0
    if backward:
        blk = lambda k: (0, S5_NCH - 1 - k, 0)
    else:
        blk = lambda k: (0, (k + SEQ // S5_T) % S5_NCH, 0)
    w_bd, a_re8, a_im8, c_bd = ops
    fixed = lambda k: (li, dr, 0, 0, 0)
    in_specs = [pl.BlockSpec((B, S5_T, GW), blk),
                pl.BlockSpec((None, None, 2, GW // 2, 2 * S5_HALF), fixed),
                pl.BlockSpec((None, None, S5_LT, 8, 128), fixed),
                pl.BlockSpec((None, None, S5_LT, 8, 128), fixed),
                pl.BlockSpec((None, None, 2, 2 * S5_HALF, GW // 2), fixed)]
    args = [u3, w_bd, a_re8, a_im8, c_bd]
    if backward:
        in_specs += [pl.BlockSpec((B, S5_T, GW), blk),
                     pl.BlockSpec((1, GW), lambda k: (0, 0)),
                     pl.BlockSpec((GW, GW), lambda k: (0, 0)),
                     pl.BlockSpec((1, GW), lambda k: (0, 0))]
        args += list(glu)
    return pl.pallas_call(
        functools.partial(_s5_kernel, backward),
        grid=(S5_NCH,),
        in_specs=in_specs,
        out_specs=pl.BlockSpec((B, S5_T, GW), blk),
        out_shape=jax.ShapeDtypeStruct((B, S2, GW), BF16 if backward else F32),
        scratch_shapes=[pltpu.VMEM((S5_LT, 8, 128), F32),
                        pltpu.VMEM((S5_LT, 8, 128), F32),
                        pltpu.VMEM((2 * S5_LT, S5_PITCH * S5_T // 2, 128), F32),
                        pltpu.VMEM((2 * S5_LT, S5_PITCH * S5_T // 2, 128), F32)],
        compiler_params=_cp(("arbitrary",), 48),
        name="s5_bwd_glu" if backward else "s5_fwd",
    )(*args)


OP_SUB = 2


def _outproj_kernel(nx, full, *refs):
    per = nx + 7
    cw_ref, w_ref, g2_ref, wr_ref, x1_ref, h2_ref, aff_ref = refs[OP_SUB * per:]
    tiles = [full(OP_SUB * pl.program_id(0) + s) for s in range(OP_SUB)]
    acts_all = []
    for s in range(OP_SUB):
        a0_ref, a1_ref, a2_ref, p_ref, zp_ref, zn_ref, _ = refs[s * per + nx:(s + 1) * per]
        acts_all.append((a0_ref[...], a1_ref[...], a2_ref[...],
                         _conv_tile(tiles[s], p_ref, zp_ref, zn_ref, cw_ref)))
    outs = []
    for s in range(OP_SUB):
        acts = acts_all[s]
        halves = []
        for c in (slice(0, D // 2), slice(D // 2, D)):
            o = jnp.dot(acts[0], w_ref[0:GW, c], preferred_element_type=F32)
            for m in range(1, 4):
                o = o + jnp.dot(acts[m], w_ref[m * GW:(m + 1) * GW, c], preferred_element_type=F32)
            halves.append(o)
        outs.append(jnp.concatenate(halves, axis=1))
    for s in range(OP_SUB):
        tr = refs[s * per:(s + 1) * per]
        mod_ref = tr[-1]
        tile = tiles[s]
        rows = slice(s * TM, (s + 1) * TM)
        x = _stream_rows(tr[:nx], tile)
        x1 = x + mod_ref[2:3, :] * outs[s]
        x1_ref[rows, :] = x1
        h2 = _modnorm(x1, g2_ref[...], mod_ref[3:4, :], mod_ref[4:5, :]).astype(BF16)
        h2_ref[rows, :] = h2
        lg2 = lax.dot_general(wr_ref[...], h2, _NT_DIMS, preferred_element_type=F32)
        lg = lg2[:N_EXP] + lg2[N_EXP:]
        e = jnp.exp(lg - jnp.max(lg, axis=0, keepdims=True))
        aff_ref[:, rows] = e / jnp.sum(e, axis=0, keepdims=True)


def _outproj(sgu_o, attn_o, ssm_o, p_conv, conv_w, w_out_bf, xs, mod, g2, wr2, li, with_ctx):
    ntiles = NT if with_ctx else NLT
    full = (lambda i: i) if with_ctx else _lat_tile
    in_specs, args = [], []
    for s in range(OP_SUB):
        out_t = lambda i, s=s: OP_SUB * i + s
        full_t = lambda i, s=s: full(OP_SUB * i + s)
        frow = lambda i, f=full_t: (f(i), 0)
        orow = lambda i, f=out_t: (f(i), 0)
        in_specs += _stream_specs(xs)(full_t) + [
            pl.BlockSpec((TM, GW), frow),
            pl.BlockSpec((TM, GW), orow),
            pl.BlockSpec((TM, GW), frow)] + _conv_specs(full_t)[:3] + [
            pl.BlockSpec((None, 6, D), lambda i, f=full_t: (_seg_of_tile(f(i)), 0, 0))]
        args += [*xs, sgu_o, attn_o, ssm_o, p_conv, p_conv, p_conv, mod]
    in_specs += [_conv_specs(full)[3],
                 pl.BlockSpec((None, D, D), lambda i: (li, 0, 0)),
                 pl.BlockSpec((1, D), lambda i: (0, 0)),
                 pl.BlockSpec((2 * N_EXP, D), lambda i: (0, 0))]
    args += [conv_w, w_out_bf, g2, wr2]
    blk = OP_SUB * TM
    return pl.pallas_call(
        functools.partial(_outproj_kernel, len(xs), full),
        grid=(ntiles // OP_SUB,),
        in_specs=in_specs,
        out_specs=[pl.BlockSpec((blk, D), lambda i: (i, 0)),
                   pl.BlockSpec((blk, D), lambda i: (i, 0)),
                   pl.BlockSpec((N_EXP, blk), lambda i: (0, i))],
        out_shape=[jax.ShapeDtypeStruct((ntiles * TM, D), F32),
                   jax.ShapeDtypeStruct((ntiles * TM, D), BF16),
                   jax.ShapeDtypeStruct((N_EXP, ntiles * TM), F32)],
        compiler_params=_cp(("arbitrary",), 56),
        name="outproj",
    )(*args)


def _one_hot_rows(rank_row, cap):
    slot = lax.broadcasted_iota(jnp.int32, (cap, rank_row.shape[-1]), 0).astype(F32)
    return jnp.where(rank_row == slot, 1.0, 0.0).astype(BF16)


def _cap_thresholds(bits_caps):
    thrs = [jnp.zeros((bits.shape[0], 1), jnp.int32) for bits, _ in bits_caps]
    for bit in range(30, -1, -1):
        for i, (bits, cap) in enumerate(bits_caps):
            cand = thrs[i] | (1 << bit)
            cnt = jnp.sum(jnp.where(bits >= cand, 1.0, 0.0), axis=1, keepdims=True)
            thrs[i] = jnp.where(cnt >= cap, cand, thrs[i])
    return thrs


def _select_one(a, bits, thr, tri, cap, rank_ref, w_ref):
    ne, n = a.shape
    gt = jnp.where(bits > thr, 1.0, 0.0)
    eq = jnp.where(bits == thr, 1.0, 0.0)
    need = cap - jnp.sum(gt, axis=1, keepdims=True)
    eq_before = jnp.dot(eq.astype(BF16), tri, preferred_element_type=F32) - eq
    sel = gt + eq * jnp.where(eq_before < need, 1.0, 0.0)
    rank = jnp.dot(sel.astype(BF16), tri, preferred_element_type=F32) - 1.0
    rank = jnp.where(sel > 0.5, rank, -1.0)
    slot = lax.broadcasted_iota(jnp.int32, (cap, n), 0).astype(F32)
    for e in range(ne):
        rank_ref[e] = rank[e:e + 1, :]
        hit = rank[e:e + 1, :] == slot
        w_ref[e * cap:(e + 1) * cap, :] = jnp.sum(jnp.where(hit, a[e:e + 1, :], 0.0),
                                                  axis=1, keepdims=True)


def _select_kernel(with_ctx, a_ref, tri_ref, *out_refs):
    sets = [(a_ref[:, :SEQ], CAP, tri_ref[...])]
    if with_ctx:
        sets.append((a_ref[:, SEQ:], CAP_C, tri_ref[:CTX, :CTX]))
    bits = [pltpu.bitcast(a, jnp.int32) for a, _, _ in sets]
    thrs = _cap_thresholds([(b, cap) for b, (_, cap, _) in zip(bits, sets)])
    for i, (a, cap, tri) in enumerate(sets):
        _select_one(a, bits[i], thrs[i], tri, cap, out_refs[2 * i], out_refs[2 * i + 1])


def _select(aff_t, tri, with_ctx):
    out_specs = [pl.BlockSpec((None, N_EXP, 1, SEQ), lambda b: (b, 0, 0, 0)),
                 pl.BlockSpec((None, N_EXP * CAP, 1), lambda b: (b, 0, 0))]
    out_shape = [jax.ShapeDtypeStruct((B, N_EXP, 1, SEQ), F32),
                 jax.ShapeDtypeStruct((B, N_EXP * CAP, 1), F32)]
    if with_ctx:
        out_specs += [pl.BlockSpec((None, N_EXP, 1, CTX), lambda b: (b, 0, 0, 0)),
                      pl.BlockSpec((None, N_EXP * CAP_C, 1), lambda b: (b, 0, 0))]
        out_shape += [jax.ShapeDtypeStruct((B, N_EXP, 1, CTX), F32),
                      jax.ShapeDtypeStruct((B, N_EXP * CAP_C, 1), F32)]
    return pl.pallas_call(
        functools.partial(_select_kernel, with_ctx),
        grid=(B,),
        in_specs=[pl.BlockSpec((N_EXP, S2 if with_ctx else SEQ), lambda b: (0, b)),
                  pl.BlockSpec((SEQ, SEQ), lambda b: (0, 0))],
        out_specs=out_specs,
        out_shape=out_shape,
        compiler_params=_cp(("arbitrary",), 48),
        name="select",
    )(aff_t, tri)


def _gather_kernel(cap, rank_ref, h_ref, o_ref):
    p = jnp.concatenate([_one_hot_rows(rank_ref[j], cap) for j in range(rank_ref.shape[0])], axis=0)
    o_ref[...] = jnp.dot(p, h_ref[...], preferred_element_type=F32).astype(BF16)


def _gather(rank, h3, ctx_only):
    if ctx_only:
        cap, n, ne = CAP_C, CTX, N_EXP
        hmap = lambda b, r: (b, TPB - 1, 0)
    else:
        cap, n, ne = CAP, SEQ, 2
        hmap = lambda b, r: (b, 0, 0)
    return pl.pallas_call(
        functools.partial(_gather_kernel, cap),
        grid=(B, N_EXP // ne),
        in_specs=[pl.BlockSpec((None, ne, 1, n), lambda b, r: (b, r, 0, 0)),
                  pl.BlockSpec((None, n, D), hmap)],
        out_specs=pl.BlockSpec((None, ne * cap, D), lambda b, r: (b, r, 0)),
        out_shape=jax.ShapeDtypeStruct((B, N_EXP * cap, D), BF16),
        compiler_params=_cp(("arbitrary", "arbitrary"), 48),
        name="gather_ctx" if ctx_only else "gather",
    )(rank, h3)


def _ffn_kernel(with_ctx, *refs):
    if with_ctx:
        x_ref, xc_ref, wg_ref, wu_ref, wd_ref, ws_ref, wsc_ref, y_ref, yc_ref, acc = refs
    else:
        x_ref, wg_ref, wu_ref, wd_ref, ws_ref, y_ref, acc = refs
    f = pl.program_id(1)
    last = FF // FF_T - 1
    nl = B * CAP

    def step(kind):
        wg = wg_ref[...].astype(BF16)
        wu = wu_ref[...].astype(BF16)
        wd = wd_ref[...].astype(BF16)
        x = x_ref[...].reshape(nl, D)
        if with_ctx:
            x = jnp.concatenate([x, xc_ref[...].reshape(B * CAP_C, D)], axis=0)
        gate = jnp.dot(x, wg, preferred_element_type=F32)
        up = jnp.dot(x, wu, preferred_element_type=F32)
        hid = (gate * jax.nn.sigmoid(gate) * up).astype(BF16)
        part = jnp.dot(hid, wd, preferred_element_type=F32)
        if kind == "first":
            acc[...] = part
        elif kind == "mid":
            acc[...] += part
        else:
            y = acc[...] + part
            y_ref[...] = (y[:nl] * ws_ref[...].reshape(nl, 1)).astype(BF16).reshape(y_ref.shape)
            if with_ctx:
                yc = y[nl:] * wsc_ref[...].reshape(B * CAP_C, 1)
                yc_ref[...] = yc.astype(BF16).reshape(yc_ref.shape)

    pl.when(f == 0)(lambda: step("first"))
    pl.when(jnp.logical_and(f > 0, f < last))(lambda: step("mid"))
    pl.when(f == last)(lambda: step("last"))


def _ffn(xs, ws, xc, wsc, w_gate, w_up, w_down, li):
    with_ctx = xc is not None
    in_specs = [pl.BlockSpec((B, CAP, D), lambda e, f: (0, e, 0))]
    args = [xs]
    if with_ctx:
        in_specs.append(pl.BlockSpec((B, CAP_C, D), lambda e, f: (0, e, 0)))
        args.append(xc)
    in_specs += [pl.BlockSpec((None, None, D, FF_T), lambda e, f: (li, e, 0, f)),
                 pl.BlockSpec((None, None, D, FF_T), lambda e, f: (li, e, 0, f)),
                 pl.BlockSpec((None, None, FF_T, D), lambda e, f: (li, e, f, 0)),
                 pl.BlockSpec((B, CAP, 1), lambda e, f: (0, e, 0))]
    args += [w_gate, w_up, w_down, ws]
    out_specs = [pl.BlockSpec((B, CAP, D), lambda e, f: (0, e, 0))]
    out_shape = [jax.ShapeDtypeStruct((B, N_EXP * CAP, D), BF16)]
    scratch = [pltpu.VMEM((B * (CAP + CAP_C if with_ctx else CAP), D), F32)]
    if with_ctx:
        in_specs.append(pl.BlockSpec((B, CAP_C, 1), lambda e, f: (0, e, 0)))
        args.append(wsc)
        out_specs.append(pl.BlockSpec((B, CAP_C, D), lambda e, f: (0, e, 0)))
        out_shape.append(jax.ShapeDtypeStruct((B, N_EXP * CAP_C, D), BF16))
    return pl.pallas_call(
        functools.partial(_ffn_kernel, with_ctx),
        grid=(N_EXP, FF // FF_T),
        in_specs=in_specs,
        out_specs=out_specs,
        out_shape=out_shape,
        scratch_shapes=scratch,
        compiler_params=_cp(("arbitrary", "arbitrary"), 56),
        name="ffn",
    )(*args)


_TN_DIMS = (((0,), (0,)), ((), ()))


def _scatter_kernel(with_ctx, final, *refs):
    refs = list(refs)
    p_ref, y_ref = refs[:2]
    pc_ref, yc_ref = refs[2:4] if with_ctx else (None, None)
    rest = refs[4:] if with_ctx else refs[2:]
    x_ref, mod_ref = rest[:2]
    gf_ref = rest[2] if final else None
    o_ref = rest[-1]

    def finish(rank_ref, yr, cap):
        ng, ge = 4, N_EXP // 4

        def hot(j):
            return jnp.concatenate([_one_hot_rows(rank_ref[e], cap)
                                    for e in range(j * ge, (j + 1) * ge)], axis=0)

        upd = None
        p_next = hot(0)
        for j in range(ng):
            p_cur = p_next
            if j + 1 < ng:
                p_next = hot(j + 1)
            part = lax.dot_general(p_cur, yr[j * ge * cap:(j + 1) * ge * cap, :], _TN_DIMS,
                                   preferred_element_type=F32)
            upd = part if upd is None else upd + part
        x = x_ref[...] + mod_ref[5:6, :] * upd
        o_ref[...] = _rms(x, gf_ref[...]) if final else x

    if with_ctx:
        t = pl.program_id(1)
        pl.when(t < LPB)(lambda: finish(p_ref, y_ref, CAP))
        pl.when(t == LPB)(lambda: finish(pc_ref, yc_ref, CAP_C))
    else:
        finish(p_ref, y_ref, CAP)


def _scatter(rank, y, rank_c, yc, x3, mod, final_g):
    with_ctx = rank_c is not None
    final = final_g is not None
    tpb = TPB if with_ctx else LPB
    in_specs = [pl.BlockSpec((None, N_EXP, 1, TM), lambda b, t: (b, 0, 0, jnp.minimum(t, LPB - 1))),
                pl.BlockSpec((None, N_EXP * CAP, D), lambda b, t: (b, 0, 0))]
    args = [rank, y]
    if with_ctx:
        in_specs += [pl.BlockSpec((None, N_EXP, 1, CTX), lambda b, t: (b, 0, 0, 0)),
                     pl.BlockSpec((None, N_EXP * CAP_C, D), lambda b, t: (b, 0, 0))]
        args += [rank_c, yc]
    in_specs += [pl.BlockSpec((None, TM, D), lambda b, t: (b, t, 0)),
                 pl.BlockSpec((None, 6, D), lambda b, t: (jnp.where(t == LPB, B, b), 0, 0))]
    args += [x3, mod]
    if final:
        in_specs.append(pl.BlockSpec((1, D), lambda b, t: (0, 0)))
        args.append(final_g)
    return pl.pallas_call(
        functools.partial(_scatter_kernel, with_ctx, final),
        grid=(B, tpb),
        in_specs=in_specs,
        out_specs=pl.BlockSpec((None, TM, D), lambda b, t: (b, t, 0)),
        out_shape=jax.ShapeDtypeStruct((B, tpb * TM, D), F32),
        compiler_params=_cp(("arbitrary", "arbitrary"), 56),
        name="scatter",
    )(*args)


def _rope_tables():
    n_freq = QK_ROPE // 4
    grid_w = 64
    pos = jnp.arange(SEQ, dtype=F32)
    inv_freq = 10000.0 ** (-jnp.arange(n_freq, dtype=F32) / n_freq)
    ang_r = jnp.floor(pos / grid_w)[:, None] * inv_freq
    ang_c = (pos - grid_w * jnp.floor(pos / grid_w))[:, None] * inv_freq
    cr, sr, cc, sc = jnp.cos(ang_r), jnp.sin(ang_r), jnp.cos(ang_c), jnp.sin(ang_c)
    cos = jnp.concatenate([cr, cr, cc, cc], axis=1)
    sin = jnp.concatenate([-sr, sr, -sc, sc], axis=1)
    cos = jnp.concatenate([cos, jnp.ones((CTX, QK_ROPE), F32)], axis=0)
    sin = jnp.concatenate([sin, jnp.zeros((CTX, QK_ROPE), F32)], axis=0)
    z = jnp.zeros((S2, QK_ROPE), F32)
    qs = ATT_SCALE * LOG2E
    tq1 = jnp.concatenate([jnp.full((S2, QK_NOPE), qs, F32), cos * qs, z], axis=1)
    tq2 = jnp.concatenate([jnp.zeros((S2, QK_NOPE), F32), sin * qs, z], axis=1)
    tk1 = jnp.concatenate([cos, z], axis=1)
    tk2 = jnp.concatenate([sin, z], axis=1)
    return tq1, tq2, tk1, tk2


def _pair_swap(w):
    return jnp.concatenate([w[..., 16:32], w[..., 0:16], w[..., 48:64], w[..., 32:48]], axis=-1)


def _s5_place_kernel(bre_ref, bim_ref, cre_ref, cim_ref, w_ref, c_ref):
    hg = S5_G // 2
    w_ref[...] = jnp.zeros_like(w_ref)
    c_ref[...] = jnp.zeros_like(c_ref)
    for g in range(hg):
        r = slice(g * S5_CH, (g + 1) * S5_CH)
        for part, (b_ref, k_ref, sign) in enumerate(((bre_ref, cre_ref, 1.0), (bim_ref, cim_ref, -1.0))):
            s = slice(part * S5_HALF + g * S5_N, part * S5_HALF + (g + 1) * S5_N)
            w_ref[r, s] = b_ref[g].astype(BF16)
            c_ref[s, r] = (sign * k_ref[g]).astype(BF16)


def _s5_operators(a_re, a_im, log_dt, b_re, b_im, c_re, c_im):
    a = lax.complex(jnp.minimum(a_re.astype(F32), S5_MAX_RE), a_im.astype(F32))
    dt = jnp.exp(log_dt.astype(F32))[..., None]
    abar = jnp.exp(a * dt)
    bbar = ((abar - 1.0) / a)[..., None] * lax.complex(b_re.astype(F32), b_im.astype(F32))
    hg = S5_G // 2

    def per_group(m):
        return m.reshape(DEPTH, 2, 2, hg, m.shape[-2], m.shape[-1])

    bt = jnp.swapaxes(bbar, -1, -2)
    ct_re = jnp.swapaxes(c_re.astype(F32), -1, -2)
    ct_im = jnp.swapaxes(c_im.astype(F32), -1, -2)
    bspec = pl.BlockSpec((None, None, None, hg, S5_CH, S5_N), lambda l, z, h: (l, z, h, 0, 0, 0))
    cspec = pl.BlockSpec((None, None, None, hg, S5_N, S5_CH), lambda l, z, h: (l, z, h, 0, 0, 0))
    w_bd, c_bd = pl.pallas_call(
        _s5_place_kernel,
        grid=(DEPTH, 2, 2),
        in_specs=[bspec, bspec, cspec, cspec],
        out_specs=[pl.BlockSpec((None, None, None, GW // 2, 2 * S5_HALF), lambda l, z, h: (l, z, h, 0, 0)),
                   pl.BlockSpec((None, None, None, 2 * S5_HALF, GW // 2), lambda l, z, h: (l, z, h, 0, 0))],
        out_shape=[jax.ShapeDtypeStruct((DEPTH, 2, 2, GW // 2, 2 * S5_HALF), BF16),
                   jax.ShapeDtypeStruct((DEPTH, 2, 2, 2 * S5_HALF, GW // 2), BF16)],
        compiler_params=_cp(("arbitrary",) * 3, 32),
        name="s5_place",
    )(per_group(jnp.real(bt)), per_group(jnp.imag(bt)), per_group(ct_re), per_group(ct_im))

    def rows8(m):
        m = m.reshape(DEPTH, 2, 1, 2, S5_LT, 128)
        m = jnp.broadcast_to(m, (DEPTH, 2, B, 2, S5_LT, 128)).reshape(DEPTH, 2, 2 * B, S5_LT, 128)
        return jnp.transpose(m, (0, 1, 3, 2, 4))

    return w_bd, rows8(jnp.real(abar)), rows8(jnp.imag(abar)), c_bd


def kernel(x, c, ctx, c_ctx, norm1_g, norm2_g, w_ada, b_ada, w_in, w_out, sgu_norm_g, sgu_w,
           sgu_b, mla_q_norm_g, mla_w_uq, mla_kv_norm_g, mla_w_ukv, s5_a_re, s5_a_im, s5_log_dt,
           s5_b_re, s5_b_im, s5_c_re, s5_c_im, s5_d, s5_w_glu, s5_b_glu, conv_w, moe_w_router,
           moe_w_gate, moe_w_up, moe_w_down, final_norm_g):
    c8 = jnp.concatenate([c, c_ctx[None, :], jnp.zeros((3, D), F32)], axis=0)
    mod_all = _modulation(c8, w_ada, b_ada).reshape(DEPTH, 8, 6, D)
    rope_t = _rope_tables()
    tri = jnp.triu(jnp.ones((SEQ, SEQ), BF16))
    w_in_r = _winprep(jnp.transpose(w_in, (0, 2, 1)))
    w_out_bf = w_out.astype(BF16)
    s5_ops = _s5_operators(s5_a_re, s5_a_im, s5_log_dt, s5_b_re, s5_b_im, s5_c_re, s5_c_im)
    xs = (x.reshape(B * SEQ, D), ctx.reshape(B * CTX, D))

    for i in range(DEPTH):
        last = i == DEPTH - 1
        mod = mod_all[i]
        if i > 0:
            xs = (x3.reshape(B * S2, D),)

        wq = mla_w_uq[i].reshape(Q_LORA, MLA_HEADS, QK_NOPE + QK_ROPE)
        wq_r = wq[:, :, QK_NOPE:]
        wq_ext = jnp.concatenate([wq[:, :, :QK_NOPE], wq_r, _pair_swap(wq_r)], axis=2)
        wq_ext = wq_ext.reshape(Q_LORA, MLA_HEADS * QK_PAD).astype(BF16)
        wkv = mla_w_ukv[i].reshape(KV_LORA, MLA_HEADS, 2 * QK_NOPE)
        wkv_ext = jnp.concatenate([wkv[:, :, :QK_NOPE].reshape(KV_LORA, -1),
                                   wkv[:, :, QK_NOPE:].reshape(KV_LORA, -1)], axis=1).astype(BF16)
        sgu_p = (sgu_norm_g[i][None, :], sgu_w[i].astype(BF16),
                 jnp.repeat(jnp.swapaxes(sgu_b[i], 0, 1), 128, axis=1))
        mla_p = (mla_q_norm_g[i][None, :], mla_kv_norm_g[i][None, :], wq_ext, wkv_ext) + rope_t
        sgu_o, q, kc, v, p_s5, p_conv = _inproj(xs, mod, norm1_g[i][None, :], w_in_r, i,
                                                sgu_p, mla_p)

        qk3 = (B, S2, MLA_HEADS * QK_PAD)
        attn_o = _attention(q.reshape(qk3), kc.reshape(qk3), v.reshape(qk3), not last)
        attn_o = attn_o.reshape(-1, GW)

        u3 = p_s5.reshape(B, S2, GW)
        y_fwd = _s5_pass(u3, s5_ops, i, None)
        ssm_o = _s5_pass(u3, s5_ops, i,
                         (y_fwd, s5_d[i][None, :], s5_w_glu[i].astype(BF16), s5_b_glu[i][None, :]))
        ssm_o = ssm_o.reshape(B * S2, GW)

        wr_t = jnp.transpose(moe_w_router[i])
        wr_hi = wr_t.astype(BF16)
        wr2 = jnp.concatenate([wr_hi, (wr_t - wr_hi.astype(F32)).astype(BF16)], axis=0)
        x1, h2, aff_t = _outproj(sgu_o, attn_o, ssm_o, p_conv, conv_w[i], w_out_bf, xs, mod,
                                 norm2_g[i][None, :], wr2, i, not last)

        rows_b = SEQ if last else S2
        sel = _select(aff_t, tri, not last)
        h3 = h2.reshape(B, rows_b, D)
        xs = _gather(sel[0], h3, ctx_only=False)
        xc = _gather(sel[2], h3, ctx_only=True) if not last else None
        ys = _ffn(xs, sel[1], xc, sel[3] if not last else None, moe_w_gate, moe_w_up, moe_w_down, i)
        x1_3 = x1.reshape(B, rows_b, D)
        if last:
            x3 = _scatter(sel[0], ys[0], None, None, x1_3, mod, final_norm_g[None, :])
        else:
            x3 = _scatter(sel[0], ys[0], sel[2], ys[1], x1_3, mod, None)

    return x3
```

```python
import functools

import jax
import jax.numpy as jnp
from jax import lax
from jax.experimental import pallas as pl
from jax.experimental.pallas import tpu as pltpu

F32 = jnp.float32
BF16 = jnp.bfloat16

D = 2048
B = 4
SEQ = 2048
CTX = 256
S2 = SEQ + CTX
DEPTH = 2
GW = 512
EPS = 1e-6

TM = 256
TPB = S2 // TM
LPB = SEQ // TM
NT = B * TPB
NLT = B * LPB

SGU_HEADS = 4
CHUNK = 128
MLA_HEADS = 4
QK_NOPE = 128
QK_ROPE = 64
QK_PAD = 256
Q_LORA = 384
KV_LORA = 256
ATT_SCALE = (QK_NOPE + QK_ROPE) ** -0.5
LOG2E = 1.4426950408889634

S5_G = 32
S5_N = 64
S5_CH = 16
S5_MAX_RE = -1e-4
S5_T = 128
S5_NCH = S2 // S5_T
S5_HALF = (S5_G // 2) * S5_N
S5_LT = S5_HALF // 128
S5_PITCH = 9

N_EXP = 16
FF = D // 2
CAP = 2 * SEQ // N_EXP
CAP_C = 2 * CTX // N_EXP
FF_T = 256

COL_MLA = 2 * GW
MLA_W = Q_LORA + KV_LORA + 2 * QK_ROPE
COL_S5 = COL_MLA + MLA_W
COL_CONV = COL_S5 + GW
IN_W = COL_CONV + 3 * GW
MIB = 1024 * 1024


def _cp(sem, vmem_mb):
    return pltpu.CompilerParams(dimension_semantics=sem, vmem_limit_bytes=vmem_mb * MIB)


def _lat_tile(i):
    return (i // LPB) * TPB + i % LPB


def _seg_of_tile(t):
    return jnp.where(t % TPB == TPB - 1, B, t // TPB)


def _rms(x, g):
    return x * lax.rsqrt(jnp.mean(x * x, axis=-1, keepdims=True) + EPS) * g


def _modnorm(x, g, shift, scale):
    return _rms(x, g) * (1.0 + scale) + shift


def _mod_kernel(c_ref, w_ref, b_ref, o_ref):
    a = c_ref[...]
    a = a * jax.nn.sigmoid(a)
    o_ref[...] = jnp.dot(a.astype(BF16), w_ref[...].astype(BF16),
                         preferred_element_type=F32) + b_ref[...]


def _modulation(c8, w_ada, b_ada):
    tn = 1024
    return pl.pallas_call(
        _mod_kernel,
        grid=(DEPTH, 6 * D // tn),
        in_specs=[pl.BlockSpec((8, D), lambda l, j: (0, 0)),
                  pl.BlockSpec((None, D, tn), lambda l, j: (l, 0, j)),
                  pl.BlockSpec((None, 1, tn), lambda l, j: (l, 0, j))],
        out_specs=pl.BlockSpec((None, 8, tn), lambda l, j: (l, 0, j)),
        out_shape=jax.ShapeDtypeStruct((DEPTH, 8, 6 * D), F32),
        compiler_params=_cp(("arbitrary", "arbitrary"), 40),
        name="modulation",
    )(c8, w_ada, b_ada.reshape(DEPTH, 1, 6 * D))


KR0 = COL_MLA + Q_LORA + KV_LORA
IN_RAW = IN_W - QK_ROPE


WP_T = 256
WP_SWAP = KR0 // WP_T


def _winprep_kernel(prev_ref, cur_ref, o_ref):
    j = pl.program_id(1)
    keep = WP_T - QK_ROPE

    @pl.when(j < WP_SWAP)
    def _():
        o_ref[...] = cur_ref[...].astype(BF16)

    @pl.when(j == WP_SWAP)
    def _():
        cur = cur_ref[...]
        o_ref[:keep, :] = cur[:keep, :].astype(BF16)
        kr = cur[keep - QK_ROPE:keep, :]
        sw = jnp.concatenate([kr[16:32], kr[0:16], kr[48:64], kr[32:48]], axis=0)
        o_ref[keep:, :] = sw.astype(BF16)

    @pl.when(j > WP_SWAP)
    def _():
        o_ref[:QK_ROPE, :] = prev_ref[...].astype(BF16)
        o_ref[QK_ROPE:, :] = cur_ref[:keep, :].astype(BF16)


def _winprep(w_in_t):
    assert KR0 + QK_ROPE == (WP_SWAP + 1) * WP_T - QK_ROPE
    sub = WP_T // QK_ROPE
    return pl.pallas_call(
        _winprep_kernel,
        grid=(DEPTH, IN_W // WP_T),
        in_specs=[pl.BlockSpec((None, QK_ROPE, D), lambda l, j: (l, jnp.maximum(sub * j - 1, 0), 0)),
                  pl.BlockSpec((None, WP_T, D), lambda l, j: (l, j, 0))],
        out_specs=pl.BlockSpec((None, WP_T, D), lambda l, j: (l, j, 0)),
        out_shape=jax.ShapeDtypeStruct((DEPTH, IN_W, D), BF16),
        compiler_params=_cp(("arbitrary", "arbitrary"), 32),
        name="winprep",
    )(w_in_t, w_in_t)


def _stream_rows(refs, tile):
    if len(refs) == 1:
        return refs[0][...]
    return jnp.where(tile % TPB == TPB - 1, refs[1][...], refs[0][...])


def _stream_specs(xs):
    if len(xs) == 1:
        return lambda full: [pl.BlockSpec((TM, D), lambda i: (full(i), 0))]
    lat = lambda t: (t // TPB) * LPB + jnp.minimum(t % TPB, LPB - 1)
    return lambda full: [pl.BlockSpec((TM, D), lambda i: (lat(full(i)), 0)),
                         pl.BlockSpec((CTX, D), lambda i: (full(i) // TPB, 0))]


def _inproj_kernel(nx, *refs):
    (mod_ref, g_ref, w_ref, sg_ref, sw_ref, sb_ref,
     gq_ref, gkv_ref, wq_ref, wkv_ref, tq1_ref, tq2_ref, tk1_ref, tk2_ref,
     sgu_ref, q_ref, kc_ref, v_ref, s5_ref, conv_ref) = refs[nx:]
    x = _stream_rows(refs[:nx], pl.program_id(0))
    h = _modnorm(x, g_ref[...], mod_ref[0:1, :], mod_ref[1:2, :]).astype(BF16)

    def mm(a, b):
        return lax.dot_general(h, w_ref[a:b, :], _NT_DIMS, preferred_element_type=F32)

    p_a = mm(0, COL_MLA)
    pm = mm(COL_MLA, COL_S5)

    p = jax.nn.gelu(p_a)
    u = p[:, :GW]
    vb = _rms(p[:, GW:], sg_ref[...]).astype(BF16)
    for ck in range(TM // CHUNK):
        r = slice(ck * CHUNK, (ck + 1) * CHUNK)
        for hd in range(SGU_HEADS):
            c = slice(hd * 128, (hd + 1) * 128)
            m = jnp.dot(sw_ref[hd], vb[r, c], preferred_element_type=F32)
            sgu_ref[r, c] = (u[r, c] * (m + sb_ref[:, c])).astype(BF16)

    s5_ref[...] = mm(COL_S5, COL_CONV)
    conv_ref[:, :GW] = mm(COL_CONV, COL_CONV + GW)

    cq = _rms(pm[:, :Q_LORA], gq_ref[...]).astype(BF16)
    q = jnp.dot(cq, wq_ref[...], preferred_element_type=F32)
    tq1 = tq1_ref[...]
    tq2 = tq2_ref[...]
    for hd in range(MLA_HEADS):
        c = slice(hd * QK_PAD, (hd + 1) * QK_PAD)
        blk = q[:, c]
        q_ref[:, c] = (blk * tq1 + pltpu.roll(blk, QK_PAD - QK_ROPE, 1) * tq2).astype(BF16)
    ckv = _rms(pm[:, Q_LORA:Q_LORA + KV_LORA], gkv_ref[...]).astype(BF16)
    kv = jnp.dot(ckv, wkv_ref[...], preferred_element_type=F32)
    ones = jnp.ones((TM, 128), BF16)
    for hd in range(MLA_HEADS):
        v_ref[:, hd * QK_PAD:hd * QK_PAD + 128] = kv[:, GW + hd * 128:GW + (hd + 1) * 128].astype(BF16)
        v_ref[:, hd * QK_PAD + 128:(hd + 1) * QK_PAD] = ones
    kt = pm[:, Q_LORA + KV_LORA:]
    kr = (kt * tk1_ref[...] + pltpu.roll(kt, QK_ROPE, 1) * tk2_ref[...]).astype(BF16)
    for hd in range(MLA_HEADS):
        kc_ref[:, hd * QK_PAD:hd * QK_PAD + QK_NOPE] = kv[:, hd * 128:(hd + 1) * 128].astype(BF16)
        kc_ref[:, hd * QK_PAD + QK_NOPE:(hd + 1) * QK_PAD] = kr

    conv_ref[:, GW:] = mm(COL_CONV + GW, COL_CONV + 2 * GW) * mm(COL_CONV + 2 * GW, IN_W)


def _inproj(xs, mod, g1, w_in_r, li, sgu_p, mla_p):
    fix2 = lambda i: (0, 0)
    pos = lambda i: (i % TPB, 0)
    row = lambda i: (i, 0)
    qkw = MLA_HEADS * QK_PAD
    return pl.pallas_call(
        functools.partial(_inproj_kernel, len(xs)),
        grid=(NT,),
        in_specs=_stream_specs(xs)(lambda i: i) + [
                  pl.BlockSpec((None, 6, D), lambda i: (_seg_of_tile(i), 0, 0)),
                  pl.BlockSpec((1, D), fix2),
                  pl.BlockSpec((None, IN_W, D), lambda i: (li, 0, 0)),
                  pl.BlockSpec((1, GW), fix2),
                  pl.BlockSpec((SGU_HEADS, CHUNK, CHUNK), lambda i: (0, 0, 0)),
                  pl.BlockSpec((CHUNK, GW), fix2),
                  pl.BlockSpec((1, Q_LORA), fix2),
                  pl.BlockSpec((1, KV_LORA), fix2),
                  pl.BlockSpec((Q_LORA, qkw), fix2),
                  pl.BlockSpec((KV_LORA, 2 * GW), fix2),
                  pl.BlockSpec((TM, QK_PAD), pos),
                  pl.BlockSpec((TM, QK_PAD), pos),
                  pl.BlockSpec((TM, 128), pos),
                  pl.BlockSpec((TM, 128), pos)],
        out_specs=[pl.BlockSpec((TM, GW), row),
                   pl.BlockSpec((TM, qkw), row),
                   pl.BlockSpec((TM, qkw), row),
                   pl.BlockSpec((TM, qkw), row),
                   pl.BlockSpec((TM, GW), row),
                   pl.BlockSpec((TM, 2 * GW), row)],
        out_shape=[jax.ShapeDtypeStruct((B * S2, GW), BF16),
                   jax.ShapeDtypeStruct((B * S2, qkw), BF16),
                   jax.ShapeDtypeStruct((B * S2, qkw), BF16),
                   jax.ShapeDtypeStruct((B * S2, qkw), BF16),
                   jax.ShapeDtypeStruct((B * S2, GW), F32),
                   jax.ShapeDtypeStruct((B * S2, 2 * GW), F32)],
        compiler_params=_cp(("arbitrary",), 56),
        name="inproj",
    )(*xs, mod, g1, w_in_r, *sgu_p, *mla_p)


def _conv_tile(tile, p_ref, zp_ref, zn_ref, w_ref):
    r = tile % TPB
    bg = p_ref[:, :GW]
    z = p_ref[:, GW:]
    row = lax.broadcasted_iota(jnp.int32, (TM, GW), 0)
    has_prev = jnp.logical_and(r != 0, r != TPB - 1)
    has_next = r < LPB - 1
    prev_row = zp_ref[7:8, :] * has_prev.astype(F32)
    next_row = zn_ref[0:1, :] * has_next.astype(F32)
    zm = jnp.where(row == 0, prev_row, pltpu.roll(z, 1, 0))
    zp = jnp.where(row == TM - 1, next_row, pltpu.roll(z, TM - 1, 0))
    y = w_ref[0:1, :] * zm + w_ref[1:2, :] * z + w_ref[2:3, :] * zp
    return (bg * y).astype(BF16)


def _conv_specs(full):
    rb = TM // 8
    nrb = B * S2 // 8
    return [pl.BlockSpec((TM, 2 * GW), lambda i: (full(i), 0)),
            pl.BlockSpec((8, GW), lambda i: (jnp.maximum(full(i) * rb - 1, 0), 1)),
            pl.BlockSpec((8, GW), lambda i: (jnp.minimum((full(i) + 1) * rb, nrb - 1), 1)),
            pl.BlockSpec((3, GW), lambda i: (0, 0))]


_NT_DIMS = (((1,), (1,)), ((), ()))


AT_SUB = 4


def _attn_kernel(with_ctx, q_ref, kc_ref, v_ref, o_ref):
    def run(k0, nsub):
        def scores(it):
            r, hd = it
            cq = slice(hd * QK_PAD, (hd + 1) * QK_PAD)
            return lax.dot_general(q_ref[r * TM:(r + 1) * TM, cq], kc_ref[k0:, cq], _NT_DIMS,
                                   preferred_element_type=F32)

        def weights(s):
            return jnp.exp2(s - jnp.max(s, axis=-1, keepdims=True)).astype(BF16)

        def values(it, e):
            r, hd = it
            o = jnp.dot(e, v_ref[k0:, hd * QK_PAD:(hd + 1) * QK_PAD], preferred_element_type=F32)
            o_ref[r * TM:(r + 1) * TM, hd * 128:(hd + 1) * 128] = (
                (o[:, :128] / o[:, 128:129]).astype(BF16))

        items = [(r, hd) for r in range(nsub) for hd in range(MLA_HEADS)]
        s_next = scores(items[0])
        e_prev = None
        for n, it in enumerate(items):
            s_cur = s_next
            if n + 1 < len(items):
                s_next = scores(items[n + 1])
            e_cur = weights(s_cur)
            if e_prev is not None:
                values(items[n - 1], e_prev)
            e_prev = e_cur
        values(items[-1], e_prev)

    if with_ctx:
        t = pl.program_id(1)
        pl.when(t < LPB // AT_SUB)(lambda: run(0, AT_SUB))
        pl.when(t == LPB // AT_SUB)(lambda: run(SEQ, 1))
    else:
        run(0, AT_SUB)


def _attention(q3, kc3, v3, with_ctx):
    steps = LPB // AT_SUB + (1 if with_ctx else 0)
    qkw = MLA_HEADS * QK_PAD
    blk = AT_SUB * TM
    return pl.pallas_call(
        functools.partial(_attn_kernel, with_ctx),
        grid=(B, steps),
        in_specs=[pl.BlockSpec((None, blk, qkw), lambda b, t: (b, t, 0)),
                  pl.BlockSpec((None, S2, qkw), lambda b, t: (b, 0, 0)),
                  pl.BlockSpec((None, S2, qkw), lambda b, t: (b, 0, 0))],
        out_specs=pl.BlockSpec((None, blk, GW), lambda b, t: (b, t, 0)),
        out_shape=jax.ShapeDtypeStruct((B, S2 if with_ctx else SEQ, GW), BF16),
        compiler_params=_cp(("arbitrary", "arbitrary"), 48),
        name="attn",
    )(q3, kc3, v3)


def _s5_kernel(backward, *refs):
    if backward:
        (u_ref, w_ref, are_ref, aim_ref, c_ref, yf_ref, d_ref, wg_ref, bg_ref,
         o_ref, st_re, st_im, buf0, buf1) = refs
    else:
        u_ref, w_ref, are_ref, aim_ref, c_ref, o_ref, st_re, st_im, buf0, buf1 = refs
    bufs = (buf0, buf1)
    k = pl.program_id(0)
    hw = GW // 2
    nlt = S5_LT
    ht = S5_T // 2
    npc = nlt

    @pl.when(k == 0)
    def _():
        st_re[...] = jnp.zeros_like(st_re)
        st_im[...] = jnp.zeros_like(st_im)

    def rows_of(ref, hf):
        return jnp.concatenate([ref[b, hf * ht:(hf + 1) * ht, :] for b in range(B)], axis=0)

    def expand_pieces(hf):
        ub = rows_of(u_ref, hf).astype(BF16)
        buf = bufs[hf]

        def piece(h, n):
            bu = jnp.dot(ub[:, h * hw:(h + 1) * hw], w_ref[h, :, n * 256:(n + 1) * 256],
                         preferred_element_type=F32)
            for b in range(B):
                for cc in range(2):
                    buf[2 * n + cc, pl.ds(2 * b + h, ht, stride=S5_PITCH), :] = (
                        bu[b * ht:(b + 1) * ht, cc * 128:(cc + 1) * 128])

        return [functools.partial(piece, h, n) for h in range(2) for n in range(npc)]

    a_re = are_ref[...]
    a_im = aim_ref[...]
    state = [st_re[...], st_im[...]]

    def scan_pieces(hf, per):
        order = list(range(ht - 1, -1, -1)) if backward else list(range(ht))
        buf = bufs[hf]

        def piece(steps):
            sr, si = state
            for j in steps:
                r0 = S5_PITCH * j
                br = buf[0:nlt, r0:r0 + 8, :]
                bi = buf[nlt:2 * nlt, r0:r0 + 8, :]
                sr, si = a_re * sr - a_im * si + br, a_re * si + a_im * sr + bi
                buf[0:nlt, r0:r0 + 8, :] = sr
                buf[nlt:2 * nlt, r0:r0 + 8, :] = si
            state[0], state[1] = sr, si

        return [functools.partial(piece, order[i:i + per]) for i in range(0, ht, per)]

    acc = {}

    def readout_pieces(hf):
        buf = bufs[hf]

        def piece(h, n):
            s = jnp.concatenate(
                [jnp.concatenate([buf[2 * n + cc, pl.ds(2 * b + h, ht, stride=S5_PITCH), :]
                                  for cc in range(2)], axis=1) for b in range(B)], axis=0)
            part = jnp.dot(s.astype(BF16), c_ref[h, n * 256:(n + 1) * 256, :],
                           preferred_element_type=F32)
            acc[(hf, h)] = part if n == 0 else acc[(hf, h)] + part

        return [functools.partial(piece, h, n) for h in range(2) for n in range(npc)]

    def finish(hf):
        y = jnp.concatenate([acc[(hf, 0)], acc[(hf, 1)]], axis=1)
        if backward:
            y = y + rows_of(yf_ref, hf) + d_ref[...] * rows_of(u_ref, hf)
            g = jax.nn.gelu(y)
            z = jnp.dot(g.astype(BF16), wg_ref[...], preferred_element_type=F32) + bg_ref[...]
            y = (g * jax.nn.sigmoid(z)).astype(BF16)
        for b in range(B):
            o_ref[b, hf * ht:(hf + 1) * ht, :] = y[b * ht:(b + 1) * ht]

    def interleave(mxu, vpu):
        for i, m in enumerate(mxu):
            m()
            if i < len(vpu):
                vpu[i]()

    first, second = (1, 0) if backward else (0, 1)
    per = ht // (2 * npc)
    for m in expand_pieces(first):
        m()
    interleave(expand_pieces(second), scan_pieces(first, per))
    interleave(readout_pieces(first), scan_pieces(second, per))
    st_re[...] = state[0]
    st_im[...] = state[1]
    finish(first)
    for m in readout_pieces(second):
        m()
    finish(second)


def _s5_pass(u3, ops, li, glu):
    backward = glu is not None
    dr = 1 if backward else 0
    if backward:
        blk = lambda k: (0, S5_NCH - 1 - k, 0)
    else:
        blk = lambda k: (0, (k + SEQ // S5_T) % S5_NCH, 0)
    w_bd, a_re8, a_im8, c_bd = ops
    fixed = lambda k: (li, dr, 0, 0, 0)
    in_specs = [pl.BlockSpec((B, S5_T, GW), blk),
                pl.BlockSpec((None, None, 2, GW // 2, 2 * S5_HALF), fixed),
                pl.BlockSpec((None, None, S5_LT, 8, 128), fixed),
                pl.BlockSpec((None, None, S5_LT, 8, 128), fixed),
                pl.BlockSpec((None, None, 2, 2 * S5_HALF, GW // 2), fixed)]
    args = [u3, w_bd, a_re8, a_im8, c_bd]
    if backward:
        in_specs += [pl.BlockSpec((B, S5_T, GW), blk),
                     pl.BlockSpec((1, GW), lambda k: (0, 0)),
                     pl.BlockSpec((GW, GW), lambda k: (0, 0)),
                     pl.BlockSpec((1, GW), lambda k: (0, 0))]
        args += list(glu)
    return pl.pallas_call(
        functools.partial(_s5_kernel, backward),
        grid=(S5_NCH,),
        in_specs=in_specs,
        out_specs=pl.BlockSpec((B, S5_T, GW), blk),
        out_shape=jax.ShapeDtypeStruct((B, S2, GW), BF16 if backward else F32),
        scratch_shapes=[pltpu.VMEM((S5_LT, 8, 128), F32),
                        pltpu.VMEM((S5_LT, 8, 128), F32),
                        pltpu.VMEM((2 * S5_LT, S5_PITCH * S5_T // 2, 128), F32),
                        pltpu.VMEM((2 * S5_LT, S5_PITCH * S5_T // 2, 128), F32)],
        compiler_params=_cp(("arbitrary",), 48),
        name="s5_bwd_glu" if backward else "s5_fwd",
    )(*args)


OP_SUB = 2


def _outproj_kernel(nx, full, *refs):
    per = nx + 7
    cw_ref, w_ref, g2_ref, wr_ref, x1_ref, h2_ref, aff_ref = refs[OP_SUB * per:]
    tiles = [full(OP_SUB * pl.program_id(0) + s) for s in range(OP_SUB)]
    acts_all = []
    for s in range(OP_SUB):
        a0_ref, a1_ref, a2_ref, p_ref, zp_ref, zn_ref, _ = refs[s * per + nx:(s + 1) * per]
        acts_all.append((a0_ref[...], a1_ref[...], a2_ref[...],
                         _conv_tile(tiles[s], p_ref, zp_ref, zn_ref, cw_ref)))
    outs = []
    for s in range(OP_SUB):
        acts = acts_all[s]
        halves = []
        for c in (slice(0, D // 2), slice(D // 2, D)):
            o = jnp.dot(acts[0], w_ref[0:GW, c], preferred_element_type=F32)
            for m in range(1, 4):
                o = o + jnp.dot(acts[m], w_ref[m * GW:(m + 1) * GW, c], preferred_element_type=F32)
            halves.append(o)
        outs.append(jnp.concatenate(halves, axis=1))
    for s in range(OP_SUB):
        tr = refs[s * per:(s + 1) * per]
        mod_ref = tr[-1]
        tile = tiles[s]
        rows = slice(s * TM, (s + 1) * TM)
        x = _stream_rows(tr[:nx], tile)
        x1 = x + mod_ref[2:3, :] * outs[s]
        x1_ref[rows, :] = x1
        h2 = _modnorm(x1, g2_ref[...], mod_ref[3:4, :], mod_ref[4:5, :]).astype(BF16)
        h2_ref[rows, :] = h2
        lg2 = lax.dot_general(wr_ref[...], h2, _NT_DIMS, preferred_element_type=F32)
        lg = lg2[:N_EXP] + lg2[N_EXP:]
        e = jnp.exp(lg - jnp.max(lg, axis=0, keepdims=True))
        aff_ref[:, rows] = e / jnp.sum(e, axis=0, keepdims=True)


def _outproj(sgu_o, attn_o, ssm_o, p_conv, conv_w, w_out_bf, xs, mod, g2, wr2, li, with_ctx):
    ntiles = NT if with_ctx else NLT
    full = (lambda i: i) if with_ctx else _lat_tile
    in_specs, args = [], []
    for s in range(OP_SUB):
        out_t = lambda i, s=s: OP_SUB * i + s
        full_t = lambda i, s=s: full(OP_SUB * i + s)
        frow = lambda i, f=full_t: (f(i), 0)
        orow = lambda i, f=out_t: (f(i), 0)
        in_specs += _stream_specs(xs)(full_t) + [
            pl.BlockSpec((TM, GW), frow),
            pl.BlockSpec((TM, GW), orow),
            pl.BlockSpec((TM, GW), frow)] + _conv_specs(full_t)[:3] + [
            pl.BlockSpec((None, 6, D), lambda i, f=full_t: (_seg_of_tile(f(i)), 0, 0))]
        args += [*xs, sgu_o, attn_o, ssm_o, p_conv, p_conv, p_conv, mod]
    in_specs += [_conv_specs(full)[3],
                 pl.BlockSpec((None, D, D), lambda i: (li, 0, 0)),
                 pl.BlockSpec((1, D), lambda i: (0, 0)),
                 pl.BlockSpec((2 * N_EXP, D), lambda i: (0, 0))]
    args += [conv_w, w_out_bf, g2, wr2]
    blk = OP_SUB * TM
    return pl.pallas_call(
        functools.partial(_outproj_kernel, len(xs), full),
        grid=(ntiles // OP_SUB,),
        in_specs=in_specs,
        out_specs=[pl.BlockSpec((blk, D), lambda i: (i, 0)),
                   pl.BlockSpec((blk, D), lambda i: (i, 0)),
                   pl.BlockSpec((N_EXP, blk), lambda i: (0, i))],
        out_shape=[jax.ShapeDtypeStruct((ntiles * TM, D), F32),
                   jax.ShapeDtypeStruct((ntiles * TM, D), BF16),
                   jax.ShapeDtypeStruct((N_EXP, ntiles * TM), F32)],
        compiler_params=_cp(("arbitrary",), 56),
        name="outproj",
    )(*args)


def _one_hot_rows(rank_row, cap):
    slot = lax.broadcasted_iota(jnp.int32, (cap, rank_row.shape[-1]), 0).astype(F32)
    return jnp.where(rank_row == slot, 1.0, 0.0).astype(BF16)


def _cap_thresholds(bits_caps):
    thrs = [jnp.zeros((bits.shape[0], 1), jnp.int32) for bits, _ in bits_caps]
    for bit in range(30, -1, -1):
        for i, (bits, cap) in enumerate(bits_caps):
            cand = thrs[i] | (1 << bit)
            cnt = jnp.sum(jnp.where(bits >= cand, 1.0, 0.0), axis=1, keepdims=True)
            thrs[i] = jnp.where(cnt >= cap, cand, thrs[i])
    return thrs


def _select_one(a, bits, thr, tri, cap, rank_ref, w_ref):
    ne, n = a.shape
    gt = jnp.where(bits > thr, 1.0, 0.0)
    eq = jnp.where(bits == thr, 1.0, 0.0)
    need = cap - jnp.sum(gt, axis=1, keepdims=True)
    eq_before = jnp.dot(eq.astype(BF16), tri, preferred_element_type=F32) - eq
    sel = gt + eq * jnp.where(eq_before < need, 1.0, 0.0)
    rank = jnp.dot(sel.astype(BF16), tri, preferred_element_type=F32) - 1.0
    rank = jnp.where(sel > 0.5, rank, -1.0)
    slot = lax.broadcasted_iota(jnp.int32, (cap, n), 0).astype(F32)
    for e in range(ne):
        rank_ref[e] = rank[e:e + 1, :]
        hit = rank[e:e + 1, :] == slot
        w_ref[e * cap:(e + 1) * cap, :] = jnp.sum(jnp.where(hit, a[e:e + 1, :], 0.0),
                                                  axis=1, keepdims=True)


SEL_SUB = 4


def _select_kernel(with_ctx, a_ref, tri_ref, *out_refs):
    rows_b = S2 if with_ctx else SEQ
    sets = []
    for j in range(SEL_SUB):
        o = j * rows_b
        sets.append((a_ref[:, o:o + SEQ], CAP, tri_ref[...], out_refs[0].at[j], out_refs[1].at[j]))
        if with_ctx:
            sets.append((a_ref[:, o + SEQ:o + S2], CAP_C, tri_ref[:CTX, :CTX],
                         out_refs[2].at[j], out_refs[3].at[j]))
    bits = [pltpu.bitcast(s[0], jnp.int32) for s in sets]
    thrs = _cap_thresholds([(b, s[1]) for b, s in zip(bits, sets)])
    for i, (a, cap, tri, rank_ref, w_ref) in enumerate(sets):
        _select_one(a, bits[i], thrs[i], tri, cap, rank_ref, w_ref)


def _select(aff_t, tri, with_ctx):
    rows_b = S2 if with_ctx else SEQ
    out_specs = [pl.BlockSpec((SEL_SUB, N_EXP, 1, SEQ), lambda b: (b, 0, 0, 0)),
                 pl.BlockSpec((SEL_SUB, N_EXP * CAP, 1), lambda b: (b, 0, 0))]
    out_shape = [jax.ShapeDtypeStruct((B, N_EXP, 1, SEQ), F32),
                 jax.ShapeDtypeStruct((B, N_EXP * CAP, 1), F32)]
    if with_ctx:
        out_specs += [pl.BlockSpec((SEL_SUB, N_EXP, 1, CTX), lambda b: (b, 0, 0, 0)),
                      pl.BlockSpec((SEL_SUB, N_EXP * CAP_C, 1), lambda b: (b, 0, 0))]
        out_shape += [jax.ShapeDtypeStruct((B, N_EXP, 1, CTX), F32),
                      jax.ShapeDtypeStruct((B, N_EXP * CAP_C, 1), F32)]
    return pl.pallas_call(
        functools.partial(_select_kernel, with_ctx),
        grid=(B // SEL_SUB,),
        in_specs=[pl.BlockSpec((N_EXP, SEL_SUB * rows_b), lambda b: (0, b)),
                  pl.BlockSpec((SEQ, SEQ), lambda b: (0, 0))],
        out_specs=out_specs,
        out_shape=out_shape,
        compiler_params=_cp(("arbitrary",), 48),
        name="select",
    )(aff_t, tri)


def _gather_kernel(cap, rank_ref, h_ref, o_ref):
    p = jnp.concatenate([_one_hot_rows(rank_ref[j], cap) for j in range(rank_ref.shape[0])], axis=0)
    o_ref[...] = jnp.dot(p, h_ref[...], preferred_element_type=F32).astype(BF16)


def _gather(rank, h3, ctx_only):
    if ctx_only:
        cap, n, ne = CAP_C, CTX, N_EXP
        hmap = lambda b, r: (b, TPB - 1, 0)
    else:
        cap, n, ne = CAP, SEQ, 2
        hmap = lambda b, r: (b, 0, 0)
    return pl.pallas_call(
        functools.partial(_gather_kernel, cap),
        grid=(B, N_EXP // ne),
        in_specs=[pl.BlockSpec((None, ne, 1, n), lambda b, r: (b, r, 0, 0)),
                  pl.BlockSpec((None, n, D), hmap)],
        out_specs=pl.BlockSpec((None, ne * cap, D), lambda b, r: (b, r, 0)),
        out_shape=jax.ShapeDtypeStruct((B, N_EXP * cap, D), BF16),
        compiler_params=_cp(("arbitrary", "arbitrary"), 48),
        name="gather_ctx" if ctx_only else "gather",
    )(rank, h3)


def _ffn_kernel(with_ctx, *refs):
    if with_ctx:
        x_ref, xc_ref, wg_ref, wu_ref, wd_ref, ws_ref, wsc_ref, y_ref, yc_ref, acc = refs
    else:
        x_ref, wg_ref, wu_ref, wd_ref, ws_ref, y_ref, acc = refs
    f = pl.program_id(1)
    last = FF // FF_T - 1
    nl = B * CAP

    def step(kind):
        wg = wg_ref[...].astype(BF16)
        wu = wu_ref[...].astype(BF16)
        wd = wd_ref[...].astype(BF16)
        x = x_ref[...].reshape(nl, D)
        if with_ctx:
            x = jnp.concatenate([x, xc_ref[...].reshape(B * CAP_C, D)], axis=0)
        gate = jnp.dot(x, wg, preferred_element_type=F32)
        up = jnp.dot(x, wu, preferred_element_type=F32)
        hid = (gate * jax.nn.sigmoid(gate) * up).astype(BF16)
        part = jnp.dot(hid, wd, preferred_element_type=F32)
        if kind == "first":
            acc[...] = part
        elif kind == "mid":
            acc[...] += part
        else:
            y = acc[...] + part
            y_ref[...] = (y[:nl] * ws_ref[...].reshape(nl, 1)).astype(BF16).reshape(y_ref.shape)
            if with_ctx:
                yc = y[nl:] * wsc_ref[...].reshape(B * CAP_C, 1)
                yc_ref[...] = yc.astype(BF16).reshape(yc_ref.shape)

    pl.when(f == 0)(lambda: step("first"))
    pl.when(jnp.logical_and(f > 0, f < last))(lambda: step("mid"))
    pl.when(f == last)(lambda: step("last"))


def _ffn(xs, ws, xc, wsc, w_gate, w_up, w_down, li):
    with_ctx = xc is not None
    in_specs = [pl.BlockSpec((B, CAP, D), lambda e, f: (0, e, 0))]
    args = [xs]
    if with_ctx:
        in_specs.append(pl.BlockSpec((B, CAP_C, D), lambda e, f: (0, e, 0)))
        args.append(xc)
    in_specs += [pl.BlockSpec((None, None, D, FF_T), lambda e, f: (li, e, 0, f)),
                 pl.BlockSpec((None, None, D, FF_T), lambda e, f: (li, e, 0, f)),
                 pl.BlockSpec((None, None, FF_T, D), lambda e, f: (li, e, f, 0)),
                 pl.BlockSpec((B, CAP, 1), lambda e, f: (0, e, 0))]
    args += [w_gate, w_up, w_down, ws]
    out_specs = [pl.BlockSpec((B, CAP, D), lambda e, f: (0, e, 0))]
    out_shape = [jax.ShapeDtypeStruct((B, N_EXP * CAP, D), BF16)]
    scratch = [pltpu.VMEM((B * (CAP + CAP_C if with_ctx else CAP), D), F32)]
    if with_ctx:
        in_specs.append(pl.BlockSpec((B, CAP_C, 1), lambda e, f: (0, e, 0)))
        args.append(wsc)
        out_specs.append(pl.BlockSpec((B, CAP_C, D), lambda e, f: (0, e, 0)))
        out_shape.append(jax.ShapeDtypeStruct((B, N_EXP * CAP_C, D), BF16))
    return pl.pallas_call(
        functools.partial(_ffn_kernel, with_ctx),
        grid=(N_EXP, FF // FF_T),
        in_specs=in_specs,
        out_specs=out_specs,
        out_shape=out_shape,
        scratch_shapes=scratch,
        compiler_params=_cp(("arbitrary", "arbitrary"), 56),
        name="ffn",
    )(*args)


_TN_DIMS = (((0,), (0,)), ((), ()))


def _scatter_kernel(with_ctx, final, *refs):
    refs = list(refs)
    p_ref, y_ref = refs[:2]
    pc_ref, yc_ref = refs[2:4] if with_ctx else (None, None)
    rest = refs[4:] if with_ctx else refs[2:]
    x_ref, mod_ref = rest[:2]
    gf_ref = rest[2] if final else None
    o_ref = rest[-1]

    def finish(rank_ref, yr, cap):
        ng, ge = 4, N_EXP // 4

        def hot(j):
            return jnp.concatenate([_one_hot_rows(rank_ref[e], cap)
                                    for e in range(j * ge, (j + 1) * ge)], axis=0)

        upd = None
        p_next = hot(0)
        for j in range(ng):
            p_cur = p_next
            if j + 1 < ng:
                p_next = hot(j + 1)
            part = lax.dot_general(p_cur, yr[j * ge * cap:(j + 1) * ge * cap, :], _TN_DIMS,
                                   preferred_element_type=F32)
            upd = part if upd is None else upd + part
        x = x_ref[...] + mod_ref[5:6, :] * upd
        o_ref[...] = _rms(x, gf_ref[...]) if final else x

    if with_ctx:
        t = pl.program_id(1)
        pl.when(t < LPB)(lambda: finish(p_ref, y_ref, CAP))
        pl.when(t == LPB)(lambda: finish(pc_ref, yc_ref, CAP_C))
    else:
        finish(p_ref, y_ref, CAP)


def _scatter(rank, y, rank_c, yc, x3, mod, final_g):
    with_ctx = rank_c is not None
    final = final_g is not None
    tpb = TPB if with_ctx else LPB
    in_specs = [pl.BlockSpec((None, N_EXP, 1, TM), lambda b, t: (b, 0, 0, jnp.minimum(t, LPB - 1))),
                pl.BlockSpec((None, N_EXP * CAP, D), lambda b, t: (b, 0, 0))]
    args = [rank, y]
    if with_ctx:
        in_specs += [pl.BlockSpec((None, N_EXP, 1, CTX), lambda b, t: (b, 0, 0, 0)),
                     pl.BlockSpec((None, N_EXP * CAP_C, D), lambda b, t: (b, 0, 0))]
        args += [rank_c, yc]
    in_specs += [pl.BlockSpec((None, TM, D), lambda b, t: (b, t, 0)),
                 pl.BlockSpec((None, 6, D), lambda b, t: (jnp.where(t == LPB, B, b), 0, 0))]
    args += [x3, mod]
    if final:
        in_specs.append(pl.BlockSpec((1, D), lambda b, t: (0, 0)))
        args.append(final_g)
    return pl.pallas_call(
        functools.partial(_scatter_kernel, with_ctx, final),
        grid=(B, tpb),
        in_specs=in_specs,
        out_specs=pl.BlockSpec((None, TM, D), lambda b, t: (b, t, 0)),
        out_shape=jax.ShapeDtypeStruct((B, tpb * TM, D), F32),
        compiler_params=_cp(("arbitrary", "arbitrary"), 56),
        name="scatter",
    )(*args)


def _rope_tables():
    n_freq = QK_ROPE // 4
    grid_w = 64
    pos = jnp.arange(SEQ, dtype=F32)
    inv_freq = 10000.0 ** (-jnp.arange(n_freq, dtype=F32) / n_freq)
    ang_r = jnp.floor(pos / grid_w)[:, None] * inv_freq
    ang_c = (pos - grid_w * jnp.floor(pos / grid_w))[:, None] * inv_freq
    cr, sr, cc, sc = jnp.cos(ang_r), jnp.sin(ang_r), jnp.cos(ang_c), jnp.sin(ang_c)
    cos = jnp.concatenate([cr, cr, cc, cc], axis=1)
    sin = jnp.concatenate([-sr, sr, -sc, sc], axis=1)
    cos = jnp.concatenate([cos, jnp.ones((CTX, QK_ROPE), F32)], axis=0)
    sin = jnp.concatenate([sin, jnp.zeros((CTX, QK_ROPE), F32)], axis=0)
    z = jnp.zeros((S2, QK_ROPE), F32)
    qs = ATT_SCALE * LOG2E
    tq1 = jnp.concatenate([jnp.full((S2, QK_NOPE), qs, F32), cos * qs, z], axis=1)
    tq2 = jnp.concatenate([jnp.zeros((S2, QK_NOPE), F32), sin * qs, z], axis=1)
    tk1 = jnp.concatenate([cos, z], axis=1)
    tk2 = jnp.concatenate([sin, z], axis=1)
    return tq1, tq2, tk1, tk2


def _pair_swap(w):
    return jnp.concatenate([w[..., 16:32], w[..., 0:16], w[..., 48:64], w[..., 32:48]], axis=-1)


def _s5_place_kernel(bre_ref, bim_ref, cre_ref, cim_ref, w_ref, c_ref):
    hg = S5_G // 2
    w_ref[...] = jnp.zeros_like(w_ref)
    c_ref[...] = jnp.zeros_like(c_ref)
    for g in range(hg):
        r = slice(g * S5_CH, (g + 1) * S5_CH)
        for part, (b_ref, k_ref, sign) in enumerate(((bre_ref, cre_ref, 1.0), (bim_ref, cim_ref, -1.0))):
            s = slice(part * S5_HALF + g * S5_N, part * S5_HALF + (g + 1) * S5_N)
            w_ref[r, s] = b_ref[g].astype(BF16)
            c_ref[s, r] = (sign * k_ref[g]).astype(BF16)


def _s5_operators(a_re, a_im, log_dt, b_re, b_im, c_re, c_im):
    a = lax.complex(jnp.minimum(a_re.astype(F32), S5_MAX_RE), a_im.astype(F32))
    dt = jnp.exp(log_dt.astype(F32))[..., None]
    abar = jnp.exp(a * dt)
    bbar = ((abar - 1.0) / a)[..., None] * lax.complex(b_re.astype(F32), b_im.astype(F32))
    hg = S5_G // 2

    def per_group(m):
        return m.reshape(DEPTH, 2, 2, hg, m.shape[-2], m.shape[-1])

    bt = jnp.swapaxes(bbar, -1, -2)
    ct_re = jnp.swapaxes(c_re.astype(F32), -1, -2)
    ct_im = jnp.swapaxes(c_im.astype(F32), -1, -2)
    bspec = pl.BlockSpec((None, None, None, hg, S5_CH, S5_N), lambda l, z, h: (l, z, h, 0, 0, 0))
    cspec = pl.BlockSpec((None, None, None, hg, S5_N, S5_CH), lambda l, z, h: (l, z, h, 0, 0, 0))
    w_bd, c_bd = pl.pallas_call(
        _s5_place_kernel,
        grid=(DEPTH, 2, 2),
        in_specs=[bspec, bspec, cspec, cspec],
        out_specs=[pl.BlockSpec((None, None, None, GW // 2, 2 * S5_HALF), lambda l, z, h: (l, z, h, 0, 0)),
                   pl.BlockSpec((None, None, None, 2 * S5_HALF, GW // 2), lambda l, z, h: (l, z, h, 0, 0))],
        out_shape=[jax.ShapeDtypeStruct((DEPTH, 2, 2, GW // 2, 2 * S5_HALF), BF16),
                   jax.ShapeDtypeStruct((DEPTH, 2, 2, 2 * S5_HALF, GW // 2), BF16)],
        compiler_params=_cp(("arbitrary",) * 3, 32),
        name="s5_place",
    )(per_group(jnp.real(bt)), per_group(jnp.imag(bt)), per_group(ct_re), per_group(ct_im))

    def rows8(m):
        m = m.reshape(DEPTH, 2, 1, 2, S5_LT, 128)
        m = jnp.broadcast_to(m, (DEPTH, 2, B, 2, S5_LT, 128)).reshape(DEPTH, 2, 2 * B, S5_LT, 128)
        return jnp.transpose(m, (0, 1, 3, 2, 4))

    return w_bd, rows8(jnp.real(abar)), rows8(jnp.imag(abar)), c_bd


def kernel(x, c, ctx, c_ctx, norm1_g, norm2_g, w_ada, b_ada, w_in, w_out, sgu_norm_g, sgu_w,
           sgu_b, mla_q_norm_g, mla_w_uq, mla_kv_norm_g, mla_w_ukv, s5_a_re, s5_a_im, s5_log_dt,
           s5_b_re, s5_b_im, s5_c_re, s5_c_im, s5_d, s5_w_glu, s5_b_glu, conv_w, moe_w_router,
           moe_w_gate, moe_w_up, moe_w_down, final_norm_g):
    c8 = jnp.concatenate([c, c_ctx[None, :], jnp.zeros((3, D), F32)], axis=0)
    mod_all = _modulation(c8, w_ada, b_ada).reshape(DEPTH, 8, 6, D)
    rope_t = _rope_tables()
    tri = jnp.triu(jnp.ones((SEQ, SEQ), BF16))
    w_in_r = _winprep(jnp.transpose(w_in, (0, 2, 1)))
    w_out_bf = w_out.astype(BF16)
    s5_ops = _s5_operators(s5_a_re, s5_a_im, s5_log_dt, s5_b_re, s5_b_im, s5_c_re, s5_c_im)
    xs = (x.reshape(B * SEQ, D), ctx.reshape(B * CTX, D))

    for i in range(DEPTH):
        last = i == DEPTH - 1
        mod = mod_all[i]
        if i > 0:
            xs = (x3.reshape(B * S2, D),)

        wq = mla_w_uq[i].reshape(Q_LORA, MLA_HEADS, QK_NOPE + QK_ROPE)
        wq_r = wq[:, :, QK_NOPE:]
        wq_ext = jnp.concatenate([wq[:, :, :QK_NOPE], wq_r, _pair_swap(wq_r)], axis=2)
        wq_ext = wq_ext.reshape(Q_LORA, MLA_HEADS * QK_PAD).astype(BF16)
        wkv = mla_w_ukv[i].reshape(KV_LORA, MLA_HEADS, 2 * QK_NOPE)
        wkv_ext = jnp.concatenate([wkv[:, :, :QK_NOPE].reshape(KV_LORA, -1),
                                   wkv[:, :, QK_NOPE:].reshape(KV_LORA, -1)], axis=1).astype(BF16)
        sgu_p = (sgu_norm_g[i][None, :], sgu_w[i].astype(BF16),
                 jnp.repeat(jnp.swapaxes(sgu_b[i], 0, 1), 128, axis=1))
        mla_p = (mla_q_norm_g[i][None, :], mla_kv_norm_g[i][None, :], wq_ext, wkv_ext) + rope_t
        sgu_o, q, kc, v, p_s5, p_conv = _inproj(xs, mod, norm1_g[i][None, :], w_in_r, i,
                                                sgu_p, mla_p)

        qk3 = (B, S2, MLA_HEADS * QK_PAD)
        attn_o = _attention(q.reshape(qk3), kc.reshape(qk3), v.reshape(qk3), not last)
        attn_o = attn_o.reshape(-1, GW)

        u3 = p_s5.reshape(B, S2, GW)
        y_fwd = _s5_pass(u3, s5_ops, i, None)
        ssm_o = _s5_pass(u3, s5_ops, i,
                         (y_fwd, s5_d[i][None, :], s5_w_glu[i].astype(BF16), s5_b_glu[i][None, :]))
        ssm_o = ssm_o.reshape(B * S2, GW)

        wr_t = jnp.transpose(moe_w_router[i])
        wr_hi = wr_t.astype(BF16)
        wr2 = jnp.concatenate([wr_hi, (wr_t - wr_hi.astype(F32)).astype(BF16)], axis=0)
        x1, h2, aff_t = _outproj(sgu_o, attn_o, ssm_o, p_conv, conv_w[i], w_out_bf, xs, mod,
                                 norm2_g[i][None, :], wr2, i, not last)

        rows_b = SEQ if last else S2
        sel = _select(aff_t, tri, not last)
        h3 = h2.reshape(B, rows_b, D)
        xs = _gather(sel[0], h3, ctx_only=False)
        xc = _gather(sel[2], h3, ctx_only=True) if not last else None
        ys = _ffn(xs, sel[1], xc, sel[3] if not last else None, moe_w_gate, moe_w_up, moe_w_down, i)
        x1_3 = x1.reshape(B, rows_b, D)
        if last:
            x3 = _scatter(sel[0], ys[0], None, None, x1_3, mod, final_norm_g[None, :])
        else:
            x3 = _scatter(sel[0], ys[0], sel[2], ys[1], x1_3, mod, None)

    return x3
```

```python
import functools

import jax
import jax.numpy as jnp
from jax import lax
from jax.experimental import pallas as pl
from jax.experimental.pallas import tpu as pltpu

F32 = jnp.float32
BF16 = jnp.bfloat16

D = 2048
B = 4
SEQ = 2048
CTX = 256
S2 = SEQ + CTX
DEPTH = 2
GW = 512
EPS = 1e-6

TM = 256
TPB = S2 // TM
LPB = SEQ // TM
NT = B * TPB
NLT = B * LPB

SGU_HEADS = 4
CHUNK = 128
MLA_HEADS = 4
QK_NOPE = 128
QK_ROPE = 64
QK_PAD = 256
Q_LORA = 384
KV_LORA = 256
ATT_SCALE = (QK_NOPE + QK_ROPE) ** -0.5
LOG2E = 1.4426950408889634

S5_G = 32
S5_N = 64
S5_CH = 16
S5_MAX_RE = -1e-4
S5_T = 128
S5_NCH = S2 // S5_T
S5_HALF = (S5_G // 2) * S5_N
S5_LT = S5_HALF // 128
S5_PITCH = 9

N_EXP = 16
FF = D // 2
CAP = 2 * SEQ // N_EXP
CAP_C = 2 * CTX // N_EXP
FF_T = 256

COL_MLA = 2 * GW
MLA_W = Q_LORA + KV_LORA + 2 * QK_ROPE
COL_S5 = COL_MLA + MLA_W
COL_CONV = COL_S5 + GW
IN_W = COL_CONV + 3 * GW
MIB = 1024 * 1024


def _cp(sem, vmem_mb):
    return pltpu.CompilerParams(dimension_semantics=sem, vmem_limit_bytes=vmem_mb * MIB)


def _lat_tile(i):
    return (i // LPB) * TPB + i % LPB


def _seg_of_tile(t):
    return jnp.where(t % TPB == TPB - 1, B, t // TPB)


def _rms(x, g):
    return x * lax.rsqrt(jnp.mean(x * x, axis=-1, keepdims=True) + EPS) * g


def _modnorm(x, g, shift, scale):
    return _rms(x, g) * (1.0 + scale) + shift


def _mod_kernel(c_ref, w_ref, b_ref, o_ref):
    a = c_ref[...]
    a = a * jax.nn.sigmoid(a)
    o_ref[...] = jnp.dot(a.astype(BF16), w_ref[...].astype(BF16),
                         preferred_element_type=F32) + b_ref[...]


def _modulation(c8, w_ada, b_ada):
    tn = 1024
    return pl.pallas_call(
        _mod_kernel,
        grid=(DEPTH, 6 * D // tn),
        in_specs=[pl.BlockSpec((8, D), lambda l, j: (0, 0)),
                  pl.BlockSpec((None, D, tn), lambda l, j: (l, 0, j)),
                  pl.BlockSpec((None, 1, tn), lambda l, j: (l, 0, j))],
        out_specs=pl.BlockSpec((None, 8, tn), lambda l, j: (l, 0, j)),
        out_shape=jax.ShapeDtypeStruct((DEPTH, 8, 6 * D), F32),
        compiler_params=_cp(("arbitrary", "arbitrary"), 40),
        name="modulation",
    )(c8, w_ada, b_ada.reshape(DEPTH, 1, 6 * D))


KR0 = COL_MLA + Q_LORA + KV_LORA
IN_RAW = IN_W - QK_ROPE


WP_T = 256
WP_SWAP = KR0 // WP_T


def _winprep_kernel(prev_ref, cur_ref, o_ref):
    j = pl.program_id(1)
    keep = WP_T - QK_ROPE

    @pl.when(j < WP_SWAP)
    def _():
        o_ref[...] = cur_ref[...].astype(BF16)

    @pl.when(j == WP_SWAP)
    def _():
        cur = cur_ref[...]
        o_ref[:keep, :] = cur[:keep, :].astype(BF16)
        kr = cur[keep - QK_ROPE:keep, :]
        sw = jnp.concatenate([kr[16:32], kr[0:16], kr[48:64], kr[32:48]], axis=0)
        o_ref[keep:, :] = sw.astype(BF16)

    @pl.when(j > WP_SWAP)
    def _():
        o_ref[:QK_ROPE, :] = prev_ref[...].astype(BF16)
        o_ref[QK_ROPE:, :] = cur_ref[:keep, :].astype(BF16)


def _winprep(w_in_t):
    assert KR0 + QK_ROPE == (WP_SWAP + 1) * WP_T - QK_ROPE
    sub = WP_T // QK_ROPE
    return pl.pallas_call(
        _winprep_kernel,
        grid=(DEPTH, IN_W // WP_T),
        in_specs=[pl.BlockSpec((None, QK_ROPE, D), lambda l, j: (l, jnp.maximum(sub * j - 1, 0), 0)),
                  pl.BlockSpec((None, WP_T, D), lambda l, j: (l, j, 0))],
        out_specs=pl.BlockSpec((None, WP_T, D), lambda l, j: (l, j, 0)),
        out_shape=jax.ShapeDtypeStruct((DEPTH, IN_W, D), BF16),
        compiler_params=_cp(("arbitrary", "arbitrary"), 32),
        name="winprep",
    )(w_in_t, w_in_t)


def _stream_rows(refs, tile):
    if len(refs) == 1:
        return refs[0][...]
    return jnp.where(tile % TPB == TPB - 1, refs[1][...], refs[0][...])


def _stream_specs(xs):
    if len(xs) == 1:
        return lambda full: [pl.BlockSpec((TM, D), lambda i: (full(i), 0))]
    lat = lambda t: (t // TPB) * LPB + jnp.minimum(t % TPB, LPB - 1)
    return lambda full: [pl.BlockSpec((TM, D), lambda i: (lat(full(i)), 0)),
                         pl.BlockSpec((CTX, D), lambda i: (full(i) // TPB, 0))]


def _inproj_kernel(nx, *refs):
    (mod_ref, g_ref, w_ref, sg_ref, sw_ref, sb_ref,
     gq_ref, gkv_ref, wq_ref, wkv_ref, tq1_ref, tq2_ref, tk1_ref, tk2_ref,
     sgu_ref, q_ref, kc_ref, v_ref, s5_ref, conv_ref) = refs[nx:]
    x = _stream_rows(refs[:nx], pl.program_id(0))
    h = _modnorm(x, g_ref[...], mod_ref[0:1, :], mod_ref[1:2, :]).astype(BF16)

    def mm(a, b):
        return lax.dot_general(h, w_ref[a:b, :], _NT_DIMS, preferred_element_type=F32)

    p_a = mm(0, COL_MLA)
    pm = mm(COL_MLA, COL_S5)

    p = jax.nn.gelu(p_a)
    u = p[:, :GW]
    vb = _rms(p[:, GW:], sg_ref[...]).astype(BF16)
    for ck in range(TM // CHUNK):
        r = slice(ck * CHUNK, (ck + 1) * CHUNK)
        for hd in range(SGU_HEADS):
            c = slice(hd * 128, (hd + 1) * 128)
            m = jnp.dot(sw_ref[hd], vb[r, c], preferred_element_type=F32)
            sgu_ref[r, c] = (u[r, c] * (m + sb_ref[:, c])).astype(BF16)

    s5_ref[...] = mm(COL_S5, COL_CONV)
    conv_ref[:, :GW] = mm(COL_CONV, COL_CONV + GW)

    cq = _rms(pm[:, :Q_LORA], gq_ref[...]).astype(BF16)
    q = jnp.dot(cq, wq_ref[...], preferred_element_type=F32)
    tq1 = tq1_ref[...]
    tq2 = tq2_ref[...]
    for hd in range(MLA_HEADS):
        c = slice(hd * QK_PAD, (hd + 1) * QK_PAD)
        blk = q[:, c]
        q_ref[:, c] = (blk * tq1 + pltpu.roll(blk, QK_PAD - QK_ROPE, 1) * tq2).astype(BF16)
    ckv = _rms(pm[:, Q_LORA:Q_LORA + KV_LORA], gkv_ref[...]).astype(BF16)
    kv = jnp.dot(ckv, wkv_ref[...], preferred_element_type=F32)
    ones = jnp.ones((TM, 128), BF16)
    for hd in range(MLA_HEADS):
        v_ref[:, hd * QK_PAD:hd * QK_PAD + 128] = kv[:, GW + hd * 128:GW + (hd + 1) * 128].astype(BF16)
        v_ref[:, hd * QK_PAD + 128:(hd + 1) * QK_PAD] = ones
    kt = pm[:, Q_LORA + KV_LORA:]
    kr = (kt * tk1_ref[...] + pltpu.roll(kt, QK_ROPE, 1) * tk2_ref[...]).astype(BF16)
    for hd in range(MLA_HEADS):
        kc_ref[:, hd * QK_PAD:hd * QK_PAD + QK_NOPE] = kv[:, hd * 128:(hd + 1) * 128].astype(BF16)
        kc_ref[:, hd * QK_PAD + QK_NOPE:(hd + 1) * QK_PAD] = kr

    conv_ref[:, GW:] = mm(COL_CONV + GW, COL_CONV + 2 * GW) * mm(COL_CONV + 2 * GW, IN_W)


def _inproj(xs, mod, g1, w_in_r, li, sgu_p, mla_p):
    fix2 = lambda i: (0, 0)
    pos = lambda i: (i % TPB, 0)
    row = lambda i: (i, 0)
    qkw = MLA_HEADS * QK_PAD
    return pl.pallas_call(
        functools.partial(_inproj_kernel, len(xs)),
        grid=(NT,),
        in_specs=_stream_specs(xs)(lambda i: i) + [
                  pl.BlockSpec((None, 6, D), lambda i: (_seg_of_tile(i), 0, 0)),
                  pl.BlockSpec((1, D), fix2),
                  pl.BlockSpec((None, IN_W, D), lambda i: (li, 0, 0)),
                  pl.BlockSpec((1, GW), fix2),
                  pl.BlockSpec((SGU_HEADS, CHUNK, CHUNK), lambda i: (0, 0, 0)),
                  pl.BlockSpec((CHUNK, GW), fix2),
                  pl.BlockSpec((1, Q_LORA), fix2),
                  pl.BlockSpec((1, KV_LORA), fix2),
                  pl.BlockSpec((Q_LORA, qkw), fix2),
                  pl.BlockSpec((KV_LORA, 2 * GW), fix2),
                  pl.BlockSpec((TM, QK_PAD), pos),
                  pl.BlockSpec((TM, QK_PAD), pos),
                  pl.BlockSpec((TM, 128), pos),
                  pl.BlockSpec((TM, 128), pos)],
        out_specs=[pl.BlockSpec((TM, GW), row),
                   pl.BlockSpec((TM, qkw), row),
                   pl.BlockSpec((TM, qkw), row),
                   pl.BlockSpec((TM, qkw), row),
                   pl.BlockSpec((TM, GW), row),
                   pl.BlockSpec((TM, 2 * GW), row)],
        out_shape=[jax.ShapeDtypeStruct((B * S2, GW), BF16),
                   jax.ShapeDtypeStruct((B * S2, qkw), BF16),
                   jax.ShapeDtypeStruct((B * S2, qkw), BF16),
                   jax.ShapeDtypeStruct((B * S2, qkw), BF16),
                   jax.ShapeDtypeStruct((B * S2, GW), F32),
                   jax.ShapeDtypeStruct((B * S2, 2 * GW), F32)],
        compiler_params=_cp(("arbitrary",), 56),
        name="inproj",
    )(*xs, mod, g1, w_in_r, *sgu_p, *mla_p)


def _conv_tile(tile, p_ref, zp_ref, zn_ref, w_ref):
    r = tile % TPB
    bg = p_ref[:, :GW]
    z = p_ref[:, GW:]
    row = lax.broadcasted_iota(jnp.int32, (TM, GW), 0)
    has_prev = jnp.logical_and(r != 0, r != TPB - 1)
    has_next = r < LPB - 1
    prev_row = zp_ref[7:8, :] * has_prev.astype(F32)
    next_row = zn_ref[0:1, :] * has_next.astype(F32)
    zm = jnp.where(row == 0, prev_row, pltpu.roll(z, 1, 0))
    zp = jnp.where(row == TM - 1, next_row, pltpu.roll(z, TM - 1, 0))
    y = w_ref[0:1, :] * zm + w_ref[1:2, :] * z + w_ref[2:3, :] * zp
    return (bg * y).astype(BF16)


def _conv_specs(full):
    rb = TM // 8
    nrb = B * S2 // 8
    return [pl.BlockSpec((TM, 2 * GW), lambda i: (full(i), 0)),
            pl.BlockSpec((8, GW), lambda i: (jnp.maximum(full(i) * rb - 1, 0), 1)),
            pl.BlockSpec((8, GW), lambda i: (jnp.minimum((full(i) + 1) * rb, nrb - 1), 1)),
            pl.BlockSpec((3, GW), lambda i: (0, 0))]


_NT_DIMS = (((1,), (1,)), ((), ()))


AT_SUB = 4


def _attn_kernel(with_ctx, q_ref, kc_ref, v_ref, o_ref):
    def run(k0, nsub):
        def scores(it):
            r, hd = it
            cq = slice(hd * QK_PAD, (hd + 1) * QK_PAD)
            return lax.dot_general(q_ref[r * TM:(r + 1) * TM, cq], kc_ref[k0:, cq], _NT_DIMS,
                                   preferred_element_type=F32)

        def weights(s):
            return jnp.exp2(s - jnp.max(s, axis=-1, keepdims=True)).astype(BF16)

        def values(it, e):
            r, hd = it
            o = jnp.dot(e, v_ref[k0:, hd * QK_PAD:(hd + 1) * QK_PAD], preferred_element_type=F32)
            o_ref[r * TM:(r + 1) * TM, hd * 128:(hd + 1) * 128] = (
                (o[:, :128] / o[:, 128:129]).astype(BF16))

        items = [(r, hd) for r in range(nsub) for hd in range(MLA_HEADS)]
        s_next = scores(items[0])
        e_prev = None
        for n, it in enumerate(items):
            s_cur = s_next
            if n + 1 < len(items):
                s_next = scores(items[n + 1])
            e_cur = weights(s_cur)
            if e_prev is not None:
                values(items[n - 1], e_prev)
            e_prev = e_cur
        values(items[-1], e_prev)

    if with_ctx:
        t = pl.program_id(1)
        pl.when(t < LPB // AT_SUB)(lambda: run(0, AT_SUB))
        pl.when(t == LPB // AT_SUB)(lambda: run(SEQ, 1))
    else:
        run(0, AT_SUB)


def _attention(q3, kc3, v3, with_ctx):
    steps = LPB // AT_SUB + (1 if with_ctx else 0)
    qkw = MLA_HEADS * QK_PAD
    blk = AT_SUB * TM
    return pl.pallas_call(
        functools.partial(_attn_kernel, with_ctx),
        grid=(B, steps),
        in_specs=[pl.BlockSpec((None, blk, qkw), lambda b, t: (b, t, 0)),
                  pl.BlockSpec((None, S2, qkw), lambda b, t: (b, 0, 0)),
                  pl.BlockSpec((None, S2, qkw), lambda b, t: (b, 0, 0))],
        out_specs=pl.BlockSpec((None, blk, GW), lambda b, t: (b, t, 0)),
        out_shape=jax.ShapeDtypeStruct((B, S2 if with_ctx else SEQ, GW), BF16),
        compiler_params=_cp(("arbitrary", "arbitrary"), 48),
        name="attn",
    )(q3, kc3, v3)


def _s5_kernel(backward, *refs):
    if backward:
        (u_ref, w_ref, are_ref, aim_ref, c_ref, yf_ref, d_ref, wg_ref, bg_ref,
         o_ref, st_re, st_im, buf0, buf1) = refs
    else:
        u_ref, w_ref, are_ref, aim_ref, c_ref, o_ref, st_re, st_im, buf0, buf1 = refs
    bufs = (buf0, buf1)
    k = pl.program_id(0)
    hw = GW // 2
    nlt = S5_LT
    ht = S5_T // 2
    npc = nlt

    @pl.when(k == 0)
    def _():
        st_re[...] = jnp.zeros_like(st_re)
        st_im[...] = jnp.zeros_like(st_im)

    def rows_of(ref, hf):
        return jnp.concatenate([ref[b, hf * ht:(hf + 1) * ht, :] for b in range(B)], axis=0)

    def expand_pieces(hf):
        ub = rows_of(u_ref, hf).astype(BF16)
        buf = bufs[hf]

        def piece(h, n):
            bu = jnp.dot(ub[:, h * hw:(h + 1) * hw], w_ref[h, :, n * 256:(n + 1) * 256],
                         preferred_element_type=F32)
            for b in range(B):
                for cc in range(2):
                    buf[2 * n + cc, pl.ds(2 * b + h, ht, stride=S5_PITCH), :] = (
                        bu[b * ht:(b + 1) * ht, cc * 128:(cc + 1) * 128])

        return [functools.partial(piece, h, n) for h in range(2) for n in range(npc)]

    a_re = are_ref[...]
    a_im = aim_ref[...]
    state = [st_re[...], st_im[...]]

    def scan_pieces(hf, per):
        order = list(range(ht - 1, -1, -1)) if backward else list(range(ht))
        buf = bufs[hf]

        def piece(steps):
            sr, si = state
            for j in steps:
                r0 = S5_PITCH * j
                br = buf[0:nlt, r0:r0 + 8, :]
                bi = buf[nlt:2 * nlt, r0:r0 + 8, :]
                sr, si = a_re * sr - a_im * si + br, a_re * si + a_im * sr + bi
                buf[0:nlt, r0:r0 + 8, :] = sr
                buf[nlt:2 * nlt, r0:r0 + 8, :] = si
            state[0], state[1] = sr, si

        return [functools.partial(piece, order[i:i + per]) for i in range(0, ht, per)]

    acc = {}

    def readout_pieces(hf):
        buf = bufs[hf]

        def piece(h, n):
            s = jnp.concatenate(
                [jnp.concatenate([buf[2 * n + cc, pl.ds(2 * b + h, ht, stride=S5_PITCH), :]
                                  for cc in range(2)], axis=1) for b in range(B)], axis=0)
            part = jnp.dot(s.astype(BF16), c_ref[h, n * 256:(n + 1) * 256, :],
                           preferred_element_type=F32)
            acc[(hf, h)] = part if n == 0 else acc[(hf, h)] + part

        return [functools.partial(piece, h, n) for h in range(2) for n in range(npc)]

    def finish(hf):
        y = jnp.concatenate([acc[(hf, 0)], acc[(hf, 1)]], axis=1)
        if backward:
            y = y + rows_of(yf_ref, hf) + d_ref[...] * rows_of(u_ref, hf)
            g = jax.nn.gelu(y)
            z = jnp.dot(g.astype(BF16), wg_ref[...], preferred_element_type=F32) + bg_ref[...]
            y = (g * jax.nn.sigmoid(z)).astype(BF16)
        for b in range(B):
            o_ref[b, hf * ht:(hf + 1) * ht, :] = y[b * ht:(b + 1) * ht]

    def interleave(mxu, vpu):
        for i, m in enumerate(mxu):
            m()
            if i < len(vpu):
                vpu[i]()

    first, second = (1, 0) if backward else (0, 1)
    per = ht // (2 * npc)
    for m in expand_pieces(first):
        m()
    interleave(expand_pieces(second), scan_pieces(first, per))
    interleave(readout_pieces(first), scan_pieces(second, per))
    st_re[...] = state[0]
    st_im[...] = state[1]
    finish(first)
    for m in readout_pieces(second):
        m()
    finish(second)


def _s5_pass(u3, ops, li, glu):
    backward = glu is not None
    dr = 1 if backward else 0
    if backward:
        blk = lambda k: (0, S5_NCH - 1 - k, 0)
    else:
        blk = lambda k: (0, (k + SEQ // S5_T) % S5_NCH, 0)
    w_bd, a_re8, a_im8, c_bd = ops
    fixed = lambda k: (li, dr, 0, 0, 0)
    in_specs = [pl.BlockSpec((B, S5_T, GW), blk),
                pl.BlockSpec((None, None, 2, GW // 2, 2 * S5_HALF), fixed),
                pl.BlockSpec((None, None, S5_LT, 8, 128), fixed),
                pl.BlockSpec((None, None, S5_LT, 8, 128), fixed),
                pl.BlockSpec((None, None, 2, 2 * S5_HALF, GW // 2), fixed)]
    args = [u3, w_bd, a_re8, a_im8, c_bd]
    if backward:
        in_specs += [pl.BlockSpec((B, S5_T, GW), blk),
                     pl.BlockSpec((1, GW), lambda k: (0, 0)),
                     pl.BlockSpec((GW, GW), lambda k: (0, 0)),
                     pl.BlockSpec((1, GW), lambda k: (0, 0))]
        args += list(glu)
    return pl.pallas_call(
        functools.partial(_s5_kernel, backward),
        grid=(S5_NCH,),
        in_specs=in_specs,
        out_specs=pl.BlockSpec((B, S5_T, GW), blk),
        out_shape=jax.ShapeDtypeStruct((B, S2, GW), BF16 if backward else F32),
        scratch_shapes=[pltpu.VMEM((S5_LT, 8, 128), F32),
                        pltpu.VMEM((S5_LT, 8, 128), F32),
                        pltpu.VMEM((2 * S5_LT, S5_PITCH * S5_T // 2, 128), F32),
                        pltpu.VMEM((2 * S5_LT, S5_PITCH * S5_T // 2, 128), F32)],
        compiler_params=_cp(("arbitrary",), 48),
        name="s5_bwd_glu" if backward else "s5_fwd",
    )(*args)


OP_SUB = 2


def _outproj_kernel(nx, full, *refs):
    per = nx + 7
    cw_ref, w_ref, g2_ref, wr_ref, x1_ref, h2_ref, aff_ref = refs[OP_SUB * per:]
    tiles = [full(OP_SUB * pl.program_id(0) + s) for s in range(OP_SUB)]
    acts_all = []
    for s in range(OP_SUB):
        a0_ref, a1_ref, a2_ref, p_ref, zp_ref, zn_ref, _ = refs[s * per + nx:(s + 1) * per]
        acts_all.append((a0_ref[...], a1_ref[...], a2_ref[...],
                         _conv_tile(tiles[s], p_ref, zp_ref, zn_ref, cw_ref)))
    outs = []
    for s in range(OP_SUB):
        acts = acts_all[s]
        halves = []
        for c in (slice(0, D // 2), slice(D // 2, D)):
            o = jnp.dot(acts[0], w_ref[0:GW, c], preferred_element_type=F32)
            for m in range(1, 4):
                o = o + jnp.dot(acts[m], w_ref[m * GW:(m + 1) * GW, c], preferred_element_type=F32)
            halves.append(o)
        outs.append(jnp.concatenate(halves, axis=1))
    for s in range(OP_SUB):
        tr = refs[s * per:(s + 1) * per]
        mod_ref = tr[-1]
        tile = tiles[s]
        rows = slice(s * TM, (s + 1) * TM)
        x = _stream_rows(tr[:nx], tile)
        x1 = x + mod_ref[2:3, :] * outs[s]
        x1_ref[rows, :] = x1
        h2 = _modnorm(x1, g2_ref[...], mod_ref[3:4, :], mod_ref[4:5, :]).astype(BF16)
        h2_ref[rows, :] = h2
        lg2 = lax.dot_general(wr_ref[...], h2, _NT_DIMS, preferred_element_type=F32)
        lg = lg2[:N_EXP] + lg2[N_EXP:]
        e = jnp.exp(lg - jnp.max(lg, axis=0, keepdims=True))
        aff_ref[:, rows] = e / jnp.sum(e, axis=0, keepdims=True)


def _outproj(sgu_o, attn_o, ssm_o, p_conv, conv_w, w_out_bf, xs, mod, g2, wr2, li, with_ctx):
    ntiles = NT if with_ctx else NLT
    full = (lambda i: i) if with_ctx else _lat_tile
    in_specs, args = [], []
    for s in range(OP_SUB):
        out_t = lambda i, s=s: OP_SUB * i + s
        full_t = lambda i, s=s: full(OP_SUB * i + s)
        frow = lambda i, f=full_t: (f(i), 0)
        orow = lambda i, f=out_t: (f(i), 0)
        in_specs += _stream_specs(xs)(full_t) + [
            pl.BlockSpec((TM, GW), frow),
            pl.BlockSpec((TM, GW), orow),
            pl.BlockSpec((TM, GW), frow)] + _conv_specs(full_t)[:3] + [
            pl.BlockSpec((None, 6, D), lambda i, f=full_t: (_seg_of_tile(f(i)), 0, 0))]
        args += [*xs, sgu_o, attn_o, ssm_o, p_conv, p_conv, p_conv, mod]
    in_specs += [_conv_specs(full)[3],
                 pl.BlockSpec((None, D, D), lambda i: (li, 0, 0)),
                 pl.BlockSpec((1, D), lambda i: (0, 0)),
                 pl.BlockSpec((2 * N_EXP, D), lambda i: (0, 0))]
    args += [conv_w, w_out_bf, g2, wr2]
    blk = OP_SUB * TM
    return pl.pallas_call(
        functools.partial(_outproj_kernel, len(xs), full),
        grid=(ntiles // OP_SUB,),
        in_specs=in_specs,
        out_specs=[pl.BlockSpec((blk, D), lambda i: (i, 0)),
                   pl.BlockSpec((blk, D), lambda i: (i, 0)),
                   pl.BlockSpec((N_EXP, blk), lambda i: (0, i))],
        out_shape=[jax.ShapeDtypeStruct((ntiles * TM, D), F32),
                   jax.ShapeDtypeStruct((ntiles * TM, D), BF16),
                   jax.ShapeDtypeStruct((N_EXP, ntiles * TM), F32)],
        compiler_params=_cp(("arbitrary",), 56),
        name="outproj",
    )(*args)


def _one_hot_rows(rank_row, cap):
    slot = lax.broadcasted_iota(jnp.int32, (cap, rank_row.shape[-1]), 0).astype(F32)
    return jnp.where(rank_row == slot, 1.0, 0.0).astype(BF16)


def _cap_thresholds(bits_caps):
    thrs = [jnp.zeros((bits.shape[0], 1), jnp.int32) for bits, _ in bits_caps]
    for bit in range(30, -1, -1):
        for i, (bits, cap) in enumerate(bits_caps):
            cand = thrs[i] | (1 << bit)
            cnt = jnp.sum(jnp.where(bits >= cand, 1.0, 0.0), axis=1, keepdims=True)
            thrs[i] = jnp.where(cnt >= cap, cand, thrs[i])
    return thrs


def _select_one(a, bits, thr, tri, cap, rank_ref, w_ref):
    ne, n = a.shape
    gt = jnp.where(bits > thr, 1.0, 0.0)
    eq = jnp.where(bits == thr, 1.0, 0.0)
    need = cap - jnp.sum(gt, axis=1, keepdims=True)
    eq_before = jnp.dot(eq.astype(BF16), tri, preferred_element_type=F32) - eq
    sel = gt + eq * jnp.where(eq_before < need, 1.0, 0.0)
    rank = jnp.dot(sel.astype(BF16), tri, preferred_element_type=F32) - 1.0
    rank = jnp.where(sel > 0.5, rank, -1.0)
    slot = lax.broadcasted_iota(jnp.int32, (cap, n), 0).astype(F32)
    for e in range(ne):
        rank_ref[e] = rank[e:e + 1, :]
        hit = rank[e:e + 1, :] == slot
        w_ref[e * cap:(e + 1) * cap, :] = jnp.sum(jnp.where(hit, a[e:e + 1, :], 0.0),
                                                  axis=1, keepdims=True)


SEL_SUB = 4


def _select_kernel(with_ctx, a_ref, tri_ref, *out_refs):
    rows_b = S2 if with_ctx else SEQ
    sets = []
    for j in range(SEL_SUB):
        o = j * rows_b
        sets.append((a_ref[:, o:o + SEQ], CAP, tri_ref[...], out_refs[0].at[j], out_refs[1].at[j]))
        if with_ctx:
            sets.append((a_ref[:, o + SEQ:o + S2], CAP_C, tri_ref[:CTX, :CTX],
                         out_refs[2].at[j], out_refs[3].at[j]))
    bits = [pltpu.bitcast(s[0], jnp.int32) for s in sets]
    thrs = _cap_thresholds([(b, s[1]) for b, s in zip(bits, sets)])
    for i, (a, cap, tri, rank_ref, w_ref) in enumerate(sets):
        _select_one(a, bits[i], thrs[i], tri, cap, rank_ref, w_ref)


def _select(aff_t, tri, with_ctx):
    rows_b = S2 if with_ctx else SEQ
    out_specs = [pl.BlockSpec((SEL_SUB, N_EXP, 1, SEQ), lambda b: (b, 0, 0, 0)),
                 pl.BlockSpec((SEL_SUB, N_EXP * CAP, 1), lambda b: (b, 0, 0))]
    out_shape = [jax.ShapeDtypeStruct((B, N_EXP, 1, SEQ), F32),
                 jax.ShapeDtypeStruct((B, N_EXP * CAP, 1), F32)]
    if with_ctx:
        out_specs += [pl.BlockSpec((SEL_SUB, N_EXP, 1, CTX), lambda b: (b, 0, 0, 0)),
                      pl.BlockSpec((SEL_SUB, N_EXP * CAP_C, 1), lambda b: (b, 0, 0))]
        out_shape += [jax.ShapeDtypeStruct((B, N_EXP, 1, CTX), F32),
                      jax.ShapeDtypeStruct((B, N_EXP * CAP_C, 1), F32)]
    return pl.pallas_call(
        functools.partial(_select_kernel, with_ctx),
        grid=(B // SEL_SUB,),
        in_specs=[pl.BlockSpec((N_EXP, SEL_SUB * rows_b), lambda b: (0, b)),
                  pl.BlockSpec((SEQ, SEQ), lambda b: (0, 0))],
        out_specs=out_specs,
        out_shape=out_shape,
        compiler_params=_cp(("arbitrary",), 48),
        name="select",
    )(aff_t, tri)


def _gather_kernel(cap, rank_ref, h_ref, o_ref):
    p = jnp.concatenate([_one_hot_rows(rank_ref[j], cap) for j in range(rank_ref.shape[0])], axis=0)
    o_ref[...] = jnp.dot(p, h_ref[...], preferred_element_type=F32).astype(BF16)


def _gather(rank, h3, ctx_only):
    if ctx_only:
        cap, n, ne = CAP_C, CTX, N_EXP
        hmap = lambda b, r: (b, TPB - 1, 0)
    else:
        cap, n, ne = CAP, SEQ, 2
        hmap = lambda b, r: (b, 0, 0)
    return pl.pallas_call(
        functools.partial(_gather_kernel, cap),
        grid=(B, N_EXP // ne),
        in_specs=[pl.BlockSpec((None, ne, 1, n), lambda b, r: (b, r, 0, 0)),
                  pl.BlockSpec((None, n, D), hmap)],
        out_specs=pl.BlockSpec((None, ne * cap, D), lambda b, r: (b, r, 0)),
        out_shape=jax.ShapeDtypeStruct((B, N_EXP * cap, D), BF16),
        compiler_params=_cp(("arbitrary", "arbitrary"), 48),
        name="gather_ctx" if ctx_only else "gather",
    )(rank, h3)


def _ffn_kernel(with_ctx, *refs):
    if with_ctx:
        (x_hbm, xc_ref, wg_ref, wu_ref, wd_ref, ws_ref, wsc_ref, y_ref, yc_ref,
         acc, xbuf, xsem) = refs
    else:
        x_hbm, wg_ref, wu_ref, wd_ref, ws_ref, y_ref, acc, xbuf, xsem = refs
    e = pl.program_id(0)
    f = pl.program_id(1)
    last = FF // FF_T - 1
    nl = B * CAP

    def x_copy(ei):
        slot = ei % 2
        src = x_hbm.at[:, pl.ds(pl.multiple_of(ei * CAP, CAP), CAP), :]
        return pltpu.make_async_copy(src, xbuf.at[slot], xsem.at[slot])

    @pl.when(jnp.logical_and(e == 0, f == 0))
    def _():
        x_copy(e).start()

    @pl.when(f == 0)
    def _():
        x_copy(e).wait()

    @pl.when(jnp.logical_and(f == 1, e + 1 < N_EXP))
    def _():
        x_copy(e + 1).start()

    def step(kind):
        wg = wg_ref[...].astype(BF16)
        wu = wu_ref[...].astype(BF16)
        wd = wd_ref[...].astype(BF16)
        x = xbuf[e % 2].reshape(nl, D)
        if with_ctx:
            x = jnp.concatenate([x, xc_ref[...].reshape(B * CAP_C, D)], axis=0)
        gate = jnp.dot(x, wg, preferred_element_type=F32)
        up = jnp.dot(x, wu, preferred_element_type=F32)
        hid = (gate * jax.nn.sigmoid(gate) * up).astype(BF16)
        part = jnp.dot(hid, wd, preferred_element_type=F32)
        if kind == "first":
            acc[...] = part
        elif kind == "mid":
            acc[...] += part
        else:
            y = acc[...] + part
            y_ref[...] = (y[:nl] * ws_ref[...].reshape(nl, 1)).astype(BF16).reshape(y_ref.shape)
            if with_ctx:
                yc = y[nl:] * wsc_ref[...].reshape(B * CAP_C, 1)
                yc_ref[...] = yc.astype(BF16).reshape(yc_ref.shape)

    pl.when(f == 0)(lambda: step("first"))
    pl.when(jnp.logical_and(f > 0, f < last))(lambda: step("mid"))
    pl.when(f == last)(lambda: step("last"))


def _ffn(xs, ws, xc, wsc, w_gate, w_up, w_down, li):
    with_ctx = xc is not None
    in_specs = [pl.BlockSpec(memory_space=pl.ANY)]
    args = [xs]
    if with_ctx:
        in_specs.append(pl.BlockSpec((B, CAP_C, D), lambda e, f: (0, e, 0)))
        args.append(xc)
    in_specs += [pl.BlockSpec((None, None, D, FF_T), lambda e, f: (li, e, 0, f)),
                 pl.BlockSpec((None, None, D, FF_T), lambda e, f: (li, e, 0, f)),
                 pl.BlockSpec((None, None, FF_T, D), lambda e, f: (li, e, f, 0)),
                 pl.BlockSpec((B, CAP, 1), lambda e, f: (0, e, 0))]
    args += [w_gate, w_up, w_down, ws]
    out_specs = [pl.BlockSpec((B, CAP, D), lambda e, f: (0, e, 0))]
    out_shape = [jax.ShapeDtypeStruct((B, N_EXP * CAP, D), BF16)]
    scratch = [pltpu.VMEM((B * (CAP + CAP_C if with_ctx else CAP), D), F32)]
    if with_ctx:
        in_specs.append(pl.BlockSpec((B, CAP_C, 1), lambda e, f: (0, e, 0)))
        args.append(wsc)
        out_specs.append(pl.BlockSpec((B, CAP_C, D), lambda e, f: (0, e, 0)))
        out_shape.append(jax.ShapeDtypeStruct((B, N_EXP * CAP_C, D), BF16))
    scratch += [pltpu.VMEM((2, B, CAP, D), BF16), pltpu.SemaphoreType.DMA((2,))]
    return pl.pallas_call(
        functools.partial(_ffn_kernel, with_ctx),
        grid=(N_EXP, FF // FF_T),
        in_specs=in_specs,
        out_specs=out_specs,
        out_shape=out_shape,
        scratch_shapes=scratch,
        compiler_params=_cp(("arbitrary", "arbitrary"), 56),
        name="ffn",
    )(*args)


_TN_DIMS = (((0,), (0,)), ((), ()))


def _scatter_kernel(with_ctx, final, *refs):
    refs = list(refs)
    p_ref, y_ref = refs[:2]
    pc_ref, yc_ref = refs[2:4] if with_ctx else (None, None)
    rest = refs[4:] if with_ctx else refs[2:]
    x_ref, mod_ref = rest[:2]
    gf_ref = rest[2] if final else None
    o_ref = rest[-1]

    def finish(rank_ref, yr, cap):
        ng, ge = 4, N_EXP // 4

        def hot(j):
            return jnp.concatenate([_one_hot_rows(rank_ref[e], cap)
                                    for e in range(j * ge, (j + 1) * ge)], axis=0)

        upd = None
        p_next = hot(0)
        for j in range(ng):
            p_cur = p_next
            if j + 1 < ng:
                p_next = hot(j + 1)
            part = lax.dot_general(p_cur, yr[j * ge * cap:(j + 1) * ge * cap, :], _TN_DIMS,
                                   preferred_element_type=F32)
            upd = part if upd is None else upd + part
        x = x_ref[...] + mod_ref[5:6, :] * upd
        o_ref[...] = _rms(x, gf_ref[...]) if final else x

    if with_ctx:
        t = pl.program_id(1)
        pl.when(t < LPB)(lambda: finish(p_ref, y_ref, CAP))
        pl.when(t == LPB)(lambda: finish(pc_ref, yc_ref, CAP_C))
    else:
        finish(p_ref, y_ref, CAP)


def _scatter(rank, y, rank_c, yc, x3, mod, final_g):
    with_ctx = rank_c is not None
    final = final_g is not None
    tpb = TPB if with_ctx else LPB
    in_specs = [pl.BlockSpec((None, N_EXP, 1, TM), lambda b, t: (b, 0, 0, jnp.minimum(t, LPB - 1))),
                pl.BlockSpec((None, N_EXP * CAP, D), lambda b, t: (b, 0, 0))]
    args = [rank, y]
    if with_ctx:
        in_specs += [pl.BlockSpec((None, N_EXP, 1, CTX), lambda b, t: (b, 0, 0, 0)),
                     pl.BlockSpec((None, N_EXP * CAP_C, D), lambda b, t: (b, 0, 0))]
        args += [rank_c, yc]
    in_specs += [pl.BlockSpec((None, TM, D), lambda b, t: (b, t, 0)),
                 pl.BlockSpec((None, 6, D), lambda b, t: (jnp.where(t == LPB, B, b), 0, 0))]
    args += [x3, mod]
    if final:
        in_specs.append(pl.BlockSpec((1, D), lambda b, t: (0, 0)))
        args.append(final_g)
    return pl.pallas_call(
        functools.partial(_scatter_kernel, with_ctx, final),
        grid=(B, tpb),
        in_specs=in_specs,
        out_specs=pl.BlockSpec((None, TM, D), lambda b, t: (b, t, 0)),
        out_shape=jax.ShapeDtypeStruct((B, tpb * TM, D), F32),
        compiler_params=_cp(("arbitrary", "arbitrary"), 56),
        name="scatter",
    )(*args)


def _rope_tables():
    n_freq = QK_ROPE // 4
    grid_w = 64
    pos = jnp.arange(SEQ, dtype=F32)
    inv_freq = 10000.0 ** (-jnp.arange(n_freq, dtype=F32) / n_freq)
    ang_r = jnp.floor(pos / grid_w)[:, None] * inv_freq
    ang_c = (pos - grid_w * jnp.floor(pos / grid_w))[:, None] * inv_freq
    cr, sr, cc, sc = jnp.cos(ang_r), jnp.sin(ang_r), jnp.cos(ang_c), jnp.sin(ang_c)
    cos = jnp.concatenate([cr, cr, cc, cc], axis=1)
    sin = jnp.concatenate([-sr, sr, -sc, sc], axis=1)
    cos = jnp.concatenate([cos, jnp.ones((CTX, QK_ROPE), F32)], axis=0)
    sin = jnp.concatenate([sin, jnp.zeros((CTX, QK_ROPE), F32)], axis=0)
    z = jnp.zeros((S2, QK_ROPE), F32)
    qs = ATT_SCALE * LOG2E
    tq1 = jnp.concatenate([jnp.full((S2, QK_NOPE), qs, F32), cos * qs, z], axis=1)
    tq2 = jnp.concatenate([jnp.zeros((S2, QK_NOPE), F32), sin * qs, z], axis=1)
    tk1 = jnp.concatenate([cos, z], axis=1)
    tk2 = jnp.concatenate([sin, z], axis=1)
    return tq1, tq2, tk1, tk2


def _pair_swap(w):
    return jnp.concatenate([w[..., 16:32], w[..., 0:16], w[..., 48:64], w[..., 32:48]], axis=-1)


def _s5_place_kernel(bre_ref, bim_ref, cre_ref, cim_ref, w_ref, c_ref):
    hg = S5_G // 2
    w_ref[...] = jnp.zeros_like(w_ref)
    c_ref[...] = jnp.zeros_like(c_ref)
    for g in range(hg):
        r = slice(g * S5_CH, (g + 1) * S5_CH)
        for part, (b_ref, k_ref, sign) in enumerate(((bre_ref, cre_ref, 1.0), (bim_ref, cim_ref, -1.0))):
            s = slice(part * S5_HALF + g * S5_N, part * S5_HALF + (g + 1) * S5_N)
            w_ref[r, s] = b_ref[g].astype(BF16)
            c_ref[s, r] = (sign * k_ref[g]).astype(BF16)


def _s5_operators(a_re, a_im, log_dt, b_re, b_im, c_re, c_im):
    a = lax.complex(jnp.minimum(a_re.astype(F32), S5_MAX_RE), a_im.astype(F32))
    dt = jnp.exp(log_dt.astype(F32))[..., None]
    abar = jnp.exp(a * dt)
    bbar = ((abar - 1.0) / a)[..., None] * lax.complex(b_re.astype(F32), b_im.astype(F32))
    hg = S5_G // 2

    def per_group(m):
        return m.reshape(DEPTH, 2, 2, hg, m.shape[-2], m.shape[-1])

    bt = jnp.swapaxes(bbar, -1, -2)
    ct_re = jnp.swapaxes(c_re.astype(F32), -1, -2)
    ct_im = jnp.swapaxes(c_im.astype(F32), -1, -2)
    bspec = pl.BlockSpec((None, None, None, hg, S5_CH, S5_N), lambda l, z, h: (l, z, h, 0, 0, 0))
    cspec = pl.BlockSpec((None, None, None, hg, S5_N, S5_CH), lambda l, z, h: (l, z, h, 0, 0, 0))
    w_bd, c_bd = pl.pallas_call(
        _s5_place_kernel,
        grid=(DEPTH, 2, 2),
        in_specs=[bspec, bspec, cspec, cspec],
        out_specs=[pl.BlockSpec((None, None, None, GW // 2, 2 * S5_HALF), lambda l, z, h: (l, z, h, 0, 0)),
                   pl.BlockSpec((None, None, None, 2 * S5_HALF, GW // 2), lambda l, z, h: (l, z, h, 0, 0))],
        out_shape=[jax.ShapeDtypeStruct((DEPTH, 2, 2, GW // 2, 2 * S5_HALF), BF16),
                   jax.ShapeDtypeStruct((DEPTH, 2, 2, 2 * S5_HALF, GW // 2), BF16)],
        compiler_params=_cp(("arbitrary",) * 3, 32),
        name="s5_place",
    )(per_group(jnp.real(bt)), per_group(jnp.imag(bt)), per_group(ct_re), per_group(ct_im))

    def rows8(m):
        m = m.reshape(DEPTH, 2, 1, 2, S5_LT, 128)
        m = jnp.broadcast_to(m, (DEPTH, 2, B, 2, S5_LT, 128)).reshape(DEPTH, 2, 2 * B, S5_LT, 128)
        return jnp.transpose(m, (0, 1, 3, 2, 4))

    return w_bd, rows8(jnp.real(abar)), rows8(jnp.imag(abar)), c_bd


def kernel(x, c, ctx, c_ctx, norm1_g, norm2_g, w_ada, b_ada, w_in, w_out, sgu_norm_g, sgu_w,
           sgu_b, mla_q_norm_g, mla_w_uq, mla_kv_norm_g, mla_w_ukv, s5_a_re, s5_a_im, s5_log_dt,
           s5_b_re, s5_b_im, s5_c_re, s5_c_im, s5_d, s5_w_glu, s5_b_glu, conv_w, moe_w_router,
           moe_w_gate, moe_w_up, moe_w_down, final_norm_g):
    c8 = jnp.concatenate([c, c_ctx[None, :], jnp.zeros((3, D), F32)], axis=0)
    mod_all = _modulation(c8, w_ada, b_ada).reshape(DEPTH, 8, 6, D)
    rope_t = _rope_tables()
    tri = jnp.triu(jnp.ones((SEQ, SEQ), BF16))
    w_in_r = _winprep(jnp.transpose(w_in, (0, 2, 1)))
    w_out_bf = w_out.astype(BF16)
    s5_ops = _s5_operators(s5_a_re, s5_a_im, s5_log_dt, s5_b_re, s5_b_im, s5_c_re, s5_c_im)
    xs = (x.reshape(B * SEQ, D), ctx.reshape(B * CTX, D))

    for i in range(DEPTH):
        last = i == DEPTH - 1
        mod = mod_all[i]
        if i > 0:
            xs = (x3.reshape(B * S2, D),)

        wq = mla_w_uq[i].reshape(Q_LORA, MLA_HEADS, QK_NOPE + QK_ROPE)
        wq_r = wq[:, :, QK_NOPE:]
        wq_ext = jnp.concatenate([wq[:, :, :QK_NOPE], wq_r, _pair_swap(wq_r)], axis=2)
        wq_ext = wq_ext.reshape(Q_LORA, MLA_HEADS * QK_PAD).astype(BF16)
        wkv = mla_w_ukv[i].reshape(KV_LORA, MLA_HEADS, 2 * QK_NOPE)
        wkv_ext = jnp.concatenate([wkv[:, :, :QK_NOPE].reshape(KV_LORA, -1),
                                   wkv[:, :, QK_NOPE:].reshape(KV_LORA, -1)], axis=1).astype(BF16)
        sgu_p = (sgu_norm_g[i][None, :], sgu_w[i].astype(BF16),
                 jnp.repeat(jnp.swapaxes(sgu_b[i], 0, 1), 128, axis=1))
        mla_p = (mla_q_norm_g[i][None, :], mla_kv_norm_g[i][None, :], wq_ext, wkv_ext) + rope_t
        sgu_o, q, kc, v, p_s5, p_conv = _inproj(xs, mod, norm1_g[i][None, :], w_in_r, i,
                                                sgu_p, mla_p)

        qk3 = (B, S2, MLA_HEADS * QK_PAD)
        attn_o = _attention(q.reshape(qk3), kc.reshape(qk3), v.reshape(qk3), not last)
        attn_o = attn_o.reshape(-1, GW)

        u3 = p_s5.reshape(B, S2, GW)
        y_fwd = _s5_pass(u3, s5_ops, i, None)
        ssm_o = _s5_pass(u3, s5_ops, i,
                         (y_fwd, s5_d[i][None, :], s5_w_glu[i].astype(BF16), s5_b_glu[i][None, :]))
        ssm_o = ssm_o.reshape(B * S2, GW)

        wr_t = jnp.transpose(moe_w_router[i])
        wr_hi = wr_t.astype(BF16)
        wr2 = jnp.concatenate([wr_hi, (wr_t - wr_hi.astype(F32)).astype(BF16)], axis=0)
        x1, h2, aff_t = _outproj(sgu_o, attn_o, ssm_o, p_conv, conv_w[i], w_out_bf, xs, mod,
                                 norm2_g[i][None, :], wr2, i, not last)

        rows_b = SEQ if last else S2
        sel = _select(aff_t, tri, not last)
        h3 = h2.reshape(B, rows_b, D)
        xs = _gather(sel[0], h3, ctx_only=False)
        xc = _gather(sel[2], h3, ctx_only=True) if not last else None
        ys = _ffn(xs, sel[1], xc, sel[3] if not last else None, moe_w_gate, moe_w_up, moe_w_down, i)
        x1_3 = x1.reshape(B, rows_b, D)
        if last:
            x3 = _scatter(sel[0], ys[0], None, None, x1_3, mod, final_norm_g[None, :])
        else:
            x3 = _scatter(sel[0], ys[0], sel[2], ys[1], x1_3, mod, None)

    return x3
```

```python
import functools

import jax
import jax.numpy as jnp
from jax import lax
from jax.experimental import pallas as pl
from jax.experimental.pallas import tpu as pltpu

F32 = jnp.float32
BF16 = jnp.bfloat16

D = 2048
B = 4
SEQ = 2048
CTX = 256
S2 = SEQ + CTX
DEPTH = 2
GW = 512
EPS = 1e-6

TM = 256
TPB = S2 // TM
LPB = SEQ // TM
NT = B * TPB
NLT = B * LPB

SGU_HEADS = 4
CHUNK = 128
MLA_HEADS = 4
QK_NOPE = 128
QK_ROPE = 64
QK_PAD = 256
Q_LORA = 384
KV_LORA = 256
ATT_SCALE = (QK_NOPE + QK_ROPE) ** -0.5
LOG2E = 1.4426950408889634

S5_G = 32
S5_N = 64
S5_CH = 16
S5_MAX_RE = -1e-4
S5_T = 128
S5_NCH = S2 // S5_T
S5_HALF = (S5_G // 2) * S5_N
S5_LT = S5_HALF // 128
S5_PITCH = 9

N_EXP = 16
FF = D // 2
CAP = 2 * SEQ // N_EXP
CAP_C = 2 * CTX // N_EXP
FF_T = 256

COL_MLA = 2 * GW
MLA_W = Q_LORA + KV_LORA + 2 * QK_ROPE
COL_S5 = COL_MLA + MLA_W
COL_CONV = COL_S5 + GW
IN_W = COL_CONV + 3 * GW
MIB = 1024 * 1024


def _cp(sem, vmem_mb):
    return pltpu.CompilerParams(dimension_semantics=sem, vmem_limit_bytes=vmem_mb * MIB)


def _lat_tile(i):
    return (i // LPB) * TPB + i % LPB


def _seg_of_tile(t):
    return jnp.where(t % TPB == TPB - 1, B, t // TPB)


def _rms(x, g):
    return x * lax.rsqrt(jnp.mean(x * x, axis=-1, keepdims=True) + EPS) * g


def _modnorm(x, g, shift, scale):
    return _rms(x, g) * (1.0 + scale) + shift


def _mod_kernel(c_ref, w_ref, b_ref, o_ref):
    a = c_ref[...]
    a = a * jax.nn.sigmoid(a)
    o_ref[...] = jnp.dot(a.astype(BF16), w_ref[...].astype(BF16),
                         preferred_element_type=F32) + b_ref[...]


def _modulation(c8, w_ada, b_ada):
    tn = 1024
    return pl.pallas_call(
        _mod_kernel,
        grid=(DEPTH, 6 * D // tn),
        in_specs=[pl.BlockSpec((8, D), lambda l, j: (0, 0)),
                  pl.BlockSpec((None, D, tn), lambda l, j: (l, 0, j)),
                  pl.BlockSpec((None, 1, tn), lambda l, j: (l, 0, j))],
        out_specs=pl.BlockSpec((None, 8, tn), lambda l, j: (l, 0, j)),
        out_shape=jax.ShapeDtypeStruct((DEPTH, 8, 6 * D), F32),
        compiler_params=_cp(("arbitrary", "arbitrary"), 40),
        name="modulation",
    )(c8, w_ada, b_ada.reshape(DEPTH, 1, 6 * D))


KR0 = COL_MLA + Q_LORA + KV_LORA
IN_RAW = IN_W - QK_ROPE


WP_T = 256
WP_SWAP = KR0 // WP_T


def _winprep_kernel(prev_ref, cur_ref, o_ref):
    j = pl.program_id(1)
    keep = WP_T - QK_ROPE

    @pl.when(j < WP_SWAP)
    def _():
        o_ref[...] = cur_ref[...].astype(BF16)

    @pl.when(j == WP_SWAP)
    def _():
        cur = cur_ref[...]
        o_ref[:keep, :] = cur[:keep, :].astype(BF16)
        kr = cur[keep - QK_ROPE:keep, :]
        sw = jnp.concatenate([kr[16:32], kr[0:16], kr[48:64], kr[32:48]], axis=0)
        o_ref[keep:, :] = sw.astype(BF16)

    @pl.when(j > WP_SWAP)
    def _():
        o_ref[:QK_ROPE, :] = prev_ref[...].astype(BF16)
        o_ref[QK_ROPE:, :] = cur_ref[:keep, :].astype(BF16)


def _winprep(w_in_t):
    assert KR0 + QK_ROPE == (WP_SWAP + 1) * WP_T - QK_ROPE
    sub = WP_T // QK_ROPE
    return pl.pallas_call(
        _winprep_kernel,
        grid=(DEPTH, IN_W // WP_T),
        in_specs=[pl.BlockSpec((None, QK_ROPE, D), lambda l, j: (l, jnp.maximum(sub * j - 1, 0), 0)),
                  pl.BlockSpec((None, WP_T, D), lambda l, j: (l, j, 0))],
        out_specs=pl.BlockSpec((None, WP_T, D), lambda l, j: (l, j, 0)),
        out_shape=jax.ShapeDtypeStruct((DEPTH, IN_W, D), BF16),
        compiler_params=_cp(("arbitrary", "arbitrary"), 32),
        name="winprep",
    )(w_in_t, w_in_t)


def _stream_rows(refs, tile):
    if len(refs) == 1:
        return refs[0][...]
    return jnp.where(tile % TPB == TPB - 1, refs[1][...], refs[0][...])


def _stream_specs(xs):
    if len(xs) == 1:
        return lambda full: [pl.BlockSpec((TM, D), lambda i: (full(i), 0))]
    lat = lambda t: (t // TPB) * LPB + jnp.minimum(t % TPB, LPB - 1)
    return lambda full: [pl.BlockSpec((TM, D), lambda i: (lat(full(i)), 0)),
                         pl.BlockSpec((CTX, D), lambda i: (full(i) // TPB, 0))]


def _inproj_kernel(nx, *refs):
    (mod_ref, g_ref, w_ref, sg_ref, sw_ref, sb_ref,
     gq_ref, gkv_ref, wq_ref, wkv_ref, tq1_ref, tq2_ref, tk1_ref, tk2_ref,
     sgu_ref, q_ref, kc_ref, v_ref, s5_ref, conv_ref) = refs[nx:]
    x = _stream_rows(refs[:nx], pl.program_id(0))
    h = _modnorm(x, g_ref[...], mod_ref[0:1, :], mod_ref[1:2, :]).astype(BF16)

    def mm(a, b):
        return lax.dot_general(h, w_ref[a:b, :], _NT_DIMS, preferred_element_type=F32)

    p_a = mm(0, COL_MLA)
    pm = mm(COL_MLA, COL_S5)

    p = jax.nn.gelu(p_a)
    u = p[:, :GW]
    vb = _rms(p[:, GW:], sg_ref[...]).astype(BF16)
    for ck in range(TM // CHUNK):
        r = slice(ck * CHUNK, (ck + 1) * CHUNK)
        for hd in range(SGU_HEADS):
            c = slice(hd * 128, (hd + 1) * 128)
            m = jnp.dot(sw_ref[hd], vb[r, c], preferred_element_type=F32)
            sgu_ref[r, c] = (u[r, c] * (m + sb_ref[:, c])).astype(BF16)

    s5_ref[...] = mm(COL_S5, COL_CONV)
    conv_ref[:, :GW] = mm(COL_CONV, COL_CONV + GW)

    cq = _rms(pm[:, :Q_LORA], gq_ref[...]).astype(BF16)
    q = jnp.dot(cq, wq_ref[...], preferred_element_type=F32)
    tq1 = tq1_ref[...]
    tq2 = tq2_ref[...]
    for hd in range(MLA_HEADS):
        c = slice(hd * QK_PAD, (hd + 1) * QK_PAD)
        blk = q[:, c]
        q_ref[:, c] = (blk * tq1 + pltpu.roll(blk, QK_PAD - QK_ROPE, 1) * tq2).astype(BF16)
    ckv = _rms(pm[:, Q_LORA:Q_LORA + KV_LORA], gkv_ref[...]).astype(BF16)
    kv = jnp.dot(ckv, wkv_ref[...], preferred_element_type=F32)
    ones = jnp.ones((TM, 128), BF16)
    for hd in range(MLA_HEADS):
        v_ref[:, hd * QK_PAD:hd * QK_PAD + 128] = kv[:, GW + hd * 128:GW + (hd + 1) * 128].astype(BF16)
        v_ref[:, hd * QK_PAD + 128:(hd + 1) * QK_PAD] = ones
    kt = pm[:, Q_LORA + KV_LORA:]
    kr = (kt * tk1_ref[...] + pltpu.roll(kt, QK_ROPE, 1) * tk2_ref[...]).astype(BF16)
    for hd in range(MLA_HEADS):
        kc_ref[:, hd * QK_PAD:hd * QK_PAD + QK_NOPE] = kv[:, hd * 128:(hd + 1) * 128].astype(BF16)
        kc_ref[:, hd * QK_PAD + QK_NOPE:(hd + 1) * QK_PAD] = kr

    conv_ref[:, GW:] = mm(COL_CONV + GW, COL_CONV + 2 * GW) * mm(COL_CONV + 2 * GW, IN_W)


def _inproj(xs, mod, g1, w_in_r, li, sgu_p, mla_p):
    fix2 = lambda i: (0, 0)
    pos = lambda i: (i % TPB, 0)
    row = lambda i: (i, 0)
    qkw = MLA_HEADS * QK_PAD
    return pl.pallas_call(
        functools.partial(_inproj_kernel, len(xs)),
        grid=(NT,),
        in_specs=_stream_specs(xs)(lambda i: i) + [
                  pl.BlockSpec((None, 6, D), lambda i: (_seg_of_tile(i), 0, 0)),
                  pl.BlockSpec((1, D), fix2),
                  pl.BlockSpec((None, IN_W, D), lambda i: (li, 0, 0)),
                  pl.BlockSpec((1, GW), fix2),
                  pl.BlockSpec((SGU_HEADS, CHUNK, CHUNK), lambda i: (0, 0, 0)),
                  pl.BlockSpec((CHUNK, GW), fix2),
                  pl.BlockSpec((1, Q_LORA), fix2),
                  pl.BlockSpec((1, KV_LORA), fix2),
                  pl.BlockSpec((Q_LORA, qkw), fix2),
                  pl.BlockSpec((KV_LORA, 2 * GW), fix2),
                  pl.BlockSpec((TM, QK_PAD), pos),
                  pl.BlockSpec((TM, QK_PAD), pos),
                  pl.BlockSpec((TM, 128), pos),
                  pl.BlockSpec((TM, 128), pos)],
        out_specs=[pl.BlockSpec((TM, GW), row),
                   pl.BlockSpec((TM, qkw), row),
                   pl.BlockSpec((TM, qkw), row),
                   pl.BlockSpec((TM, qkw), row),
                   pl.BlockSpec((TM, GW), row),
                   pl.BlockSpec((TM, 2 * GW), row)],
        out_shape=[jax.ShapeDtypeStruct((B * S2, GW), BF16),
                   jax.ShapeDtypeStruct((B * S2, qkw), BF16),
                   jax.ShapeDtypeStruct((B * S2, qkw), BF16),
                   jax.ShapeDtypeStruct((B * S2, qkw), BF16),
                   jax.ShapeDtypeStruct((B * S2, GW), F32),
                   jax.ShapeDtypeStruct((B * S2, 2 * GW), F32)],
        compiler_params=_cp(("arbitrary",), 56),
        name="inproj",
    )(*xs, mod, g1, w_in_r, *sgu_p, *mla_p)


def _conv_tile(tile, p_ref, zp_ref, zn_ref, w_ref):
    r = tile % TPB
    bg = p_ref[:, :GW]
    z = p_ref[:, GW:]
    row = lax.broadcasted_iota(jnp.int32, (TM, GW), 0)
    has_prev = jnp.logical_and(r != 0, r != TPB - 1)
    has_next = r < LPB - 1
    prev_row = zp_ref[7:8, :] * has_prev.astype(F32)
    next_row = zn_ref[0:1, :] * has_next.astype(F32)
    zm = jnp.where(row == 0, prev_row, pltpu.roll(z, 1, 0))
    zp = jnp.where(row == TM - 1, next_row, pltpu.roll(z, TM - 1, 0))
    y = w_ref[0:1, :] * zm + w_ref[1:2, :] * z + w_ref[2:3, :] * zp
    return (bg * y).astype(BF16)


def _conv_specs(full):
    rb = TM // 8
    nrb = B * S2 // 8
    return [pl.BlockSpec((TM, 2 * GW), lambda i: (full(i), 0)),
            pl.BlockSpec((8, GW), lambda i: (jnp.maximum(full(i) * rb - 1, 0), 1)),
            pl.BlockSpec((8, GW), lambda i: (jnp.minimum((full(i) + 1) * rb, nrb - 1), 1)),
            pl.BlockSpec((3, GW), lambda i: (0, 0))]


_NT_DIMS = (((1,), (1,)), ((), ()))


AT_SUB = 4


def _attn_kernel(with_ctx, q_ref, kc_ref, v_ref, o_ref):
    def run(k0, nsub):
        def scores(it):
            r, hd = it
            cq = slice(hd * QK_PAD, (hd + 1) * QK_PAD)
            return lax.dot_general(q_ref[r * TM:(r + 1) * TM, cq], kc_ref[k0:, cq], _NT_DIMS,
                                   preferred_element_type=F32)

        def weights(s):
            return jnp.exp2(s - jnp.max(s, axis=-1, keepdims=True)).astype(BF16)

        def values(it, e):
            r, hd = it
            o = jnp.dot(e, v_ref[k0:, hd * QK_PAD:(hd + 1) * QK_PAD], preferred_element_type=F32)
            o_ref[r * TM:(r + 1) * TM, hd * 128:(hd + 1) * 128] = (
                (o[:, :128] / o[:, 128:129]).astype(BF16))

        items = [(r, hd) for r in range(nsub) for hd in range(MLA_HEADS)]
        s_next = scores(items[0])
        e_prev = None
        for n, it in enumerate(items):
            s_cur = s_next
            if n + 1 < len(items):
                s_next = scores(items[n + 1])
            e_cur = weights(s_cur)
            if e_prev is not None:
                values(items[n - 1], e_prev)
            e_prev = e_cur
        values(items[-1], e_prev)

    if with_ctx:
        t = pl.program_id(1)
        pl.when(t < LPB // AT_SUB)(lambda: run(0, AT_SUB))
        pl.when(t == LPB // AT_SUB)(lambda: run(SEQ, 1))
    else:
        run(0, AT_SUB)


def _attention(q3, kc3, v3, with_ctx):
    steps = LPB // AT_SUB + (1 if with_ctx else 0)
    qkw = MLA_HEADS * QK_PAD
    blk = AT_SUB * TM
    return pl.pallas_call(
        functools.partial(_attn_kernel, with_ctx),
        grid=(B, steps),
        in_specs=[pl.BlockSpec((None, blk, qkw), lambda b, t: (b, t, 0)),
                  pl.BlockSpec((None, S2, qkw), lambda b, t: (b, 0, 0)),
                  pl.BlockSpec((None, S2, qkw), lambda b, t: (b, 0, 0))],
        out_specs=pl.BlockSpec((None, blk, GW), lambda b, t: (b, t, 0)),
        out_shape=jax.ShapeDtypeStruct((B, S2 if with_ctx else SEQ, GW), BF16),
        compiler_params=_cp(("arbitrary", "arbitrary"), 48),
        name="attn",
    )(q3, kc3, v3)


def _s5_kernel(backward, *refs):
    if backward:
        (u_ref, w_ref, are_ref, aim_ref, c_ref, yf_ref, d_ref, wg_ref, bg_ref,
         o_ref, st_re, st_im, buf0, buf1) = refs
    else:
        u_ref, w_ref, are_ref, aim_ref, c_ref, o_ref, st_re, st_im, buf0, buf1 = refs
    bufs = (buf0, buf1)
    k = pl.program_id(0)
    hw = GW // 2
    nlt = S5_LT
    ht = S5_T // 2
    npc = nlt

    @pl.when(k == 0)
    def _():
        st_re[...] = jnp.zeros_like(st_re)
        st_im[...] = jnp.zeros_like(st_im)

    def rows_of(ref, hf):
        return jnp.concatenate([ref[b, hf * ht:(hf + 1) * ht, :] for b in range(B)], axis=0)

    def expand_pieces(hf):
        ub = rows_of(u_ref, hf).astype(BF16)
        buf = bufs[hf]

        def piece(h, n):
            bu = jnp.dot(ub[:, h * hw:(h + 1) * hw], w_ref[h, :, n * 256:(n + 1) * 256],
                         preferred_element_type=F32)
            for b in range(B):
                for cc in range(2):
                    buf[2 * n + cc, pl.ds(2 * b + h, ht, stride=S5_PITCH), :] = (
                        bu[b * ht:(b + 1) * ht, cc * 128:(cc + 1) * 128])

        return [functools.partial(piece, h, n) for h in range(2) for n in range(npc)]

    a_re = are_ref[...]
    a_im = aim_ref[...]
    state = [st_re[...], st_im[...]]

    def scan_pieces(hf, per):
        order = list(range(ht - 1, -1, -1)) if backward else list(range(ht))
        buf = bufs[hf]

        def piece(steps):
            sr, si = state
            for j in steps:
                r0 = S5_PITCH * j
                br = buf[0:nlt, r0:r0 + 8, :]
                bi = buf[nlt:2 * nlt, r0:r0 + 8, :]
                sr, si = a_re * sr - a_im * si + br, a_re * si + a_im * sr + bi
                buf[0:nlt, r0:r0 + 8, :] = sr
                buf[nlt:2 * nlt, r0:r0 + 8, :] = si
            state[0], state[1] = sr, si

        return [functools.partial(piece, order[i:i + per]) for i in range(0, ht, per)]

    acc = {}

    def readout_pieces(hf):
        buf = bufs[hf]

        def piece(h, n):
            s = jnp.concatenate(
                [jnp.concatenate([buf[2 * n + cc, pl.ds(2 * b + h, ht, stride=S5_PITCH), :]
                                  for cc in range(2)], axis=1) for b in range(B)], axis=0)
            part = jnp.dot(s.astype(BF16), c_ref[h, n * 256:(n + 1) * 256, :],
                           preferred_element_type=F32)
            acc[(hf, h)] = part if n == 0 else acc[(hf, h)] + part

        return [functools.partial(piece, h, n) for h in range(2) for n in range(npc)]

    def finish(hf):
        y = jnp.concatenate([acc[(hf, 0)], acc[(hf, 1)]], axis=1)
        if backward:
            y = y + rows_of(yf_ref, hf) + d_ref[...] * rows_of(u_ref, hf)
            g = jax.nn.gelu(y)
            z = jnp.dot(g.astype(BF16), wg_ref[...], preferred_element_type=F32) + bg_ref[...]
            y = (g * jax.nn.sigmoid(z)).astype(BF16)
        for b in range(B):
            o_ref[b, hf * ht:(hf + 1) * ht, :] = y[b * ht:(b + 1) * ht]

    def interleave(mxu, vpu):
        for i, m in enumerate(mxu):
            m()
            if i < len(vpu):
                vpu[i]()

    first, second = (1, 0) if backward else (0, 1)
    per = ht // (2 * npc)
    for m in expand_pieces(first):
        m()
    interleave(expand_pieces(second), scan_pieces(first, per))
    interleave(readout_pieces(first), scan_pieces(second, per))
    st_re[...] = state[0]
    st_im[...] = state[1]
    finish(first)
    for m in readout_pieces(second):
        m()
    finish(second)


def _s5_pass(u3, ops, li, glu):
    backward = glu is not None
    dr = 1 if backward else 0
    if backward:
        blk = lambda k: (0, S5_NCH - 1 - k, 0)
    else:
        blk = lambda k: (0, (k + SEQ // S5_T) % S5_NCH, 0)
    w_bd, a_re8, a_im8, c_bd = ops
    fixed = lambda k: (li, dr, 0, 0, 0)
    in_specs = [pl.BlockSpec((B, S5_T, GW), blk),
                pl.BlockSpec((None, None, 2, GW // 2, 2 * S5_HALF), fixed),
                pl.BlockSpec((None, None, S5_LT, 8, 128), fixed),
                pl.BlockSpec((None, None, S5_LT, 8, 128), fixed),
                pl.BlockSpec((None, None, 2, 2 * S5_HALF, GW // 2), fixed)]
    args = [u3, w_bd, a_re8, a_im8, c_bd]
    if backward:
        in_specs += [pl.BlockSpec((B, S5_T, GW), blk),
                     pl.BlockSpec((1, GW), lambda k: (0, 0)),
                     pl.BlockSpec((GW, GW), lambda k: (0, 0)),
                     pl.BlockSpec((1, GW), lambda k: (0, 0))]
        args += list(glu)
    return pl.pallas_call(
        functools.partial(_s5_kernel, backward),
        grid=(S5_NCH,),
        in_specs=in_specs,
        out_specs=pl.BlockSpec((B, S5_T, GW), blk),
        out_shape=jax.ShapeDtypeStruct((B, S2, GW), BF16 if backward else F32),
        scratch_shapes=[pltpu.VMEM((S5_LT, 8, 128), F32),
                        pltpu.VMEM((S5_LT, 8, 128), F32),
                        pltpu.VMEM((2 * S5_LT, S5_PITCH * S5_T // 2, 128), F32),
                        pltpu.VMEM((2 * S5_LT, S5_PITCH * S5_T // 2, 128), F32)],
        compiler_params=_cp(("arbitrary",), 48),
        name="s5_bwd_glu" if backward else "s5_fwd",
    )(*args)


OP_SUB = 2


def _outproj_kernel(nx, full, *refs):
    per = nx + 7
    cw_ref, w_ref, g2_ref, wr_ref, x1_ref, h2_ref, aff_ref = refs[OP_SUB * per:]
    tiles = [full(OP_SUB * pl.program_id(0) + s) for s in range(OP_SUB)]
    acts_all = []
    for s in range(OP_SUB):
        a0_ref, a1_ref, a2_ref, p_ref, zp_ref, zn_ref, _ = refs[s * per + nx:(s + 1) * per]
        acts_all.append((a0_ref[...], a1_ref[...], a2_ref[...],
                         _conv_tile(tiles[s], p_ref, zp_ref, zn_ref, cw_ref)))
    outs = []
    for s in range(OP_SUB):
        acts = acts_all[s]
        halves = []
        for c in (slice(0, D // 2), slice(D // 2, D)):
            o = jnp.dot(acts[0], w_ref[0:GW, c], preferred_element_type=F32)
            for m in range(1, 4):
                o = o + jnp.dot(acts[m], w_ref[m * GW:(m + 1) * GW, c], preferred_element_type=F32)
            halves.append(o)
        outs.append(jnp.concatenate(halves, axis=1))
    for s in range(OP_SUB):
        tr = refs[s * per:(s + 1) * per]
        mod_ref = tr[-1]
        tile = tiles[s]
        rows = slice(s * TM, (s + 1) * TM)
        x = _stream_rows(tr[:nx], tile)
        x1 = x + mod_ref[2:3, :] * outs[s]
        x1_ref[rows, :] = x1
        h2 = _modnorm(x1, g2_ref[...], mod_ref[3:4, :], mod_ref[4:5, :]).astype(BF16)
        h2_ref[rows, :] = h2
        lg2 = lax.dot_general(wr_ref[...], h2, _NT_DIMS, preferred_element_type=F32)
        lg = lg2[:N_EXP] + lg2[N_EXP:]
        e = jnp.exp(lg - jnp.max(lg, axis=0, keepdims=True))
        aff_ref[:, rows] = e / jnp.sum(e, axis=0, keepdims=True)


def _outproj(sgu_o, attn_o, ssm_o, p_conv, conv_w, w_out_bf, xs, mod, g2, wr2, li, with_ctx):
    ntiles = NT if with_ctx else NLT
    full = (lambda i: i) if with_ctx else _lat_tile
    in_specs, args = [], []
    for s in range(OP_SUB):
        out_t = lambda i, s=s: OP_SUB * i + s
        full_t = lambda i, s=s: full(OP_SUB * i + s)
        frow = lambda i, f=full_t: (f(i), 0)
        orow = lambda i, f=out_t: (f(i), 0)
        in_specs += _stream_specs(xs)(full_t) + [
            pl.BlockSpec((TM, GW), frow),
            pl.BlockSpec((TM, GW), orow),
            pl.BlockSpec((TM, GW), frow)] + _conv_specs(full_t)[:3] + [
            pl.BlockSpec((None, 6, D), lambda i, f=full_t: (_seg_of_tile(f(i)), 0, 0))]
        args += [*xs, sgu_o, attn_o, ssm_o, p_conv, p_conv, p_conv, mod]
    in_specs += [_conv_specs(full)[3],
                 pl.BlockSpec((None, D, D), lambda i: (li, 0, 0)),
                 pl.BlockSpec((1, D), lambda i: (0, 0)),
                 pl.BlockSpec((2 * N_EXP, D), lambda i: (0, 0))]
    args += [conv_w, w_out_bf, g2, wr2]
    blk = OP_SUB * TM
    return pl.pallas_call(
        functools.partial(_outproj_kernel, len(xs), full),
        grid=(ntiles // OP_SUB,),
        in_specs=in_specs,
        out_specs=[pl.BlockSpec((blk, D), lambda i: (i, 0)),
                   pl.BlockSpec((blk, D), lambda i: (i, 0)),
                   pl.BlockSpec((N_EXP, blk), lambda i: (0, i))],
        out_shape=[jax.ShapeDtypeStruct((ntiles * TM, D), F32),
                   jax.ShapeDtypeStruct((ntiles * TM, D), BF16),
                   jax.ShapeDtypeStruct((N_EXP, ntiles * TM), F32)],
        compiler_params=_cp(("arbitrary",), 56),
        name="outproj",
    )(*args)


def _one_hot_rows(rank_row, cap):
    slot = lax.broadcasted_iota(jnp.int32, (cap, rank_row.shape[-1]), 0).astype(F32)
    return jnp.where(rank_row == slot, 1.0, 0.0).astype(BF16)


def _cap_thresholds(bits_caps):
    thrs = [jnp.zeros((bits.shape[0], 1), jnp.int32) for bits, _ in bits_caps]
    for bit in range(30, -1, -1):
        for i, (bits, cap) in enumerate(bits_caps):
            cand = thrs[i] | (1 << bit)
            cnt = jnp.sum(jnp.where(bits >= cand, 1.0, 0.0), axis=1, keepdims=True)
            thrs[i] = jnp.where(cnt >= cap, cand, thrs[i])
    return thrs


def _select_one(a, bits, thr, tri, cap, rank_ref, w_ref):
    ne, n = a.shape
    gt = jnp.where(bits > thr, 1.0, 0.0)
    eq = jnp.where(bits == thr, 1.0, 0.0)
    need = cap - jnp.sum(gt, axis=1, keepdims=True)
    eq_before = jnp.dot(eq.astype(BF16), tri, preferred_element_type=F32) - eq
    sel = gt + eq * jnp.where(eq_before < need, 1.0, 0.0)
    rank = jnp.dot(sel.astype(BF16), tri, preferred_element_type=F32) - 1.0
    rank = jnp.where(sel > 0.5, rank, -1.0)
    slot = lax.broadcasted_iota(jnp.int32, (cap, n), 0).astype(F32)
    for e in range(ne):
        rank_ref[e] = rank[e:e + 1, :]
        hit = rank[e:e + 1, :] == slot
        w_ref[e * cap:(e + 1) * cap, :] = jnp.sum(jnp.where(hit, a[e:e + 1, :], 0.0),
                                                  axis=1, keepdims=True)


SEL_SUB = 4


def _select_kernel(with_ctx, a_ref, tri_ref, *out_refs):
    rows_b = S2 if with_ctx else SEQ
    sets = []
    for j in range(SEL_SUB):
        o = j * rows_b
        sets.append((a_ref[:, o:o + SEQ], CAP, tri_ref[...], out_refs[0].at[j], out_refs[1].at[j]))
        if with_ctx:
            sets.append((a_ref[:, o + SEQ:o + S2], CAP_C, tri_ref[:CTX, :CTX],
                         out_refs[2].at[j], out_refs[3].at[j]))
    bits = [pltpu.bitcast(s[0], jnp.int32) for s in sets]
    thrs = _cap_thresholds([(b, s[1]) for b, s in zip(bits, sets)])
    for i, (a, cap, tri, rank_ref, w_ref) in enumerate(sets):
        _select_one(a, bits[i], thrs[i], tri, cap, rank_ref, w_ref)


def _select(aff_t, tri, with_ctx):
    rows_b = S2 if with_ctx else SEQ
    out_specs = [pl.BlockSpec((SEL_SUB, N_EXP, 1, SEQ), lambda b: (b, 0, 0, 0)),
                 pl.BlockSpec((SEL_SUB, N_EXP * CAP, 1), lambda b: (b, 0, 0))]
    out_shape = [jax.ShapeDtypeStruct((B, N_EXP, 1, SEQ), F32),
                 jax.ShapeDtypeStruct((B, N_EXP * CAP, 1), F32)]
    if with_ctx:
        out_specs += [pl.BlockSpec((SEL_SUB, N_EXP, 1, CTX), lambda b: (b, 0, 0, 0)),
                      pl.BlockSpec((SEL_SUB, N_EXP * CAP_C, 1), lambda b: (b, 0, 0))]
        out_shape += [jax.ShapeDtypeStruct((B, N_EXP, 1, CTX), F32),
                      jax.ShapeDtypeStruct((B, N_EXP * CAP_C, 1), F32)]
    return pl.pallas_call(
        functools.partial(_select_kernel, with_ctx),
        grid=(B // SEL_SUB,),
        in_specs=[pl.BlockSpec((N_EXP, SEL_SUB * rows_b), lambda b: (0, b)),
                  pl.BlockSpec((SEQ, SEQ), lambda b: (0, 0))],
        out_specs=out_specs,
        out_shape=out_shape,
        compiler_params=_cp(("arbitrary",), 48),
        name="select",
    )(aff_t, tri)


def _gather_kernel(cap, rank_ref, h_ref, o_ref):
    p = jnp.concatenate([_one_hot_rows(rank_ref[j], cap) for j in range(rank_ref.shape[0])], axis=0)
    o_ref[...] = jnp.dot(p, h_ref[...], preferred_element_type=F32).astype(BF16)


def _gather(rank, h3, ctx_only):
    if ctx_only:
        cap, n, ne = CAP_C, CTX, N_EXP
        hmap = lambda b, r: (b, TPB - 1, 0)
    else:
        cap, n, ne = CAP, SEQ, 4
        hmap = lambda b, r: (b, 0, 0)
    return pl.pallas_call(
        functools.partial(_gather_kernel, cap),
        grid=(B, N_EXP // ne),
        in_specs=[pl.BlockSpec((None, ne, 1, n), lambda b, r: (b, r, 0, 0)),
                  pl.BlockSpec((None, n, D), hmap)],
        out_specs=pl.BlockSpec((None, ne * cap, D), lambda b, r: (b, r, 0)),
        out_shape=jax.ShapeDtypeStruct((B, N_EXP * cap, D), BF16),
        compiler_params=_cp(("arbitrary", "arbitrary"), 48),
        name="gather_ctx" if ctx_only else "gather",
    )(rank, h3)


def _ffn_kernel(with_ctx, *refs):
    if with_ctx:
        x_ref, xc_ref, wg_ref, wu_ref, wd_ref, ws_ref, wsc_ref, y_ref, yc_ref, acc = refs
    else:
        x_ref, wg_ref, wu_ref, wd_ref, ws_ref, y_ref, acc = refs
    f = pl.program_id(1)
    last = FF // FF_T - 1
    nl = B * CAP

    def step(kind):
        wg = wg_ref[...].astype(BF16)
        wu = wu_ref[...].astype(BF16)
        wd = wd_ref[...].astype(BF16)
        x = x_ref[...].reshape(nl, D)
        if with_ctx:
            x = jnp.concatenate([x, xc_ref[...].reshape(B * CAP_C, D)], axis=0)
        gate = jnp.dot(x, wg, preferred_element_type=F32)
        up = jnp.dot(x, wu, preferred_element_type=F32)
        hid = (gate * jax.nn.sigmoid(gate) * up).astype(BF16)
        part = jnp.dot(hid, wd, preferred_element_type=F32)
        if kind == "first":
            acc[...] = part
        elif kind == "mid":
            acc[...] += part
        else:
            y = acc[...] + part
            y_ref[...] = (y[:nl] * ws_ref[...].reshape(nl, 1)).astype(BF16).reshape(y_ref.shape)
            if with_ctx:
                yc = y[nl:] * wsc_ref[...].reshape(B * CAP_C, 1)
                yc_ref[...] = yc.astype(BF16).reshape(yc_ref.shape)

    pl.when(f == 0)(lambda: step("first"))
    pl.when(jnp.logical_and(f > 0, f < last))(lambda: step("mid"))
    pl.when(f == last)(lambda: step("last"))


def _ffn(xs, ws, xc, wsc, w_gate, w_up, w_down, li):
    with_ctx = xc is not None
    in_specs = [pl.BlockSpec((B, CAP, D), lambda e, f: (0, e, 0))]
    args = [xs]
    if with_ctx:
        in_specs.append(pl.BlockSpec((B, CAP_C, D), lambda e, f: (0, e, 0)))
        args.append(xc)
    in_specs += [pl.BlockSpec((None, None, D, FF_T), lambda e, f: (li, e, 0, f)),
                 pl.BlockSpec((None, None, D, FF_T), lambda e, f: (li, e, 0, f)),
                 pl.BlockSpec((None, None, FF_T, D), lambda e, f: (li, e, f, 0)),
                 pl.BlockSpec((B, CAP, 1), lambda e, f: (0, e, 0))]
    args += [w_gate, w_up, w_down, ws]
    out_specs = [pl.BlockSpec((B, CAP, D), lambda e, f: (0, e, 0))]
    out_shape = [jax.ShapeDtypeStruct((B, N_EXP * CAP, D), BF16)]
    scratch = [pltpu.VMEM((B * (CAP + CAP_C if with_ctx else CAP), D), F32)]
    if with_ctx:
        in_specs.append(pl.BlockSpec((B, CAP_C, 1), lambda e, f: (0, e, 0)))
        args.append(wsc)
        out_specs.append(pl.BlockSpec((B, CAP_C, D), lambda e, f: (0, e, 0)))
        out_shape.append(jax.ShapeDtypeStruct((B, N_EXP * CAP_C, D), BF16))
    return pl.pallas_call(
        functools.partial(_ffn_kernel, with_ctx),
        grid=(N_EXP, FF // FF_T),
        in_specs=in_specs,
        out_specs=out_specs,
        out_shape=out_shape,
        scratch_shapes=scratch,
        compiler_params=_cp(("arbitrary", "arbitrary"), 56),
        name="ffn",
    )(*args)


_TN_DIMS = (((0,), (0,)), ((), ()))


def _scatter_kernel(with_ctx, final, *refs):
    refs = list(refs)
    p_ref, y_ref = refs[:2]
    pc_ref, yc_ref = refs[2:4] if with_ctx else (None, None)
    rest = refs[4:] if with_ctx else refs[2:]
    x_ref, mod_ref = rest[:2]
    gf_ref = rest[2] if final else None
    o_ref = rest[-1]

    def finish(rank_ref, yr, cap):
        ng, ge = 4, N_EXP // 4

        def hot(j):
            return jnp.concatenate([_one_hot_rows(rank_ref[e], cap)
                                    for e in range(j * ge, (j + 1) * ge)], axis=0)

        upd = None
        p_next = hot(0)
        for j in range(ng):
            p_cur = p_next
            if j + 1 < ng:
                p_next = hot(j + 1)
            part = lax.dot_general(p_cur, yr[j * ge * cap:(j + 1) * ge * cap, :], _TN_DIMS,
                                   preferred_element_type=F32)
            upd = part if upd is None else upd + part
        x = x_ref[...] + mod_ref[5:6, :] * upd
        o_ref[...] = _rms(x, gf_ref[...]) if final else x

    if with_ctx:
        t = pl.program_id(1)
        pl.when(t < LPB)(lambda: finish(p_ref, y_ref, CAP))
        pl.when(t == LPB)(lambda: finish(pc_ref, yc_ref, CAP_C))
    else:
        finish(p_ref, y_ref, CAP)


def _scatter(rank, y, rank_c, yc, x3, mod, final_g):
    with_ctx = rank_c is not None
    final = final_g is not None
    tpb = TPB if with_ctx else LPB
    in_specs = [pl.BlockSpec((None, N_EXP, 1, TM), lambda b, t: (b, 0, 0, jnp.minimum(t, LPB - 1))),
                pl.BlockSpec((None, N_EXP * CAP, D), lambda b, t: (b, 0, 0))]
    args = [rank, y]
    if with_ctx:
        in_specs += [pl.BlockSpec((None, N_EXP, 1, CTX), lambda b, t: (b, 0, 0, 0)),
                     pl.BlockSpec((None, N_EXP * CAP_C, D), lambda b, t: (b, 0, 0))]
        args += [rank_c, yc]
    in_specs += [pl.BlockSpec((None, TM, D), lambda b, t: (b, t, 0)),
                 pl.BlockSpec((None, 6, D), lambda b, t: (jnp.where(t == LPB, B, b), 0, 0))]
    args += [x3, mod]
    if final:
        in_specs.append(pl.BlockSpec((1, D), lambda b, t: (0, 0)))
        args.append(final_g)
    return pl.pallas_call(
        functools.partial(_scatter_kernel, with_ctx, final),
        grid=(B, tpb),
        in_specs=in_specs,
        out_specs=pl.BlockSpec((None, TM, D), lambda b, t: (b, t, 0)),
        out_shape=jax.ShapeDtypeStruct((B, tpb * TM, D), F32),
        compiler_params=_cp(("arbitrary", "arbitrary"), 56),
        name="scatter",
    )(*args)


def _rope_tables():
    n_freq = QK_ROPE // 4
    grid_w = 64
    pos = jnp.arange(SEQ, dtype=F32)
    inv_freq = 10000.0 ** (-jnp.arange(n_freq, dtype=F32) / n_freq)
    ang_r = jnp.floor(pos / grid_w)[:, None] * inv_freq
    ang_c = (pos - grid_w * jnp.floor(pos / grid_w))[:, None] * inv_freq
    cr, sr, cc, sc = jnp.cos(ang_r), jnp.sin(ang_r), jnp.cos(ang_c), jnp.sin(ang_c)
    cos = jnp.concatenate([cr, cr, cc, cc], axis=1)
    sin = jnp.concatenate([-sr, sr, -sc, sc], axis=1)
    cos = jnp.concatenate([cos, jnp.ones((CTX, QK_ROPE), F32)], axis=0)
    sin = jnp.concatenate([sin, jnp.zeros((CTX, QK_ROPE), F32)], axis=0)
    z = jnp.zeros((S2, QK_ROPE), F32)
    qs = ATT_SCALE * LOG2E
    tq1 = jnp.concatenate([jnp.full((S2, QK_NOPE), qs, F32), cos * qs, z], axis=1)
    tq2 = jnp.concatenate([jnp.zeros((S2, QK_NOPE), F32), sin * qs, z], axis=1)
    tk1 = jnp.concatenate([cos, z], axis=1)
    tk2 = jnp.concatenate([sin, z], axis=1)
    return tq1, tq2, tk1, tk2


def _pair_swap(w):
    return jnp.concatenate([w[..., 16:32], w[..., 0:16], w[..., 48:64], w[..., 32:48]], axis=-1)


def _s5_place_kernel(bre_ref, bim_ref, cre_ref, cim_ref, w_ref, c_ref):
    hg = S5_G // 2
    w_ref[...] = jnp.zeros_like(w_ref)
    c_ref[...] = jnp.zeros_like(c_ref)
    for g in range(hg):
        r = slice(g * S5_CH, (g + 1) * S5_CH)
        for part, (b_ref, k_ref, sign) in enumerate(((bre_ref, cre_ref, 1.0), (bim_ref, cim_ref, -1.0))):
            s = slice(part * S5_HALF + g * S5_N, part * S5_HALF + (g + 1) * S5_N)
            w_ref[r, s] = b_ref[g].astype(BF16)
            c_ref[s, r] = (sign * k_ref[g]).astype(BF16)


def _s5_operators(a_re, a_im, log_dt, b_re, b_im, c_re, c_im):
    a = lax.complex(jnp.minimum(a_re.astype(F32), S5_MAX_RE), a_im.astype(F32))
    dt = jnp.exp(log_dt.astype(F32))[..., None]
    abar = jnp.exp(a * dt)
    bbar = ((abar - 1.0) / a)[..., None] * lax.complex(b_re.astype(F32), b_im.astype(F32))
    hg = S5_G // 2

    def per_group(m):
        return m.reshape(DEPTH, 2, 2, hg, m.shape[-2], m.shape[-1])

    bt = jnp.swapaxes(bbar, -1, -2)
    ct_re = jnp.swapaxes(c_re.astype(F32), -1, -2)
    ct_im = jnp.swapaxes(c_im.astype(F32), -1, -2)
    bspec = pl.BlockSpec((None, None, None, hg, S5_CH, S5_N), lambda l, z, h: (l, z, h, 0, 0, 0))
    cspec = pl.BlockSpec((None, None, None, hg, S5_N, S5_CH), lambda l, z, h: (l, z, h, 0, 0, 0))
    w_bd, c_bd = pl.pallas_call(
        _s5_place_kernel,
        grid=(DEPTH, 2, 2),
        in_specs=[bspec, bspec, cspec, cspec],
        out_specs=[pl.BlockSpec((None, None, None, GW // 2, 2 * S5_HALF), lambda l, z, h: (l, z, h, 0, 0)),
                   pl.BlockSpec((None, None, None, 2 * S5_HALF, GW // 2), lambda l, z, h: (l, z, h, 0, 0))],
        out_shape=[jax.ShapeDtypeStruct((DEPTH, 2, 2, GW // 2, 2 * S5_HALF), BF16),
                   jax.ShapeDtypeStruct((DEPTH, 2, 2, 2 * S5_HALF, GW // 2), BF16)],
        compiler_params=_cp(("arbitrary",) * 3, 32),
        name="s5_place",
    )(per_group(jnp.real(bt)), per_group(jnp.imag(bt)), per_group(ct_re), per_group(ct_im))

    def rows8(m):
        m = m.reshape(DEPTH, 2, 1, 2, S5_LT, 128)
        m = jnp.broadcast_to(m, (DEPTH, 2, B, 2, S5_LT, 128)).reshape(DEPTH, 2, 2 * B, S5_LT, 128)
        return jnp.transpose(m, (0, 1, 3, 2, 4))

    return w_bd, rows8(jnp.real(abar)), rows8(jnp.imag(abar)), c_bd


def kernel(x, c, ctx, c_ctx, norm1_g, norm2_g, w_ada, b_ada, w_in, w_out, sgu_norm_g, sgu_w,
           sgu_b, mla_q_norm_g, mla_w_uq, mla_kv_norm_g, mla_w_ukv, s5_a_re, s5_a_im, s5_log_dt,
           s5_b_re, s5_b_im, s5_c_re, s5_c_im, s5_d, s5_w_glu, s5_b_glu, conv_w, moe_w_router,
           moe_w_gate, moe_w_up, moe_w_down, final_norm_g):
    c8 = jnp.concatenate([c, c_ctx[None, :], jnp.zeros((3, D), F32)], axis=0)
    mod_all = _modulation(c8, w_ada, b_ada).reshape(DEPTH, 8, 6, D)
    rope_t = _rope_tables()
    tri = jnp.triu(jnp.ones((SEQ, SEQ), BF16))
    w_in_r = _winprep(jnp.transpose(w_in, (0, 2, 1)))
    w_out_bf = w_out.astype(BF16)
    s5_ops = _s5_operators(s5_a_re, s5_a_im, s5_log_dt, s5_b_re, s5_b_im, s5_c_re, s5_c_im)
    xs = (x.reshape(B * SEQ, D), ctx.reshape(B * CTX, D))

    for i in range(DEPTH):
        last = i == DEPTH - 1
        mod = mod_all[i]
        if i > 0:
            xs = (x3.reshape(B * S2, D),)

        wq = mla_w_uq[i].reshape(Q_LORA, MLA_HEADS, QK_NOPE + QK_ROPE)
        wq_r = wq[:, :, QK_NOPE:]
        wq_ext = jnp.concatenate([wq[:, :, :QK_NOPE], wq_r, _pair_swap(wq_r)], axis=2)
        wq_ext = wq_ext.reshape(Q_LORA, MLA_HEADS * QK_PAD).astype(BF16)
        wkv = mla_w_ukv[i].reshape(KV_LORA, MLA_HEADS, 2 * QK_NOPE)
        wkv_ext = jnp.concatenate([wkv[:, :, :QK_NOPE].reshape(KV_LORA, -1),
                                   wkv[:, :, QK_NOPE:].reshape(KV_LORA, -1)], axis=1).astype(BF16)
        sgu_p = (sgu_norm_g[i][None, :], sgu_w[i].astype(BF16),
                 jnp.repeat(jnp.swapaxes(sgu_b[i], 0, 1), 128, axis=1))
        mla_p = (mla_q_norm_g[i][None, :], mla_kv_norm_g[i][None, :], wq_ext, wkv_ext) + rope_t
        sgu_o, q, kc, v, p_s5, p_conv = _inproj(xs, mod, norm1_g[i][None, :], w_in_r, i,
                                                sgu_p, mla_p)

        qk3 = (B, S2, MLA_HEADS * QK_PAD)
        attn_o = _attention(q.reshape(qk3), kc.reshape(qk3), v.reshape(qk3), not last)
        attn_o = attn_o.reshape(-1, GW)

        u3 = p_s5.reshape(B, S2, GW)
        y_fwd = _s5_pass(u3, s5_ops, i, None)
        ssm_o = _s5_pass(u3, s5_ops, i,
                         (y_fwd, s5_d[i][None, :], s5_w_glu[i].astype(BF16), s5_b_glu[i][None, :]))
        ssm_o = ssm_o.reshape(B * S2, GW)

        wr_t = jnp.transpose(moe_w_router[i])
        wr_hi = wr_t.astype(BF16)
        wr2 = jnp.concatenate([wr_hi, (wr_t - wr_hi.astype(F32)).astype(BF16)], axis=0)
        x1, h2, aff_t = _outproj(sgu_o, attn_o, ssm_o, p_conv, conv_w[i], w_out_bf, xs, mod,
                                 norm2_g[i][None, :], wr2, i, not last)

        rows_b = SEQ if last else S2
        sel = _select(aff_t, tri, not last)
        h3 = h2.reshape(B, rows_b, D)
        xs = _gather(sel[0], h3, ctx_only=False)
        xc = _gather(sel[2], h3, ctx_only=True) if not last else None
        ys = _ffn(xs, sel[1], xc, sel[3] if not last else None, moe_w_gate, moe_w_up, moe_w_down, i)
        x1_3 = x1.reshape(B, rows_b, D)
        if last:
            x3 = _scatter(sel[0], ys[0], None, None, x1_3, mod, final_norm_g[None, :])
        else:
            x3 = _scatter(sel[0], ys[0], sel[2], ys[1], x1_3, mod, None)

    return x3
```
